```python
import jax, jax.numpy as jnp
from jax import lax
import numpy as np

D_MODEL = 1024
BATCH = 16
SEQ = 4096
DEPTH = 1
DEC_BATCH = 32
DEC_SEQ = 32
PAST_LEN = 1024

CHUNK = 64
N_META = 16
A_HEADS = 8
A_KV_HEADS = 2
A_GROUP = A_HEADS // A_KV_HEADS
A_HEAD_DIM = 64
A_Q = A_HEADS * A_HEAD_DIM
A_KV = A_KV_HEADS * A_HEAD_DIM
IDX_HEADS = 8
IDX_DIM = 64
IDX_Q = IDX_HEADS * IDX_DIM
TOPK_MAX = 256
Q_BLOCK = 128
B_HEADS = 4
B_KEY_DIM = 128
B_VAL_DIM = 128
B_KW = B_HEADS * B_KEY_DIM
B_WIDTH = B_HEADS * B_VAL_DIM
REC_BLOCK = 16
D_FF = 2816
ALPHA = (2.0 * DEPTH) ** 0.25
BETA = (8.0 * DEPTH) ** -0.25
LN_EPS = 1e-5
RMS_EPS = 1e-6
NEG_INF = -1e30
IN_SPLIT = (A_Q, A_KV, A_KV, IDX_Q, IDX_DIM, IDX_HEADS, B_KW, B_KW, B_WIDTH, B_WIDTH, D_MODEL, D_MODEL)
IN_TOTAL = A_Q + 2 * A_KV + IDX_Q + IDX_DIM + IDX_HEADS + 2 * B_KW + 2 * B_WIDTH + 2 * D_MODEL

kernel_name = 'hybrid_dsa_hgrn2_streaming_step'


def layer_norm(x, g, b):
    xf = x.astype(jnp.float32)
    mu = jnp.mean(xf, -1, keepdims=True)
    var = jnp.mean(jnp.square(xf - mu), -1, keepdims=True)
    y = (xf - mu) * lax.rsqrt(var + LN_EPS)
    return (y * g.astype(jnp.float32) + b.astype(jnp.float32)).astype(x.dtype)


def swiglu(x, wg, wu, wd):
    return (jax.nn.silu(x @ wg) * (x @ wu)) @ wd


def ffn_step(h, g, b, wg, wu, wd):
    return layer_norm(ALPHA * h + 0.5 * swiglu(h, wg, wu, wd), g, b)


def alibi_slopes():
    return 2.0 ** (-8.0 * jnp.arange(1, A_HEADS + 1, dtype=jnp.float32) / A_HEADS)


def mixer_inputs(h, w_in, lb):
    B, T, _ = h.shape
    offs = np.cumsum(IN_SPLIT)[:-1].tolist()
    aq, ak, av, iq, ik, iw, bq, bf, bi, bg, ga, gb = jnp.split(h @ w_in, offs, axis=-1)
    att = (aq.reshape(B, T, A_HEADS, A_HEAD_DIM),
           ak.reshape(B, T, A_KV_HEADS, A_HEAD_DIM),
           av.reshape(B, T, A_KV_HEADS, A_HEAD_DIM),
           iq.reshape(B, T, IDX_HEADS, IDX_DIM),
           ik,
           iw * (IDX_HEADS ** -0.5))
    lbh = lb.reshape(B_HEADS, B_KEY_DIM).astype(jnp.float32)
    fx = bf.astype(jnp.float32).reshape(B, T, B_HEADS, B_KEY_DIM)
    logf = jnp.log(lbh + (1.0 - lbh) * jax.nn.sigmoid(fx))
    kk = (1.0 - lbh) * jax.nn.sigmoid(-fx)
    rq = jax.nn.silu(bq.astype(jnp.float32)).reshape(B, T, B_HEADS, B_KEY_DIM)
    rv = bi.astype(jnp.float32).reshape(B, T, B_HEADS, B_VAL_DIM)
    return att, (rq, kk, rv, logf), (bg, ga, gb)


def dsa_block(q, qi, wi, qpos, k, v, ki, kpos, topk, slopes):
    B, Q = q.shape[:2]
    qc = qpos // CHUNK
    kc = kpos // CHUNK
    logits = jnp.einsum('bqhd,bsd->bqhs', qi, ki).astype(jnp.float32) * (IDX_DIM ** -0.5)
    score = jnp.einsum('bqhs,bqh->bqs', jax.nn.relu(logits), wi.astype(jnp.float32))
    score = jnp.where((kc[None, :] <= qc[:, None])[None], score, -jnp.inf)
    _, idx = lax.top_k(score, topk)
    sel_pos = kpos[idx]
    valid = (sel_pos // CHUNK) <= qc[None, :, None]
    take = jax.vmap(lambda a, i: a[i])
    ks = take(k, idx)
    vs = take(v, idx)
    qg = q.reshape(B, Q, A_KV_HEADS, A_GROUP, A_HEAD_DIM)
    s = jnp.einsum('bqhgd,bqkhd->bqhgk', qg, ks).astype(jnp.float32) * (A_HEAD_DIM ** -0.5)
    dist = jnp.abs(qpos[None, :, None] - sel_pos).astype(jnp.float32)
    s = s - slopes.reshape(A_KV_HEADS, A_GROUP)[None, None, :, :, None] * dist[:, :, None, None, :]
    s = jnp.where(valid[:, :, None, None, :], s, NEG_INF)
    p = jax.nn.softmax(s, axis=-1).astype(vs.dtype)
    o = jnp.einsum('bqhgk,bqkhd->bqhgd', p, vs)
    return o.reshape(B, Q, A_Q)


def attend_prompt(q, qi, wi, k, v, ki):
    B, L = q.shape[:2]
    n_frames = L - N_META
    pos = jnp.arange(L, dtype=jnp.int32) - N_META
    topk = min(TOPK_MAX, n_frames // 4)
    slopes = alibi_slopes()
    o_meta = dsa_block(q[:, :N_META], qi[:, :N_META], wi[:, :N_META], pos[:N_META],
                       k, v, ki, pos, topk, slopes)
    nb = n_frames // Q_BLOCK

    def blk(a):
        return a[:, N_META:].reshape((B, nb, Q_BLOCK) + a.shape[2:]).swapaxes(0, 1)

    xs = (blk(q), blk(qi), blk(wi), pos[N_META:].reshape(nb, Q_BLOCK))
    o = lax.map(lambda t: dsa_block(t[0], t[1], t[2], t[3], k, v, ki, pos, topk, slopes), xs)
    o = o.swapaxes(0, 1).reshape(B, n_frames, A_Q)
    return jnp.concatenate([o_meta, o], axis=1)


def attend_sample(q, qi, wi, k, v, ki, cache_k, cache_v, cache_kidx):
    P = cache_k.shape[1]
    T = q.shape[1]
    K_all = jnp.concatenate([cache_k.astype(k.dtype), k], axis=1)
    V_all = jnp.concatenate([cache_v.astype(v.dtype), v], axis=1)
    KI_all = jnp.concatenate([cache_kidx.astype(ki.dtype), ki], axis=1)
    pos = jnp.arange(P + T, dtype=jnp.int32)
    topk = min(TOPK_MAX, (P + T) // 4)
    return dsa_block(q, qi, wi, pos[P:], K_all, V_all, KI_all, pos, topk, alibi_slopes())


def hgrn2_block(S, q, k, v, logf):
    T = q.shape[1]
    b = jnp.cumsum(logf, axis=1)
    causal = jnp.tril(jnp.ones((T, T), dtype=bool))
    diff = b[:, :, None] - b[:, None, :]
    decay = jnp.where(causal[None, :, :, None, None], jnp.exp(jnp.minimum(diff, 0.0)), 0.0)
    A = jnp.einsum('bthk,bshk,btshk->bhts', q, k, decay)
    o = jnp.einsum('bhts,bshv->bthv', A, v) + jnp.einsum('bthk,bhkv->bthv', q * jnp.exp(b), S)
    bT = b[:, -1]
    S_new = S * jnp.exp(bT)[..., None] + jnp.einsum('bshk,bshv->bhkv', k * jnp.exp(bT[:, None] - b), v)
    return S_new, o


def recur_prompt(q, k, v, logf):
    B, L = q.shape[:2]
    nb = L // REC_BLOCK

    def blk(a):
        return a.reshape((B, nb, REC_BLOCK) + a.shape[2:]).swapaxes(0, 1)

    S0 = jnp.zeros((B, B_HEADS, B_KEY_DIM, B_VAL_DIM), jnp.float32)

    def step(S, t):
        return hgrn2_block(S, t[0], t[1], t[2], t[3])

    S, o = lax.scan(step, S0, (blk(q), blk(k), blk(v), blk(logf)))
    return o.swapaxes(0, 1).reshape(B, L, B_HEADS, B_VAL_DIM), S


def mixer_output(h, oa, orec, gates, gn, wa, wb, wo, ln_g, ln_b):
    bg, ga, gb = gates
    B, T = h.shape[:2]
    on = orec * lax.rsqrt(jnp.mean(jnp.square(orec), -1, keepdims=True) + RMS_EPS) * gn.astype(jnp.float32)
    ob = on.reshape(B, T, B_WIDTH).astype(h.dtype) * jax.nn.silu(bg)
    mixed = (jax.nn.sigmoid(ga) * (oa @ wa) + jax.nn.sigmoid(gb) * (ob @ wb)) @ wo
    return layer_norm(ALPHA * h + mixed, ln_g, ln_b)


def setup_inputs(seed: int = 0) -> dict:
    key = jax.random.key(seed)
    ks = iter(jax.random.split(key, 64))

    def nrm(shape, scale=1.0):
        return jax.random.normal(next(ks), shape, jnp.float32) * scale

    din = D_MODEL ** -0.5
    in_scales = (din, din, din * BETA, din, din, din, din, din, din * BETA, din, din, din)
    w_in = jnp.concatenate([nrm((DEPTH, D_MODEL, n), s) for n, s in zip(IN_SPLIT, in_scales)], axis=-1)
    return {
        'x_prompt': nrm((BATCH, SEQ, D_MODEL)),
        'x_sample': nrm((DEC_BATCH, DEC_SEQ, D_MODEL)),
        'cache_k': nrm((DEPTH, DEC_BATCH, PAST_LEN, A_KV_HEADS, A_HEAD_DIM)),
        'cache_v': nrm((DEPTH, DEC_BATCH, PAST_LEN, A_KV_HEADS, A_HEAD_DIM), BETA),
        'cache_kidx': nrm((DEPTH, DEC_BATCH, PAST_LEN, IDX_DIM)),
        'state_hgrn': nrm((DEPTH, DEC_BATCH, B_HEADS, B_KEY_DIM, B_VAL_DIM), 0.5),
        'meta': nrm((N_META, D_MODEL)),
        'ln1_g': 1.0 + nrm((DEPTH, D_MODEL), 0.02),
        'ln1_b': nrm((DEPTH, D_MODEL), 0.02),
        'ffn1_wg': nrm((DEPTH, D_MODEL, D_FF), din),
        'ffn1_wu': nrm((DEPTH, D_MODEL, D_FF), din),
        'ffn1_wd': nrm((DEPTH, D_FF, D_MODEL), (D_FF ** -0.5) * BETA),
        'w_in': w_in,
        'lb_param': nrm((DEPTH + 1, B_KW), 0.1),
        'g_norm': 1.0 + nrm((DEPTH, B_HEADS, B_VAL_DIM), 0.02),
        'w_a_proj': nrm((DEPTH, A_Q, D_MODEL), (A_Q ** -0.5) * BETA),
        'w_b_proj': nrm((DEPTH, B_WIDTH, D_MODEL), (B_WIDTH ** -0.5) * BETA),
        'w_out': nrm((DEPTH, D_MODEL, D_MODEL), din * BETA),
        'ln2_g': 1.0 + nrm((DEPTH, D_MODEL), 0.02),
        'ln2_b': nrm((DEPTH, D_MODEL), 0.02),
        'ffn2_wg': nrm((DEPTH, D_MODEL, D_FF), din),
        'ffn2_wu': nrm((DEPTH, D_MODEL, D_FF), din),
        'ffn2_wd': nrm((DEPTH, D_FF, D_MODEL), (D_FF ** -0.5) * BETA),
        'ln3_g': 1.0 + nrm((DEPTH, D_MODEL), 0.02),
        'ln3_b': nrm((DEPTH, D_MODEL), 0.02),
    }


def reference(x_prompt, x_sample, cache_k, cache_v, cache_kidx, state_hgrn, meta,
              ln1_g, ln1_b, ffn1_wg, ffn1_wu, ffn1_wd, w_in, lb_param, g_norm,
              w_a_proj, w_b_proj, w_out, ln2_g, ln2_b, ffn2_wg, ffn2_wu, ffn2_wd, ln3_g, ln3_b):
    lb_all = jnp.cumsum(jax.nn.softmax(lb_param.astype(jnp.float32), axis=0), axis=0)
    Bp = x_prompt.shape[0]
    hp = jnp.concatenate([jnp.broadcast_to(meta.astype(x_prompt.dtype)[None], (Bp, N_META, meta.shape[-1])), x_prompt], axis=1)
    hs = x_sample
    kp_l, vp_l, kip_l, sp_l, ks_l, vs_l, kis_l, ss_l = [], [], [], [], [], [], [], []
    for l in range(DEPTH):
        hp = ffn_step(hp, ln1_g[l], ln1_b[l], ffn1_wg[l], ffn1_wu[l], ffn1_wd[l])
        (q, k, v, qi, ki, wi), rec, gates = mixer_inputs(hp, w_in[l], lb_all[l])
        oa = attend_prompt(q, qi, wi, k, v, ki)
        orec, Sp = recur_prompt(*rec)
        hp = mixer_output(hp, oa, orec, gates, g_norm[l], w_a_proj[l], w_b_proj[l], w_out[l], ln2_g[l], ln2_b[l])
        hp = ffn_step(hp, ln3_g[l], ln3_b[l], ffn2_wg[l], ffn2_wu[l], ffn2_wd[l])
        kp_l.append(k)
        vp_l.append(v)
        kip_l.append(ki)
        sp_l.append(Sp.astype(state_hgrn.dtype))
        hs = ffn_step(hs, ln1_g[l], ln1_b[l], ffn1_wg[l], ffn1_wu[l], ffn1_wd[l])
        (q, k, v, qi, ki, wi), rec, gates = mixer_inputs(hs, w_in[l], lb_all[l])
        oa = attend_sample(q, qi, wi, k, v, ki, cache_k[l], cache_v[l], cache_kidx[l])
        Ss, orec = hgrn2_block(state_hgrn[l].astype(jnp.float32), *rec)
        hs = mixer_output(hs, oa, orec, gates, g_norm[l], w_a_proj[l], w_b_proj[l], w_out[l], ln2_g[l], ln2_b[l])
        hs = ffn_step(hs, ln3_g[l], ln3_b[l], ffn2_wg[l], ffn2_wu[l], ffn2_wd[l])
        ks_l.append(k)
        vs_l.append(v)
        kis_l.append(ki)
        ss_l.append(Ss.astype(state_hgrn.dtype))
    y_prompt = hp[:, N_META:]
    return (y_prompt, hs, jnp.stack(kp_l), jnp.stack(vp_l), jnp.stack(kip_l), jnp.stack(sp_l),
            jnp.stack(ks_l), jnp.stack(vs_l), jnp.stack(kis_l), jnp.stack(ss_l))
```

```python
import functools

import jax
import jax.numpy as jnp
from jax import lax
from jax.experimental import pallas as pl
from jax.experimental.pallas import tpu as pltpu

F32 = jnp.float32
BF16 = jnp.bfloat16
I32 = jnp.int32

D_MODEL = 1024
D_FF = 2816
FF_CHUNK = 256
N_FF_CHUNKS = D_FF // FF_CHUNK
CHUNK = 64
N_META = 16
A_HEADS = 8
A_KV_HEADS = 2
A_GROUP = A_HEADS // A_KV_HEADS
A_HEAD_DIM = 64
A_Q = A_HEADS * A_HEAD_DIM
A_KV = A_KV_HEADS * A_HEAD_DIM
IDX_HEADS = 8
IDX_DIM = 64
IDX_Q = IDX_HEADS * IDX_DIM
TOPK_MAX = 256
B_HEADS = 4
B_KEY_DIM = 128
B_VAL_DIM = 128
B_KW = B_HEADS * B_KEY_DIM
B_WIDTH = B_HEADS * B_VAL_DIM
REC_BLOCK = 16
DEPTH = 1
ALPHA = (2.0 * DEPTH) ** 0.25
LN_EPS = 1e-5
RMS_EPS = 1e-6
NEG_INF = -1e30
INT_MIN = -(2 ** 31)
LANES = 128
VMEM_LIMIT = 56 * 1024 * 1024


def _resident(shape):
    n = len(shape)
    return pl.BlockSpec(shape, lambda *_: (0,) * n, pipeline_mode=pl.Buffered(1))


def _layer_norm(x, g, b):
    mu = jnp.mean(x, axis=-1, keepdims=True)
    xc = x - mu
    var = jnp.mean(xc * xc, axis=-1, keepdims=True)
    return xc * lax.rsqrt(var + LN_EPS) * g + b


def _swiglu_step(x, wg_ref, wu_ref, wd_ref):
    xb = x.astype(BF16)

    def body(c, acc):
        g = jnp.dot(xb, wg_ref[c], preferred_element_type=F32)
        u = jnp.dot(xb, wu_ref[c], preferred_element_type=F32)
        a = (g * jax.nn.sigmoid(g) * u).astype(BF16)
        return acc + jnp.dot(a, wd_ref[c], preferred_element_type=F32)

    acc = lax.fori_loop(0, N_FF_CHUNKS, body, jnp.zeros(x.shape, F32))
    return ALPHA * x + 0.5 * acc


_PROJ_OUT = (
    ("aq", A_Q, BF16), ("ak", A_KV, F32), ("av", A_KV, F32), ("iq", IDX_Q, BF16),
    ("ik", IDX_DIM, F32), ("iw", IDX_HEADS, F32),
    ("bq", B_KW, F32), ("bf", B_KW, F32), ("bi", B_WIDTH, F32),
    ("bg", B_WIDTH, F32), ("ga", D_MODEL, F32), ("gb", D_MODEL, F32),
)
_PROJ_SCALE = {"aq": A_HEAD_DIM ** -0.5, "iw": (IDX_HEADS ** -0.5) * (IDX_DIM ** -0.5)}


def _ffn_in_kernel(x_ref, g_ref, b_ref, wg_ref, wu_ref, wd_ref, *rest):
    n = len(_PROJ_OUT)
    w_refs, h_ref, out_refs = rest[:n], rest[n], rest[n + 1:]
    x = x_ref[...]
    h = _layer_norm(_swiglu_step(x, wg_ref, wu_ref, wd_ref), g_ref[...], b_ref[...])
    h_ref[...] = h
    hb = h.astype(BF16)
    for (name, _, dt), w_ref, o_ref in zip(_PROJ_OUT, w_refs, out_refs):
        y = jnp.dot(hb, w_ref[...], preferred_element_type=F32)
        if name in _PROJ_SCALE:
            y = y * _PROJ_SCALE[name]
        o_ref[...] = y.astype(dt)


def _ffn_in(x, ln_g, ln_b, wg, wu, wd, w_pieces, tm):
    m = x.shape[0]
    assert m % tm == 0
    row = lambda w: pl.BlockSpec((tm, w), lambda i: (i, 0))
    in_specs = [row(D_MODEL), _resident(ln_g.shape), _resident(ln_b.shape),
                _resident(wg.shape), _resident(wu.shape), _resident(wd.shape)]
    in_specs += [_resident(w.shape) for w in w_pieces]
    out_shape = [jax.ShapeDtypeStruct((m, D_MODEL), F32)]
    out_specs = [row(D_MODEL)]
    for _, width, dt in _PROJ_OUT:
        out_shape.append(jax.ShapeDtypeStruct((m, width), dt))
        out_specs.append(row(width))
    outs = pl.pallas_call(
        _ffn_in_kernel,
        grid=(m // tm,),
        in_specs=in_specs,
        out_specs=out_specs,
        out_shape=out_shape,
        compiler_params=pltpu.CompilerParams(
            dimension_semantics=("arbitrary",), vmem_limit_bytes=VMEM_LIMIT),
        name="ffn_in",
    )(x, ln_g, ln_b, wg, wu, wd, *w_pieces)
    res = {"h": outs[0]}
    for (name, _, _), o in zip(_PROJ_OUT, outs[1:]):
        res[name] = o
    return res


def _attend_kernel(nkb_ref, qpos_ref, lim_ref, iq_ref, wi_ref, q_ref, ki_ref, k_ref, v_ref,
                   o_ref, key_scr, m_scr, l_scr, acc_scr, *, topk, koff, kb_size, slopes):
    tq = q_ref.shape[0]
    nkb = nkb_ref[pl.program_id(1)]
    lim = lim_ref[...]
    qpos = qpos_ref[...]
    sub = kb_size // LANES

    def lane_ids(s0):
        return s0 + lax.broadcasted_iota(I32, (tq, kb_size), 1)

    def score_block(kb, carry):
        s0 = pl.multiple_of(kb * kb_size, kb_size)
        kib = ki_ref[pl.ds(s0, kb_size), :]
        acc = jnp.zeros((tq, kb_size), F32)
        for h in range(IDX_HEADS):
            lg = lax.dot_general(iq_ref[:, h * IDX_DIM:(h + 1) * IDX_DIM], kib,
                                 (((1,), (1,)), ((), ())), preferred_element_type=F32)
            acc = acc + jnp.maximum(lg, 0.0) * wi_ref[:, h:h + 1]
        bits = lax.bitcast_convert_type(acc, I32)
        key = bits ^ ((bits >> 31) & 0x7FFFFFFF)
        key_scr[:, pl.ds(s0, kb_size)] = jnp.where(lane_ids(s0) < lim, key, INT_MIN)
        return carry

    lax.fori_loop(0, nkb, score_block, 0)

    def count(pred):
        def body(kb, acc):
            s0 = pl.multiple_of(kb * kb_size, kb_size)
            hit = jnp.where(pred(key_scr[:, pl.ds(s0, kb_size)], lane_ids(s0)), 1.0, 0.0)
            for c in range(sub):
                acc = acc + hit[:, c * LANES:(c + 1) * LANES]
            return acc
        acc = lax.fori_loop(0, nkb, body, jnp.zeros((tq, LANES), F32))
        return jnp.sum(acc, axis=1, keepdims=True)

    def bit_step(i, t):
        cand = t + lax.shift_left(jnp.int32(1), 31 - i)
        cnt = count(lambda key, col: key >= cand)
        return jnp.where(cnt >= topk, cand, t)

    t = lax.fori_loop(0, 32, bit_step, jnp.full((tq, 1), INT_MIN, I32))
    thr = jnp.maximum(t, INT_MIN + 1)
    n_ge = count(lambda key, col: key >= thr)

    @pl.when(jnp.max(n_ge) > topk)
    def _():
        n_gt = count(lambda key, col: key > thr)
        need = topk - n_gt
        n_bits = (key_scr.shape[1] - 1).bit_length()

        def idx_step(i, m0):
            cand = m0 + lax.shift_left(jnp.int32(1), n_bits - 1 - i)
            g = count(lambda key, col: (key == thr) & (col < cand))
            return jnp.where(g < need, cand, m0)

        m0 = lax.fori_loop(0, n_bits, idx_step, jnp.zeros((tq, 1), I32))
        drop_row = n_ge > topk

        def demote(kb, carry):
            s0 = pl.multiple_of(kb * kb_size, kb_size)
            key = key_scr[:, pl.ds(s0, kb_size)]
            drop = drop_row & (key == thr) & (lane_ids(s0) > m0)
            key_scr[:, pl.ds(s0, kb_size)] = jnp.where(drop, INT_MIN, key)
            return carry

        lax.fori_loop(0, nkb, demote, 0)

    m_scr[...] = jnp.full(m_scr.shape, NEG_INF, F32)
    l_scr[...] = jnp.zeros(l_scr.shape, F32)
    acc_scr[...] = jnp.zeros(acc_scr.shape, F32)

    def attend_block(kb, carry):
        s0 = pl.multiple_of(kb * kb_size, kb_size)
        col = lane_ids(s0)
        bias = jnp.where(key_scr[:, pl.ds(s0, kb_size)] >= thr, 0.0, NEG_INF)
        dist = jnp.abs(qpos - (col - koff)).astype(F32)
        kblk = k_ref[pl.ds(s0, kb_size), :]
        vblk = v_ref[pl.ds(s0, kb_size), :]
        for h in range(A_HEADS):
            g = h // A_GROUP
            s = lax.dot_general(q_ref[:, h * A_HEAD_DIM:(h + 1) * A_HEAD_DIM],
                                kblk[:, g * A_HEAD_DIM:(g + 1) * A_HEAD_DIM],
                                (((1,), (1,)), ((), ())), preferred_element_type=F32)
            s = (s - slopes[h] * dist) + bias
            m_prev = m_scr[h]
            m_new = jnp.maximum(m_prev, jnp.max(s, axis=1, keepdims=True))
            p = jnp.exp(s - m_new[:, :1])
            alpha = jnp.exp(m_prev - m_new)
            l_scr[h] = alpha * l_scr[h] + jnp.sum(p, axis=1, keepdims=True)
            pv = jnp.dot(p.astype(BF16), vblk[:, g * A_HEAD_DIM:(g + 1) * A_HEAD_DIM],
                         preferred_element_type=F32)
            acc_scr[h] = alpha[:, :A_HEAD_DIM] * acc_scr[h] + pv
            m_scr[h] = m_new
        return carry

    lax.fori_loop(0, nkb, attend_block, 0)
    for h in range(A_HEADS):
        o = acc_scr[h] / l_scr[h][:, :A_HEAD_DIM]
        o_ref[:, h * A_HEAD_DIM:(h + 1) * A_HEAD_DIM] = o.astype(o_ref.dtype)


def _attend(iq, wi, q, ki, k, v, qpos, lim, *, tq, kb_size, topk, koff):
    bsz, nq, _ = q.shape
    s_pad = k.shape[1]
    assert nq % tq == 0 and s_pad % kb_size == 0
    nt = nq // tq
    nkb = (jnp.max(lim.reshape(nt, tq), axis=1) + kb_size - 1) // kb_size
    slopes = tuple(float(2.0 ** (-8.0 * (i + 1) / A_HEADS)) for i in range(A_HEADS))
    qrow = lambda w: pl.BlockSpec((None, tq, w), lambda b, j, n: (b, j, 0))
    krow = lambda w: pl.BlockSpec((None, s_pad, w), lambda b, j, n: (b, 0, 0))
    pos = pl.BlockSpec((tq, 1), lambda b, j, n: (j, 0))
    return pl.pallas_call(
        functools.partial(_attend_kernel, topk=topk, koff=koff, kb_size=kb_size, slopes=slopes),
        grid_spec=pltpu.PrefetchScalarGridSpec(
            num_scalar_prefetch=1,
            grid=(bsz, nt),
            in_specs=[pos, pos, qrow(IDX_Q), qrow(IDX_HEADS), qrow(A_Q),
                      krow(IDX_DIM), krow(A_KV), krow(A_KV)],
            out_specs=qrow(A_Q),
            scratch_shapes=[pltpu.VMEM((tq, s_pad), I32),
                            pltpu.VMEM((A_HEADS, tq, LANES), F32),
                            pltpu.VMEM((A_HEADS, tq, LANES), F32),
                            pltpu.VMEM((A_HEADS, tq, A_HEAD_DIM), F32)],
        ),
        out_shape=jax.ShapeDtypeStruct((bsz, nq, A_Q), BF16),
        compiler_params=pltpu.CompilerParams(
            dimension_semantics=("arbitrary", "arbitrary"), vmem_limit_bytes=VMEM_LIMIT),
        name="attend",
    )(nkb.astype(I32), qpos.reshape(nq, 1), lim.reshape(nq, 1), iq, wi, q, ki, k, v)


def _hgrn2_kernel(bq_ref, bf_ref, bi_ref, lb_ref, tri_ref, s0_ref, o_ref, sout_ref,
                  st_scr, q_scr, k_scr, b_scr):
    c = pl.program_id(1)
    ct = bq_ref.shape[0]

    @pl.when(c == 0)
    def _():
        for h in range(B_HEADS):
            st_scr[h] = s0_ref[h].T

    lb = lb_ref[...]
    fx = bf_ref[...]
    logf = jnp.log(lb + (1.0 - lb) * jax.nn.sigmoid(fx))
    k_scr[...] = (1.0 - lb) * jax.nn.sigmoid(-fx)
    bq = bq_ref[...]
    q_scr[...] = bq * jax.nn.sigmoid(bq)
    b_scr[...] = jnp.dot(tri_ref[...], logf, precision=lax.Precision.HIGHEST,
                         preferred_element_type=F32)
    row = lax.broadcasted_iota(I32, (REC_BLOCK, 1), 0)

    def block(blk, carry):
        rows = pl.ds(pl.multiple_of(blk * REC_BLOCK, REC_BLOCK), REC_BLOCK)
        for h in range(B_HEADS):
            ls = slice(h * B_KEY_DIM, (h + 1) * B_KEY_DIM)
            b = b_scr[rows, ls]
            qb = q_scr[rows, ls]
            kb = k_scr[rows, ls]
            vb = bi_ref[rows, ls]
            b_end = b[REC_BLOCK - 1:REC_BLOCK]
            st = st_scr[h]
            o_rows = []
            for t in range(REC_BLOCK):
                dec = jnp.exp(jnp.minimum(b[t:t + 1] - b, 0.0))
                a = jnp.sum((qb[t:t + 1] * dec) * kb, axis=1, keepdims=True)
                a = jnp.where(row <= t, a, 0.0)
                o_rows.append(jnp.sum(a * vb, axis=0, keepdims=True))
            o_intra = jnp.concatenate(o_rows, axis=0)
            o_inter = lax.dot_general((qb * jnp.exp(b)).astype(BF16), st.astype(BF16),
                                      (((1,), (1,)), ((), ())), preferred_element_type=F32)
            o_ref[rows, ls] = o_intra + o_inter
            upd = lax.dot_general(vb.astype(BF16), (kb * jnp.exp(b_end - b)).astype(BF16),
                                  (((0,), (0,)), ((), ())), preferred_element_type=F32)
            st_scr[h] = st * jnp.exp(b_end) + upd
        return carry

    lax.fori_loop(0, ct // REC_BLOCK, block, 0)

    @pl.when(c == pl.num_programs(1) - 1)
    def _():
        for h in range(B_HEADS):
            sout_ref[h] = st_scr[h].T


def _hgrn2(bq, bf, bi, lb, s0, ct):
    bsz, t, _ = bq.shape
    assert t % ct == 0 and ct % REC_BLOCK == 0
    r = jnp.arange(ct)
    tri = ((r[:, None] >= r[None, :]) & (r[:, None] // REC_BLOCK == r[None, :] // REC_BLOCK)).astype(F32)
    tok = pl.BlockSpec((None, ct, B_KW), lambda b, c: (b, c, 0))
    if s0.shape[0] == 1:
        s_in = pl.BlockSpec((None, B_HEADS, B_KEY_DIM, B_VAL_DIM), lambda b, c: (0, 0, 0, 0))
    else:
        s_in = pl.BlockSpec((None, B_HEADS, B_KEY_DIM, B_VAL_DIM), lambda b, c: (b, 0, 0, 0))
    s_out = pl.BlockSpec((None, B_HEADS, B_KEY_DIM, B_VAL_DIM), lambda b, c: (b, 0, 0, 0))
    return pl.pallas_call(
        _hgrn2_kernel,
        grid=(bsz, t // ct),
        in_specs=[tok, tok, tok, _resident(lb.shape), _resident(tri.shape), s_in],
        out_specs=[tok, s_out],
        out_shape=[jax.ShapeDtypeStruct((bsz, t, B_WIDTH), F32),
                   jax.ShapeDtypeStruct((bsz, B_HEADS, B_KEY_DIM, B_VAL_DIM), F32)],
        scratch_shapes=[pltpu.VMEM((B_HEADS, B_VAL_DIM, B_KEY_DIM), F32)]
        + [pltpu.VMEM((ct, B_KW), F32)] * 3,
        compiler_params=pltpu.CompilerParams(
            dimension_semantics=("arbitrary", "arbitrary"), vmem_limit_bytes=VMEM_LIMIT),
        name="hgrn2",
    )(bq, bf, bi, lb, tri, s0)


def _mix_out_kernel(h_ref, oa_ref, orec_ref, bg_ref, ga_ref, gb_ref, gn_ref, wa_ref, wb_ref, wo_ref,
                    g2_ref, b2_ref, wg_ref, wu_ref, wd_ref, g3_ref, b3_ref, y_ref):
    orec = orec_ref[...]
    gn = gn_ref[...]
    parts = []
    for h in range(B_HEADS):
        ls = slice(h * B_VAL_DIM, (h + 1) * B_VAL_DIM)
        oh = orec[:, ls]
        ms = jnp.mean(oh * oh, axis=-1, keepdims=True)
        parts.append(oh * lax.rsqrt(ms + RMS_EPS) * gn[:, ls])
    on = jnp.concatenate(parts, axis=-1)
    bg = bg_ref[...]
    ob = on * (bg * jax.nn.sigmoid(bg))
    pa = jnp.dot(oa_ref[...], wa_ref[...], preferred_element_type=F32)
    pb = jnp.dot(ob.astype(BF16), wb_ref[...], preferred_element_type=F32)
    mixed = jax.nn.sigmoid(ga_ref[...]) * pa + jax.nn.sigmoid(gb_ref[...]) * pb
    mixed = jnp.dot(mixed.astype(BF16), wo_ref[...], preferred_element_type=F32)
    h2 = _layer_norm(ALPHA * h_ref[...] + mixed, g2_ref[...], b2_ref[...])
    y_ref[...] = _layer_norm(_swiglu_step(h2, wg_ref, wu_ref, wd_ref), g3_ref[...], b3_ref[...])


def _mix_out(h, oa, orec, bg, ga, gb, gn, wa, wb, wo, g2, b2, wg, wu, wd, g3, b3, tm):
    m = h.shape[0]
    assert m % tm == 0
    row = lambda w: pl.BlockSpec((tm, w), lambda i: (i, 0))
    weights = (gn, wa, wb, wo, g2, b2, wg, wu, wd, g3, b3)
    return pl.pallas_call(
        _mix_out_kernel,
        grid=(m // tm,),
        in_specs=[row(D_MODEL), row(A_Q), row(B_WIDTH), row(B_WIDTH), row(D_MODEL), row(D_MODEL)]
        + [_resident(w.shape) for w in weights],
        out_specs=row(D_MODEL),
        out_shape=jax.ShapeDtypeStruct((m, D_MODEL), F32),
        compiler_params=pltpu.CompilerParams(
            dimension_semantics=("arbitrary",), vmem_limit_bytes=VMEM_LIMIT),
        name="mix_out",
    )(h, oa, orec, bg, ga, gb, *weights)


def _ffn_weights(wg, wu, wd):
    wg = wg.astype(BF16).reshape(D_MODEL, N_FF_CHUNKS, FF_CHUNK).transpose(1, 0, 2)
    wu = wu.astype(BF16).reshape(D_MODEL, N_FF_CHUNKS, FF_CHUNK).transpose(1, 0, 2)
    wd = wd.astype(BF16).reshape(N_FF_CHUNKS, FF_CHUNK, D_MODEL)
    return wg, wu, wd


def _pad_rows(a, n):
    return jnp.pad(a, ((0, 0), (0, n - a.shape[1]), (0, 0)))


def _token_tile(m):
    for tm in (256, 128, 64, 32, 16, 8):
        if m % tm == 0:
            return tm
    raise ValueError(m)


def kernel(x_prompt, x_sample, cache_k, cache_v, cache_kidx, state_hgrn, meta, ln1_g, ln1_b, ffn1_wg, ffn1_wu, ffn1_wd, w_in, lb_param, g_norm, w_a_proj, w_b_proj, w_out, ln2_g, ln2_b, ffn2_wg, ffn2_wu, ffn2_wd, ln3_g, ln3_b):
    assert ln1_g.shape[0] == DEPTH
    bp, seq, _ = x_prompt.shape
    bs, dseq, _ = x_sample.shape
    past = cache_k.shape[2]
    l = 0
    vec = lambda a: a.reshape(1, -1).astype(F32)

    lb_all = jnp.cumsum(jax.nn.softmax(lb_param.astype(F32), axis=0), axis=0)
    lb = lb_all[l].reshape(1, B_KW)
    f1 = _ffn_weights(ffn1_wg[l], ffn1_wu[l], ffn1_wd[l])
    f2 = _ffn_weights(ffn2_wg[l], ffn2_wu[l], ffn2_wd[l])
    w_pieces, off = [], 0
    for _, width, _ in _PROJ_OUT:
        w_pieces.append(w_in[l][:, off:off + width].astype(BF16))
        off += width
    assert off == w_in.shape[-1]

    def stage1(x):
        return _ffn_in(x, vec(ln1_g[l]), vec(ln1_b[l]), *f1, w_pieces, _token_tile(x.shape[0]))

    def stage4(p, oa, orec):
        m = p["h"].shape[0]
        return _mix_out(p["h"], oa.reshape(m, A_Q), orec.reshape(m, B_WIDTH), p["bg"], p["ga"], p["gb"],
                        vec(g_norm[l]), w_a_proj[l].astype(BF16), w_b_proj[l].astype(BF16),
                        w_out[l].astype(BF16), vec(ln2_g[l]), vec(ln2_b[l]), *f2,
                        vec(ln3_g[l]), vec(ln3_b[l]), _token_tile(m))

    pm = stage1(meta.astype(F32))
    pp = stage1(x_prompt.reshape(bp * seq, D_MODEL))
    ps = stage1(x_sample.reshape(bs * dseq, D_MODEL))
    per_p = lambda a: a.reshape(bp, seq, a.shape[-1])
    per_s = lambda a: a.reshape(bs, dseq, a.shape[-1])

    def with_meta(name):
        rows = jnp.broadcast_to(pm[name][None], (bp, N_META, pm[name].shape[-1]))
        return jnp.concatenate([rows, per_p(pp[name])], axis=1)

    k_p, v_p, ki_p = with_meta("ak"), with_meta("av"), with_meta("ik")

    tq_p = 128
    kb_p = 512
    s_p = -(-(N_META + seq) // kb_p) * kb_p
    qpos = jnp.arange(seq, dtype=I32)
    lim = N_META + CHUNK * (qpos // CHUNK + 1)
    oa_p = _attend(per_p(pp["iq"]), per_p(pp["iw"]), per_p(pp["aq"]),
                   _pad_rows(ki_p.astype(BF16), s_p), _pad_rows(k_p.astype(BF16), s_p),
                   _pad_rows(v_p.astype(BF16), s_p), qpos, lim,
                   tq=tq_p, kb_size=kb_p, topk=min(TOPK_MAX, seq // 4), koff=N_META)

    n_all = past + dseq
    kb_s = 384
    s_s = -(-n_all // kb_s) * kb_s
    k_all = jnp.concatenate([cache_k[l].reshape(bs, past, A_KV).astype(BF16), per_s(ps["ak"]).astype(BF16)], axis=1)
    v_all = jnp.concatenate([cache_v[l].reshape(bs, past, A_KV).astype(BF16), per_s(ps["av"]).astype(BF16)], axis=1)
    ki_all = jnp.concatenate([cache_kidx[l].astype(BF16), per_s(ps["ik"]).astype(BF16)], axis=1)
    oa_s = _attend(per_s(ps["iq"]), per_s(ps["iw"]), per_s(ps["aq"]),
                   _pad_rows(ki_all, s_s), _pad_rows(k_all, s_s), _pad_rows(v_all, s_s),
                   past + jnp.arange(dseq, dtype=I32), jnp.full((dseq,), n_all, I32),
                   tq=dseq, kb_size=kb_s, topk=min(TOPK_MAX, n_all // 4), koff=0)

    zero_state = jnp.zeros((1, B_HEADS, B_KEY_DIM, B_VAL_DIM), F32)
    _, st_m = _hgrn2(pm["bq"][None], pm["bf"][None], pm["bi"][None], lb, zero_state, N_META)
    orec_p, st_p = _hgrn2(per_p(pp["bq"]), per_p(pp["bf"]), per_p(pp["bi"]), lb, st_m, 128)
    orec_s, st_s = _hgrn2(per_s(ps["bq"]), per_s(ps["bf"]), per_s(ps["bi"]), lb,
                          state_hgrn[l].astype(F32), dseq)

    y_p = stage4(pp, oa_p, orec_p).reshape(bp, seq, D_MODEL)
    y_s = stage4(ps, oa_s, orec_s).reshape(bs, dseq, D_MODEL)

    kv5 = lambda a: a.reshape(1, a.shape[0], a.shape[1], A_KV_HEADS, A_HEAD_DIM)
    return (y_p, y_s, kv5(k_p), kv5(v_p), ki_p[None], st_p[None].astype(state_hgrn.dtype),
            kv5(per_s(ps["ak"])), kv5(per_s(ps["av"])), per_s(ps["ik"])[None],
            st_s[None].astype(state_hgrn.dtype))
```

```python
import functools

import jax
import jax.numpy as jnp
from jax import lax
from jax.experimental import pallas as pl
from jax.experimental.pallas import tpu as pltpu

F32 = jnp.float32
BF16 = jnp.bfloat16
I32 = jnp.int32

D_MODEL = 1024
D_FF = 2816
FF_CHUNK = 256
N_FF_CHUNKS = D_FF // FF_CHUNK
CHUNK = 64
N_META = 16
A_HEADS = 8
A_KV_HEADS = 2
A_GROUP = A_HEADS // A_KV_HEADS
A_HEAD_DIM = 64
A_Q = A_HEADS * A_HEAD_DIM
A_KV = A_KV_HEADS * A_HEAD_DIM
IDX_HEADS = 8
IDX_DIM = 64
IDX_Q = IDX_HEADS * IDX_DIM
TOPK_MAX = 256
B_HEADS = 4
B_KEY_DIM = 128
B_VAL_DIM = 128
B_KW = B_HEADS * B_KEY_DIM
B_WIDTH = B_HEADS * B_VAL_DIM
REC_BLOCK = 16
DEPTH = 1
ALPHA = (2.0 * DEPTH) ** 0.25
LN_EPS = 1e-5
RMS_EPS = 1e-6
NEG_INF = -1e30
INT_MIN = -(2 ** 31)
LANES = 128
VMEM_LIMIT = 56 * 1024 * 1024


def _resident(shape):
    n = len(shape)
    return pl.BlockSpec(shape, lambda *_: (0,) * n, pipeline_mode=pl.Buffered(1))


def _layer_norm(x, g, b):
    mu = jnp.mean(x, axis=-1, keepdims=True)
    xc = x - mu
    var = jnp.mean(xc * xc, axis=-1, keepdims=True)
    return xc * lax.rsqrt(var + LN_EPS) * g + b


def _swiglu_step(x, wg_ref, wu_ref, wd_ref):
    xb = x.astype(BF16)

    def body(c, acc):
        g = jnp.dot(xb, wg_ref[c], preferred_element_type=F32)
        u = jnp.dot(xb, wu_ref[c], preferred_element_type=F32)
        a = (g * jax.nn.sigmoid(g) * u).astype(BF16)
        return acc + jnp.dot(a, wd_ref[c], preferred_element_type=F32)

    acc = lax.fori_loop(0, N_FF_CHUNKS, body, jnp.zeros(x.shape, F32))
    return ALPHA * x + 0.5 * acc


_PROJ_OUT = (
    ("aq", A_Q, BF16), ("ak", A_KV, F32), ("av", A_KV, F32), ("iq", IDX_Q, BF16),
    ("ik", IDX_DIM, F32), ("iw", IDX_HEADS, F32),
    ("bq", B_KW, F32), ("bf", B_KW, F32), ("bi", B_WIDTH, F32),
    ("bg", B_WIDTH, F32), ("ga", D_MODEL, F32), ("gb", D_MODEL, F32),
)
_PROJ_SCALE = {"aq": A_HEAD_DIM ** -0.5, "iw": (IDX_HEADS ** -0.5) * (IDX_DIM ** -0.5)}


def _ffn_in_kernel(x_ref, g_ref, b_ref, wg_ref, wu_ref, wd_ref, *rest):
    n = len(_PROJ_OUT)
    w_refs, h_ref, out_refs = rest[:n], rest[n], rest[n + 1:]
    x = x_ref[...]
    h = _layer_norm(_swiglu_step(x, wg_ref, wu_ref, wd_ref), g_ref[...], b_ref[...])
    h_ref[...] = h
    hb = h.astype(BF16)
    for (name, _, dt), w_ref, o_ref in zip(_PROJ_OUT, w_refs, out_refs):
        y = jnp.dot(hb, w_ref[...], preferred_element_type=F32)
        if name in _PROJ_SCALE:
            y = y * _PROJ_SCALE[name]
        o_ref[...] = y.astype(dt)


def _ffn_in(x, ln_g, ln_b, wg, wu, wd, w_pieces, tm):
    m = x.shape[0]
    assert m % tm == 0
    row = lambda w: pl.BlockSpec((tm, w), lambda i: (i, 0))
    in_specs = [row(D_MODEL), _resident(ln_g.shape), _resident(ln_b.shape),
                _resident(wg.shape), _resident(wu.shape), _resident(wd.shape)]
    in_specs += [_resident(w.shape) for w in w_pieces]
    out_shape = [jax.ShapeDtypeStruct((m, D_MODEL), F32)]
    out_specs = [row(D_MODEL)]
    for _, width, dt in _PROJ_OUT:
        out_shape.append(jax.ShapeDtypeStruct((m, width), dt))
        out_specs.append(row(width))
    outs = pl.pallas_call(
        _ffn_in_kernel,
        grid=(m // tm,),
        in_specs=in_specs,
        out_specs=out_specs,
        out_shape=out_shape,
        compiler_params=pltpu.CompilerParams(
            dimension_semantics=("arbitrary",), vmem_limit_bytes=VMEM_LIMIT),
        name="ffn_in",
    )(x, ln_g, ln_b, wg, wu, wd, *w_pieces)
    res = {"h": outs[0]}
    for (name, _, _), o in zip(_PROJ_OUT, outs[1:]):
        res[name] = o
    return res


def _attend_kernel(nkb_ref, qpos_ref, lim_ref, iq_ref, wi_ref, q_ref, ki_ref, k_ref, v_ref,
                   o_ref, key_scr, t_scr, *, topk, koff, kb_size, slopes):
    tq = q_ref.shape[0]
    nkb = nkb_ref[pl.program_id(1)]
    lim = lim_ref[...]
    qpos = qpos_ref[...]
    sub = kb_size // LANES

    def lane_ids(s0):
        return s0 + lax.broadcasted_iota(I32, (tq, kb_size), 1)

    def score_block(kb, carry):
        s0 = pl.multiple_of(kb * kb_size, kb_size)
        kib = ki_ref[pl.ds(s0, kb_size), :]
        acc = jnp.zeros((tq, kb_size), F32)
        for h in range(IDX_HEADS):
            lg = lax.dot_general(iq_ref[:, h * IDX_DIM:(h + 1) * IDX_DIM], kib,
                                 (((1,), (1,)), ((), ())), preferred_element_type=F32)
            acc = acc + jnp.maximum(lg, 0.0) * wi_ref[:, h:h + 1]
        bits = lax.bitcast_convert_type(acc, I32)
        key = bits ^ ((bits >> 31) & 0x7FFFFFFF)
        key_scr[:, pl.ds(s0, kb_size)] = jnp.where(lane_ids(s0) < lim, key, INT_MIN)
        return carry

    lax.fori_loop(0, nkb, score_block, 0)

    def bcast(x):
        return jnp.broadcast_to(x, (tq, LANES))

    def count(pred):
        def body(kb, acc):
            for c in range(sub):
                off = pl.multiple_of(kb * kb_size + c * LANES, LANES)
                col = off + lax.broadcasted_iota(I32, (tq, LANES), 1)
                acc = acc + jnp.where(pred(key_scr[:, pl.ds(off, LANES)], col), 1.0, 0.0)
            return acc
        acc = lax.fori_loop(0, nkb, body, jnp.zeros((tq, LANES), F32))
        return jnp.sum(acc, axis=1, keepdims=True)

    def bit_step(state):
        i, t, settled, _ = state
        stop = jnp.min(settled)
        cand = t + lax.shift_left(jnp.int32(1), 31 - i)
        cand_b = bcast(cand)
        cnt = count(lambda key, col: key >= cand_b)
        take = (cnt >= topk) & (settled == 0.0)
        settled = jnp.where(take & (cnt == topk), 1.0, settled)
        return i + 1, jnp.where(take, cand, t), settled, stop

    _, t, _, _ = lax.while_loop(
        lambda state: (state[0] < 32) & (state[3] == 0.0), bit_step,
        (jnp.int32(0), jnp.full((tq, 1), INT_MIN, I32), (lim <= topk).astype(F32), jnp.float32(0.0)))
    thr = jnp.maximum(t, INT_MIN + 1)
    thr_b = bcast(thr)
    n_ge = count(lambda key, col: key >= thr_b)

    @pl.when(jnp.max(n_ge) > topk)
    def _():
        n_gt = count(lambda key, col: key > thr_b)
        need = topk - n_gt
        n_bits = (key_scr.shape[1] - 1).bit_length()

        def idx_step(i, m0):
            cand = m0 + lax.shift_left(jnp.int32(1), n_bits - 1 - i)
            cand_b = bcast(cand)
            g = count(lambda key, col: (key == thr_b) & (col < cand_b))
            return jnp.where(g < need, cand, m0)

        m0 = lax.fori_loop(0, n_bits, idx_step, jnp.zeros((tq, 1), I32))
        drop_row = n_ge > topk

        def demote(kb, carry):
            s0 = pl.multiple_of(kb * kb_size, kb_size)
            key = key_scr[:, pl.ds(s0, kb_size)]
            drop = drop_row & (key == thr) & (lane_ids(s0) > m0)
            key_scr[:, pl.ds(s0, kb_size)] = jnp.where(drop, INT_MIN, key)
            return carry

        lax.fori_loop(0, nkb, demote, 0)

    def attend_block(kb, carry):
        ms, ls, accs = carry
        s0 = pl.multiple_of(kb * kb_size, kb_size)
        col = lane_ids(s0)
        bias = jnp.where(key_scr[:, pl.ds(s0, kb_size)] >= thr, 0.0, NEG_INF)
        dist = jnp.abs(qpos - (col - koff)).astype(F32)
        kblk = k_ref[pl.ds(s0, kb_size), :]
        vblk = v_ref[pl.ds(s0, kb_size), :]
        new_m, new_l, new_acc = [], [], []
        for h in range(A_HEADS):
            g = h // A_GROUP
            s = lax.dot_general(q_ref[:, h * A_HEAD_DIM:(h + 1) * A_HEAD_DIM],
                                kblk[:, g * A_HEAD_DIM:(g + 1) * A_HEAD_DIM],
                                (((1,), (1,)), ((), ())), preferred_element_type=F32)
            s = (s - slopes[h] * dist) + bias
            t_scr[h] = s
            new_m.append(jnp.maximum(ms[h], jnp.max(s, axis=1, keepdims=True)))
        for h in range(A_HEADS):
            g = h // A_GROUP
            p = jnp.exp(t_scr[h] - new_m[h])
            alpha = jnp.exp(ms[h] - new_m[h])
            new_l.append(alpha * ls[h] + jnp.sum(p, axis=1, keepdims=True))
            pv = jnp.dot(p.astype(BF16), vblk[:, g * A_HEAD_DIM:(g + 1) * A_HEAD_DIM],
                         preferred_element_type=F32)
            new_acc.append(alpha * accs[h] + pv)
        return tuple(new_m), tuple(new_l), tuple(new_acc)

    init = (tuple(jnp.full((tq, 1), NEG_INF, F32) for _ in range(A_HEADS)),
            tuple(jnp.zeros((tq, 1), F32) for _ in range(A_HEADS)),
            tuple(jnp.zeros((tq, A_HEAD_DIM), F32) for _ in range(A_HEADS)))
    _, ls, accs = lax.fori_loop(0, nkb, attend_block, init)
    for h in range(A_HEADS):
        o_ref[:, h * A_HEAD_DIM:(h + 1) * A_HEAD_DIM] = (accs[h] / ls[h]).astype(o_ref.dtype)


def _attend(iq, wi, q, ki, k, v, qpos, lim, *, tq, kb_size, topk, koff):
    bsz, nq, _ = q.shape
    s_pad = k.shape[1]
    assert nq % tq == 0 and s_pad % kb_size == 0
    nt = nq // tq
    nkb = (jnp.max(lim.reshape(nt, tq), axis=1) + kb_size - 1) // kb_size
    slopes = tuple(float(2.0 ** (-8.0 * (i + 1) / A_HEADS)) for i in range(A_HEADS))
    qrow = lambda w: pl.BlockSpec((None, tq, w), lambda b, j, n: (b, j, 0))
    krow = lambda w: pl.BlockSpec((None, s_pad, w), lambda b, j, n: (b, 0, 0))
    pos = pl.BlockSpec((tq, 1), lambda b, j, n: (j, 0))
    return pl.pallas_call(
        functools.partial(_attend_kernel, topk=topk, koff=koff, kb_size=kb_size, slopes=slopes),
        grid_spec=pltpu.PrefetchScalarGridSpec(
            num_scalar_prefetch=1,
            grid=(bsz, nt),
            in_specs=[pos, pos, qrow(IDX_Q), qrow(IDX_HEADS), qrow(A_Q),
                      krow(IDX_DIM), krow(A_KV), krow(A_KV)],
            out_specs=qrow(A_Q),
            scratch_shapes=[pltpu.VMEM((tq, s_pad), I32),
                            pltpu.VMEM((A_HEADS, tq, kb_size), F32)],
        ),
        out_shape=jax.ShapeDtypeStruct((bsz, nq, A_Q), BF16),
        compiler_params=pltpu.CompilerParams(
            dimension_semantics=("arbitrary", "arbitrary"), vmem_limit_bytes=VMEM_LIMIT),
        name="attend",
    )(nkb.astype(I32), qpos.reshape(nq, 1), lim.reshape(nq, 1), iq, wi, q, ki, k, v)


def _hgrn2_kernel(bq_ref, bf_ref, bi_ref, lb_ref, tri_ref, s0_ref, o_ref, sout_ref,
                  st_scr, q_scr, k_scr, b_scr):
    c = pl.program_id(1)
    ct = bq_ref.shape[0]

    @pl.when(c == 0)
    def _():
        for h in range(B_HEADS):
            st_scr[h] = s0_ref[h].T

    lb = lb_ref[...]
    fx = bf_ref[...]
    logf = jnp.log(lb + (1.0 - lb) * jax.nn.sigmoid(fx))
    k_scr[...] = (1.0 - lb) * jax.nn.sigmoid(-fx)
    bq = bq_ref[...]
    q_scr[...] = bq * jax.nn.sigmoid(bq)
    b_scr[...] = jnp.dot(tri_ref[...], logf, precision=lax.Precision.HIGHEST,
                         preferred_element_type=F32)
    row = lax.broadcasted_iota(I32, (REC_BLOCK, 1), 0)

    def block(blk, carry):
        rows = pl.ds(pl.multiple_of(blk * REC_BLOCK, REC_BLOCK), REC_BLOCK)
        for h in range(B_HEADS):
            ls = slice(h * B_KEY_DIM, (h + 1) * B_KEY_DIM)
            b = b_scr[rows, ls]
            qb = q_scr[rows, ls]
            kb = k_scr[rows, ls]
            vb = bi_ref[rows, ls]
            b_end = b[REC_BLOCK - 1:REC_BLOCK]
            st = st_scr[h]
            o_rows = []
            for t in range(REC_BLOCK):
                dec = jnp.exp(jnp.minimum(b[t:t + 1] - b, 0.0))
                a = jnp.sum((qb[t:t + 1] * dec) * kb, axis=1, keepdims=True)
                a = jnp.where(row <= t, a, 0.0)
                o_rows.append(jnp.sum(a * vb, axis=0, keepdims=True))
            o_intra = jnp.concatenate(o_rows, axis=0)
            o_inter = lax.dot_general((qb * jnp.exp(b)).astype(BF16), st.astype(BF16),
                                      (((1,), (1,)), ((), ())), preferred_element_type=F32)
            o_ref[rows, ls] = o_intra + o_inter
            upd = lax.dot_general(vb.astype(BF16), (kb * jnp.exp(b_end - b)).astype(BF16),
                                  (((0,), (0,)), ((), ())), preferred_element_type=F32)
            st_scr[h] = st * jnp.exp(b_end) + upd
        return carry

    lax.fori_loop(0, ct // REC_BLOCK, block, 0)

    @pl.when(c == pl.num_programs(1) - 1)
    def _():
        for h in range(B_HEADS):
            sout_ref[h] = st_scr[h].T


def _hgrn2(bq, bf, bi, lb, s0, ct):
    bsz, t, _ = bq.shape
    assert t % ct == 0 and ct % REC_BLOCK == 0
    r = jnp.arange(ct)
    tri = ((r[:, None] >= r[None, :]) & (r[:, None] // REC_BLOCK == r[None, :] // REC_BLOCK)).astype(F32)
    tok = pl.BlockSpec((None, ct, B_KW), lambda b, c: (b, c, 0))
    if s0.shape[0] == 1:
        s_in = pl.BlockSpec((None, B_HEADS, B_KEY_DIM, B_VAL_DIM), lambda b, c: (0, 0, 0, 0))
    else:
        s_in = pl.BlockSpec((None, B_HEADS, B_KEY_DIM, B_VAL_DIM), lambda b, c: (b, 0, 0, 0))
    s_out = pl.BlockSpec((None, B_HEADS, B_KEY_DIM, B_VAL_DIM), lambda b, c: (b, 0, 0, 0))
    return pl.pallas_call(
        _hgrn2_kernel,
        grid=(bsz, t // ct),
        in_specs=[tok, tok, tok, _resident(lb.shape), _resident(tri.shape), s_in],
        out_specs=[tok, s_out],
        out_shape=[jax.ShapeDtypeStruct((bsz, t, B_WIDTH), F32),
                   jax.ShapeDtypeStruct((bsz, B_HEADS, B_KEY_DIM, B_VAL_DIM), F32)],
        scratch_shapes=[pltpu.VMEM((B_HEADS, B_VAL_DIM, B_KEY_DIM), F32)]
        + [pltpu.VMEM((ct, B_KW), F32)] * 3,
        compiler_params=pltpu.CompilerParams(
            dimension_semantics=("arbitrary", "arbitrary"), vmem_limit_bytes=VMEM_LIMIT),
        name="hgrn2",
    )(bq, bf, bi, lb, tri, s0)


def _mix_out_kernel(h_ref, oa_ref, orec_ref, bg_ref, ga_ref, gb_ref, gn_ref, wa_ref, wb_ref, wo_ref,
                    g2_ref, b2_ref, wg_ref, wu_ref, wd_ref, g3_ref, b3_ref, y_ref):
    orec = orec_ref[...]
    gn = gn_ref[...]
    parts = []
    for h in range(B_HEADS):
        ls = slice(h * B_VAL_DIM, (h + 1) * B_VAL_DIM)
        oh = orec[:, ls]
        ms = jnp.mean(oh * oh, axis=-1, keepdims=True)
        parts.append(oh * lax.rsqrt(ms + RMS_EPS) * gn[:, ls])
    on = jnp.concatenate(parts, axis=-1)
    bg = bg_ref[...]
    ob = on * (bg * jax.nn.sigmoid(bg))
    pa = jnp.dot(oa_ref[...], wa_ref[...], preferred_element_type=F32)
    pb = jnp.dot(ob.astype(BF16), wb_ref[...], preferred_element_type=F32)
    mixed = jax.nn.sigmoid(ga_ref[...]) * pa + jax.nn.sigmoid(gb_ref[...]) * pb
    mixed = jnp.dot(mixed.astype(BF16), wo_ref[...], preferred_element_type=F32)
    h2 = _layer_norm(ALPHA * h_ref[...] + mixed, g2_ref[...], b2_ref[...])
    y_ref[...] = _layer_norm(_swiglu_step(h2, wg_ref, wu_ref, wd_ref), g3_ref[...], b3_ref[...])


def _mix_out(h, oa, orec, bg, ga, gb, gn, wa, wb, wo, g2, b2, wg, wu, wd, g3, b3, tm):
    m = h.shape[0]
    assert m % tm == 0
    row = lambda w: pl.BlockSpec((tm, w), lambda i: (i, 0))
    weights = (gn, wa, wb, wo, g2, b2, wg, wu, wd, g3, b3)
    return pl.pallas_call(
        _mix_out_kernel,
        grid=(m // tm,),
        in_specs=[row(D_MODEL), row(A_Q), row(B_WIDTH), row(B_WIDTH), row(D_MODEL), row(D_MODEL)]
        + [_resident(w.shape) for w in weights],
        out_specs=row(D_MODEL),
        out_shape=jax.ShapeDtypeStruct((m, D_MODEL), F32),
        compiler_params=pltpu.CompilerParams(
            dimension_semantics=("arbitrary",), vmem_limit_bytes=VMEM_LIMIT),
        name="mix_out",
    )(h, oa, orec, bg, ga, gb, *weights)


def _ffn_weights(wg, wu, wd):
    wg = wg.astype(BF16).reshape(D_MODEL, N_FF_CHUNKS, FF_CHUNK).transpose(1, 0, 2)
    wu = wu.astype(BF16).reshape(D_MODEL, N_FF_CHUNKS, FF_CHUNK).transpose(1, 0, 2)
    wd = wd.astype(BF16).reshape(N_FF_CHUNKS, FF_CHUNK, D_MODEL)
    return wg, wu, wd


def _pad_rows(a, n):
    return jnp.pad(a, ((0, 0), (0, n - a.shape[1]), (0, 0)))


def _token_tile(m):
    for tm in (256, 128, 64, 32, 16, 8):
        if m % tm == 0:
            return tm
    raise ValueError(m)


def kernel(x_prompt, x_sample, cache_k, cache_v, cache_kidx, state_hgrn, meta, ln1_g, ln1_b, ffn1_wg, ffn1_wu, ffn1_wd, w_in, lb_param, g_norm, w_a_proj, w_b_proj, w_out, ln2_g, ln2_b, ffn2_wg, ffn2_wu, ffn2_wd, ln3_g, ln3_b):
    assert ln1_g.shape[0] == DEPTH
    bp, seq, _ = x_prompt.shape
    bs, dseq, _ = x_sample.shape
    past = cache_k.shape[2]
    l = 0
    vec = lambda a: a.reshape(1, -1).astype(F32)

    lb_all = jnp.cumsum(jax.nn.softmax(lb_param.astype(F32), axis=0), axis=0)
    lb = lb_all[l].reshape(1, B_KW)
    f1 = _ffn_weights(ffn1_wg[l], ffn1_wu[l], ffn1_wd[l])
    f2 = _ffn_weights(ffn2_wg[l], ffn2_wu[l], ffn2_wd[l])
    w_pieces, off = [], 0
    for _, width, _ in _PROJ_OUT:
        w_pieces.append(w_in[l][:, off:off + width].astype(BF16))
        off += width
    assert off == w_in.shape[-1]

    def stage1(x):
        return _ffn_in(x, vec(ln1_g[l]), vec(ln1_b[l]), *f1, w_pieces, _token_tile(x.shape[0]))

    def stage4(p, oa, orec):
        m = p["h"].shape[0]
        return _mix_out(p["h"], oa.reshape(m, A_Q), orec.reshape(m, B_WIDTH), p["bg"], p["ga"], p["gb"],
                        vec(g_norm[l]), w_a_proj[l].astype(BF16), w_b_proj[l].astype(BF16),
                        w_out[l].astype(BF16), vec(ln2_g[l]), vec(ln2_b[l]), *f2,
                        vec(ln3_g[l]), vec(ln3_b[l]), _token_tile(m))

    pm = stage1(meta.astype(F32))
    pp = stage1(x_prompt.reshape(bp * seq, D_MODEL))
    ps = stage1(x_sample.reshape(bs * dseq, D_MODEL))
    per_p = lambda a: a.reshape(bp, seq, a.shape[-1])
    per_s = lambda a: a.reshape(bs, dseq, a.shape[-1])

    def with_meta(name):
        rows = jnp.broadcast_to(pm[name][None], (bp, N_META, pm[name].shape[-1]))
        return jnp.concatenate([rows, per_p(pp[name])], axis=1)

    k_p, v_p, ki_p = with_meta("ak"), with_meta("av"), with_meta("ik")

    tq_p = 128
    kb_p = 512
    s_p = -(-(N_META + seq) // kb_p) * kb_p
    qpos = jnp.arange(seq, dtype=I32)
    lim = N_META + CHUNK * (qpos // CHUNK + 1)
    oa_p = _attend(per_p(pp["iq"]), per_p(pp["iw"]), per_p(pp["aq"]),
                   _pad_rows(ki_p.astype(BF16), s_p), _pad_rows(k_p.astype(BF16), s_p),
                   _pad_rows(v_p.astype(BF16), s_p), qpos, lim,
                   tq=tq_p, kb_size=kb_p, topk=min(TOPK_MAX, seq // 4), koff=N_META)

    n_all = past + dseq
    kb_s = 384
    s_s = -(-n_all // kb_s) * kb_s
    k_all = jnp.concatenate([cache_k[l].reshape(bs, past, A_KV).astype(BF16), per_s(ps["ak"]).astype(BF16)], axis=1)
    v_all = jnp.concatenate([cache_v[l].reshape(bs, past, A_KV).astype(BF16), per_s(ps["av"]).astype(BF16)], axis=1)
    ki_all = jnp.concatenate([cache_kidx[l].astype(BF16), per_s(ps["ik"]).astype(BF16)], axis=1)
    oa_s = _attend(per_s(ps["iq"]), per_s(ps["iw"]), per_s(ps["aq"]),
                   _pad_rows(ki_all, s_s), _pad_rows(k_all, s_s), _pad_rows(v_all, s_s),
                   past + jnp.arange(dseq, dtype=I32), jnp.full((dseq,), n_all, I32),
                   tq=dseq, kb_size=kb_s, topk=min(TOPK_MAX, n_all // 4), koff=0)

    zero_state = jnp.zeros((1, B_HEADS, B_KEY_DIM, B_VAL_DIM), F32)
    _, st_m = _hgrn2(pm["bq"][None], pm["bf"][None], pm["bi"][None], lb, zero_state, N_META)
    orec_p, st_p = _hgrn2(per_p(pp["bq"]), per_p(pp["bf"]), per_p(pp["bi"]), lb, st_m, 128)
    orec_s, st_s = _hgrn2(per_s(ps["bq"]), per_s(ps["bf"]), per_s(ps["bi"]), lb,
                          state_hgrn[l].astype(F32), dseq)

    y_p = stage4(pp, oa_p, orec_p).reshape(bp, seq, D_MODEL)
    y_s = stage4(ps, oa_s, orec_s).reshape(bs, dseq, D_MODEL)

    kv5 = lambda a: a.reshape(1, a.shape[0], a.shape[1], A_KV_HEADS, A_HEAD_DIM)
    return (y_p, y_s, kv5(k_p), kv5(v_p), ki_p[None], st_p[None].astype(state_hgrn.dtype),
            kv5(per_s(ps["ak"])), kv5(per_s(ps["av"])), per_s(ps["ik"])[None],
            st_s[None].astype(state_hgrn.dtype))
```

```python
import functools

import jax
import jax.numpy as jnp
from jax import lax
from jax.experimental import pallas as pl
from jax.experimental.pallas import tpu as pltpu

F32 = jnp.float32
BF16 = jnp.bfloat16
I32 = jnp.int32
I16 = jnp.int16

D_MODEL = 1024
D_FF = 2816
FF_CHUNK = 256
N_FF_CHUNKS = D_FF // FF_CHUNK
CHUNK = 64
N_META = 16
A_HEADS = 8
A_KV_HEADS = 2
A_GROUP = A_HEADS // A_KV_HEADS
A_HEAD_DIM = 64
A_Q = A_HEADS * A_HEAD_DIM
A_KV = A_KV_HEADS * A_HEAD_DIM
IDX_HEADS = 8
IDX_DIM = 64
IDX_Q = IDX_HEADS * IDX_DIM
TOPK_MAX = 256
B_HEADS = 4
B_KEY_DIM = 128
B_VAL_DIM = 128
B_KW = B_HEADS * B_KEY_DIM
B_WIDTH = B_HEADS * B_VAL_DIM
REC_BLOCK = 16
DEPTH = 1
ALPHA = (2.0 * DEPTH) ** 0.25
LN_EPS = 1e-5
RMS_EPS = 1e-6
NEG_INF = -1e30
LOG2E = 1.4426950408889634
INT_MIN = -(2 ** 31)
INT16_MIN = -(2 ** 15)
LANES = 128
VMEM_LIMIT = 56 * 1024 * 1024


def _resident(shape):
    n = len(shape)
    return pl.BlockSpec(shape, lambda *_: (0,) * n, pipeline_mode=pl.Buffered(1))


def _layer_norm(x, g, b):
    mu = jnp.mean(x, axis=-1, keepdims=True)
    xc = x - mu
    var = jnp.mean(xc * xc, axis=-1, keepdims=True)
    return xc * lax.rsqrt(var + LN_EPS) * g + b


def _swiglu_step(x, wg_ref, wu_ref, wd_ref):
    xb = x.astype(BF16)

    def body(c, acc):
        g = jnp.dot(xb, wg_ref[c], preferred_element_type=F32)
        u = jnp.dot(xb, wu_ref[c], preferred_element_type=F32)
        a = (g * jax.nn.sigmoid(g) * u).astype(BF16)
        return acc + jnp.dot(a, wd_ref[c], preferred_element_type=F32)

    acc = lax.fori_loop(0, N_FF_CHUNKS, body, jnp.zeros(x.shape, F32))
    return ALPHA * x + 0.5 * acc


_PROJ_OUT = (
    ("aq", A_Q, BF16), ("ak", A_KV, F32), ("av", A_KV, F32), ("iq", IDX_Q, BF16),
    ("ik", IDX_DIM, F32), ("iw", IDX_HEADS, F32),
    ("bq", B_KW, F32), ("bf", B_KW, F32), ("bi", B_WIDTH, F32),
    ("bg", B_WIDTH, F32), ("ga", D_MODEL, F32), ("gb", D_MODEL, F32),
)
_PROJ_SCALE = {"aq": (A_HEAD_DIM ** -0.5) * LOG2E, "iw": (IDX_HEADS ** -0.5) * (IDX_DIM ** -0.5)}
_HEAD_MAJOR = {"aq": A_HEADS, "iq": IDX_HEADS}


def _ffn_in_kernel(x_ref, g_ref, b_ref, wg_ref, wu_ref, wd_ref, *rest):
    n = len(_PROJ_OUT)
    w_refs, h_ref, out_refs = rest[:n], rest[n], rest[n + 1:]
    x = x_ref[...]
    h = _layer_norm(_swiglu_step(x, wg_ref, wu_ref, wd_ref), g_ref[...], b_ref[...])
    h_ref[...] = h
    hb = h.astype(BF16)
    for (name, _, dt), w_ref, o_ref in zip(_PROJ_OUT, w_refs, out_refs):
        y = jnp.dot(hb, w_ref[...], preferred_element_type=F32)
        if name in _PROJ_SCALE:
            y = y * _PROJ_SCALE[name]
        if name in _HEAD_MAJOR:
            hd = y.shape[1] // _HEAD_MAJOR[name]
            for h in range(_HEAD_MAJOR[name]):
                o_ref[h] = y[:, h * hd:(h + 1) * hd].astype(dt)
        else:
            o_ref[...] = y.astype(dt)


def _ffn_in(x, ln_g, ln_b, wg, wu, wd, w_pieces, tm):
    m = x.shape[0]
    assert m % tm == 0
    row = lambda w: pl.BlockSpec((tm, w), lambda i: (i, 0))
    in_specs = [row(D_MODEL), _resident(ln_g.shape), _resident(ln_b.shape),
                _resident(wg.shape), _resident(wu.shape), _resident(wd.shape)]
    in_specs += [_resident(w.shape) for w in w_pieces]
    out_shape = [jax.ShapeDtypeStruct((m, D_MODEL), F32)]
    out_specs = [row(D_MODEL)]
    for name, width, dt in _PROJ_OUT:
        if name in _HEAD_MAJOR:
            nh = _HEAD_MAJOR[name]
            out_shape.append(jax.ShapeDtypeStruct((nh, m, width // nh), dt))
            out_specs.append(pl.BlockSpec((nh, tm, width // nh), lambda i: (0, i, 0)))
        else:
            out_shape.append(jax.ShapeDtypeStruct((m, width), dt))
            out_specs.append(row(width))
    outs = pl.pallas_call(
        _ffn_in_kernel,
        grid=(m // tm,),
        in_specs=in_specs,
        out_specs=out_specs,
        out_shape=out_shape,
        compiler_params=pltpu.CompilerParams(
            dimension_semantics=("arbitrary",), vmem_limit_bytes=VMEM_LIMIT),
        name="ffn_in",
    )(x, ln_g, ln_b, wg, wu, wd, *w_pieces)
    res = {"h": outs[0]}
    for (name, _, _), o in zip(_PROJ_OUT, outs[1:]):
        res[name] = o
    return res


def _attend_kernel(nkb_ref, qpos_ref, lim_ref, iq_ref, wi_ref, q_ref, ki_ref, k_ref, v_ref,
                   o_ref, key_scr, t_scr, *, topk, koff, kb_size, slopes):
    tq = q_ref.shape[0]
    nkb = nkb_ref[pl.program_id(1)]
    lim = lim_ref[...]
    qpos = qpos_ref[...]
    sub = kb_size // LANES

    def lane_ids(s0):
        return s0 + lax.broadcasted_iota(I32, (tq, kb_size), 1)

    def score_block(kb, carry):
        s0 = pl.multiple_of(kb * kb_size, kb_size)
        kib = ki_ref[pl.ds(s0, kb_size), :]
        acc = jnp.zeros((tq, kb_size), F32)
        for h in range(IDX_HEADS):
            lg = lax.dot_general(iq_ref[:, h * IDX_DIM:(h + 1) * IDX_DIM], kib,
                                 (((1,), (1,)), ((), ())), preferred_element_type=F32)
            acc = acc + jnp.maximum(lg, 0.0) * wi_ref[:, h:h + 1]
        bits = lax.bitcast_convert_type(acc, I32)
        key = bits ^ ((bits >> 31) & 0x7FFFFFFF)
        key_scr[:, pl.ds(s0, kb_size)] = jnp.where(lane_ids(s0) < lim, key, INT_MIN)
        return carry

    lax.fori_loop(0, nkb, score_block, 0)

    def bcast(x):
        return jnp.broadcast_to(x, (tq, LANES))

    def count(pred):
        def body(kb, acc):
            for c in range(sub):
                off = pl.multiple_of(kb * kb_size + c * LANES, LANES)
                col = off + lax.broadcasted_iota(I32, (tq, LANES), 1)
                acc = acc + jnp.where(pred(key_scr[:, pl.ds(off, LANES)], col), 1.0, 0.0)
            return acc
        acc = lax.fori_loop(0, nkb, body, jnp.zeros((tq, LANES), F32))
        return jnp.sum(acc, axis=1, keepdims=True)

    def bit_step(state):
        i, t, settled, _ = state
        stop = jnp.min(settled)
        cand = t + lax.shift_left(jnp.int32(1), 31 - i)
        cand_b = bcast(cand)
        cnt = count(lambda key, col: key >= cand_b)
        take = (cnt >= topk) & (settled == 0.0)
        settled = jnp.where(take & (cnt == topk), 1.0, settled)
        return i + 1, jnp.where(take, cand, t), settled, stop

    _, t, _, _ = lax.while_loop(
        lambda state: (state[0] < 32) & (state[3] == 0.0), bit_step,
        (jnp.int32(0), jnp.full((tq, 1), INT_MIN, I32), (lim <= topk).astype(F32), jnp.float32(0.0)))
    thr = jnp.maximum(t, INT_MIN + 1)
    thr_b = bcast(thr)
    n_ge = count(lambda key, col: key >= thr_b)

    @pl.when(jnp.max(n_ge) > topk)
    def _():
        n_gt = count(lambda key, col: key > thr_b)
        need = topk - n_gt
        n_bits = (key_scr.shape[1] - 1).bit_length()

        def idx_step(i, m0):
            cand = m0 + lax.shift_left(jnp.int32(1), n_bits - 1 - i)
            cand_b = bcast(cand)
            g = count(lambda key, col: (key == thr_b) & (col < cand_b))
            return jnp.where(g < need, cand, m0)

        m0 = lax.fori_loop(0, n_bits, idx_step, jnp.zeros((tq, 1), I32))
        drop_row = n_ge > topk

        def demote(kb, carry):
            s0 = pl.multiple_of(kb * kb_size, kb_size)
            key = key_scr[:, pl.ds(s0, kb_size)]
            drop = drop_row & (key == thr) & (lane_ids(s0) > m0)
            key_scr[:, pl.ds(s0, kb_size)] = jnp.where(drop, INT_MIN, key)
            return carry

        lax.fori_loop(0, nkb, demote, 0)

    def attend_block(kb, carry):
        ms, ls, accs = carry
        s0 = pl.multiple_of(kb * kb_size, kb_size)
        col = lane_ids(s0)
        bias = jnp.where(key_scr[:, pl.ds(s0, kb_size)] >= thr, 0.0, NEG_INF)
        dist = jnp.abs(qpos - (col - koff)).astype(F32)
        kblk = k_ref[pl.ds(s0, kb_size), :]
        vblk = v_ref[pl.ds(s0, kb_size), :]
        new_m, new_l, new_acc = [], [], []
        for h in range(A_HEADS):
            g = h // A_GROUP
            s = lax.dot_general(q_ref[:, h * A_HEAD_DIM:(h + 1) * A_HEAD_DIM],
                                kblk[:, g * A_HEAD_DIM:(g + 1) * A_HEAD_DIM],
                                (((1,), (1,)), ((), ())), preferred_element_type=F32)
            s = (s - slopes[h] * dist) + bias
            t_scr[h] = s
            new_m.append(jnp.maximum(ms[h], jnp.max(s, axis=1, keepdims=True)))
        for h in range(A_HEADS):
            g = h // A_GROUP
            p = jnp.exp2(t_scr[h] - new_m[h])
            alpha = jnp.exp2(ms[h] - new_m[h])
            new_l.append(alpha * ls[h] + jnp.sum(p, axis=1, keepdims=True))
            pv = jnp.dot(p.astype(BF16), vblk[:, g * A_HEAD_DIM:(g + 1) * A_HEAD_DIM],
                         preferred_element_type=F32)
            new_acc.append(alpha * accs[h] + pv)
        return tuple(new_m), tuple(new_l), tuple(new_acc)

    init = (tuple(jnp.full((tq, 1), NEG_INF, F32) for _ in range(A_HEADS)),
            tuple(jnp.zeros((tq, 1), F32) for _ in range(A_HEADS)),
            tuple(jnp.zeros((tq, A_HEAD_DIM), F32) for _ in range(A_HEADS)))
    _, ls, accs = lax.fori_loop(0, nkb, attend_block, init)
    for h in range(A_HEADS):
        o_ref[:, h * A_HEAD_DIM:(h + 1) * A_HEAD_DIM] = (accs[h] / ls[h]).astype(o_ref.dtype)


def _attend(iq, wi, q, ki, k, v, qpos, lim, *, tq, kb_size, topk, koff):
    bsz, nq, _ = q.shape
    s_pad = k.shape[1]
    assert nq % tq == 0 and s_pad % kb_size == 0
    nt = nq // tq
    nkb = (jnp.max(lim.reshape(nt, tq), axis=1) + kb_size - 1) // kb_size
    slopes = _alibi_slopes_log2()
    qrow = lambda w: pl.BlockSpec((None, tq, w), lambda b, j, n: (b, j, 0))
    krow = lambda w: pl.BlockSpec((None, s_pad, w), lambda b, j, n: (b, 0, 0))
    pos = pl.BlockSpec((tq, 1), lambda b, j, n: (j, 0))
    return pl.pallas_call(
        functools.partial(_attend_kernel, topk=topk, koff=koff, kb_size=kb_size, slopes=slopes),
        grid_spec=pltpu.PrefetchScalarGridSpec(
            num_scalar_prefetch=1,
            grid=(bsz, nt),
            in_specs=[pos, pos, qrow(IDX_Q), qrow(IDX_HEADS), qrow(A_Q),
                      krow(IDX_DIM), krow(A_KV), krow(A_KV)],
            out_specs=qrow(A_Q),
            scratch_shapes=[pltpu.VMEM((tq, s_pad), I32),
                            pltpu.VMEM((A_HEADS, tq, kb_size), F32)],
        ),
        out_shape=jax.ShapeDtypeStruct((bsz, nq, A_Q), BF16),
        compiler_params=pltpu.CompilerParams(
            dimension_semantics=("arbitrary", "arbitrary"), vmem_limit_bytes=VMEM_LIMIT),
        name="attend",
    )(nkb.astype(I32), qpos.reshape(nq, 1), lim.reshape(nq, 1), iq, wi, q, ki, k, v)


def _fold_rows(x, op, rows=8):
    while x.shape[0] > rows:
        half = x.shape[0] // 2
        x = op(x[:half], x[half:])
    return x


def _alibi_slopes_log2():
    return tuple(float(2.0 ** (-8.0 * (i + 1) / A_HEADS)) * LOG2E for i in range(A_HEADS))


def _attend_t_kernel(nkb_ref, qpos_ref, lim_ref, iq_ref, wi_ref, q_ref, ki_ref, k_ref, vt_ref,
                     o_ref, key_scr, hi_scr, lo_scr, t_scr, *, topk, koff, kb_size, slopes):
    tq = o_ref.shape[0]
    nkb = nkb_ref[pl.program_id(1)]
    lim = lim_ref[...]
    qpos = qpos_ref[...]
    n_pairs = A_HEADS // 2

    def row_ids(s0):
        return s0 + lax.broadcasted_iota(I32, (kb_size, tq), 0)

    def head_pair(ref, pr):
        return ref[2 * pr:2 * pr + 2].reshape(2 * tq, ref.shape[-1])

    def score_block(kb, carry):
        s0 = pl.multiple_of(kb * kb_size, kb_size)
        kib = ki_ref[pl.ds(s0, kb_size), :]
        acc = jnp.zeros((kb_size, tq), F32)
        for pr in range(IDX_HEADS // 2):
            lg = lax.dot_general(kib, head_pair(iq_ref, pr), (((1,), (1,)), ((), ())),
                                 preferred_element_type=F32)
            for u in range(2):
                h = 2 * pr + u
                acc = acc + jnp.maximum(lg[:, u * tq:(u + 1) * tq], 0.0) * wi_ref[h:h + 1, :]
        bits = lax.bitcast_convert_type(acc, I32)
        key = bits ^ ((bits >> 31) & 0x7FFFFFFF)
        key = jnp.where(row_ids(s0) < lim, key, INT_MIN)
        key_scr[pl.ds(s0, kb_size), :] = key
        hi_scr[pl.ds(s0, kb_size), :] = (key >> 16).astype(I16)
        return carry

    lax.fori_loop(0, nkb, score_block, 0)

    def count(pred):
        def body(kb, acc):
            s0 = pl.multiple_of(kb * kb_size, kb_size)
            hit = jnp.where(pred(key_scr[pl.ds(s0, kb_size), :], s0), 1.0, 0.0)
            return acc + _fold_rows(hit, jnp.add)
        acc = lax.fori_loop(0, nkb, body, jnp.zeros((8, tq), F32))
        return jnp.sum(acc, axis=0, keepdims=True)

    def count16(scr, pred):
        def body(kb, acc):
            s0 = pl.multiple_of(kb * kb_size, kb_size)
            hit = jnp.where(pred(scr[pl.ds(s0, kb_size), :]), jnp.int16(1), jnp.int16(0))
            return acc + _fold_rows(hit, jnp.add, 16)
        acc = lax.fori_loop(0, nkb, body, jnp.zeros((16, tq), I16))
        return jnp.sum(acc.astype(F32), axis=0, keepdims=True)

    def kth_largest16(scr, kth):
        def bit_step(i, t):
            cand = t + lax.shift_left(jnp.int32(1), 15 - i)
            cand16 = cand.astype(I16)
            return jnp.where(count16(scr, lambda blk: blk >= cand16) >= kth, cand, t)
        return lax.fori_loop(0, 16, bit_step, jnp.full((1, tq), INT16_MIN, I32))

    t_hi = kth_largest16(hi_scr, topk)
    t_hi16 = t_hi.astype(I16)
    kth_lo = topk - count16(hi_scr, lambda blk: blk > t_hi16)

    def low_plane(kb, carry):
        s0 = pl.multiple_of(kb * kb_size, kb_size)
        key = key_scr[pl.ds(s0, kb_size), :]
        lo = jnp.where((key >> 16) == t_hi, (key & 0xFFFF) + INT16_MIN, INT16_MIN)
        lo_scr[pl.ds(s0, kb_size), :] = lo.astype(I16)
        return carry

    lax.fori_loop(0, nkb, low_plane, 0)
    t_lo = kth_largest16(lo_scr, kth_lo)
    t = lax.shift_left(t_hi, 16) | (t_lo - INT16_MIN)
    thr = jnp.maximum(t, INT_MIN + 1)
    n_ge = count(lambda key, s0: key >= thr)

    @pl.when(jnp.max(n_ge) > topk)
    def _():
        n_gt = count(lambda key, s0: key > thr)
        need = topk - n_gt
        n_bits = (key_scr.shape[0] - 1).bit_length()

        def idx_step(i, m0):
            cand = m0 + lax.shift_left(jnp.int32(1), n_bits - 1 - i)
            g = count(lambda key, s0: (key == thr) & (row_ids(s0) < cand))
            return jnp.where(g < need, cand, m0)

        m0 = lax.fori_loop(0, n_bits, idx_step, jnp.zeros((1, tq), I32))
        drop_q = n_ge > topk

        def demote(kb, carry):
            s0 = pl.multiple_of(kb * kb_size, kb_size)
            key = key_scr[pl.ds(s0, kb_size), :]
            drop = drop_q & (key == thr) & (row_ids(s0) > m0)
            key_scr[pl.ds(s0, kb_size), :] = jnp.where(drop, INT_MIN, key)
            return carry

        lax.fori_loop(0, nkb, demote, 0)

    def attend_block(kb, carry):
        ms, ls, accs = carry
        s0 = pl.multiple_of(kb * kb_size, kb_size)
        bias = jnp.where(key_scr[pl.ds(s0, kb_size), :] >= thr, 0.0, NEG_INF)
        dist = jnp.abs(qpos - (row_ids(s0) - koff)).astype(F32)
        new_m, new_l, new_acc = [], [], []
        for pr in range(n_pairs):
            g = (2 * pr) // A_GROUP
            s2 = lax.dot_general(k_ref[g, pl.ds(s0, kb_size), :], head_pair(q_ref, pr),
                                 (((1,), (1,)), ((), ())), preferred_element_type=F32)
            for u in range(2):
                h = 2 * pr + u
                s = (s2[:, u * tq:(u + 1) * tq] - slopes[h] * dist) + bias
                t_scr[h] = s
                blk_max = jnp.max(_fold_rows(s, jnp.maximum), axis=0, keepdims=True)
                new_m.append(jnp.maximum(ms[h], blk_max))
        for g in range(A_KV_HEADS):
            ps, alphas = [], []
            for h in range(g * A_GROUP, (g + 1) * A_GROUP):
                p = jnp.exp2(t_scr[h] - new_m[h])
                alpha = jnp.exp2(ms[h] - new_m[h])
                blk_sum = jnp.sum(_fold_rows(p, jnp.add), axis=0, keepdims=True)
                new_l.append(alpha * ls[h] + blk_sum)
                ps.append(p.astype(BF16))
                alphas.append(alpha)
            pv = jnp.dot(vt_ref[g, :, pl.ds(s0, kb_size)], jnp.concatenate(ps, axis=1),
                         preferred_element_type=F32)
            new_acc.append(jnp.concatenate(alphas, axis=1) * accs[g] + pv)
        return tuple(new_m), tuple(new_l), tuple(new_acc)

    init = (tuple(jnp.full((1, tq), NEG_INF, F32) for _ in range(A_HEADS)),
            tuple(jnp.zeros((1, tq), F32) for _ in range(A_HEADS)),
            tuple(jnp.zeros((A_HEAD_DIM, A_GROUP * tq), F32) for _ in range(A_KV_HEADS)))
    _, ls, accs = lax.fori_loop(0, nkb, attend_block, init)
    for pr in range(n_pairs):
        rows = []
        for h in (2 * pr, 2 * pr + 1):
            g, hh = divmod(h, A_GROUP)
            rows.append(accs[g][:, hh * tq:(hh + 1) * tq] / ls[h])
        o_ref[:, 2 * pr * A_HEAD_DIM:(2 * pr + 2) * A_HEAD_DIM] = (
            jnp.concatenate(rows, axis=0).T.astype(o_ref.dtype))


def _attend_t(iq, wi_t, q, ki, k, vt, qpos, lim, *, kb_size, topk, koff):
    tq = LANES
    _, bsz, nq, _ = q.shape
    s_pad = ki.shape[1]
    assert nq % tq == 0 and s_pad % kb_size == 0
    nt = nq // tq
    nkb = (jnp.max(lim.reshape(nt, tq), axis=1) + kb_size - 1) // kb_size
    heads = lambda n: pl.BlockSpec((n, None, tq, A_HEAD_DIM), lambda b, j, _: (0, b, j, 0))
    pos = pl.BlockSpec((1, tq), lambda b, j, _: (0, j))
    return pl.pallas_call(
        functools.partial(_attend_t_kernel, topk=topk, koff=koff, kb_size=kb_size,
                          slopes=_alibi_slopes_log2()),
        grid_spec=pltpu.PrefetchScalarGridSpec(
            num_scalar_prefetch=1,
            grid=(bsz, nt),
            in_specs=[pos, pos, heads(IDX_HEADS),
                      pl.BlockSpec((None, IDX_HEADS, tq), lambda b, j, _: (b, 0, j)),
                      heads(A_HEADS),
                      pl.BlockSpec((None, s_pad, IDX_DIM), lambda b, j, _: (b, 0, 0)),
                      pl.BlockSpec((A_KV_HEADS, None, s_pad, A_HEAD_DIM), lambda b, j, _: (0, b, 0, 0)),
                      pl.BlockSpec((A_KV_HEADS, None, A_HEAD_DIM, s_pad), lambda b, j, _: (0, b, 0, 0))],
            out_specs=pl.BlockSpec((None, tq, A_Q), lambda b, j, _: (b, j, 0)),
            scratch_shapes=[pltpu.VMEM((s_pad, tq), I32), pltpu.VMEM((s_pad, tq), I16),
                            pltpu.VMEM((s_pad, tq), I16),
                            pltpu.VMEM((A_HEADS, kb_size, tq), F32)],
        ),
        out_shape=jax.ShapeDtypeStruct((bsz, nq, A_Q), BF16),
        compiler_params=pltpu.CompilerParams(
            dimension_semantics=("arbitrary", "arbitrary"), vmem_limit_bytes=VMEM_LIMIT),
        name="attend_t",
    )(nkb.astype(I32), qpos.reshape(1, nq), lim.reshape(1, nq), iq, wi_t, q, ki, k, vt)


def _hgrn2_kernel(bq_ref, bf_ref, bi_ref, lb_ref, tri_ref, s0_ref, o_ref, sout_ref,
                  st_scr, q_scr, k_scr, b_scr):
    c = pl.program_id(1)
    ct = bq_ref.shape[0]

    @pl.when(c == 0)
    def _():
        for h in range(B_HEADS):
            st_scr[h] = s0_ref[h].T

    lb = lb_ref[...]
    fx = bf_ref[...]
    logf = jnp.log(lb + (1.0 - lb) * jax.nn.sigmoid(fx))
    k_scr[...] = (1.0 - lb) * jax.nn.sigmoid(-fx)
    bq = bq_ref[...]
    q_scr[...] = bq * jax.nn.sigmoid(bq)
    b_scr[...] = jnp.dot(tri_ref[...], logf, precision=lax.Precision.HIGHEST,
                         preferred_element_type=F32)
    row = lax.broadcasted_iota(I32, (REC_BLOCK, 1), 0)

    def block(blk, carry):
        rows = pl.ds(pl.multiple_of(blk * REC_BLOCK, REC_BLOCK), REC_BLOCK)
        for h in range(B_HEADS):
            ls = slice(h * B_KEY_DIM, (h + 1) * B_KEY_DIM)
            b = b_scr[rows, ls]
            qb = q_scr[rows, ls]
            kb = k_scr[rows, ls]
            vb = bi_ref[rows, ls]
            b_end = b[REC_BLOCK - 1:REC_BLOCK]
            st = st_scr[h]
            o_rows = []
            for t in range(REC_BLOCK):
                dec = jnp.exp(jnp.minimum(b[t:t + 1] - b, 0.0))
                a = jnp.sum((qb[t:t + 1] * dec) * kb, axis=1, keepdims=True)
                a = jnp.where(row <= t, a, 0.0)
                o_rows.append(jnp.sum(a * vb, axis=0, keepdims=True))
            o_intra = jnp.concatenate(o_rows, axis=0)
            o_inter = lax.dot_general((qb * jnp.exp(b)).astype(BF16), st.astype(BF16),
                                      (((1,), (1,)), ((), ())), preferred_element_type=F32)
            o_ref[rows, ls] = o_intra + o_inter
            upd = lax.dot_general(vb.astype(BF16), (kb * jnp.exp(b_end - b)).astype(BF16),
                                  (((0,), (0,)), ((), ())), preferred_element_type=F32)
            st_scr[h] = st * jnp.exp(b_end) + upd
        return carry

    lax.fori_loop(0, ct // REC_BLOCK, block, 0)

    @pl.when(c == pl.num_programs(1) - 1)
    def _():
        for h in range(B_HEADS):
            sout_ref[h] = st_scr[h].T


def _hgrn2(bq, bf, bi, lb, s0, ct):
    bsz, t, _ = bq.shape
    assert t % ct == 0 and ct % REC_BLOCK == 0
    r = jnp.arange(ct)
    tri = ((r[:, None] >= r[None, :]) & (r[:, None] // REC_BLOCK == r[None, :] // REC_BLOCK)).astype(F32)
    tok = pl.BlockSpec((None, ct, B_KW), lambda b, c: (b, c, 0))
    if s0.shape[0] == 1:
        s_in = pl.BlockSpec((None, B_HEADS, B_KEY_DIM, B_VAL_DIM), lambda b, c: (0, 0, 0, 0))
    else:
        s_in = pl.BlockSpec((None, B_HEADS, B_KEY_DIM, B_VAL_DIM), lambda b, c: (b, 0, 0, 0))
    s_out = pl.BlockSpec((None, B_HEADS, B_KEY_DIM, B_VAL_DIM), lambda b, c: (b, 0, 0, 0))
    return pl.pallas_call(
        _hgrn2_kernel,
        grid=(bsz, t // ct),
        in_specs=[tok, tok, tok, _resident(lb.shape), _resident(tri.shape), s_in],
        out_specs=[tok, s_out],
        out_shape=[jax.ShapeDtypeStruct((bsz, t, B_WIDTH), F32),
                   jax.ShapeDtypeStruct((bsz, B_HEADS, B_KEY_DIM, B_VAL_DIM), F32)],
        scratch_shapes=[pltpu.VMEM((B_HEADS, B_VAL_DIM, B_KEY_DIM), F32)]
        + [pltpu.VMEM((ct, B_KW), F32)] * 3,
        compiler_params=pltpu.CompilerParams(
            dimension_semantics=("arbitrary", "arbitrary"), vmem_limit_bytes=VMEM_LIMIT),
        name="hgrn2",
    )(bq, bf, bi, lb, tri, s0)


def _mix_out_kernel(h_ref, oa_ref, orec_ref, bg_ref, ga_ref, gb_ref, gn_ref, wa_ref, wb_ref, wo_ref,
                    g2_ref, b2_ref, wg_ref, wu_ref, wd_ref, g3_ref, b3_ref, y_ref):
    orec = orec_ref[...]
    gn = gn_ref[...]
    parts = []
    for h in range(B_HEADS):
        ls = slice(h * B_VAL_DIM, (h + 1) * B_VAL_DIM)
        oh = orec[:, ls]
        ms = jnp.mean(oh * oh, axis=-1, keepdims=True)
        parts.append(oh * lax.rsqrt(ms + RMS_EPS) * gn[:, ls])
    on = jnp.concatenate(parts, axis=-1)
    bg = bg_ref[...]
    ob = on * (bg * jax.nn.sigmoid(bg))
    pa = jnp.dot(oa_ref[...], wa_ref[...], preferred_element_type=F32)
    pb = jnp.dot(ob.astype(BF16), wb_ref[...], preferred_element_type=F32)
    mixed = jax.nn.sigmoid(ga_ref[...]) * pa + jax.nn.sigmoid(gb_ref[...]) * pb
    mixed = jnp.dot(mixed.astype(BF16), wo_ref[...], preferred_element_type=F32)
    h2 = _layer_norm(ALPHA * h_ref[...] + mixed, g2_ref[...], b2_ref[...])
    y_ref[...] = _layer_norm(_swiglu_step(h2, wg_ref, wu_ref, wd_ref), g3_ref[...], b3_ref[...])


def _mix_out(h, oa, orec, bg, ga, gb, gn, wa, wb, wo, g2, b2, wg, wu, wd, g3, b3, tm):
    m = h.shape[0]
    assert m % tm == 0
    row = lambda w: pl.BlockSpec((tm, w), lambda i: (i, 0))
    weights = (gn, wa, wb, wo, g2, b2, wg, wu, wd, g3, b3)
    return pl.pallas_call(
        _mix_out_kernel,
        grid=(m // tm,),
        in_specs=[row(D_MODEL), row(A_Q), row(B_WIDTH), row(B_WIDTH), row(D_MODEL), row(D_MODEL)]
        + [_resident(w.shape) for w in weights],
        out_specs=row(D_MODEL),
        out_shape=jax.ShapeDtypeStruct((m, D_MODEL), F32),
        compiler_params=pltpu.CompilerParams(
            dimension_semantics=("arbitrary",), vmem_limit_bytes=VMEM_LIMIT),
        name="mix_out",
    )(h, oa, orec, bg, ga, gb, *weights)


def _ffn_weights(wg, wu, wd):
    wg = wg.astype(BF16).reshape(D_MODEL, N_FF_CHUNKS, FF_CHUNK).transpose(1, 0, 2)
    wu = wu.astype(BF16).reshape(D_MODEL, N_FF_CHUNKS, FF_CHUNK).transpose(1, 0, 2)
    wd = wd.astype(BF16).reshape(N_FF_CHUNKS, FF_CHUNK, D_MODEL)
    return wg, wu, wd


def _pad_rows(a, n):
    return jnp.pad(a, ((0, 0), (0, n - a.shape[1]), (0, 0)))


def _token_tile(m):
    for tm in (256, 128, 64, 32, 16, 8):
        if m % tm == 0:
            return tm
    raise ValueError(m)


def kernel(x_prompt, x_sample, cache_k, cache_v, cache_kidx, state_hgrn, meta, ln1_g, ln1_b, ffn1_wg, ffn1_wu, ffn1_wd, w_in, lb_param, g_norm, w_a_proj, w_b_proj, w_out, ln2_g, ln2_b, ffn2_wg, ffn2_wu, ffn2_wd, ln3_g, ln3_b):
    assert ln1_g.shape[0] == DEPTH
    bp, seq, _ = x_prompt.shape
    bs, dseq, _ = x_sample.shape
    past = cache_k.shape[2]
    l = 0
    vec = lambda a: a.reshape(1, -1).astype(F32)

    lb_all = jnp.cumsum(jax.nn.softmax(lb_param.astype(F32), axis=0), axis=0)
    lb = lb_all[l].reshape(1, B_KW)
    f1 = _ffn_weights(ffn1_wg[l], ffn1_wu[l], ffn1_wd[l])
    f2 = _ffn_weights(ffn2_wg[l], ffn2_wu[l], ffn2_wd[l])
    w_pieces, off = [], 0
    for _, width, _ in _PROJ_OUT:
        w_pieces.append(w_in[l][:, off:off + width].astype(BF16))
        off += width
    assert off == w_in.shape[-1]

    def stage1(x):
        return _ffn_in(x, vec(ln1_g[l]), vec(ln1_b[l]), *f1, w_pieces, _token_tile(x.shape[0]))

    def stage4(p, oa, orec):
        m = p["h"].shape[0]
        return _mix_out(p["h"], oa.reshape(m, A_Q), orec.reshape(m, B_WIDTH), p["bg"], p["ga"], p["gb"],
                        vec(g_norm[l]), w_a_proj[l].astype(BF16), w_b_proj[l].astype(BF16),
                        w_out[l].astype(BF16), vec(ln2_g[l]), vec(ln2_b[l]), *f2,
                        vec(ln3_g[l]), vec(ln3_b[l]), _token_tile(m))

    pm = stage1(meta.astype(F32))
    pp = stage1(x_prompt.reshape(bp * seq, D_MODEL))
    ps = stage1(x_sample.reshape(bs * dseq, D_MODEL))
    per_p = lambda a: a.reshape(bp, seq, a.shape[-1])
    per_s = lambda a: a.reshape(bs, dseq, a.shape[-1])

    def with_meta(name):
        rows = jnp.broadcast_to(pm[name][None], (bp, N_META, pm[name].shape[-1]))
        return jnp.concatenate([rows, per_p(pp[name])], axis=1)

    k_p, v_p, ki_p = with_meta("ak"), with_meta("av"), with_meta("ik")

    kb_p = 512
    s_p = -(-(N_META + seq) // kb_p) * kb_p
    qpos = jnp.arange(seq, dtype=I32)
    lim = N_META + CHUNK * (qpos // CHUNK + 1)
    heads_p = lambda a: a.reshape(a.shape[0], bp, seq, a.shape[-1])
    kv_heads = lambda a: _pad_rows(a.astype(BF16), s_p).reshape(bp, s_p, A_KV_HEADS, A_HEAD_DIM)
    oa_p = _attend_t(heads_p(pp["iq"]), per_p(pp["iw"]).transpose(0, 2, 1), heads_p(pp["aq"]),
                     _pad_rows(ki_p.astype(BF16), s_p), kv_heads(k_p).transpose(2, 0, 1, 3),
                     kv_heads(v_p).transpose(2, 0, 3, 1), qpos, lim,
                     kb_size=kb_p, topk=min(TOPK_MAX, seq // 4), koff=N_META)

    n_all = past + dseq
    kb_s = 384
    s_s = -(-n_all // kb_s) * kb_s
    k_all = jnp.concatenate([cache_k[l].reshape(bs, past, A_KV).astype(BF16), per_s(ps["ak"]).astype(BF16)], axis=1)
    v_all = jnp.concatenate([cache_v[l].reshape(bs, past, A_KV).astype(BF16), per_s(ps["av"]).astype(BF16)], axis=1)
    ki_all = jnp.concatenate([cache_kidx[l].astype(BF16), per_s(ps["ik"]).astype(BF16)], axis=1)
    tokens_s = lambda a: a.transpose(1, 0, 2).reshape(bs, dseq, a.shape[0] * a.shape[-1])
    oa_s = _attend(tokens_s(ps["iq"]), per_s(ps["iw"]), tokens_s(ps["aq"]),
                   _pad_rows(ki_all, s_s), _pad_rows(k_all, s_s), _pad_rows(v_all, s_s),
                   past + jnp.arange(dseq, dtype=I32), jnp.full((dseq,), n_all, I32),
                   tq=dseq, kb_size=kb_s, topk=min(TOPK_MAX, n_all // 4), koff=0)

    zero_state = jnp.zeros((1, B_HEADS, B_KEY_DIM, B_VAL_DIM), F32)
    _, st_m = _hgrn2(pm["bq"][None], pm["bf"][None], pm["bi"][None], lb, zero_state, N_META)
    orec_p, st_p = _hgrn2(per_p(pp["bq"]), per_p(pp["bf"]), per_p(pp["bi"]), lb, st_m, 128)
    orec_s, st_s = _hgrn2(per_s(ps["bq"]), per_s(ps["bf"]), per_s(ps["bi"]), lb,
                          state_hgrn[l].astype(F32), dseq)

    y_p = stage4(pp, oa_p, orec_p).reshape(bp, seq, D_MODEL)
    y_s = stage4(ps, oa_s, orec_s).reshape(bs, dseq, D_MODEL)

    kv5 = lambda a: a.reshape(1, a.shape[0], a.shape[1], A_KV_HEADS, A_HEAD_DIM)
    return (y_p, y_s, kv5(k_p), kv5(v_p), ki_p[None], st_p[None].astype(state_hgrn.dtype),
            kv5(per_s(ps["ak"])), kv5(per_s(ps["av"])), per_s(ps["ik"])[None],
            st_s[None].astype(state_hgrn.dtype))
```

```python
import functools

import jax
import jax.numpy as jnp
from jax import lax
from jax.experimental import pallas as pl
from jax.experimental.pallas import tpu as pltpu

F32 = jnp.float32
BF16 = jnp.bfloat16
I32 = jnp.int32
I16 = jnp.int16

D_MODEL = 1024
D_FF = 2816
FF_CHUNK = 256
N_FF_CHUNKS = D_FF // FF_CHUNK
CHUNK = 64
N_META = 16
A_HEADS = 8
A_KV_HEADS = 2
A_GROUP = A_HEADS // A_KV_HEADS
A_HEAD_DIM = 64
A_Q = A_HEADS * A_HEAD_DIM
A_KV = A_KV_HEADS * A_HEAD_DIM
IDX_HEADS = 8
IDX_DIM = 64
IDX_Q = IDX_HEADS * IDX_DIM
TOPK_MAX = 256
B_HEADS = 4
B_KEY_DIM = 128
B_VAL_DIM = 128
B_KW = B_HEADS * B_KEY_DIM
B_WIDTH = B_HEADS * B_VAL_DIM
REC_BLOCK = 16
DEPTH = 1
ALPHA = (2.0 * DEPTH) ** 0.25
LN_EPS = 1e-5
RMS_EPS = 1e-6
NEG_INF = -1e30
LOG2E = 1.4426950408889634
INT_MIN = -(2 ** 31)
INT16_MIN = -(2 ** 15)
LANES = 128
VMEM_LIMIT = 56 * 1024 * 1024


def _resident(shape):
    n = len(shape)
    return pl.BlockSpec(shape, lambda *_: (0,) * n, pipeline_mode=pl.Buffered(1))


def _layer_norm(x, g, b):
    mu = jnp.mean(x, axis=-1, keepdims=True)
    xc = x - mu
    var = jnp.mean(xc * xc, axis=-1, keepdims=True)
    return xc * lax.rsqrt(var + LN_EPS) * g + b


def _swiglu_step(x, wg_ref, wu_ref, wd_ref):
    xb = x.astype(BF16)

    acc = jnp.zeros(x.shape, F32)
    for c in range(N_FF_CHUNKS):
        g = jnp.dot(xb, wg_ref[c], preferred_element_type=F32)
        u = jnp.dot(xb, wu_ref[c], preferred_element_type=F32)
        a = (g * jax.nn.sigmoid(g) * u).astype(BF16)
        acc = acc + jnp.dot(a, wd_ref[c], preferred_element_type=F32)
    return ALPHA * x + 0.5 * acc


_PROJ_OUT = (
    ("aq", A_Q, BF16), ("ak", A_KV, F32), ("av", A_KV, F32), ("iq", IDX_Q, BF16),
    ("ik", IDX_DIM, F32), ("iw", IDX_HEADS, F32),
    ("bq", B_KW, F32), ("bf", B_KW, F32), ("bi", B_WIDTH, F32),
    ("bg", B_WIDTH, F32), ("ga", D_MODEL, F32), ("gb", D_MODEL, F32),
)
_PROJ_SCALE = {"aq": (A_HEAD_DIM ** -0.5) * LOG2E, "iw": (IDX_HEADS ** -0.5) * (IDX_DIM ** -0.5)}
_HEAD_MAJOR = {"aq": A_HEADS, "iq": IDX_HEADS}


def _ffn_in_kernel(x_ref, g_ref, b_ref, wg_ref, wu_ref, wd_ref, *rest):
    n = len(_PROJ_OUT)
    w_refs, h_ref, out_refs = rest[:n], rest[n], rest[n + 1:]
    x = x_ref[...]
    h = _layer_norm(_swiglu_step(x, wg_ref, wu_ref, wd_ref), g_ref[...], b_ref[...])
    h_ref[...] = h
    hb = h.astype(BF16)
    for (name, _, dt), w_ref, o_ref in zip(_PROJ_OUT, w_refs, out_refs):
        y = jnp.dot(hb, w_ref[...], preferred_element_type=F32)
        if name in _PROJ_SCALE:
            y = y * _PROJ_SCALE[name]
        if name in _HEAD_MAJOR:
            hd = y.shape[1] // _HEAD_MAJOR[name]
            for h in range(_HEAD_MAJOR[name]):
                o_ref[h] = y[:, h * hd:(h + 1) * hd].astype(dt)
        else:
            o_ref[...] = y.astype(dt)


def _ffn_in(x, ln_g, ln_b, wg, wu, wd, w_pieces, tm):
    m = x.shape[0]
    assert m % tm == 0
    row = lambda w: pl.BlockSpec((tm, w), lambda i: (i, 0))
    in_specs = [row(D_MODEL), _resident(ln_g.shape), _resident(ln_b.shape),
                _resident(wg.shape), _resident(wu.shape), _resident(wd.shape)]
    in_specs += [_resident(w.shape) for w in w_pieces]
    out_shape = [jax.ShapeDtypeStruct((m, D_MODEL), F32)]
    out_specs = [row(D_MODEL)]
    for name, width, dt in _PROJ_OUT:
        if name in _HEAD_MAJOR:
            nh = _HEAD_MAJOR[name]
            out_shape.append(jax.ShapeDtypeStruct((nh, m, width // nh), dt))
            out_specs.append(pl.BlockSpec((nh, tm, width // nh), lambda i: (0, i, 0)))
        else:
            out_shape.append(jax.ShapeDtypeStruct((m, width), dt))
            out_specs.append(row(width))
    outs = pl.pallas_call(
        _ffn_in_kernel,
        grid=(m // tm,),
        in_specs=in_specs,
        out_specs=out_specs,
        out_shape=out_shape,
        compiler_params=pltpu.CompilerParams(
            dimension_semantics=("arbitrary",), vmem_limit_bytes=VMEM_LIMIT),
        name="ffn_in",
    )(x, ln_g, ln_b, wg, wu, wd, *w_pieces)
    res = {"h": outs[0]}
    for (name, _, _), o in zip(_PROJ_OUT, outs[1:]):
        res[name] = o
    return res


def _attend_kernel(nkb_ref, qpos_ref, lim_ref, iq_ref, wi_ref, q_ref, ki_ref, k_ref, v_ref,
                   o_ref, key_scr, t_scr, *, topk, koff, kb_size, slopes):
    tq = q_ref.shape[0]
    nkb = nkb_ref[pl.program_id(1)]
    lim = lim_ref[...]
    qpos = qpos_ref[...]
    sub = kb_size // LANES

    def lane_ids(s0):
        return s0 + lax.broadcasted_iota(I32, (tq, kb_size), 1)

    def score_block(kb, carry):
        s0 = pl.multiple_of(kb * kb_size, kb_size)
        kib = ki_ref[pl.ds(s0, kb_size), :]
        acc = jnp.zeros((tq, kb_size), F32)
        for h in range(IDX_HEADS):
            lg = lax.dot_general(iq_ref[:, h * IDX_DIM:(h + 1) * IDX_DIM], kib,
                                 (((1,), (1,)), ((), ())), preferred_element_type=F32)
            acc = acc + jnp.maximum(lg, 0.0) * wi_ref[:, h:h + 1]
        bits = lax.bitcast_convert_type(acc, I32)
        key = bits ^ ((bits >> 31) & 0x7FFFFFFF)
        key_scr[:, pl.ds(s0, kb_size)] = jnp.where(lane_ids(s0) < lim, key, INT_MIN)
        return carry

    lax.fori_loop(0, nkb, score_block, 0)

    def bcast(x):
        return jnp.broadcast_to(x, (tq, LANES))

    def count(pred):
        def body(kb, acc):
            for c in range(sub):
                off = pl.multiple_of(kb * kb_size + c * LANES, LANES)
                col = off + lax.broadcasted_iota(I32, (tq, LANES), 1)
                acc = acc + jnp.where(pred(key_scr[:, pl.ds(off, LANES)], col), 1.0, 0.0)
            return acc
        acc = lax.fori_loop(0, nkb, body, jnp.zeros((tq, LANES), F32))
        return jnp.sum(acc, axis=1, keepdims=True)

    def bit_step(state):
        i, t, settled, _ = state
        stop = jnp.min(settled)
        cand = t + lax.shift_left(jnp.int32(1), 31 - i)
        cand_b = bcast(cand)
        cnt = count(lambda key, col: key >= cand_b)
        take = (cnt >= topk) & (settled == 0.0)
        settled = jnp.where(take & (cnt == topk), 1.0, settled)
        return i + 1, jnp.where(take, cand, t), settled, stop

    _, t, _, _ = lax.while_loop(
        lambda state: (state[0] < 32) & (state[3] == 0.0), bit_step,
        (jnp.int32(0), jnp.full((tq, 1), INT_MIN, I32), (lim <= topk).astype(F32), jnp.float32(0.0)))
    thr = jnp.maximum(t, INT_MIN + 1)
    thr_b = bcast(thr)
    n_ge = count(lambda key, col: key >= thr_b)

    @pl.when(jnp.max(n_ge) > topk)
    def _():
        n_gt = count(lambda key, col: key > thr_b)
        need = topk - n_gt
        n_bits = (key_scr.shape[1] - 1).bit_length()

        def idx_step(i, m0):
            cand = m0 + lax.shift_left(jnp.int32(1), n_bits - 1 - i)
            cand_b = bcast(cand)
            g = count(lambda key, col: (key == thr_b) & (col < cand_b))
            return jnp.where(g < need, cand, m0)

        m0 = lax.fori_loop(0, n_bits, idx_step, jnp.zeros((tq, 1), I32))
        drop_row = n_ge > topk

        def demote(kb, carry):
            s0 = pl.multiple_of(kb * kb_size, kb_size)
            key = key_scr[:, pl.ds(s0, kb_size)]
            drop = drop_row & (key == thr) & (lane_ids(s0) > m0)
            key_scr[:, pl.ds(s0, kb_size)] = jnp.where(drop, INT_MIN, key)
            return carry

        lax.fori_loop(0, nkb, demote, 0)

    def attend_block(kb, carry):
        ms, ls, accs = carry
        s0 = pl.multiple_of(kb * kb_size, kb_size)
        col = lane_ids(s0)
        bias = jnp.where(key_scr[:, pl.ds(s0, kb_size)] >= thr, 0.0, NEG_INF)
        dist = jnp.abs(qpos - (col - koff)).astype(F32)
        kblk = k_ref[pl.ds(s0, kb_size), :]
        vblk = v_ref[pl.ds(s0, kb_size), :]
        new_m, new_l, new_acc = [], [], []
        for h in range(A_HEADS):
            g = h // A_GROUP
            s = lax.dot_general(q_ref[:, h * A_HEAD_DIM:(h + 1) * A_HEAD_DIM],
                                kblk[:, g * A_HEAD_DIM:(g + 1) * A_HEAD_DIM],
                                (((1,), (1,)), ((), ())), preferred_element_type=F32)
            s = (s - slopes[h] * dist) + bias
            t_scr[h] = s
            new_m.append(jnp.maximum(ms[h], jnp.max(s, axis=1, keepdims=True)))
        for h in range(A_HEADS):
            g = h // A_GROUP
            p = jnp.exp2(t_scr[h] - new_m[h])
            alpha = jnp.exp2(ms[h] - new_m[h])
            new_l.append(alpha * ls[h] + jnp.sum(p, axis=1, keepdims=True))
            pv = jnp.dot(p.astype(BF16), vblk[:, g * A_HEAD_DIM:(g + 1) * A_HEAD_DIM],
                         preferred_element_type=F32)
            new_acc.append(alpha * accs[h] + pv)
        return tuple(new_m), tuple(new_l), tuple(new_acc)

    init = (tuple(jnp.full((tq, 1), NEG_INF, F32) for _ in range(A_HEADS)),
            tuple(jnp.zeros((tq, 1), F32) for _ in range(A_HEADS)),
            tuple(jnp.zeros((tq, A_HEAD_DIM), F32) for _ in range(A_HEADS)))
    _, ls, accs = lax.fori_loop(0, nkb, attend_block, init)
    for h in range(A_HEADS):
        o_ref[:, h * A_HEAD_DIM:(h + 1) * A_HEAD_DIM] = (accs[h] / ls[h]).astype(o_ref.dtype)


def _attend(iq, wi, q, ki, k, v, qpos, lim, *, tq, kb_size, topk, koff):
    bsz, nq, _ = q.shape
    s_pad = k.shape[1]
    assert nq % tq == 0 and s_pad % kb_size == 0
    nt = nq // tq
    nkb = (jnp.max(lim.reshape(nt, tq), axis=1) + kb_size - 1) // kb_size
    slopes = _alibi_slopes_log2()
    qrow = lambda w: pl.BlockSpec((None, tq, w), lambda b, j, n: (b, j, 0))
    krow = lambda w: pl.BlockSpec((None, s_pad, w), lambda b, j, n: (b, 0, 0))
    pos = pl.BlockSpec((tq, 1), lambda b, j, n: (j, 0))
    return pl.pallas_call(
        functools.partial(_attend_kernel, topk=topk, koff=koff, kb_size=kb_size, slopes=slopes),
        grid_spec=pltpu.PrefetchScalarGridSpec(
            num_scalar_prefetch=1,
            grid=(bsz, nt),
            in_specs=[pos, pos, qrow(IDX_Q), qrow(IDX_HEADS), qrow(A_Q),
                      krow(IDX_DIM), krow(A_KV), krow(A_KV)],
            out_specs=qrow(A_Q),
            scratch_shapes=[pltpu.VMEM((tq, s_pad), I32),
                            pltpu.VMEM((A_HEADS, tq, kb_size), F32)],
        ),
        out_shape=jax.ShapeDtypeStruct((bsz, nq, A_Q), BF16),
        compiler_params=pltpu.CompilerParams(
            dimension_semantics=("arbitrary", "arbitrary"), vmem_limit_bytes=VMEM_LIMIT),
        name="attend",
    )(nkb.astype(I32), qpos.reshape(nq, 1), lim.reshape(nq, 1), iq, wi, q, ki, k, v)


def _fold_rows(x, op, rows=8):
    while x.shape[0] > rows:
        half = x.shape[0] // 2
        x = op(x[:half], x[half:])
    return x


VT_ROWS = A_HEAD_DIM + 16


def _alibi_slopes_log2():
    return tuple(float(2.0 ** (-8.0 * (i + 1) / A_HEADS)) * LOG2E for i in range(A_HEADS))


def _attend_t_kernel(nkb_ref, qpos_ref, lim_ref, iq_ref, wi_ref, q_ref, ki_ref, k_ref, vt_ref,
                     o_ref, key_scr, hi_scr, lo_scr, t_scr, *, topk, koff, kb_size, slopes):
    tq = o_ref.shape[0]
    nkb = nkb_ref[pl.program_id(1)]
    lim = lim_ref[...]
    qpos = qpos_ref[...]
    n_pairs = A_HEADS // 2

    def row_ids(s0):
        return s0 + lax.broadcasted_iota(I32, (kb_size, tq), 0)

    def head_pair(ref, pr):
        return ref[2 * pr:2 * pr + 2].reshape(2 * tq, ref.shape[-1])

    def score_block(kb, carry):
        s0 = pl.multiple_of(kb * kb_size, kb_size)
        kib = ki_ref[pl.ds(s0, kb_size), :]
        acc = jnp.zeros((kb_size, tq), F32)
        for pr in range(IDX_HEADS // 2):
            lg = lax.dot_general(kib, head_pair(iq_ref, pr), (((1,), (1,)), ((), ())),
                                 preferred_element_type=F32)
            for u in range(2):
                h = 2 * pr + u
                acc = acc + jnp.maximum(lg[:, u * tq:(u + 1) * tq], 0.0) * wi_ref[h:h + 1, :]
        bits = lax.bitcast_convert_type(acc, I32)
        key = bits ^ ((bits >> 31) & 0x7FFFFFFF)
        key = jnp.where(row_ids(s0) < lim, key, INT_MIN)
        key_scr[pl.ds(s0, kb_size), :] = key
        hi_scr[pl.ds(s0, kb_size), :] = (key >> 16).astype(I16)
        return carry

    lax.fori_loop(0, nkb, score_block, 0)

    def count(pred):
        def body(kb, acc):
            s0 = pl.multiple_of(kb * kb_size, kb_size)
            hit = jnp.where(pred(key_scr[pl.ds(s0, kb_size), :], s0), 1.0, 0.0)
            return acc + _fold_rows(hit, jnp.add)
        acc = lax.fori_loop(0, nkb, body, jnp.zeros((8, tq), F32))
        return jnp.sum(acc, axis=0, keepdims=True)

    def count16(scr, pred):
        def body(kb, acc):
            s0 = pl.multiple_of(kb * kb_size, kb_size)
            hit = jnp.where(pred(scr[pl.ds(s0, kb_size), :]), jnp.int16(1), jnp.int16(0))
            return acc + _fold_rows(hit, jnp.add, 16)
        acc = lax.fori_loop(0, nkb, body, jnp.zeros((16, tq), I16))
        return jnp.sum(acc.astype(F32), axis=0, keepdims=True)

    def kth_largest16(scr, kth):
        def bit_step(i, t):
            cand = t + lax.shift_left(jnp.int32(1), 15 - i)
            cand16 = cand.astype(I16)
            return jnp.where(count16(scr, lambda blk: blk >= cand16) >= kth, cand, t)
        return lax.fori_loop(0, 16, bit_step, jnp.full((1, tq), INT16_MIN, I32))

    t_hi = kth_largest16(hi_scr, topk)
    t_hi16 = t_hi.astype(I16)
    kth_lo = topk - count16(hi_scr, lambda blk: blk > t_hi16)

    def low_plane(kb, carry):
        s0 = pl.multiple_of(kb * kb_size, kb_size)
        key = key_scr[pl.ds(s0, kb_size), :]
        lo = jnp.where((key >> 16) == t_hi, (key & 0xFFFF) + INT16_MIN, INT16_MIN)
        lo_scr[pl.ds(s0, kb_size), :] = lo.astype(I16)
        return carry

    lax.fori_loop(0, nkb, low_plane, 0)
    t_lo = kth_largest16(lo_scr, kth_lo)
    t = lax.shift_left(t_hi, 16) | (t_lo - INT16_MIN)
    thr = jnp.maximum(t, INT_MIN + 1)
    n_ge = count(lambda key, s0: key >= thr)

    @pl.when(jnp.max(n_ge) > topk)
    def _():
        n_gt = count(lambda key, s0: key > thr)
        need = topk - n_gt
        n_bits = (key_scr.shape[0] - 1).bit_length()

        def idx_step(i, m0):
            cand = m0 + lax.shift_left(jnp.int32(1), n_bits - 1 - i)
            g = count(lambda key, s0: (key == thr) & (row_ids(s0) < cand))
            return jnp.where(g < need, cand, m0)

        m0 = lax.fori_loop(0, n_bits, idx_step, jnp.zeros((1, tq), I32))
        drop_q = n_ge > topk

        def demote(kb, carry):
            s0 = pl.multiple_of(kb * kb_size, kb_size)
            key = key_scr[pl.ds(s0, kb_size), :]
            drop = drop_q & (key == thr) & (row_ids(s0) > m0)
            key_scr[pl.ds(s0, kb_size), :] = jnp.where(drop, INT_MIN, key)
            return carry

        lax.fori_loop(0, nkb, demote, 0)

    def attend_block(kb, carry):
        ms, accs = carry
        s0 = pl.multiple_of(kb * kb_size, kb_size)
        bias = jnp.where(key_scr[pl.ds(s0, kb_size), :] >= thr, 0.0, NEG_INF)
        dist = jnp.abs(qpos - (row_ids(s0) - koff)).astype(F32)
        new_m, new_acc = [], []
        for pr in range(n_pairs):
            g = (2 * pr) // A_GROUP
            s2 = lax.dot_general(k_ref[g, pl.ds(s0, kb_size), :], head_pair(q_ref, pr),
                                 (((1,), (1,)), ((), ())), preferred_element_type=F32)
            for u in range(2):
                h = 2 * pr + u
                s = (s2[:, u * tq:(u + 1) * tq] - slopes[h] * dist) + bias
                t_scr[h] = s
                blk_max = jnp.max(_fold_rows(s, jnp.maximum), axis=0, keepdims=True)
                new_m.append(jnp.maximum(ms[h], blk_max))
        for g in range(A_KV_HEADS):
            ps, alphas = [], []
            for h in range(g * A_GROUP, (g + 1) * A_GROUP):
                ps.append(jnp.exp2(t_scr[h] - new_m[h]).astype(BF16))
                alphas.append(jnp.exp2(ms[h] - new_m[h]))
            pv = jnp.dot(vt_ref[g, :, pl.ds(s0, kb_size)], jnp.concatenate(ps, axis=1),
                         preferred_element_type=F32)
            new_acc.append(jnp.concatenate(alphas, axis=1) * accs[g] + pv)
        return tuple(new_m), tuple(new_acc)

    init = (tuple(jnp.full((1, tq), NEG_INF, F32) for _ in range(A_HEADS)),
            tuple(jnp.zeros((vt_ref.shape[1], A_GROUP * tq), F32) for _ in range(A_KV_HEADS)))
    _, accs = lax.fori_loop(0, nkb, attend_block, init)
    for pr in range(n_pairs):
        rows = []
        for h in (2 * pr, 2 * pr + 1):
            g, hh = divmod(h, A_GROUP)
            cols = slice(hh * tq, (hh + 1) * tq)
            rows.append(accs[g][:A_HEAD_DIM, cols] / accs[g][A_HEAD_DIM:A_HEAD_DIM + 1, cols])
        o_ref[:, 2 * pr * A_HEAD_DIM:(2 * pr + 2) * A_HEAD_DIM] = (
            jnp.concatenate(rows, axis=0).T.astype(o_ref.dtype))


def _attend_t(iq, wi_t, q, ki, k, vt, qpos, lim, *, kb_size, topk, koff):
    tq = LANES
    _, bsz, nq, _ = q.shape
    s_pad = ki.shape[1]
    assert nq % tq == 0 and s_pad % kb_size == 0
    nt = nq // tq
    nkb = (jnp.max(lim.reshape(nt, tq), axis=1) + kb_size - 1) // kb_size
    heads = lambda n: pl.BlockSpec((n, None, tq, A_HEAD_DIM), lambda b, j, _: (0, b, j, 0))
    pos = pl.BlockSpec((1, tq), lambda b, j, _: (0, j))
    return pl.pallas_call(
        functools.partial(_attend_t_kernel, topk=topk, koff=koff, kb_size=kb_size,
                          slopes=_alibi_slopes_log2()),
        grid_spec=pltpu.PrefetchScalarGridSpec(
            num_scalar_prefetch=1,
            grid=(bsz, nt),
            in_specs=[pos, pos, heads(IDX_HEADS),
                      pl.BlockSpec((None, IDX_HEADS, tq), lambda b, j, _: (b, 0, j)),
                      heads(A_HEADS),
                      pl.BlockSpec((None, s_pad, IDX_DIM), lambda b, j, _: (b, 0, 0)),
                      pl.BlockSpec((A_KV_HEADS, None, s_pad, A_HEAD_DIM), lambda b, j, _: (0, b, 0, 0)),
                      pl.BlockSpec((A_KV_HEADS, None, VT_ROWS, s_pad), lambda b, j, _: (0, b, 0, 0))],
            out_specs=pl.BlockSpec((None, tq, A_Q), lambda b, j, _: (b, j, 0)),
            scratch_shapes=[pltpu.VMEM((s_pad, tq), I32), pltpu.VMEM((s_pad, tq), I16),
                            pltpu.VMEM((s_pad, tq), I16),
                            pltpu.VMEM((A_HEADS, kb_size, tq), F32)],
        ),
        out_shape=jax.ShapeDtypeStruct((bsz, nq, A_Q), BF16),
        compiler_params=pltpu.CompilerParams(
            dimension_semantics=("arbitrary", "arbitrary"), vmem_limit_bytes=VMEM_LIMIT),
        name="attend_t",
    )(nkb.astype(I32), qpos.reshape(1, nq), lim.reshape(1, nq), iq, wi_t, q, ki, k, vt)


def _hgrn2_kernel(bq_ref, bf_ref, bi_ref, lb_ref, tri_ref, s0_ref, o_ref, sout_ref,
                  st_scr, q_scr, k_scr, b_scr):
    c = pl.program_id(1)
    ct = bq_ref.shape[0]

    @pl.when(c == 0)
    def _():
        for h in range(B_HEADS):
            st_scr[h] = s0_ref[h].T

    lb = lb_ref[...]
    fx = bf_ref[...]
    logf = jnp.log(lb + (1.0 - lb) * jax.nn.sigmoid(fx))
    k_scr[...] = (1.0 - lb) * jax.nn.sigmoid(-fx)
    bq = bq_ref[...]
    q_scr[...] = bq * jax.nn.sigmoid(bq)
    b_scr[...] = jnp.dot(tri_ref[...], logf, precision=lax.Precision.HIGHEST,
                         preferred_element_type=F32)
    row = lax.broadcasted_iota(I32, (REC_BLOCK, 1), 0)

    def block(blk, carry):
        rows = pl.ds(pl.multiple_of(blk * REC_BLOCK, REC_BLOCK), REC_BLOCK)
        for h in range(B_HEADS):
            ls = slice(h * B_KEY_DIM, (h + 1) * B_KEY_DIM)
            b = b_scr[rows, ls]
            qb = q_scr[rows, ls]
            kb = k_scr[rows, ls]
            vb = bi_ref[rows, ls]
            b_end = b[REC_BLOCK - 1:REC_BLOCK]
            st = st_scr[h]
            o_rows = []
            for t in range(REC_BLOCK):
                dec = jnp.exp(jnp.minimum(b[t:t + 1] - b, 0.0))
                a = jnp.sum((qb[t:t + 1] * dec) * kb, axis=1, keepdims=True)
                a = jnp.where(row <= t, a, 0.0)
                o_rows.append(jnp.sum(a * vb, axis=0, keepdims=True))
            o_intra = jnp.concatenate(o_rows, axis=0)
            o_inter = lax.dot_general((qb * jnp.exp(b)).astype(BF16), st.astype(BF16),
                                      (((1,), (1,)), ((), ())), preferred_element_type=F32)
            o_ref[rows, ls] = o_intra + o_inter
            upd = lax.dot_general(vb.astype(BF16), (kb * jnp.exp(b_end - b)).astype(BF16),
                                  (((0,), (0,)), ((), ())), preferred_element_type=F32)
            st_scr[h] = st * jnp.exp(b_end) + upd
        return carry

    lax.fori_loop(0, ct // REC_BLOCK, block, 0)

    @pl.when(c == pl.num_programs(1) - 1)
    def _():
        for h in range(B_HEADS):
            sout_ref[h] = st_scr[h].T


def _hgrn2(bq, bf, bi, lb, s0, ct):
    bsz, t, _ = bq.shape
    assert t % ct == 0 and ct % REC_BLOCK == 0
    r = jnp.arange(ct)
    tri = ((r[:, None] >= r[None, :]) & (r[:, None] // REC_BLOCK == r[None, :] // REC_BLOCK)).astype(F32)
    tok = pl.BlockSpec((None, ct, B_KW), lambda b, c: (b, c, 0))
    if s0.shape[0] == 1:
        s_in = pl.BlockSpec((None, B_HEADS, B_KEY_DIM, B_VAL_DIM), lambda b, c: (0, 0, 0, 0))
    else:
        s_in = pl.BlockSpec((None, B_HEADS, B_KEY_DIM, B_VAL_DIM), lambda b, c: (b, 0, 0, 0))
    s_out = pl.BlockSpec((None, B_HEADS, B_KEY_DIM, B_VAL_DIM), lambda b, c: (b, 0, 0, 0))
    return pl.pallas_call(
        _hgrn2_kernel,
        grid=(bsz, t // ct),
        in_specs=[tok, tok, tok, _resident(lb.shape), _resident(tri.shape), s_in],
        out_specs=[tok, s_out],
        out_shape=[jax.ShapeDtypeStruct((bsz, t, B_WIDTH), F32),
                   jax.ShapeDtypeStruct((bsz, B_HEADS, B_KEY_DIM, B_VAL_DIM), F32)],
        scratch_shapes=[pltpu.VMEM((B_HEADS, B_VAL_DIM, B_KEY_DIM), F32)]
        + [pltpu.VMEM((ct, B_KW), F32)] * 3,
        compiler_params=pltpu.CompilerParams(
            dimension_semantics=("arbitrary", "arbitrary"), vmem_limit_bytes=VMEM_LIMIT),
        name="hgrn2",
    )(bq, bf, bi, lb, tri, s0)


def _mix_out_kernel(h_ref, oa_ref, orec_ref, bg_ref, ga_ref, gb_ref, gn_ref, wa_ref, wb_ref, wo_ref,
                    g2_ref, b2_ref, wg_ref, wu_ref, wd_ref, g3_ref, b3_ref, y_ref):
    orec = orec_ref[...]
    gn = gn_ref[...]
    parts = []
    for h in range(B_HEADS):
        ls = slice(h * B_VAL_DIM, (h + 1) * B_VAL_DIM)
        oh = orec[:, ls]
        ms = jnp.mean(oh * oh, axis=-1, keepdims=True)
        parts.append(oh * lax.rsqrt(ms + RMS_EPS) * gn[:, ls])
    on = jnp.concatenate(parts, axis=-1)
    bg = bg_ref[...]
    ob = on * (bg * jax.nn.sigmoid(bg))
    pa = jnp.dot(oa_ref[...], wa_ref[...], preferred_element_type=F32)
    pb = jnp.dot(ob.astype(BF16), wb_ref[...], preferred_element_type=F32)
    mixed = jax.nn.sigmoid(ga_ref[...]) * pa + jax.nn.sigmoid(gb_ref[...]) * pb
    mixed = jnp.dot(mixed.astype(BF16), wo_ref[...], preferred_element_type=F32)
    h2 = _layer_norm(ALPHA * h_ref[...] + mixed, g2_ref[...], b2_ref[...])
    y_ref[...] = _layer_norm(_swiglu_step(h2, wg_ref, wu_ref, wd_ref), g3_ref[...], b3_ref[...])


def _mix_out(h, oa, orec, bg, ga, gb, gn, wa, wb, wo, g2, b2, wg, wu, wd, g3, b3, tm):
    m = h.shape[0]
    assert m % tm == 0
    row = lambda w: pl.BlockSpec((tm, w), lambda i: (i, 0))
    weights = (gn, wa, wb, wo, g2, b2, wg, wu, wd, g3, b3)
    return pl.pallas_call(
        _mix_out_kernel,
        grid=(m // tm,),
        in_specs=[row(D_MODEL), row(A_Q), row(B_WIDTH), row(B_WIDTH), row(D_MODEL), row(D_MODEL)]
        + [_resident(w.shape) for w in weights],
        out_specs=row(D_MODEL),
        out_shape=jax.ShapeDtypeStruct((m, D_MODEL), F32),
        compiler_params=pltpu.CompilerParams(
            dimension_semantics=("arbitrary",), vmem_limit_bytes=VMEM_LIMIT),
        name="mix_out",
    )(h, oa, orec, bg, ga, gb, *weights)


def _ffn_weights(wg, wu, wd):
    wg = wg.astype(BF16).reshape(D_MODEL, N_FF_CHUNKS, FF_CHUNK).transpose(1, 0, 2)
    wu = wu.astype(BF16).reshape(D_MODEL, N_FF_CHUNKS, FF_CHUNK).transpose(1, 0, 2)
    wd = wd.astype(BF16).reshape(N_FF_CHUNKS, FF_CHUNK, D_MODEL)
    return wg, wu, wd


def _pad_rows(a, n):
    return jnp.pad(a, ((0, 0), (0, n - a.shape[1]), (0, 0)))


def _token_tile(m):
    for tm in (256, 128, 64, 32, 16, 8):
        if m % tm == 0:
            return tm
    raise ValueError(m)


def kernel(x_prompt, x_sample, cache_k, cache_v, cache_kidx, state_hgrn, meta, ln1_g, ln1_b, ffn1_wg, ffn1_wu, ffn1_wd, w_in, lb_param, g_norm, w_a_proj, w_b_proj, w_out, ln2_g, ln2_b, ffn2_wg, ffn2_wu, ffn2_wd, ln3_g, ln3_b):
    assert ln1_g.shape[0] == DEPTH
    bp, seq, _ = x_prompt.shape
    bs, dseq, _ = x_sample.shape
    past = cache_k.shape[2]
    l = 0
    vec = lambda a: a.reshape(1, -1).astype(F32)

    lb_all = jnp.cumsum(jax.nn.softmax(lb_param.astype(F32), axis=0), axis=0)
    lb = lb_all[l].reshape(1, B_KW)
    f1 = _ffn_weights(ffn1_wg[l], ffn1_wu[l], ffn1_wd[l])
    f2 = _ffn_weights(ffn2_wg[l], ffn2_wu[l], ffn2_wd[l])
    w_pieces, off = [], 0
    for _, width, _ in _PROJ_OUT:
        w_pieces.append(w_in[l][:, off:off + width].astype(BF16))
        off += width
    assert off == w_in.shape[-1]

    def stage1(x):
        return _ffn_in(x, vec(ln1_g[l]), vec(ln1_b[l]), *f1, w_pieces, _token_tile(x.shape[0]))

    def stage4(p, oa, orec):
        m = p["h"].shape[0]
        return _mix_out(p["h"], oa.reshape(m, A_Q), orec.reshape(m, B_WIDTH), p["bg"], p["ga"], p["gb"],
                        vec(g_norm[l]), w_a_proj[l].astype(BF16), w_b_proj[l].astype(BF16),
                        w_out[l].astype(BF16), vec(ln2_g[l]), vec(ln2_b[l]), *f2,
                        vec(ln3_g[l]), vec(ln3_b[l]), _token_tile(m))

    pm = stage1(meta.astype(F32))
    pp = stage1(x_prompt.reshape(bp * seq, D_MODEL))
    ps = stage1(x_sample.reshape(bs * dseq, D_MODEL))
    per_p = lambda a: a.reshape(bp, seq, a.shape[-1])
    per_s = lambda a: a.reshape(bs, dseq, a.shape[-1])

    def with_meta(name):
        rows = jnp.broadcast_to(pm[name][None], (bp, N_META, pm[name].shape[-1]))
        return jnp.concatenate([rows, per_p(pp[name])], axis=1)

    k_p, v_p, ki_p = with_meta("ak"), with_meta("av"), with_meta("ik")

    kb_p = 512
    s_p = -(-(N_META + seq) // kb_p) * kb_p
    qpos = jnp.arange(seq, dtype=I32)
    lim = N_META + CHUNK * (qpos // CHUNK + 1)
    heads_p = lambda a: a.reshape(a.shape[0], bp, seq, a.shape[-1])
    kv_heads = lambda a: _pad_rows(a.astype(BF16), s_p).reshape(bp, s_p, A_KV_HEADS, A_HEAD_DIM)
    vt = kv_heads(v_p).transpose(2, 0, 3, 1)
    vt = jnp.concatenate([vt, jnp.ones_like(vt[:, :, :1]),
                          jnp.zeros_like(vt[:, :, :VT_ROWS - A_HEAD_DIM - 1])], axis=2)
    oa_p = _attend_t(heads_p(pp["iq"]), per_p(pp["iw"]).transpose(0, 2, 1), heads_p(pp["aq"]),
                     _pad_rows(ki_p.astype(BF16), s_p), kv_heads(k_p).transpose(2, 0, 1, 3),
                     vt, qpos, lim,
                     kb_size=kb_p, topk=min(TOPK_MAX, seq // 4), koff=N_META)

    n_all = past + dseq
    kb_s = 384
    s_s = -(-n_all // kb_s) * kb_s
    k_all = jnp.concatenate([cache_k[l].reshape(bs, past, A_KV).astype(BF16), per_s(ps["ak"]).astype(BF16)], axis=1)
    v_all = jnp.concatenate([cache_v[l].reshape(bs, past, A_KV).astype(BF16), per_s(ps["av"]).astype(BF16)], axis=1)
    ki_all = jnp.concatenate([cache_kidx[l].astype(BF16), per_s(ps["ik"]).astype(BF16)], axis=1)
    tokens_s = lambda a: a.transpose(1, 0, 2).reshape(bs, dseq, a.shape[0] * a.shape[-1])
    oa_s = _attend(tokens_s(ps["iq"]), per_s(ps["iw"]), tokens_s(ps["aq"]),
                   _pad_rows(ki_all, s_s), _pad_rows(k_all, s_s), _pad_rows(v_all, s_s),
                   past + jnp.arange(dseq, dtype=I32), jnp.full((dseq,), n_all, I32),
                   tq=dseq, kb_size=kb_s, topk=min(TOPK_MAX, n_all // 4), koff=0)

    zero_state = jnp.zeros((1, B_HEADS, B_KEY_DIM, B_VAL_DIM), F32)
    _, st_m = _hgrn2(pm["bq"][None], pm["bf"][None], pm["bi"][None], lb, zero_state, N_META)
    orec_p, st_p = _hgrn2(per_p(pp["bq"]), per_p(pp["bf"]), per_p(pp["bi"]), lb, st_m, 128)
    orec_s, st_s = _hgrn2(per_s(ps["bq"]), per_s(ps["bf"]), per_s(ps["bi"]), lb,
                          state_hgrn[l].astype(F32), dseq)

    y_p = stage4(pp, oa_p, orec_p).reshape(bp, seq, D_MODEL)
    y_s = stage4(ps, oa_s, orec_s).reshape(bs, dseq, D_MODEL)

    kv5 = lambda a: a.reshape(1, a.shape[0], a.shape[1], A_KV_HEADS, A_HEAD_DIM)
    return (y_p, y_s, kv5(k_p), kv5(v_p), ki_p[None], st_p[None].astype(state_hgrn.dtype),
            kv5(per_s(ps["ak"])), kv5(per_s(ps["av"])), per_s(ps["ik"])[None],
            st_s[None].astype(state_hgrn.dtype))
```

```python
import functools

import jax
import jax.numpy as jnp
import numpy as np
from jax import lax
from jax.experimental import pallas as pl
from jax.experimental.pallas import tpu as pltpu

F32 = jnp.float32
BF16 = jnp.bfloat16
I32 = jnp.int32
I16 = jnp.int16

D_MODEL = 1024
D_FF = 2816
FF_CHUNK = 256
N_FF_CHUNKS = D_FF // FF_CHUNK
CHUNK = 64
N_META = 16
A_HEADS = 8
A_KV_HEADS = 2
A_GROUP = A_HEADS // A_KV_HEADS
A_HEAD_DIM = 64
A_Q = A_HEADS * A_HEAD_DIM
A_KV = A_KV_HEADS * A_HEAD_DIM
IDX_HEADS = 8
IDX_DIM = 64
IDX_Q = IDX_HEADS * IDX_DIM
TOPK_MAX = 256
B_HEADS = 4
B_KEY_DIM = 128
B_VAL_DIM = 128
B_KW = B_HEADS * B_KEY_DIM
B_WIDTH = B_HEADS * B_VAL_DIM
REC_BLOCK = 16
DEPTH = 1
ALPHA = (2.0 * DEPTH) ** 0.25
LN_EPS = 1e-5
RMS_EPS = 1e-6
NEG_INF = -1e30
LOG2E = 1.4426950408889634
INT_MIN = -(2 ** 31)
INT16_MIN = -(2 ** 15)
LANES = 128
VMEM_LIMIT = 56 * 1024 * 1024


def _resident(shape):
    n = len(shape)
    return pl.BlockSpec(shape, lambda *_: (0,) * n, pipeline_mode=pl.Buffered(1))


def _layer_norm(x, g, b):
    mu = jnp.mean(x, axis=-1, keepdims=True)
    xc = x - mu
    var = jnp.mean(xc * xc, axis=-1, keepdims=True)
    return xc * lax.rsqrt(var + LN_EPS) * g + b


def _swiglu_step(x, wg_ref, wu_ref, wd_ref):
    xb = x.astype(BF16)

    acc = jnp.zeros(x.shape, F32)
    for c in range(N_FF_CHUNKS):
        g = jnp.dot(xb, wg_ref[c], preferred_element_type=F32)
        u = jnp.dot(xb, wu_ref[c], preferred_element_type=F32)
        a = (g * jax.nn.sigmoid(g) * u).astype(BF16)
        acc = acc + jnp.dot(a, wd_ref[c], preferred_element_type=F32)
    return ALPHA * x + 0.5 * acc


_PROJ_OUT = (
    ("aq", A_Q, BF16), ("ak", A_KV, F32), ("av", A_KV, F32), ("iq", IDX_Q, BF16),
    ("ik", IDX_DIM, F32), ("iw", IDX_HEADS, F32),
    ("bq", B_KW, F32), ("bf", B_KW, F32), ("bi", B_WIDTH, F32),
    ("bg", B_WIDTH, F32), ("ga", D_MODEL, F32), ("gb", D_MODEL, F32),
)
_PROJ_SCALE = {"aq": (A_HEAD_DIM ** -0.5) * LOG2E, "iw": (IDX_HEADS ** -0.5) * (IDX_DIM ** -0.5)}
_HEAD_MAJOR = {"aq": A_HEADS, "iq": IDX_HEADS}


def _ffn_in_kernel(x_ref, g_ref, b_ref, wg_ref, wu_ref, wd_ref, *rest):
    n = len(_PROJ_OUT)
    w_refs, h_ref, out_refs = rest[:n], rest[n], rest[n + 1:]
    x = x_ref[...]
    h = _layer_norm(_swiglu_step(x, wg_ref, wu_ref, wd_ref), g_ref[...], b_ref[...])
    h_ref[...] = h
    hb = h.astype(BF16)
    for (name, _, dt), w_ref, o_ref in zip(_PROJ_OUT, w_refs, out_refs):
        y = jnp.dot(hb, w_ref[...], preferred_element_type=F32)
        if name in _PROJ_SCALE:
            y = y * _PROJ_SCALE[name]
        if name in _HEAD_MAJOR:
            hd = y.shape[1] // _HEAD_MAJOR[name]
            for h in range(_HEAD_MAJOR[name]):
                o_ref[h] = y[:, h * hd:(h + 1) * hd].astype(dt)
        else:
            o_ref[...] = y.astype(dt)


def _ffn_in(x, ln_g, ln_b, wg, wu, wd, w_pieces, tm):
    m = x.shape[0]
    assert m % tm == 0
    row = lambda w: pl.BlockSpec((tm, w), lambda i: (i, 0))
    in_specs = [row(D_MODEL), _resident(ln_g.shape), _resident(ln_b.shape),
                _resident(wg.shape), _resident(wu.shape), _resident(wd.shape)]
    in_specs += [_resident(w.shape) for w in w_pieces]
    out_shape = [jax.ShapeDtypeStruct((m, D_MODEL), F32)]
    out_specs = [row(D_MODEL)]
    for name, width, dt in _PROJ_OUT:
        if name in _HEAD_MAJOR:
            nh = _HEAD_MAJOR[name]
            out_shape.append(jax.ShapeDtypeStruct((nh, m, width // nh), dt))
            out_specs.append(pl.BlockSpec((nh, tm, width // nh), lambda i: (0, i, 0)))
        else:
            out_shape.append(jax.ShapeDtypeStruct((m, width), dt))
            out_specs.append(row(width))
    outs = pl.pallas_call(
        _ffn_in_kernel,
        grid=(m // tm,),
        in_specs=in_specs,
        out_specs=out_specs,
        out_shape=out_shape,
        compiler_params=pltpu.CompilerParams(
            dimension_semantics=("arbitrary",), vmem_limit_bytes=VMEM_LIMIT),
        name="ffn_in",
    )(x, ln_g, ln_b, wg, wu, wd, *w_pieces)
    res = {"h": outs[0]}
    for (name, _, _), o in zip(_PROJ_OUT, outs[1:]):
        res[name] = o
    return res


def _attend_kernel(nkb_ref, qpos_ref, lim_ref, iq_ref, wi_ref, q_ref, ki_ref, k_ref, v_ref,
                   o_ref, key_scr, t_scr, *, topk, koff, kb_size, slopes):
    tq = q_ref.shape[0]
    nkb = nkb_ref[pl.program_id(1)]
    lim = lim_ref[...]
    qpos = qpos_ref[...]
    sub = kb_size // LANES

    def lane_ids(s0):
        return s0 + lax.broadcasted_iota(I32, (tq, kb_size), 1)

    def score_block(kb, carry):
        s0 = pl.multiple_of(kb * kb_size, kb_size)
        kib = ki_ref[pl.ds(s0, kb_size), :]
        acc = jnp.zeros((tq, kb_size), F32)
        for h in range(IDX_HEADS):
            lg = lax.dot_general(iq_ref[:, h * IDX_DIM:(h + 1) * IDX_DIM], kib,
                                 (((1,), (1,)), ((), ())), preferred_element_type=F32)
            acc = acc + jnp.maximum(lg, 0.0) * wi_ref[:, h:h + 1]
        bits = lax.bitcast_convert_type(acc, I32)
        key = bits ^ ((bits >> 31) & 0x7FFFFFFF)
        key_scr[:, pl.ds(s0, kb_size)] = jnp.where(lane_ids(s0) < lim, key, INT_MIN)
        return carry

    lax.fori_loop(0, nkb, score_block, 0)

    def bcast(x):
        return jnp.broadcast_to(x, (tq, LANES))

    def count(pred):
        def body(kb, acc):
            for c in range(sub):
                off = pl.multiple_of(kb * kb_size + c * LANES, LANES)
                col = off + lax.broadcasted_iota(I32, (tq, LANES), 1)
                acc = acc + jnp.where(pred(key_scr[:, pl.ds(off, LANES)], col), 1.0, 0.0)
            return acc
        acc = lax.fori_loop(0, nkb, body, jnp.zeros((tq, LANES), F32))
        return jnp.sum(acc, axis=1, keepdims=True)

    def bit_step(state):
        i, t, settled, _ = state
        stop = jnp.min(settled)
        cand = t + lax.shift_left(jnp.int32(1), 31 - i)
        cand_b = bcast(cand)
        cnt = count(lambda key, col: key >= cand_b)
        take = (cnt >= topk) & (settled == 0.0)
        settled = jnp.where(take & (cnt == topk), 1.0, settled)
        return i + 1, jnp.where(take, cand, t), settled, stop

    _, t, _, _ = lax.while_loop(
        lambda state: (state[0] < 32) & (state[3] == 0.0), bit_step,
        (jnp.int32(0), jnp.full((tq, 1), INT_MIN, I32), (lim <= topk).astype(F32), jnp.float32(0.0)))
    thr = jnp.maximum(t, INT_MIN + 1)
    thr_b = bcast(thr)
    n_ge = count(lambda key, col: key >= thr_b)

    @pl.when(jnp.max(n_ge) > topk)
    def _():
        n_gt = count(lambda key, col: key > thr_b)
        need = topk - n_gt
        n_bits = (key_scr.shape[1] - 1).bit_length()

        def idx_step(i, m0):
            cand = m0 + lax.shift_left(jnp.int32(1), n_bits - 1 - i)
            cand_b = bcast(cand)
            g = count(lambda key, col: (key == thr_b) & (col < cand_b))
            return jnp.where(g < need, cand, m0)

        m0 = lax.fori_loop(0, n_bits, idx_step, jnp.zeros((tq, 1), I32))
        drop_row = n_ge > topk

        def demote(kb, carry):
            s0 = pl.multiple_of(kb * kb_size, kb_size)
            key = key_scr[:, pl.ds(s0, kb_size)]
            drop = drop_row & (key == thr) & (lane_ids(s0) > m0)
            key_scr[:, pl.ds(s0, kb_size)] = jnp.where(drop, INT_MIN, key)
            return carry

        lax.fori_loop(0, nkb, demote, 0)

    def attend_block(kb, carry):
        ms, ls, accs = carry
        s0 = pl.multiple_of(kb * kb_size, kb_size)
        col = lane_ids(s0)
        bias = jnp.where(key_scr[:, pl.ds(s0, kb_size)] >= thr, 0.0, NEG_INF)
        dist = jnp.abs(qpos - (col - koff)).astype(F32)
        kblk = k_ref[pl.ds(s0, kb_size), :]
        vblk = v_ref[pl.ds(s0, kb_size), :]
        new_m, new_l, new_acc = [], [], []
        for h in range(A_HEADS):
            g = h // A_GROUP
            s = lax.dot_general(q_ref[:, h * A_HEAD_DIM:(h + 1) * A_HEAD_DIM],
                                kblk[:, g * A_HEAD_DIM:(g + 1) * A_HEAD_DIM],
                                (((1,), (1,)), ((), ())), preferred_element_type=F32)
            s = (s - slopes[h] * dist) + bias
            t_scr[h] = s
            new_m.append(jnp.maximum(ms[h], jnp.max(s, axis=1, keepdims=True)))
        for h in range(A_HEADS):
            g = h // A_GROUP
            p = jnp.exp2(t_scr[h] - new_m[h])
            alpha = jnp.exp2(ms[h] - new_m[h])
            new_l.append(alpha * ls[h] + jnp.sum(p, axis=1, keepdims=True))
            pv = jnp.dot(p.astype(BF16), vblk[:, g * A_HEAD_DIM:(g + 1) * A_HEAD_DIM],
                         preferred_element_type=F32)
            new_acc.append(alpha * accs[h] + pv)
        return tuple(new_m), tuple(new_l), tuple(new_acc)

    init = (tuple(jnp.full((tq, 1), NEG_INF, F32) for _ in range(A_HEADS)),
            tuple(jnp.zeros((tq, 1), F32) for _ in range(A_HEADS)),
            tuple(jnp.zeros((tq, A_HEAD_DIM), F32) for _ in range(A_HEADS)))
    _, ls, accs = lax.fori_loop(0, nkb, attend_block, init)
    for h in range(A_HEADS):
        o_ref[:, h * A_HEAD_DIM:(h + 1) * A_HEAD_DIM] = (accs[h] / ls[h]).astype(o_ref.dtype)


def _attend(iq, wi, q, ki, k, v, qpos, lim, *, tq, kb_size, topk, koff):
    bsz, nq, _ = q.shape
    s_pad = k.shape[1]
    assert nq % tq == 0 and s_pad % kb_size == 0
    nt = nq // tq
    nkb = (jnp.max(lim.reshape(nt, tq), axis=1) + kb_size - 1) // kb_size
    slopes = _alibi_slopes_log2()
    qrow = lambda w: pl.BlockSpec((None, tq, w), lambda b, j, n: (b, j, 0))
    krow = lambda w: pl.BlockSpec((None, s_pad, w), lambda b, j, n: (b, 0, 0))
    pos = pl.BlockSpec((tq, 1), lambda b, j, n: (j, 0))
    return pl.pallas_call(
        functools.partial(_attend_kernel, topk=topk, koff=koff, kb_size=kb_size, slopes=slopes),
        grid_spec=pltpu.PrefetchScalarGridSpec(
            num_scalar_prefetch=1,
            grid=(bsz, nt),
            in_specs=[pos, pos, qrow(IDX_Q), qrow(IDX_HEADS), qrow(A_Q),
                      krow(IDX_DIM), krow(A_KV), krow(A_KV)],
            out_specs=qrow(A_Q),
            scratch_shapes=[pltpu.VMEM((tq, s_pad), I32),
                            pltpu.VMEM((A_HEADS, tq, kb_size), F32)],
        ),
        out_shape=jax.ShapeDtypeStruct((bsz, nq, A_Q), BF16),
        compiler_params=pltpu.CompilerParams(
            dimension_semantics=("arbitrary", "arbitrary"), vmem_limit_bytes=VMEM_LIMIT),
        name="attend",
    )(nkb.astype(I32), qpos.reshape(nq, 1), lim.reshape(nq, 1), iq, wi, q, ki, k, v)


def _fold_rows(x, op, rows=8):
    while x.shape[0] > rows:
        half = x.shape[0] // 2
        x = op(x[:half], x[half:])
    return x


VT_ROWS = A_HEAD_DIM + 16


def _alibi_slopes_log2():
    return tuple(float(2.0 ** (-8.0 * (i + 1) / A_HEADS)) * LOG2E for i in range(A_HEADS))


def _attend_t_kernel(qpos_ref, lim_ref, iq_ref, wi_ref, q_ref, ki_ref, k_ref, vt_ref,
                     o_ref, key_scr, hi_scr, lo_scr, t_scr, *, nkb, topk, koff, kb_size, slopes):
    tq = o_ref.shape[0]
    lim = lim_ref[...]
    qpos = qpos_ref[...]
    n_pairs = A_HEADS // 2
    blocks = [(kb * kb_size, pl.ds(kb * kb_size, kb_size)) for kb in range(nkb)]

    def row_ids(s0):
        return s0 + lax.broadcasted_iota(I32, (kb_size, tq), 0)

    def head_pair(ref, pr):
        return ref[2 * pr:2 * pr + 2].reshape(2 * tq, ref.shape[-1])

    for s0, rows in blocks:
        kib = ki_ref[rows, :]
        acc = jnp.zeros((kb_size, tq), F32)
        for pr in range(IDX_HEADS // 2):
            lg = lax.dot_general(kib, head_pair(iq_ref, pr), (((1,), (1,)), ((), ())),
                                 preferred_element_type=F32)
            for u in range(2):
                h = 2 * pr + u
                acc = acc + jnp.maximum(lg[:, u * tq:(u + 1) * tq], 0.0) * wi_ref[h:h + 1, :]
        bits = lax.bitcast_convert_type(acc, I32)
        key = bits ^ ((bits >> 31) & 0x7FFFFFFF)
        key = jnp.where(row_ids(s0) < lim, key, INT_MIN)
        key_scr[rows, :] = key
        hi_scr[rows, :] = (key >> 16).astype(I16)

    def count(pred):
        acc = jnp.zeros((8, tq), F32)
        for _, rows in blocks:
            acc = acc + _fold_rows(jnp.where(pred(key_scr[rows, :]), 1.0, 0.0), jnp.add)
        return jnp.sum(acc, axis=0, keepdims=True)

    def count16(scr, pred):
        acc = jnp.zeros((16, tq), I16)
        for _, rows in blocks:
            hit = jnp.where(pred(scr[rows, :]), jnp.int16(1), jnp.int16(0))
            acc = acc + _fold_rows(hit, jnp.add, 16)
        return jnp.sum(acc.astype(F32), axis=0, keepdims=True)

    def kth_largest16(scr, kth):
        def bit_step(i, t):
            cand = t + lax.shift_left(jnp.int32(1), 15 - i)
            cand16 = cand.astype(I16)
            return jnp.where(count16(scr, lambda blk: blk >= cand16) >= kth, cand, t)
        return lax.fori_loop(0, 16, bit_step, jnp.full((1, tq), INT16_MIN, I32))

    t_hi = kth_largest16(hi_scr, topk)
    t_hi16 = t_hi.astype(I16)
    kth_lo = topk - count16(hi_scr, lambda blk: blk > t_hi16)
    for _, rows in blocks:
        key = key_scr[rows, :]
        lo = jnp.where((key >> 16) == t_hi, (key & 0xFFFF) + INT16_MIN, INT16_MIN)
        lo_scr[rows, :] = lo.astype(I16)
    t_lo = kth_largest16(lo_scr, kth_lo)
    t = lax.shift_left(t_hi, 16) | (t_lo - INT16_MIN)
    thr = jnp.maximum(t, INT_MIN + 1)
    n_ge = count(lambda key: key >= thr)

    @pl.when(jnp.max(n_ge) > topk)
    def _():
        need = topk - count(lambda key: key > thr)
        for s0, rows in blocks:
            lo_scr[rows, :] = jnp.where(key_scr[rows, :] == thr, -row_ids(s0), INT16_MIN).astype(I16)
        last = kth_largest16(lo_scr, need)
        drop_q = n_ge > topk
        for s0, rows in blocks:
            key = key_scr[rows, :]
            drop = drop_q & (key == thr) & (-row_ids(s0) < last)
            key_scr[rows, :] = jnp.where(drop, INT_MIN, key)

    ms = [jnp.full((1, tq), NEG_INF, F32) for _ in range(A_HEADS)]
    accs = [jnp.zeros((vt_ref.shape[1], A_GROUP * tq), F32) for _ in range(A_KV_HEADS)]
    for s0, rows in blocks:
        bias = jnp.where(key_scr[rows, :] >= thr, 0.0, NEG_INF)
        dist = jnp.abs(qpos - (row_ids(s0) - koff)).astype(F32)
        new_m = []
        for pr in range(n_pairs):
            g = (2 * pr) // A_GROUP
            s2 = lax.dot_general(k_ref[g, rows, :], head_pair(q_ref, pr),
                                 (((1,), (1,)), ((), ())), preferred_element_type=F32)
            for u in range(2):
                h = 2 * pr + u
                s = (s2[:, u * tq:(u + 1) * tq] - slopes[h] * dist) + bias
                t_scr[h] = s
                blk_max = jnp.max(_fold_rows(s, jnp.maximum), axis=0, keepdims=True)
                new_m.append(jnp.maximum(ms[h], blk_max))
        for g in range(A_KV_HEADS):
            ps, alphas = [], []
            for h in range(g * A_GROUP, (g + 1) * A_GROUP):
                ps.append(jnp.exp2(t_scr[h] - new_m[h]).astype(BF16))
                alphas.append(jnp.exp2(ms[h] - new_m[h]))
            pv = jnp.dot(vt_ref[g, :, rows], jnp.concatenate(ps, axis=1),
                         preferred_element_type=F32)
            accs[g] = jnp.concatenate(alphas, axis=1) * accs[g] + pv
        ms = new_m
    for pr in range(n_pairs):
        rows = []
        for h in (2 * pr, 2 * pr + 1):
            g, hh = divmod(h, A_GROUP)
            cols = slice(hh * tq, (hh + 1) * tq)
            rows.append(accs[g][:A_HEAD_DIM, cols] / accs[g][A_HEAD_DIM:A_HEAD_DIM + 1, cols])
        o_ref[:, 2 * pr * A_HEAD_DIM:(2 * pr + 2) * A_HEAD_DIM] = (
            jnp.concatenate(rows, axis=0).T.astype(o_ref.dtype))


def _attend_t(iq, wi_t, q, ki, k, vt, qpos, lim, *, kb_size, topk, koff):
    tq = LANES
    _, bsz, nq, _ = q.shape
    s_pad = ki.shape[1]
    assert nq % tq == 0 and s_pad % kb_size == 0
    nkb_of_tile = -(-lim.reshape(nq // tq, tq).max(axis=1) // kb_size)
    runs, j0 = [], 0
    for j in range(1, len(nkb_of_tile) + 1):
        if j == len(nkb_of_tile) or nkb_of_tile[j] != nkb_of_tile[j0]:
            runs.append((j0, j, int(nkb_of_tile[j0])))
            j0 = j
    qpos_row = jnp.asarray(qpos, I32).reshape(1, nq)
    lim_row = jnp.asarray(lim, I32).reshape(1, nq)
    outs = []
    for j0, j1, nkb in runs:
        heads = lambda n: pl.BlockSpec((n, None, tq, A_HEAD_DIM), lambda b, j: (0, b, j + j0, 0))
        pos = pl.BlockSpec((1, tq), lambda b, j: (0, j + j0))
        outs.append(pl.pallas_call(
            functools.partial(_attend_t_kernel, nkb=nkb, topk=topk, koff=koff, kb_size=kb_size,
                              slopes=_alibi_slopes_log2()),
            grid=(bsz, j1 - j0),
            in_specs=[pos, pos, heads(IDX_HEADS),
                      pl.BlockSpec((None, IDX_HEADS, tq), lambda b, j: (b, 0, j + j0)),
                      heads(A_HEADS),
                      pl.BlockSpec((None, s_pad, IDX_DIM), lambda b, j: (b, 0, 0)),
                      pl.BlockSpec((A_KV_HEADS, None, s_pad, A_HEAD_DIM), lambda b, j: (0, b, 0, 0)),
                      pl.BlockSpec((A_KV_HEADS, None, VT_ROWS, s_pad), lambda b, j: (0, b, 0, 0))],
            out_specs=pl.BlockSpec((None, tq, A_Q), lambda b, j: (b, j, 0)),
            scratch_shapes=[pltpu.VMEM((s_pad, tq), I32), pltpu.VMEM((s_pad, tq), I16),
                            pltpu.VMEM((s_pad, tq), I16),
                            pltpu.VMEM((A_HEADS, kb_size, tq), F32)],
            out_shape=jax.ShapeDtypeStruct((bsz, (j1 - j0) * tq, A_Q), BF16),
            compiler_params=pltpu.CompilerParams(
                dimension_semantics=("arbitrary", "arbitrary"), vmem_limit_bytes=VMEM_LIMIT),
            name=f"attend_t{nkb}",
        )(qpos_row, lim_row, iq, wi_t, q, ki, k, vt))
    return jnp.concatenate(outs, axis=1)


def _hgrn2_kernel(bq_ref, bf_ref, bi_ref, lb_ref, tri_ref, s0_ref, o_ref, sout_ref,
                  st_scr, q_scr, k_scr, b_scr):
    c = pl.program_id(1)
    ct = bq_ref.shape[0]

    @pl.when(c == 0)
    def _():
        for h in range(B_HEADS):
            st_scr[h] = s0_ref[h].T

    lb = lb_ref[...]
    fx = bf_ref[...]
    logf = jnp.log(lb + (1.0 - lb) * jax.nn.sigmoid(fx))
    k_scr[...] = (1.0 - lb) * jax.nn.sigmoid(-fx)
    bq = bq_ref[...]
    q_scr[...] = bq * jax.nn.sigmoid(bq)
    b_scr[...] = jnp.dot(tri_ref[...], logf, precision=lax.Precision.HIGHEST,
                         preferred_element_type=F32)
    row = lax.broadcasted_iota(I32, (REC_BLOCK, 1), 0)

    def block(blk, carry):
        rows = pl.ds(pl.multiple_of(blk * REC_BLOCK, REC_BLOCK), REC_BLOCK)
        for h in range(B_HEADS):
            ls = slice(h * B_KEY_DIM, (h + 1) * B_KEY_DIM)
            b = b_scr[rows, ls]
            qb = q_scr[rows, ls]
            kb = k_scr[rows, ls]
            vb = bi_ref[rows, ls]
            b_end = b[REC_BLOCK - 1:REC_BLOCK]
            st = st_scr[h]
            o_rows = []
            for t in range(REC_BLOCK):
                dec = jnp.exp(jnp.minimum(b[t:t + 1] - b, 0.0))
                a = jnp.sum((qb[t:t + 1] * dec) * kb, axis=1, keepdims=True)
                a = jnp.where(row <= t, a, 0.0)
                o_rows.append(jnp.sum(a * vb, axis=0, keepdims=True))
            o_intra = jnp.concatenate(o_rows, axis=0)
            o_inter = lax.dot_general((qb * jnp.exp(b)).astype(BF16), st.astype(BF16),
                                      (((1,), (1,)), ((), ())), preferred_element_type=F32)
            o_ref[rows, ls] = o_intra + o_inter
            upd = lax.dot_general(vb.astype(BF16), (kb * jnp.exp(b_end - b)).astype(BF16),
                                  (((0,), (0,)), ((), ())), preferred_element_type=F32)
            st_scr[h] = st * jnp.exp(b_end) + upd
        return carry

    lax.fori_loop(0, ct // REC_BLOCK, block, 0)

    @pl.when(c == pl.num_programs(1) - 1)
    def _():
        for h in range(B_HEADS):
            sout_ref[h] = st_scr[h].T


def _hgrn2(bq, bf, bi, lb, s0, ct):
    bsz, t, _ = bq.shape
    assert t % ct == 0 and ct % REC_BLOCK == 0
    r = jnp.arange(ct)
    tri = ((r[:, None] >= r[None, :]) & (r[:, None] // REC_BLOCK == r[None, :] // REC_BLOCK)).astype(F32)
    tok = pl.BlockSpec((None, ct, B_KW), lambda b, c: (b, c, 0))
    if s0.shape[0] == 1:
        s_in = pl.BlockSpec((None, B_HEADS, B_KEY_DIM, B_VAL_DIM), lambda b, c: (0, 0, 0, 0))
    else:
        s_in = pl.BlockSpec((None, B_HEADS, B_KEY_DIM, B_VAL_DIM), lambda b, c: (b, 0, 0, 0))
    s_out = pl.BlockSpec((None, B_HEADS, B_KEY_DIM, B_VAL_DIM), lambda b, c: (b, 0, 0, 0))
    return pl.pallas_call(
        _hgrn2_kernel,
        grid=(bsz, t // ct),
        in_specs=[tok, tok, tok, _resident(lb.shape), _resident(tri.shape), s_in],
        out_specs=[tok, s_out],
        out_shape=[jax.ShapeDtypeStruct((bsz, t, B_WIDTH), F32),
                   jax.ShapeDtypeStruct((bsz, B_HEADS, B_KEY_DIM, B_VAL_DIM), F32)],
        scratch_shapes=[pltpu.VMEM((B_HEADS, B_VAL_DIM, B_KEY_DIM), F32)]
        + [pltpu.VMEM((ct, B_KW), F32)] * 3,
        compiler_params=pltpu.CompilerParams(
            dimension_semantics=("arbitrary", "arbitrary"), vmem_limit_bytes=VMEM_LIMIT),
        name="hgrn2",
    )(bq, bf, bi, lb, tri, s0)


def _mix_out_kernel(h_ref, oa_ref, orec_ref, bg_ref, ga_ref, gb_ref, gn_ref, wa_ref, wb_ref, wo_ref,
                    g2_ref, b2_ref, wg_ref, wu_ref, wd_ref, g3_ref, b3_ref, y_ref):
    orec = orec_ref[...]
    gn = gn_ref[...]
    parts = []
    for h in range(B_HEADS):
        ls = slice(h * B_VAL_DIM, (h + 1) * B_VAL_DIM)
        oh = orec[:, ls]
        ms = jnp.mean(oh * oh, axis=-1, keepdims=True)
        parts.append(oh * lax.rsqrt(ms + RMS_EPS) * gn[:, ls])
    on = jnp.concatenate(parts, axis=-1)
    bg = bg_ref[...]
    ob = on * (bg * jax.nn.sigmoid(bg))
    pa = jnp.dot(oa_ref[...], wa_ref[...], preferred_element_type=F32)
    pb = jnp.dot(ob.astype(BF16), wb_ref[...], preferred_element_type=F32)
    mixed = jax.nn.sigmoid(ga_ref[...]) * pa + jax.nn.sigmoid(gb_ref[...]) * pb
    mixed = jnp.dot(mixed.astype(BF16), wo_ref[...], preferred_element_type=F32)
    h2 = _layer_norm(ALPHA * h_ref[...] + mixed, g2_ref[...], b2_ref[...])
    y_ref[...] = _layer_norm(_swiglu_step(h2, wg_ref, wu_ref, wd_ref), g3_ref[...], b3_ref[...])


def _mix_out(h, oa, orec, bg, ga, gb, gn, wa, wb, wo, g2, b2, wg, wu, wd, g3, b3, tm):
    m = h.shape[0]
    assert m % tm == 0
    row = lambda w: pl.BlockSpec((tm, w), lambda i: (i, 0))
    weights = (gn, wa, wb, wo, g2, b2, wg, wu, wd, g3, b3)
    return pl.pallas_call(
        _mix_out_kernel,
        grid=(m // tm,),
        in_specs=[row(D_MODEL), row(A_Q), row(B_WIDTH), row(B_WIDTH), row(D_MODEL), row(D_MODEL)]
        + [_resident(w.shape) for w in weights],
        out_specs=row(D_MODEL),
        out_shape=jax.ShapeDtypeStruct((m, D_MODEL), F32),
        compiler_params=pltpu.CompilerParams(
            dimension_semantics=("arbitrary",), vmem_limit_bytes=VMEM_LIMIT),
        name="mix_out",
    )(h, oa, orec, bg, ga, gb, *weights)


def _ffn_weights(wg, wu, wd):
    wg = wg.astype(BF16).reshape(D_MODEL, N_FF_CHUNKS, FF_CHUNK).transpose(1, 0, 2)
    wu = wu.astype(BF16).reshape(D_MODEL, N_FF_CHUNKS, FF_CHUNK).transpose(1, 0, 2)
    wd = wd.astype(BF16).reshape(N_FF_CHUNKS, FF_CHUNK, D_MODEL)
    return wg, wu, wd


def _pad_rows(a, n):
    return jnp.pad(a, ((0, 0), (0, n - a.shape[1]), (0, 0)))


def _token_tile(m):
    for tm in (256, 128, 64, 32, 16, 8):
        if m % tm == 0:
            return tm
    raise ValueError(m)


def kernel(x_prompt, x_sample, cache_k, cache_v, cache_kidx, state_hgrn, meta, ln1_g, ln1_b, ffn1_wg, ffn1_wu, ffn1_wd, w_in, lb_param, g_norm, w_a_proj, w_b_proj, w_out, ln2_g, ln2_b, ffn2_wg, ffn2_wu, ffn2_wd, ln3_g, ln3_b):
    assert ln1_g.shape[0] == DEPTH
    bp, seq, _ = x_prompt.shape
    bs, dseq, _ = x_sample.shape
    past = cache_k.shape[2]
    l = 0
    vec = lambda a: a.reshape(1, -1).astype(F32)

    lb_all = jnp.cumsum(jax.nn.softmax(lb_param.astype(F32), axis=0), axis=0)
    lb = lb_all[l].reshape(1, B_KW)
    f1 = _ffn_weights(ffn1_wg[l], ffn1_wu[l], ffn1_wd[l])
    f2 = _ffn_weights(ffn2_wg[l], ffn2_wu[l], ffn2_wd[l])
    w_pieces, off = [], 0
    for _, width, _ in _PROJ_OUT:
        w_pieces.append(w_in[l][:, off:off + width].astype(BF16))
        off += width
    assert off == w_in.shape[-1]

    def stage1(x):
        return _ffn_in(x, vec(ln1_g[l]), vec(ln1_b[l]), *f1, w_pieces, _token_tile(x.shape[0]))

    def stage4(p, oa, orec):
        m = p["h"].shape[0]
        return _mix_out(p["h"], oa.reshape(m, A_Q), orec.reshape(m, B_WIDTH), p["bg"], p["ga"], p["gb"],
                        vec(g_norm[l]), w_a_proj[l].astype(BF16), w_b_proj[l].astype(BF16),
                        w_out[l].astype(BF16), vec(ln2_g[l]), vec(ln2_b[l]), *f2,
                        vec(ln3_g[l]), vec(ln3_b[l]), _token_tile(m))

    pm = stage1(meta.astype(F32))
    pp = stage1(x_prompt.reshape(bp * seq, D_MODEL))
    ps = stage1(x_sample.reshape(bs * dseq, D_MODEL))
    per_p = lambda a: a.reshape(bp, seq, a.shape[-1])
    per_s = lambda a: a.reshape(bs, dseq, a.shape[-1])

    def with_meta(name):
        rows = jnp.broadcast_to(pm[name][None], (bp, N_META, pm[name].shape[-1]))
        return jnp.concatenate([rows, per_p(pp[name])], axis=1)

    k_p, v_p, ki_p = with_meta("ak"), with_meta("av"), with_meta("ik")

    kb_p = 512
    s_p = -(-(N_META + seq) // kb_p) * kb_p
    qpos = np.arange(seq, dtype=np.int32)
    lim = N_META + CHUNK * (qpos // CHUNK + 1)
    heads_p = lambda a: a.reshape(a.shape[0], bp, seq, a.shape[-1])
    kv_heads = lambda a: _pad_rows(a.astype(BF16), s_p).reshape(bp, s_p, A_KV_HEADS, A_HEAD_DIM)
    vt = kv_heads(v_p).transpose(2, 0, 3, 1)
    vt = jnp.concatenate([vt, jnp.ones_like(vt[:, :, :1]),
                          jnp.zeros_like(vt[:, :, :VT_ROWS - A_HEAD_DIM - 1])], axis=2)
    oa_p = _attend_t(heads_p(pp["iq"]), per_p(pp["iw"]).transpose(0, 2, 1), heads_p(pp["aq"]),
                     _pad_rows(ki_p.astype(BF16), s_p), kv_heads(k_p).transpose(2, 0, 1, 3),
                     vt, qpos, lim,
                     kb_size=kb_p, topk=min(TOPK_MAX, seq // 4), koff=N_META)

    n_all = past + dseq
    kb_s = 384
    s_s = -(-n_all // kb_s) * kb_s
    k_all = jnp.concatenate([cache_k[l].reshape(bs, past, A_KV).astype(BF16), per_s(ps["ak"]).astype(BF16)], axis=1)
    v_all = jnp.concatenate([cache_v[l].reshape(bs, past, A_KV).astype(BF16), per_s(ps["av"]).astype(BF16)], axis=1)
    ki_all = jnp.concatenate([cache_kidx[l].astype(BF16), per_s(ps["ik"]).astype(BF16)], axis=1)
    tokens_s = lambda a: a.transpose(1, 0, 2).reshape(bs, dseq, a.shape[0] * a.shape[-1])
    oa_s = _attend(tokens_s(ps["iq"]), per_s(ps["iw"]), tokens_s(ps["aq"]),
                   _pad_rows(ki_all, s_s), _pad_rows(k_all, s_s), _pad_rows(v_all, s_s),
                   past + jnp.arange(dseq, dtype=I32), jnp.full((dseq,), n_all, I32),
                   tq=dseq, kb_size=kb_s, topk=min(TOPK_MAX, n_all // 4), koff=0)

    zero_state = jnp.zeros((1, B_HEADS, B_KEY_DIM, B_VAL_DIM), F32)
    _, st_m = _hgrn2(pm["bq"][None], pm["bf"][None], pm["bi"][None], lb, zero_state, N_META)
    orec_p, st_p = _hgrn2(per_p(pp["bq"]), per_p(pp["bf"]), per_p(pp["bi"]), lb, st_m, 128)
    orec_s, st_s = _hgrn2(per_s(ps["bq"]), per_s(ps["bf"]), per_s(ps["bi"]), lb,
                          state_hgrn[l].astype(F32), dseq)

    y_p = stage4(pp, oa_p, orec_p).reshape(bp, seq, D_MODEL)
    y_s = stage4(ps, oa_s, orec_s).reshape(bs, dseq, D_MODEL)

    kv5 = lambda a: a.reshape(1, a.shape[0], a.shape[1], A_KV_HEADS, A_HEAD_DIM)
    return (y_p, y_s, kv5(k_p), kv5(v_p), ki_p[None], st_p[None].astype(state_hgrn.dtype),
            kv5(per_s(ps["ak"])), kv5(per_s(ps["av"])), per_s(ps["ik"])[None],
            st_s[None].astype(state_hgrn.dtype))
```

```python
import functools

import jax
import jax.numpy as jnp
import numpy as np
from jax import lax
from jax.experimental import pallas as pl
from jax.experimental.pallas import tpu as pltpu

F32 = jnp.float32
BF16 = jnp.bfloat16
I32 = jnp.int32
I16 = jnp.int16

D_MODEL = 1024
D_FF = 2816
FF_CHUNK = 256
N_FF_CHUNKS = D_FF // FF_CHUNK
CHUNK = 64
N_META = 16
A_HEADS = 8
A_KV_HEADS = 2
A_GROUP = A_HEADS // A_KV_HEADS
A_HEAD_DIM = 64
A_Q = A_HEADS * A_HEAD_DIM
A_KV = A_KV_HEADS * A_HEAD_DIM
IDX_HEADS = 8
IDX_DIM = 64
IDX_Q = IDX_HEADS * IDX_DIM
TOPK_MAX = 256
B_HEADS = 4
B_KEY_DIM = 128
B_VAL_DIM = 128
B_KW = B_HEADS * B_KEY_DIM
B_WIDTH = B_HEADS * B_VAL_DIM
DEPTH = 1
ALPHA = (2.0 * DEPTH) ** 0.25
LN_EPS = 1e-5
RMS_EPS = 1e-6
NEG_INF = -1e30
LOG2E = 1.4426950408889634
INT_MIN = -(2 ** 31)
INT16_MIN = -(2 ** 15)
LANES = 128
VMEM_LIMIT = 56 * 1024 * 1024


def _resident(shape):
    n = len(shape)
    return pl.BlockSpec(shape, lambda *_: (0,) * n, pipeline_mode=pl.Buffered(1))


def _layer_norm(x, g, b):
    mu = jnp.mean(x, axis=-1, keepdims=True)
    xc = x - mu
    var = jnp.mean(xc * xc, axis=-1, keepdims=True)
    return xc * lax.rsqrt(var + LN_EPS) * g + b


def _swiglu_step(x, wg_ref, wu_ref, wd_ref):
    xb = x.astype(BF16)

    acc = jnp.zeros(x.shape, F32)
    for c in range(N_FF_CHUNKS):
        g = jnp.dot(xb, wg_ref[c], preferred_element_type=F32)
        u = jnp.dot(xb, wu_ref[c], preferred_element_type=F32)
        a = (g * jax.nn.sigmoid(g) * u).astype(BF16)
        acc = acc + jnp.dot(a, wd_ref[c], preferred_element_type=F32)
    return ALPHA * x + 0.5 * acc


_PROJ_OUT = (
    ("aq", A_Q, BF16), ("ak", A_KV, F32), ("av", A_KV, F32), ("iq", IDX_Q, BF16),
    ("ik", IDX_DIM, F32), ("iw", IDX_HEADS, F32),
    ("bq", B_KW, F32), ("bf", B_KW, F32), ("bi", B_WIDTH, F32),
    ("bg", B_WIDTH, F32), ("ga", D_MODEL, F32), ("gb", D_MODEL, F32),
)
_PROJ_SCALE = {"aq": (A_HEAD_DIM ** -0.5) * LOG2E, "iw": (IDX_HEADS ** -0.5) * (IDX_DIM ** -0.5)}
_HEAD_MAJOR = {"aq": A_HEADS, "iq": IDX_HEADS}


def _ffn_in_kernel(x_ref, g_ref, b_ref, wg_ref, wu_ref, wd_ref, *rest):
    n = len(_PROJ_OUT)
    w_refs, h_ref, out_refs = rest[:n], rest[n], rest[n + 1:]
    x = x_ref[...]
    h = _layer_norm(_swiglu_step(x, wg_ref, wu_ref, wd_ref), g_ref[...], b_ref[...])
    h_ref[...] = h
    hb = h.astype(BF16)
    for (name, _, dt), w_ref, o_ref in zip(_PROJ_OUT, w_refs, out_refs):
        y = jnp.dot(hb, w_ref[...], preferred_element_type=F32)
        if name in _PROJ_SCALE:
            y = y * _PROJ_SCALE[name]
        if name in _HEAD_MAJOR:
            hd = y.shape[1] // _HEAD_MAJOR[name]
            for h in range(_HEAD_MAJOR[name]):
                o_ref[h] = y[:, h * hd:(h + 1) * hd].astype(dt)
        else:
            o_ref[...] = y.astype(dt)


def _ffn_in(x, ln_g, ln_b, wg, wu, wd, w_pieces, tm):
    m = x.shape[0]
    assert m % tm == 0
    row = lambda w: pl.BlockSpec((tm, w), lambda i: (i, 0))
    in_specs = [row(D_MODEL), _resident(ln_g.shape), _resident(ln_b.shape),
                _resident(wg.shape), _resident(wu.shape), _resident(wd.shape)]
    in_specs += [_resident(w.shape) for w in w_pieces]
    out_shape = [jax.ShapeDtypeStruct((m, D_MODEL), F32)]
    out_specs = [row(D_MODEL)]
    for name, width, dt in _PROJ_OUT:
        if name in _HEAD_MAJOR:
            nh = _HEAD_MAJOR[name]
            out_shape.append(jax.ShapeDtypeStruct((nh, m, width // nh), dt))
            out_specs.append(pl.BlockSpec((nh, tm, width // nh), lambda i: (0, i, 0)))
        else:
            out_shape.append(jax.ShapeDtypeStruct((m, width), dt))
            out_specs.append(row(width))
    outs = pl.pallas_call(
        _ffn_in_kernel,
        grid=(m // tm,),
        in_specs=in_specs,
        out_specs=out_specs,
        out_shape=out_shape,
        compiler_params=pltpu.CompilerParams(
            dimension_semantics=("arbitrary",), vmem_limit_bytes=VMEM_LIMIT),
        name="ffn_in",
    )(x, ln_g, ln_b, wg, wu, wd, *w_pieces)
    res = {"h": outs[0]}
    for (name, _, _), o in zip(_PROJ_OUT, outs[1:]):
        res[name] = o
    return res


def _attend_kernel(nkb_ref, qpos_ref, lim_ref, iq_ref, wi_ref, q_ref, ki_ref, k_ref, v_ref,
                   o_ref, key_scr, t_scr, *, topk, koff, kb_size, slopes):
    tq = q_ref.shape[0]
    nkb = nkb_ref[pl.program_id(1)]
    lim = lim_ref[...]
    qpos = qpos_ref[...]
    sub = kb_size // LANES

    def lane_ids(s0):
        return s0 + lax.broadcasted_iota(I32, (tq, kb_size), 1)

    def score_block(kb, carry):
        s0 = pl.multiple_of(kb * kb_size, kb_size)
        kib = ki_ref[pl.ds(s0, kb_size), :]
        acc = jnp.zeros((tq, kb_size), F32)
        for h in range(IDX_HEADS):
            lg = lax.dot_general(iq_ref[:, h * IDX_DIM:(h + 1) * IDX_DIM], kib,
                                 (((1,), (1,)), ((), ())), preferred_element_type=F32)
            acc = acc + jnp.maximum(lg, 0.0) * wi_ref[:, h:h + 1]
        bits = lax.bitcast_convert_type(acc, I32)
        key = bits ^ ((bits >> 31) & 0x7FFFFFFF)
        key_scr[:, pl.ds(s0, kb_size)] = jnp.where(lane_ids(s0) < lim, key, INT_MIN)
        return carry

    lax.fori_loop(0, nkb, score_block, 0)

    def bcast(x):
        return jnp.broadcast_to(x, (tq, LANES))

    def count(pred):
        def body(kb, acc):
            for c in range(sub):
                off = pl.multiple_of(kb * kb_size + c * LANES, LANES)
                col = off + lax.broadcasted_iota(I32, (tq, LANES), 1)
                acc = acc + jnp.where(pred(key_scr[:, pl.ds(off, LANES)], col), 1.0, 0.0)
            return acc
        acc = lax.fori_loop(0, nkb, body, jnp.zeros((tq, LANES), F32))
        return jnp.sum(acc, axis=1, keepdims=True)

    def bit_step(state):
        i, t, settled, _ = state
        stop = jnp.min(settled)
        cand = t + lax.shift_left(jnp.int32(1), 31 - i)
        cand_b = bcast(cand)
        cnt = count(lambda key, col: key >= cand_b)
        take = (cnt >= topk) & (settled == 0.0)
        settled = jnp.where(take & (cnt == topk), 1.0, settled)
        return i + 1, jnp.where(take, cand, t), settled, stop

    _, t, _, _ = lax.while_loop(
        lambda state: (state[0] < 32) & (state[3] == 0.0), bit_step,
        (jnp.int32(0), jnp.full((tq, 1), INT_MIN, I32), (lim <= topk).astype(F32), jnp.float32(0.0)))
    thr = jnp.maximum(t, INT_MIN + 1)
    thr_b = bcast(thr)
    n_ge = count(lambda key, col: key >= thr_b)

    @pl.when(jnp.max(n_ge) > topk)
    def _():
        n_gt = count(lambda key, col: key > thr_b)
        need = topk - n_gt
        n_bits = (key_scr.shape[1] - 1).bit_length()

        def idx_step(i, m0):
            cand = m0 + lax.shift_left(jnp.int32(1), n_bits - 1 - i)
            cand_b = bcast(cand)
            g = count(lambda key, col: (key == thr_b) & (col < cand_b))
            return jnp.where(g < need, cand, m0)

        m0 = lax.fori_loop(0, n_bits, idx_step, jnp.zeros((tq, 1), I32))
        drop_row = n_ge > topk

        def demote(kb, carry):
            s0 = pl.multiple_of(kb * kb_size, kb_size)
            key = key_scr[:, pl.ds(s0, kb_size)]
            drop = drop_row & (key == thr) & (lane_ids(s0) > m0)
            key_scr[:, pl.ds(s0, kb_size)] = jnp.where(drop, INT_MIN, key)
            return carry

        lax.fori_loop(0, nkb, demote, 0)

    def attend_block(kb, carry):
        ms, ls, accs = carry
        s0 = pl.multiple_of(kb * kb_size, kb_size)
        col = lane_ids(s0)
        bias = jnp.where(key_scr[:, pl.ds(s0, kb_size)] >= thr, 0.0, NEG_INF)
        dist = jnp.abs(qpos - (col - koff)).astype(F32)
        kblk = k_ref[pl.ds(s0, kb_size), :]
        vblk = v_ref[pl.ds(s0, kb_size), :]
        new_m, new_l, new_acc = [], [], []
        for h in range(A_HEADS):
            g = h // A_GROUP
            s = lax.dot_general(q_ref[:, h * A_HEAD_DIM:(h + 1) * A_HEAD_DIM],
                                kblk[:, g * A_HEAD_DIM:(g + 1) * A_HEAD_DIM],
                                (((1,), (1,)), ((), ())), preferred_element_type=F32)
            s = (s - slopes[h] * dist) + bias
            t_scr[h] = s
            new_m.append(jnp.maximum(ms[h], jnp.max(s, axis=1, keepdims=True)))
        for h in range(A_HEADS):
            g = h // A_GROUP
            p = jnp.exp2(t_scr[h] - new_m[h])
            alpha = jnp.exp2(ms[h] - new_m[h])
            new_l.append(alpha * ls[h] + jnp.sum(p, axis=1, keepdims=True))
            pv = jnp.dot(p.astype(BF16), vblk[:, g * A_HEAD_DIM:(g + 1) * A_HEAD_DIM],
                         preferred_element_type=F32)
            new_acc.append(alpha * accs[h] + pv)
        return tuple(new_m), tuple(new_l), tuple(new_acc)

    init = (tuple(jnp.full((tq, 1), NEG_INF, F32) for _ in range(A_HEADS)),
            tuple(jnp.zeros((tq, 1), F32) for _ in range(A_HEADS)),
            tuple(jnp.zeros((tq, A_HEAD_DIM), F32) for _ in range(A_HEADS)))
    _, ls, accs = lax.fori_loop(0, nkb, attend_block, init)
    for h in range(A_HEADS):
        o_ref[:, h * A_HEAD_DIM:(h + 1) * A_HEAD_DIM] = (accs[h] / ls[h]).astype(o_ref.dtype)


def _attend(iq, wi, q, ki, k, v, qpos, lim, *, tq, kb_size, topk, koff):
    bsz, nq, _ = q.shape
    s_pad = k.shape[1]
    assert nq % tq == 0 and s_pad % kb_size == 0
    nt = nq // tq
    nkb = (jnp.max(lim.reshape(nt, tq), axis=1) + kb_size - 1) // kb_size
    slopes = _alibi_slopes_log2()
    qrow = lambda w: pl.BlockSpec((None, tq, w), lambda b, j, n: (b, j, 0))
    krow = lambda w: pl.BlockSpec((None, s_pad, w), lambda b, j, n: (b, 0, 0))
    pos = pl.BlockSpec((tq, 1), lambda b, j, n: (j, 0))
    return pl.pallas_call(
        functools.partial(_attend_kernel, topk=topk, koff=koff, kb_size=kb_size, slopes=slopes),
        grid_spec=pltpu.PrefetchScalarGridSpec(
            num_scalar_prefetch=1,
            grid=(bsz, nt),
            in_specs=[pos, pos, qrow(IDX_Q), qrow(IDX_HEADS), qrow(A_Q),
                      krow(IDX_DIM), krow(A_KV), krow(A_KV)],
            out_specs=qrow(A_Q),
            scratch_shapes=[pltpu.VMEM((tq, s_pad), I32),
                            pltpu.VMEM((A_HEADS, tq, kb_size), F32)],
        ),
        out_shape=jax.ShapeDtypeStruct((bsz, nq, A_Q), BF16),
        compiler_params=pltpu.CompilerParams(
            dimension_semantics=("arbitrary", "arbitrary"), vmem_limit_bytes=VMEM_LIMIT),
        name="attend",
    )(nkb.astype(I32), qpos.reshape(nq, 1), lim.reshape(nq, 1), iq, wi, q, ki, k, v)


def _fold_rows(x, op, rows=8):
    while x.shape[0] > rows:
        half = x.shape[0] // 2
        x = op(x[:half], x[half:])
    return x


VT_ROWS = A_HEAD_DIM + 16


def _alibi_slopes_log2():
    return tuple(float(2.0 ** (-8.0 * (i + 1) / A_HEADS)) * LOG2E for i in range(A_HEADS))


def _attend_t_kernel(qpos_ref, lim_ref, iq_ref, wi_ref, q_ref, ki_ref, k_ref, vt_ref,
                     o_ref, key_scr, hi_scr, lo_scr, t_scr, *, nkb, topk, koff, kb_size, slopes):
    tq = o_ref.shape[0]
    lim = lim_ref[...]
    qpos = qpos_ref[...]
    n_pairs = A_HEADS // 2
    blocks = [(kb * kb_size, pl.ds(kb * kb_size, kb_size)) for kb in range(nkb)]

    def row_ids(s0):
        return s0 + lax.broadcasted_iota(I32, (kb_size, tq), 0)

    def head_pair(ref, pr):
        return ref[2 * pr:2 * pr + 2].reshape(2 * tq, ref.shape[-1])

    for s0, rows in blocks:
        kib = ki_ref[rows, :]
        acc = jnp.zeros((kb_size, tq), F32)
        for pr in range(IDX_HEADS // 2):
            lg = lax.dot_general(kib, head_pair(iq_ref, pr), (((1,), (1,)), ((), ())),
                                 preferred_element_type=F32)
            for u in range(2):
                h = 2 * pr + u
                acc = acc + jnp.maximum(lg[:, u * tq:(u + 1) * tq], 0.0) * wi_ref[h:h + 1, :]
        bits = lax.bitcast_convert_type(acc, I32)
        key = bits ^ ((bits >> 31) & 0x7FFFFFFF)
        key = jnp.where(row_ids(s0) < lim, key, INT_MIN)
        key_scr[rows, :] = key
        hi_scr[rows, :] = (key >> 16).astype(I16)

    def count(pred):
        acc = jnp.zeros((8, tq), F32)
        for _, rows in blocks:
            acc = acc + _fold_rows(jnp.where(pred(key_scr[rows, :]), 1.0, 0.0), jnp.add)
        return jnp.sum(acc, axis=0, keepdims=True)

    def count16(scr, pred):
        acc = jnp.zeros((16, tq), I16)
        for _, rows in blocks:
            hit = jnp.where(pred(scr[rows, :]), jnp.int16(1), jnp.int16(0))
            acc = acc + _fold_rows(hit, jnp.add, 16)
        return jnp.sum(acc.astype(F32), axis=0, keepdims=True)

    def kth_largest16(scr, kth):
        def bit_step(i, t):
            cand = t + lax.shift_left(jnp.int32(1), 15 - i)
            cand16 = cand.astype(I16)
            return jnp.where(count16(scr, lambda blk: blk >= cand16) >= kth, cand, t)
        return lax.fori_loop(0, 16, bit_step, jnp.full((1, tq), INT16_MIN, I32))

    t_hi = kth_largest16(hi_scr, topk)
    t_hi16 = t_hi.astype(I16)
    kth_lo = topk - count16(hi_scr, lambda blk: blk > t_hi16)
    for _, rows in blocks:
        key = key_scr[rows, :]
        lo = jnp.where((key >> 16) == t_hi, (key & 0xFFFF) + INT16_MIN, INT16_MIN)
        lo_scr[rows, :] = lo.astype(I16)
    t_lo = kth_largest16(lo_scr, kth_lo)
    t = lax.shift_left(t_hi, 16) | (t_lo - INT16_MIN)
    thr = jnp.maximum(t, INT_MIN + 1)
    n_ge = count(lambda key: key >= thr)

    @pl.when(jnp.max(n_ge) > topk)
    def _():
        need = topk - count(lambda key: key > thr)
        for s0, rows in blocks:
            lo_scr[rows, :] = jnp.where(key_scr[rows, :] == thr, -row_ids(s0), INT16_MIN).astype(I16)
        last = kth_largest16(lo_scr, need)
        drop_q = n_ge > topk
        for s0, rows in blocks:
            key = key_scr[rows, :]
            drop = drop_q & (key == thr) & (-row_ids(s0) < last)
            key_scr[rows, :] = jnp.where(drop, INT_MIN, key)

    ms = [jnp.full((1, tq), NEG_INF, F32) for _ in range(A_HEADS)]
    accs = [jnp.zeros((vt_ref.shape[1], A_GROUP * tq), F32) for _ in range(A_KV_HEADS)]
    for s0, rows in blocks:
        bias = jnp.where(key_scr[rows, :] >= thr, 0.0, NEG_INF)
        dist = jnp.abs(qpos - (row_ids(s0) - koff)).astype(F32)
        new_m = []
        for pr in range(n_pairs):
            g = (2 * pr) // A_GROUP
            s2 = lax.dot_general(k_ref[g, rows, :], head_pair(q_ref, pr),
                                 (((1,), (1,)), ((), ())), preferred_element_type=F32)
            for u in range(2):
                h = 2 * pr + u
                s = (s2[:, u * tq:(u + 1) * tq] - slopes[h] * dist) + bias
                t_scr[h] = s
                blk_max = jnp.max(_fold_rows(s, jnp.maximum), axis=0, keepdims=True)
                new_m.append(jnp.maximum(ms[h], blk_max))
        for g in range(A_KV_HEADS):
            ps, alphas = [], []
            for h in range(g * A_GROUP, (g + 1) * A_GROUP):
                ps.append(jnp.exp2(t_scr[h] - new_m[h]).astype(BF16))
                alphas.append(jnp.exp2(ms[h] - new_m[h]))
            pv = jnp.dot(vt_ref[g, :, rows], jnp.concatenate(ps, axis=1),
                         preferred_element_type=F32)
            accs[g] = jnp.concatenate(alphas, axis=1) * accs[g] + pv
        ms = new_m
    for pr in range(n_pairs):
        rows = []
        for h in (2 * pr, 2 * pr + 1):
            g, hh = divmod(h, A_GROUP)
            cols = slice(hh * tq, (hh + 1) * tq)
            rows.append(accs[g][:A_HEAD_DIM, cols] / accs[g][A_HEAD_DIM:A_HEAD_DIM + 1, cols])
        o_ref[:, 2 * pr * A_HEAD_DIM:(2 * pr + 2) * A_HEAD_DIM] = (
            jnp.concatenate(rows, axis=0).T.astype(o_ref.dtype))


def _attend_t(iq, wi_t, q, ki, k, vt, qpos, lim, *, kb_size, topk, koff):
    tq = LANES
    _, bsz, nq, _ = q.shape
    s_pad = ki.shape[1]
    assert nq % tq == 0 and s_pad % kb_size == 0
    nkb_of_tile = -(-lim.reshape(nq // tq, tq).max(axis=1) // kb_size)
    runs, j0 = [], 0
    for j in range(1, len(nkb_of_tile) + 1):
        if j == len(nkb_of_tile) or nkb_of_tile[j] != nkb_of_tile[j0]:
            runs.append((j0, j, int(nkb_of_tile[j0])))
            j0 = j
    qpos_row = jnp.asarray(qpos, I32).reshape(1, nq)
    lim_row = jnp.asarray(lim, I32).reshape(1, nq)
    outs = []
    for j0, j1, nkb in runs:
        heads = lambda n: pl.BlockSpec((n, None, tq, A_HEAD_DIM), lambda b, j: (0, b, j + j0, 0))
        pos = pl.BlockSpec((1, tq), lambda b, j: (0, j + j0))
        outs.append(pl.pallas_call(
            functools.partial(_attend_t_kernel, nkb=nkb, topk=topk, koff=koff, kb_size=kb_size,
                              slopes=_alibi_slopes_log2()),
            grid=(bsz, j1 - j0),
            in_specs=[pos, pos, heads(IDX_HEADS),
                      pl.BlockSpec((None, IDX_HEADS, tq), lambda b, j: (b, 0, j + j0)),
                      heads(A_HEADS),
                      pl.BlockSpec((None, s_pad, IDX_DIM), lambda b, j: (b, 0, 0)),
                      pl.BlockSpec((A_KV_HEADS, None, s_pad, A_HEAD_DIM), lambda b, j: (0, b, 0, 0)),
                      pl.BlockSpec((A_KV_HEADS, None, VT_ROWS, s_pad), lambda b, j: (0, b, 0, 0))],
            out_specs=pl.BlockSpec((None, tq, A_Q), lambda b, j: (b, j, 0)),
            scratch_shapes=[pltpu.VMEM((s_pad, tq), I32), pltpu.VMEM((s_pad, tq), I16),
                            pltpu.VMEM((s_pad, tq), I16),
                            pltpu.VMEM((A_HEADS, kb_size, tq), F32)],
            out_shape=jax.ShapeDtypeStruct((bsz, (j1 - j0) * tq, A_Q), BF16),
            compiler_params=pltpu.CompilerParams(
                dimension_semantics=("arbitrary", "arbitrary"), vmem_limit_bytes=VMEM_LIMIT),
            name=f"attend_t{nkb}",
        )(qpos_row, lim_row, iq, wi_t, q, ki, k, vt))
    return jnp.concatenate(outs, axis=1)


HGRN_CHUNK = 128


def _hgrn2_tables(ct):
    r = np.arange(ct)
    seg = [r[:, None] >= r[None, :], r[None, :] > r[:, None]]
    pair = []
    g = ct
    while g >= 2:
        mid = (r // g) * g + g // 2
        second = r >= mid
        seg.append(np.where(second[:, None],
                            (r[None, :] >= mid[:, None]) & (r[None, :] <= r[:, None]),
                            (r[None, :] > r[:, None]) & (r[None, :] < mid[:, None])))
        pair.append((r[:, None] // g == r[None, :] // g) & second[:, None] & ~second[None, :])
        g //= 2
    return np.concatenate(seg, 0).astype(np.float32), np.stack(pair).astype(np.float32)


def _hgrn2_kernel(bq_ref, bf_ref, bi_ref, lb_ref, seg_ref, pair_ref, s0_ref, o_ref, sout_ref, st_scr):
    c = pl.program_id(1)
    ct = bq_ref.shape[0]
    n_lvl = pair_ref.shape[0]

    @pl.when(c == 0)
    def _():
        for h in range(B_HEADS):
            st_scr[h] = s0_ref[h].T

    lb = lb_ref[...]
    fx = bf_ref[...]
    logf = jnp.log(lb + (1.0 - lb) * jax.nn.sigmoid(fx))
    kk = (1.0 - lb) * jax.nn.sigmoid(-fx)
    bq = bq_ref[...]
    rq = bq * jax.nn.sigmoid(bq)
    rv = bi_ref[...]
    p1 = logf.astype(BF16)
    p2 = (logf - p1.astype(F32)).astype(BF16)
    seg = seg_ref[...]
    sums = jnp.dot(seg, p1, preferred_element_type=F32) + jnp.dot(seg, p2, preferred_element_type=F32)
    b_in = sums[:ct]
    b_out = sums[ct:2 * ct]
    q_in = (rq * jnp.exp(b_in)).astype(BF16)
    k_out = (kk * jnp.exp(b_out)).astype(BF16)
    rvb = rv.astype(BF16)
    row = lax.broadcasted_iota(I32, (ct, B_KEY_DIM), 0)
    nt = (((1,), (1,)), ((), ()))

    a = [jnp.zeros((ct, ct), F32) for _ in range(B_HEADS)]
    for lvl in range(n_lvl):
        half = ct >> (lvl + 1)
        second = (row & half) != 0
        dec = jnp.exp(sums[(2 + lvl) * ct:(3 + lvl) * ct])
        for h in range(B_HEADS):
            ls = slice(h * B_KEY_DIM, (h + 1) * B_KEY_DIM)
            x = (jnp.where(second, rq[:, ls], kk[:, ls]) * dec[:, ls]).astype(BF16)
            a[h] = a[h] + pair_ref[lvl] * lax.dot_general(x, x, nt, preferred_element_type=F32)

    for h in range(B_HEADS):
        ls = slice(h * B_KEY_DIM, (h + 1) * B_KEY_DIM)
        st = st_scr[h]
        own = jnp.sum(rq[:, ls] * kk[:, ls], axis=1, keepdims=True)
        o_ref[:, ls] = (jnp.dot(a[h].astype(BF16), rvb[:, ls], preferred_element_type=F32)
                        + own * rv[:, ls]
                        + lax.dot_general(q_in[:, ls], st.astype(BF16), nt, preferred_element_type=F32))
        upd = lax.dot_general(rvb[:, ls], k_out[:, ls], (((0,), (0,)), ((), ())),
                              preferred_element_type=F32)
        st_scr[h] = st * jnp.exp(b_in[ct - 1:ct, ls]) + upd

    @pl.when(c == pl.num_programs(1) - 1)
    def _():
        for h in range(B_HEADS):
            sout_ref[h] = st_scr[h].T


def _hgrn2(bq, bf, bi, lb, s0):
    bsz, t, _ = bq.shape
    ct = HGRN_CHUNK
    tp = -(-t // ct) * ct
    if tp != t:
        pad = lambda a, v: jnp.pad(a, ((0, 0), (0, tp - t), (0, 0)), constant_values=v)
        bq, bf, bi = pad(bq, 0.0), pad(bf, 1e4), pad(bi, 0.0)
    seg, pair = _hgrn2_tables(ct)
    seg = jnp.asarray(seg, BF16)
    pair = jnp.asarray(pair, F32)
    tok = pl.BlockSpec((None, ct, B_KW), lambda b, c: (b, c, 0))
    s_map = (lambda b, c: (0, 0, 0, 0)) if s0.shape[0] == 1 else (lambda b, c: (b, 0, 0, 0))
    s_in = pl.BlockSpec((None, B_HEADS, B_KEY_DIM, B_VAL_DIM), s_map)
    s_out = pl.BlockSpec((None, B_HEADS, B_KEY_DIM, B_VAL_DIM), lambda b, c: (b, 0, 0, 0))
    o, s = pl.pallas_call(
        _hgrn2_kernel,
        grid=(bsz, tp // ct),
        in_specs=[tok, tok, tok, _resident(lb.shape), _resident(seg.shape), _resident(pair.shape), s_in],
        out_specs=[tok, s_out],
        out_shape=[jax.ShapeDtypeStruct((bsz, tp, B_WIDTH), F32),
                   jax.ShapeDtypeStruct((bsz, B_HEADS, B_KEY_DIM, B_VAL_DIM), F32)],
        scratch_shapes=[pltpu.VMEM((B_HEADS, B_VAL_DIM, B_KEY_DIM), F32)],
        compiler_params=pltpu.CompilerParams(
            dimension_semantics=("arbitrary", "arbitrary"), vmem_limit_bytes=VMEM_LIMIT),
        name="hgrn2",
    )(bq, bf, bi, lb, seg, pair, s0)
    return o[:, :t], s


def _mix_out_kernel(h_ref, oa_ref, orec_ref, bg_ref, ga_ref, gb_ref, gn_ref, wa_ref, wb_ref, wo_ref,
                    g2_ref, b2_ref, wg_ref, wu_ref, wd_ref, g3_ref, b3_ref, y_ref):
    orec = orec_ref[...]
    gn = gn_ref[...]
    parts = []
    for h in range(B_HEADS):
        ls = slice(h * B_VAL_DIM, (h + 1) * B_VAL_DIM)
        oh = orec[:, ls]
        ms = jnp.mean(oh * oh, axis=-1, keepdims=True)
        parts.append(oh * lax.rsqrt(ms + RMS_EPS) * gn[:, ls])
    on = jnp.concatenate(parts, axis=-1)
    bg = bg_ref[...]
    ob = on * (bg * jax.nn.sigmoid(bg))
    pa = jnp.dot(oa_ref[...], wa_ref[...], preferred_element_type=F32)
    pb = jnp.dot(ob.astype(BF16), wb_ref[...], preferred_element_type=F32)
    mixed = jax.nn.sigmoid(ga_ref[...]) * pa + jax.nn.sigmoid(gb_ref[...]) * pb
    mixed = jnp.dot(mixed.astype(BF16), wo_ref[...], preferred_element_type=F32)
    h2 = _layer_norm(ALPHA * h_ref[...] + mixed, g2_ref[...], b2_ref[...])
    y_ref[...] = _layer_norm(_swiglu_step(h2, wg_ref, wu_ref, wd_ref), g3_ref[...], b3_ref[...])


def _mix_out(h, oa, orec, bg, ga, gb, gn, wa, wb, wo, g2, b2, wg, wu, wd, g3, b3, tm):
    m = h.shape[0]
    assert m % tm == 0
    row = lambda w: pl.BlockSpec((tm, w), lambda i: (i, 0))
    weights = (gn, wa, wb, wo, g2, b2, wg, wu, wd, g3, b3)
    return pl.pallas_call(
        _mix_out_kernel,
        grid=(m // tm,),
        in_specs=[row(D_MODEL), row(A_Q), row(B_WIDTH), row(B_WIDTH), row(D_MODEL), row(D_MODEL)]
        + [_resident(w.shape) for w in weights],
        out_specs=row(D_MODEL),
        out_shape=jax.ShapeDtypeStruct((m, D_MODEL), F32),
        compiler_params=pltpu.CompilerParams(
            dimension_semantics=("arbitrary",), vmem_limit_bytes=VMEM_LIMIT),
        name="mix_out",
    )(h, oa, orec, bg, ga, gb, *weights)


def _ffn_weights(wg, wu, wd):
    wg = wg.astype(BF16).reshape(D_MODEL, N_FF_CHUNKS, FF_CHUNK).transpose(1, 0, 2)
    wu = wu.astype(BF16).reshape(D_MODEL, N_FF_CHUNKS, FF_CHUNK).transpose(1, 0, 2)
    wd = wd.astype(BF16).reshape(N_FF_CHUNKS, FF_CHUNK, D_MODEL)
    return wg, wu, wd


def _pad_rows(a, n):
    return jnp.pad(a, ((0, 0), (0, n - a.shape[1]), (0, 0)))


FFN_IN_TILE = 256
MIX_OUT_TILE = 512


def _token_tile(m, largest):
    tm = largest
    while tm > 8 and m % tm:
        tm //= 2
    assert m % tm == 0, m
    return tm


def kernel(x_prompt, x_sample, cache_k, cache_v, cache_kidx, state_hgrn, meta, ln1_g, ln1_b, ffn1_wg, ffn1_wu, ffn1_wd, w_in, lb_param, g_norm, w_a_proj, w_b_proj, w_out, ln2_g, ln2_b, ffn2_wg, ffn2_wu, ffn2_wd, ln3_g, ln3_b):
    assert ln1_g.shape[0] == DEPTH
    bp, seq, _ = x_prompt.shape
    bs, dseq, _ = x_sample.shape
    past = cache_k.shape[2]
    l = 0
    vec = lambda a: a.reshape(1, -1).astype(F32)

    lb_all = jnp.cumsum(jax.nn.softmax(lb_param.astype(F32), axis=0), axis=0)
    lb = lb_all[l].reshape(1, B_KW)
    f1 = _ffn_weights(ffn1_wg[l], ffn1_wu[l], ffn1_wd[l])
    f2 = _ffn_weights(ffn2_wg[l], ffn2_wu[l], ffn2_wd[l])
    w_pieces, off = [], 0
    for _, width, _ in _PROJ_OUT:
        w_pieces.append(w_in[l][:, off:off + width].astype(BF16))
        off += width
    assert off == w_in.shape[-1]

    def stage1(x):
        return _ffn_in(x, vec(ln1_g[l]), vec(ln1_b[l]), *f1, w_pieces,
                       _token_tile(x.shape[0], FFN_IN_TILE))

    def stage4(p, oa, orec):
        m = p["h"].shape[0]
        return _mix_out(p["h"], oa.reshape(m, A_Q), orec.reshape(m, B_WIDTH), p["bg"], p["ga"], p["gb"],
                        vec(g_norm[l]), w_a_proj[l].astype(BF16), w_b_proj[l].astype(BF16),
                        w_out[l].astype(BF16), vec(ln2_g[l]), vec(ln2_b[l]), *f2,
                        vec(ln3_g[l]), vec(ln3_b[l]), _token_tile(m, MIX_OUT_TILE))

    pm = stage1(meta.astype(F32))
    pp = stage1(x_prompt.reshape(bp * seq, D_MODEL))
    ps = stage1(x_sample.reshape(bs * dseq, D_MODEL))
    per_p = lambda a: a.reshape(bp, seq, a.shape[-1])
    per_s = lambda a: a.reshape(bs, dseq, a.shape[-1])

    def with_meta(name):
        rows = jnp.broadcast_to(pm[name][None], (bp, N_META, pm[name].shape[-1]))
        return jnp.concatenate([rows, per_p(pp[name])], axis=1)

    k_p, v_p, ki_p = with_meta("ak"), with_meta("av"), with_meta("ik")

    kb_p = 512
    s_p = -(-(N_META + seq) // kb_p) * kb_p
    qpos = np.arange(seq, dtype=np.int32)
    lim = N_META + CHUNK * (qpos // CHUNK + 1)
    heads_p = lambda a: a.reshape(a.shape[0], bp, seq, a.shape[-1])
    kv_heads = lambda a: _pad_rows(a.astype(BF16), s_p).reshape(bp, s_p, A_KV_HEADS, A_HEAD_DIM)
    vt = kv_heads(v_p).transpose(2, 0, 3, 1)
    vt = jnp.concatenate([vt, jnp.ones_like(vt[:, :, :1]),
                          jnp.zeros_like(vt[:, :, :VT_ROWS - A_HEAD_DIM - 1])], axis=2)
    oa_p = _attend_t(heads_p(pp["iq"]), per_p(pp["iw"]).transpose(0, 2, 1), heads_p(pp["aq"]),
                     _pad_rows(ki_p.astype(BF16), s_p), kv_heads(k_p).transpose(2, 0, 1, 3),
                     vt, qpos, lim,
                     kb_size=kb_p, topk=min(TOPK_MAX, seq // 4), koff=N_META)

    n_all = past + dseq
    kb_s = 384
    s_s = -(-n_all // kb_s) * kb_s
    k_all = jnp.concatenate([cache_k[l].reshape(bs, past, A_KV).astype(BF16), per_s(ps["ak"]).astype(BF16)], axis=1)
    v_all = jnp.concatenate([cache_v[l].reshape(bs, past, A_KV).astype(BF16), per_s(ps["av"]).astype(BF16)], axis=1)
    ki_all = jnp.concatenate([cache_kidx[l].astype(BF16), per_s(ps["ik"]).astype(BF16)], axis=1)
    tokens_s = lambda a: a.transpose(1, 0, 2).reshape(bs, dseq, a.shape[0] * a.shape[-1])
    oa_s = _attend(tokens_s(ps["iq"]), per_s(ps["iw"]), tokens_s(ps["aq"]),
                   _pad_rows(ki_all, s_s), _pad_rows(k_all, s_s), _pad_rows(v_all, s_s),
                   past + jnp.arange(dseq, dtype=I32), jnp.full((dseq,), n_all, I32),
                   tq=dseq, kb_size=kb_s, topk=min(TOPK_MAX, n_all // 4), koff=0)

    zero_state = jnp.zeros((1, B_HEADS, B_KEY_DIM, B_VAL_DIM), F32)
    _, st_m = _hgrn2(pm["bq"][None], pm["bf"][None], pm["bi"][None], lb, zero_state)
    orec_p, st_p = _hgrn2(per_p(pp["bq"]), per_p(pp["bf"]), per_p(pp["bi"]), lb, st_m)
    orec_s, st_s = _hgrn2(per_s(ps["bq"]), per_s(ps["bf"]), per_s(ps["bi"]), lb,
                          state_hgrn[l].astype(F32))

    y_p = stage4(pp, oa_p, orec_p).reshape(bp, seq, D_MODEL)
    y_s = stage4(ps, oa_s, orec_s).reshape(bs, dseq, D_MODEL)

    kv5 = lambda a: a.reshape(1, a.shape[0], a.shape[1], A_KV_HEADS, A_HEAD_DIM)
    return (y_p, y_s, kv5(k_p), kv5(v_p), ki_p[None], st_p[None].astype(state_hgrn.dtype),
            kv5(per_s(ps["ak"])), kv5(per_s(ps["av"])), per_s(ps["ik"])[None],
            st_s[None].astype(state_hgrn.dtype))
```

```python
import functools

import jax
import jax.numpy as jnp
import numpy as np
from jax import lax
from jax.experimental import pallas as pl
from jax.experimental.pallas import tpu as pltpu

F32 = jnp.float32
BF16 = jnp.bfloat16
I32 = jnp.int32
I16 = jnp.int16

D_MODEL = 1024
D_FF = 2816
FF_CHUNK = 256
N_FF_CHUNKS = D_FF // FF_CHUNK
CHUNK = 64
N_META = 16
A_HEADS = 8
A_KV_HEADS = 2
A_GROUP = A_HEADS // A_KV_HEADS
A_HEAD_DIM = 64
A_Q = A_HEADS * A_HEAD_DIM
A_KV = A_KV_HEADS * A_HEAD_DIM
IDX_HEADS = 8
IDX_DIM = 64
IDX_Q = IDX_HEADS * IDX_DIM
TOPK_MAX = 256
B_HEADS = 4
B_KEY_DIM = 128
B_VAL_DIM = 128
B_KW = B_HEADS * B_KEY_DIM
B_WIDTH = B_HEADS * B_VAL_DIM
DEPTH = 1
ALPHA = (2.0 * DEPTH) ** 0.25
LN_EPS = 1e-5
RMS_EPS = 1e-6
NEG_INF = -1e30
LOG2E = 1.4426950408889634
INT_MIN = -(2 ** 31)
INT16_MIN = -(2 ** 15)
LANES = 128
VMEM_LIMIT = 56 * 1024 * 1024


def _resident(shape):
    n = len(shape)
    return pl.BlockSpec(shape, lambda *_: (0,) * n, pipeline_mode=pl.Buffered(1))


def _layer_norm(x, g, b):
    mu = jnp.mean(x, axis=-1, keepdims=True)
    xc = x - mu
    var = jnp.mean(xc * xc, axis=-1, keepdims=True)
    return xc * lax.rsqrt(var + LN_EPS) * g + b


def _swiglu_step(x, wg_ref, wu_ref, wd_ref):
    xb = x.astype(BF16)

    acc = jnp.zeros(x.shape, F32)
    for c in range(N_FF_CHUNKS):
        g = jnp.dot(xb, wg_ref[c], preferred_element_type=F32)
        u = jnp.dot(xb, wu_ref[c], preferred_element_type=F32)
        a = (g * jax.nn.sigmoid(g) * u).astype(BF16)
        acc = acc + jnp.dot(a, wd_ref[c], preferred_element_type=F32)
    return ALPHA * x + 0.5 * acc


_PROJ_OUT = (
    ("aq", A_Q, BF16), ("ak", A_KV, F32), ("av", A_KV, F32), ("iq", IDX_Q, BF16),
    ("ik", IDX_DIM, F32), ("iw", IDX_HEADS, F32),
    ("bq", B_KW, F32), ("bf", B_KW, F32), ("bi", B_WIDTH, F32),
)
_GATE_WIDTHS = (B_WIDTH, D_MODEL, D_MODEL)
_PROJ_SCALE = {"aq": (A_HEAD_DIM ** -0.5) * LOG2E, "iw": (IDX_HEADS ** -0.5) * (IDX_DIM ** -0.5)}
_HEAD_MAJOR = {"aq": A_HEADS, "iq": IDX_HEADS}


def _ffn_in_kernel(x_ref, g_ref, b_ref, wg_ref, wu_ref, wd_ref, *rest):
    n = len(_PROJ_OUT)
    w_refs, h_ref, out_refs = rest[:n], rest[n], rest[n + 1:]
    x = x_ref[...]
    h = _layer_norm(_swiglu_step(x, wg_ref, wu_ref, wd_ref), g_ref[...], b_ref[...])
    h_ref[...] = h
    hb = h.astype(BF16)
    for (name, _, dt), w_ref, o_ref in zip(_PROJ_OUT, w_refs, out_refs):
        y = jnp.dot(hb, w_ref[...], preferred_element_type=F32)
        if name in _PROJ_SCALE:
            y = y * _PROJ_SCALE[name]
        if name in _HEAD_MAJOR:
            hd = y.shape[1] // _HEAD_MAJOR[name]
            for h in range(_HEAD_MAJOR[name]):
                o_ref[h] = y[:, h * hd:(h + 1) * hd].astype(dt)
        else:
            o_ref[...] = y.astype(dt)


def _ffn_in(x, ln_g, ln_b, wg, wu, wd, w_pieces, tm):
    m = x.shape[0]
    assert m % tm == 0
    row = lambda w: pl.BlockSpec((tm, w), lambda i: (i, 0))
    in_specs = [row(D_MODEL), _resident(ln_g.shape), _resident(ln_b.shape),
                _resident(wg.shape), _resident(wu.shape), _resident(wd.shape)]
    in_specs += [_resident(w.shape) for w in w_pieces]
    out_shape = [jax.ShapeDtypeStruct((m, D_MODEL), F32)]
    out_specs = [row(D_MODEL)]
    for name, width, dt in _PROJ_OUT:
        if name in _HEAD_MAJOR:
            nh = _HEAD_MAJOR[name]
            out_shape.append(jax.ShapeDtypeStruct((nh, m, width // nh), dt))
            out_specs.append(pl.BlockSpec((nh, tm, width // nh), lambda i: (0, i, 0)))
        else:
            out_shape.append(jax.ShapeDtypeStruct((m, width), dt))
            out_specs.append(row(width))
    outs = pl.pallas_call(
        _ffn_in_kernel,
        grid=(m // tm,),
        in_specs=in_specs,
        out_specs=out_specs,
        out_shape=out_shape,
        compiler_params=pltpu.CompilerParams(
            dimension_semantics=("arbitrary",), vmem_limit_bytes=VMEM_LIMIT),
        name="ffn_in",
    )(x, ln_g, ln_b, wg, wu, wd, *w_pieces)
    res = {"h": outs[0]}
    for (name, _, _), o in zip(_PROJ_OUT, outs[1:]):
        res[name] = o
    return res


def _attend_kernel(nkb_ref, qpos_ref, lim_ref, iq_ref, wi_ref, q_ref, ki_ref, k_ref, v_ref,
                   o_ref, key_scr, t_scr, *, topk, koff, kb_size, slopes):
    tq = q_ref.shape[0]
    nkb = nkb_ref[pl.program_id(1)]
    lim = lim_ref[...]
    qpos = qpos_ref[...]
    sub = kb_size // LANES

    def lane_ids(s0):
        return s0 + lax.broadcasted_iota(I32, (tq, kb_size), 1)

    def score_block(kb, carry):
        s0 = pl.multiple_of(kb * kb_size, kb_size)
        kib = ki_ref[pl.ds(s0, kb_size), :]
        acc = jnp.zeros((tq, kb_size), F32)
        for h in range(IDX_HEADS):
            lg = lax.dot_general(iq_ref[:, h * IDX_DIM:(h + 1) * IDX_DIM], kib,
                                 (((1,), (1,)), ((), ())), preferred_element_type=F32)
            acc = acc + jnp.maximum(lg, 0.0) * wi_ref[:, h:h + 1]
        bits = lax.bitcast_convert_type(acc, I32)
        key = bits ^ ((bits >> 31) & 0x7FFFFFFF)
        key_scr[:, pl.ds(s0, kb_size)] = jnp.where(lane_ids(s0) < lim, key, INT_MIN)
        return carry

    lax.fori_loop(0, nkb, score_block, 0)

    def bcast(x):
        return jnp.broadcast_to(x, (tq, LANES))

    def count(pred):
        def body(kb, acc):
            for c in range(sub):
                off = pl.multiple_of(kb * kb_size + c * LANES, LANES)
                col = off + lax.broadcasted_iota(I32, (tq, LANES), 1)
                acc = acc + jnp.where(pred(key_scr[:, pl.ds(off, LANES)], col), 1.0, 0.0)
            return acc
        acc = lax.fori_loop(0, nkb, body, jnp.zeros((tq, LANES), F32))
        return jnp.sum(acc, axis=1, keepdims=True)

    def bit_step(state):
        i, t, settled, _ = state
        stop = jnp.min(settled)
        cand = t + lax.shift_left(jnp.int32(1), 31 - i)
        cand_b = bcast(cand)
        cnt = count(lambda key, col: key >= cand_b)
        take = (cnt >= topk) & (settled == 0.0)
        settled = jnp.where(take & (cnt == topk), 1.0, settled)
        return i + 1, jnp.where(take, cand, t), settled, stop

    _, t, _, _ = lax.while_loop(
        lambda state: (state[0] < 32) & (state[3] == 0.0), bit_step,
        (jnp.int32(0), jnp.full((tq, 1), INT_MIN, I32), (lim <= topk).astype(F32), jnp.float32(0.0)))
    thr = jnp.maximum(t, INT_MIN + 1)
    thr_b = bcast(thr)
    n_ge = count(lambda key, col: key >= thr_b)

    @pl.when(jnp.max(n_ge) > topk)
    def _():
        n_gt = count(lambda key, col: key > thr_b)
        need = topk - n_gt
        n_bits = (key_scr.shape[1] - 1).bit_length()

        def idx_step(i, m0):
            cand = m0 + lax.shift_left(jnp.int32(1), n_bits - 1 - i)
            cand_b = bcast(cand)
            g = count(lambda key, col: (key == thr_b) & (col < cand_b))
            return jnp.where(g < need, cand, m0)

        m0 = lax.fori_loop(0, n_bits, idx_step, jnp.zeros((tq, 1), I32))
        drop_row = n_ge > topk

        def demote(kb, carry):
            s0 = pl.multiple_of(kb * kb_size, kb_size)
            key = key_scr[:, pl.ds(s0, kb_size)]
            drop = drop_row & (key == thr) & (lane_ids(s0) > m0)
            key_scr[:, pl.ds(s0, kb_size)] = jnp.where(drop, INT_MIN, key)
            return carry

        lax.fori_loop(0, nkb, demote, 0)

    def attend_block(kb, carry):
        ms, ls, accs = carry
        s0 = pl.multiple_of(kb * kb_size, kb_size)
        col = lane_ids(s0)
        bias = jnp.where(key_scr[:, pl.ds(s0, kb_size)] >= thr, 0.0, NEG_INF)
        dist = jnp.abs(qpos - (col - koff)).astype(F32)
        kblk = k_ref[pl.ds(s0, kb_size), :]
        vblk = v_ref[pl.ds(s0, kb_size), :]
        new_m, new_l, new_acc = [], [], []
        for h in range(A_HEADS):
            g = h // A_GROUP
            s = lax.dot_general(q_ref[:, h * A_HEAD_DIM:(h + 1) * A_HEAD_DIM],
                                kblk[:, g * A_HEAD_DIM:(g + 1) * A_HEAD_DIM],
                                (((1,), (1,)), ((), ())), preferred_element_type=F32)
            s = (s - slopes[h] * dist) + bias
            t_scr[h] = s
            new_m.append(jnp.maximum(ms[h], jnp.max(s, axis=1, keepdims=True)))
        for h in range(A_HEADS):
            g = h // A_GROUP
            p = jnp.exp2(t_scr[h] - new_m[h])
            alpha = jnp.exp2(ms[h] - new_m[h])
            new_l.append(alpha * ls[h] + jnp.sum(p, axis=1, keepdims=True))
            pv = jnp.dot(p.astype(BF16), vblk[:, g * A_HEAD_DIM:(g + 1) * A_HEAD_DIM],
                         preferred_element_type=F32)
            new_acc.append(alpha * accs[h] + pv)
        return tuple(new_m), tuple(new_l), tuple(new_acc)

    init = (tuple(jnp.full((tq, 1), NEG_INF, F32) for _ in range(A_HEADS)),
            tuple(jnp.zeros((tq, 1), F32) for _ in range(A_HEADS)),
            tuple(jnp.zeros((tq, A_HEAD_DIM), F32) for _ in range(A_HEADS)))
    _, ls, accs = lax.fori_loop(0, nkb, attend_block, init)
    for h in range(A_HEADS):
        o_ref[:, h * A_HEAD_DIM:(h + 1) * A_HEAD_DIM] = (accs[h] / ls[h]).astype(o_ref.dtype)


def _attend(iq, wi, q, ki, k, v, qpos, lim, *, tq, kb_size, topk, koff):
    bsz, nq, _ = q.shape
    s_pad = k.shape[1]
    assert nq % tq == 0 and s_pad % kb_size == 0
    nt = nq // tq
    nkb = (jnp.max(lim.reshape(nt, tq), axis=1) + kb_size - 1) // kb_size
    slopes = _alibi_slopes_log2()
    qrow = lambda w: pl.BlockSpec((None, tq, w), lambda b, j, n: (b, j, 0))
    krow = lambda w: pl.BlockSpec((None, s_pad, w), lambda b, j, n: (b, 0, 0))
    pos = pl.BlockSpec((tq, 1), lambda b, j, n: (j, 0))
    return pl.pallas_call(
        functools.partial(_attend_kernel, topk=topk, koff=koff, kb_size=kb_size, slopes=slopes),
        grid_spec=pltpu.PrefetchScalarGridSpec(
            num_scalar_prefetch=1,
            grid=(bsz, nt),
            in_specs=[pos, pos, qrow(IDX_Q), qrow(IDX_HEADS), qrow(A_Q),
                      krow(IDX_DIM), krow(A_KV), krow(A_KV)],
            out_specs=qrow(A_Q),
            scratch_shapes=[pltpu.VMEM((tq, s_pad), I32),
                            pltpu.VMEM((A_HEADS, tq, kb_size), F32)],
        ),
        out_shape=jax.ShapeDtypeStruct((bsz, nq, A_Q), BF16),
        compiler_params=pltpu.CompilerParams(
            dimension_semantics=("arbitrary", "arbitrary"), vmem_limit_bytes=VMEM_LIMIT),
        name="attend",
    )(nkb.astype(I32), qpos.reshape(nq, 1), lim.reshape(nq, 1), iq, wi, q, ki, k, v)


def _fold_rows(x, op, rows=8):
    while x.shape[0] > rows:
        half = x.shape[0] // 2
        x = op(x[:half], x[half:])
    return x


VT_ROWS = A_HEAD_DIM + 16


def _alibi_slopes_log2():
    return tuple(float(2.0 ** (-8.0 * (i + 1) / A_HEADS)) * LOG2E for i in range(A_HEADS))


def _attend_t_kernel(qpos_ref, lim_ref, iq_ref, wi_ref, q_ref, ki_ref, k_ref, vt_ref,
                     o_ref, key_scr, hi_scr, lo_scr, t_scr, *, nkb, topk, koff, kb_size, slopes):
    tq = o_ref.shape[0]
    lim = lim_ref[...]
    qpos = qpos_ref[...]
    n_pairs = A_HEADS // 2
    blocks = [(kb * kb_size, pl.ds(kb * kb_size, kb_size)) for kb in range(nkb)]

    def row_ids(s0):
        return s0 + lax.broadcasted_iota(I32, (kb_size, tq), 0)

    def head_pair(ref, pr):
        return ref[2 * pr:2 * pr + 2].reshape(2 * tq, ref.shape[-1])

    for s0, rows in blocks:
        kib = ki_ref[rows, :]
        acc = jnp.zeros((kb_size, tq), F32)
        for pr in range(IDX_HEADS // 2):
            lg = lax.dot_general(kib, head_pair(iq_ref, pr), (((1,), (1,)), ((), ())),
                                 preferred_element_type=F32)
            for u in range(2):
                h = 2 * pr + u
                acc = acc + jnp.maximum(lg[:, u * tq:(u + 1) * tq], 0.0) * wi_ref[h:h + 1, :]
        bits = lax.bitcast_convert_type(acc, I32)
        key = bits ^ ((bits >> 31) & 0x7FFFFFFF)
        key = jnp.where(row_ids(s0) < lim, key, INT_MIN)
        key_scr[rows, :] = key
        hi_scr[rows, :] = (key >> 16).astype(I16)

    def count(pred):
        acc = jnp.zeros((8, tq), F32)
        for _, rows in blocks:
            acc = acc + _fold_rows(jnp.where(pred(key_scr[rows, :]), 1.0, 0.0), jnp.add)
        return jnp.sum(acc, axis=0, keepdims=True)

    def count16(scr, pred):
        acc = jnp.zeros((16, tq), I16)
        for _, rows in blocks:
            hit = jnp.where(pred(scr[rows, :]), jnp.int16(1), jnp.int16(0))
            acc = acc + _fold_rows(hit, jnp.add, 16)
        return jnp.sum(acc.astype(F32), axis=0, keepdims=True)

    def kth_largest16(scr, kth):
        def bit_step(i, t):
            cand = t + lax.shift_left(jnp.int32(1), 15 - i)
            cand16 = cand.astype(I16)
            return jnp.where(count16(scr, lambda blk: blk >= cand16) >= kth, cand, t)
        return lax.fori_loop(0, 16, bit_step, jnp.full((1, tq), INT16_MIN, I32))

    t_hi = kth_largest16(hi_scr, topk)
    t_hi16 = t_hi.astype(I16)
    kth_lo = topk - count16(hi_scr, lambda blk: blk > t_hi16)
    for _, rows in blocks:
        key = key_scr[rows, :]
        lo = jnp.where((key >> 16) == t_hi, (key & 0xFFFF) + INT16_MIN, INT16_MIN)
        lo_scr[rows, :] = lo.astype(I16)
    t_lo = kth_largest16(lo_scr, kth_lo)
    t = lax.shift_left(t_hi, 16) | (t_lo - INT16_MIN)
    thr = jnp.maximum(t, INT_MIN + 1)
    n_ge = count(lambda key: key >= thr)

    @pl.when(jnp.max(n_ge) > topk)
    def _():
        need = topk - count(lambda key: key > thr)
        for s0, rows in blocks:
            lo_scr[rows, :] = jnp.where(key_scr[rows, :] == thr, -row_ids(s0), INT16_MIN).astype(I16)
        last = kth_largest16(lo_scr, need)
        drop_q = n_ge > topk
        for s0, rows in blocks:
            key = key_scr[rows, :]
            drop = drop_q & (key == thr) & (-row_ids(s0) < last)
            key_scr[rows, :] = jnp.where(drop, INT_MIN, key)

    ms = [jnp.full((1, tq), NEG_INF, F32) for _ in range(A_HEADS)]
    accs = [jnp.zeros((vt_ref.shape[1], A_GROUP * tq), F32) for _ in range(A_KV_HEADS)]
    for s0, rows in blocks:
        bias = jnp.where(key_scr[rows, :] >= thr, 0.0, NEG_INF)
        dist = jnp.abs(qpos - (row_ids(s0) - koff)).astype(F32)
        new_m = []
        for pr in range(n_pairs):
            g = (2 * pr) // A_GROUP
            s2 = lax.dot_general(k_ref[g, rows, :], head_pair(q_ref, pr),
                                 (((1,), (1,)), ((), ())), preferred_element_type=F32)
            for u in range(2):
                h = 2 * pr + u
                s = (s2[:, u * tq:(u + 1) * tq] - slopes[h] * dist) + bias
                t_scr[h] = s
                blk_max = jnp.max(_fold_rows(s, jnp.maximum), axis=0, keepdims=True)
                new_m.append(jnp.maximum(ms[h], blk_max))
        for g in range(A_KV_HEADS):
            ps, alphas = [], []
            for h in range(g * A_GROUP, (g + 1) * A_GROUP):
                ps.append(jnp.exp2(t_scr[h] - new_m[h]).astype(BF16))
                alphas.append(jnp.exp2(ms[h] - new_m[h]))
            pv = jnp.dot(vt_ref[g, :, rows], jnp.concatenate(ps, axis=1),
                         preferred_element_type=F32)
            accs[g] = jnp.concatenate(alphas, axis=1) * accs[g] + pv
        ms = new_m
    for pr in range(n_pairs):
        rows = []
        for h in (2 * pr, 2 * pr + 1):
            g, hh = divmod(h, A_GROUP)
            cols = slice(hh * tq, (hh + 1) * tq)
            rows.append(accs[g][:A_HEAD_DIM, cols] / accs[g][A_HEAD_DIM:A_HEAD_DIM + 1, cols])
        o_ref[:, 2 * pr * A_HEAD_DIM:(2 * pr + 2) * A_HEAD_DIM] = (
            jnp.concatenate(rows, axis=0).T.astype(o_ref.dtype))


def _attend_t(iq, wi_t, q, ki, k, vt, qpos, lim, *, kb_size, topk, koff):
    tq = LANES
    _, bsz, nq, _ = q.shape
    s_pad = ki.shape[1]
    assert nq % tq == 0 and s_pad % kb_size == 0
    nkb_of_tile = -(-lim.reshape(nq // tq, tq).max(axis=1) // kb_size)
    runs, j0 = [], 0
    for j in range(1, len(nkb_of_tile) + 1):
        if j == len(nkb_of_tile) or nkb_of_tile[j] != nkb_of_tile[j0]:
            runs.append((j0, j, int(nkb_of_tile[j0])))
            j0 = j
    qpos_row = jnp.asarray(qpos, I32).reshape(1, nq)
    lim_row = jnp.asarray(lim, I32).reshape(1, nq)
    outs = []
    for j0, j1, nkb in runs:
        heads = lambda n: pl.BlockSpec((n, None, tq, A_HEAD_DIM), lambda b, j: (0, b, j + j0, 0))
        pos = pl.BlockSpec((1, tq), lambda b, j: (0, j + j0))
        outs.append(pl.pallas_call(
            functools.partial(_attend_t_kernel, nkb=nkb, topk=topk, koff=koff, kb_size=kb_size,
                              slopes=_alibi_slopes_log2()),
            grid=(bsz, j1 - j0),
            in_specs=[pos, pos, heads(IDX_HEADS),
                      pl.BlockSpec((None, IDX_HEADS, tq), lambda b, j: (b, 0, j + j0)),
                      heads(A_HEADS),
                      pl.BlockSpec((None, s_pad, IDX_DIM), lambda b, j: (b, 0, 0)),
                      pl.BlockSpec((A_KV_HEADS, None, s_pad, A_HEAD_DIM), lambda b, j: (0, b, 0, 0)),
                      pl.BlockSpec((A_KV_HEADS, None, VT_ROWS, s_pad), lambda b, j: (0, b, 0, 0))],
            out_specs=pl.BlockSpec((None, tq, A_Q), lambda b, j: (b, j, 0)),
            scratch_shapes=[pltpu.VMEM((s_pad, tq), I32), pltpu.VMEM((s_pad, tq), I16),
                            pltpu.VMEM((s_pad, tq), I16),
                            pltpu.VMEM((A_HEADS, kb_size, tq), F32)],
            out_shape=jax.ShapeDtypeStruct((bsz, (j1 - j0) * tq, A_Q), BF16),
            compiler_params=pltpu.CompilerParams(
                dimension_semantics=("arbitrary", "arbitrary"), vmem_limit_bytes=VMEM_LIMIT),
            name=f"attend_t{nkb}",
        )(qpos_row, lim_row, iq, wi_t, q, ki, k, vt))
    return jnp.concatenate(outs, axis=1)


HGRN_CHUNK = 128


def _hgrn2_tables(ct):
    r = np.arange(ct)
    seg = [r[:, None] >= r[None, :], r[None, :] > r[:, None]]
    pair = []
    g = ct
    while g >= 2:
        mid = (r // g) * g + g // 2
        second = r >= mid
        seg.append(np.where(second[:, None],
                            (r[None, :] >= mid[:, None]) & (r[None, :] <= r[:, None]),
                            (r[None, :] > r[:, None]) & (r[None, :] < mid[:, None])))
        pair.append((r[:, None] // g == r[None, :] // g) & second[:, None] & ~second[None, :])
        g //= 2
    return np.concatenate(seg, 0).astype(np.float32), np.stack(pair).astype(np.float32)


def _hgrn2_kernel(bq_ref, bf_ref, bi_ref, lb_ref, seg_ref, pair_ref, s0_ref, o_ref, sout_ref, st_scr):
    c = pl.program_id(1)
    ct = bq_ref.shape[0]
    n_lvl = pair_ref.shape[0]

    @pl.when(c == 0)
    def _():
        for h in range(B_HEADS):
            st_scr[h] = s0_ref[h].T

    lb = lb_ref[...]
    fx = bf_ref[...]
    logf = jnp.log(lb + (1.0 - lb) * jax.nn.sigmoid(fx))
    kk = (1.0 - lb) * jax.nn.sigmoid(-fx)
    bq = bq_ref[...]
    rq = bq * jax.nn.sigmoid(bq)
    rv = bi_ref[...]
    p1 = logf.astype(BF16)
    p2 = (logf - p1.astype(F32)).astype(BF16)
    seg = seg_ref[...]
    sums = jnp.dot(seg, p1, preferred_element_type=F32) + jnp.dot(seg, p2, preferred_element_type=F32)
    b_in = sums[:ct]
    b_out = sums[ct:2 * ct]
    q_in = (rq * jnp.exp(b_in)).astype(BF16)
    k_out = (kk * jnp.exp(b_out)).astype(BF16)
    rvb = rv.astype(BF16)
    row = lax.broadcasted_iota(I32, (ct, B_KEY_DIM), 0)
    nt = (((1,), (1,)), ((), ()))

    a = [jnp.zeros((ct, ct), F32) for _ in range(B_HEADS)]
    for lvl in range(n_lvl):
        half = ct >> (lvl + 1)
        second = (row & half) != 0
        dec = jnp.exp(sums[(2 + lvl) * ct:(3 + lvl) * ct])
        for h in range(B_HEADS):
            ls = slice(h * B_KEY_DIM, (h + 1) * B_KEY_DIM)
            x = (jnp.where(second, rq[:, ls], kk[:, ls]) * dec[:, ls]).astype(BF16)
            a[h] = a[h] + pair_ref[lvl] * lax.dot_general(x, x, nt, preferred_element_type=F32)

    for h in range(B_HEADS):
        ls = slice(h * B_KEY_DIM, (h + 1) * B_KEY_DIM)
        st = st_scr[h]
        own = jnp.sum(rq[:, ls] * kk[:, ls], axis=1, keepdims=True)
        o_ref[:, ls] = (jnp.dot(a[h].astype(BF16), rvb[:, ls], preferred_element_type=F32)
                        + own * rv[:, ls]
                        + lax.dot_general(q_in[:, ls], st.astype(BF16), nt, preferred_element_type=F32))
        upd = lax.dot_general(rvb[:, ls], k_out[:, ls], (((0,), (0,)), ((), ())),
                              preferred_element_type=F32)
        st_scr[h] = st * jnp.exp(b_in[ct - 1:ct, ls]) + upd

    @pl.when(c == pl.num_programs(1) - 1)
    def _():
        for h in range(B_HEADS):
            sout_ref[h] = st_scr[h].T


def _hgrn2(bq, bf, bi, lb, s0):
    bsz, t, _ = bq.shape
    ct = HGRN_CHUNK
    tp = -(-t // ct) * ct
    if tp != t:
        pad = lambda a, v: jnp.pad(a, ((0, 0), (0, tp - t), (0, 0)), constant_values=v)
        bq, bf, bi = pad(bq, 0.0), pad(bf, 1e4), pad(bi, 0.0)
    seg, pair = _hgrn2_tables(ct)
    seg = jnp.asarray(seg, BF16)
    pair = jnp.asarray(pair, F32)
    tok = pl.BlockSpec((None, ct, B_KW), lambda b, c: (b, c, 0))
    s_map = (lambda b, c: (0, 0, 0, 0)) if s0.shape[0] == 1 else (lambda b, c: (b, 0, 0, 0))
    s_in = pl.BlockSpec((None, B_HEADS, B_KEY_DIM, B_VAL_DIM), s_map)
    s_out = pl.BlockSpec((None, B_HEADS, B_KEY_DIM, B_VAL_DIM), lambda b, c: (b, 0, 0, 0))
    o, s = pl.pallas_call(
        _hgrn2_kernel,
        grid=(bsz, tp // ct),
        in_specs=[tok, tok, tok, _resident(lb.shape), _resident(seg.shape), _resident(pair.shape), s_in],
        out_specs=[tok, s_out],
        out_shape=[jax.ShapeDtypeStruct((bsz, tp, B_WIDTH), F32),
                   jax.ShapeDtypeStruct((bsz, B_HEADS, B_KEY_DIM, B_VAL_DIM), F32)],
        scratch_shapes=[pltpu.VMEM((B_HEADS, B_VAL_DIM, B_KEY_DIM), F32)],
        compiler_params=pltpu.CompilerParams(
            dimension_semantics=("arbitrary", "arbitrary"), vmem_limit_bytes=VMEM_LIMIT),
        name="hgrn2",
    )(bq, bf, bi, lb, seg, pair, s0)
    return o[:, :t], s


def _mix_out_kernel(h_ref, oa_ref, orec_ref, wgate_ref, gn_ref, wa_ref, wb_ref, wo_ref,
                    g2_ref, b2_ref, wg_ref, wu_ref, wd_ref, g3_ref, b3_ref, y_ref):
    h = h_ref[...]
    gates = jnp.dot(h.astype(BF16), wgate_ref[...], preferred_element_type=F32)
    bg = gates[:, :B_WIDTH]
    ga = gates[:, B_WIDTH:B_WIDTH + D_MODEL]
    gb = gates[:, B_WIDTH + D_MODEL:]
    orec = orec_ref[...]
    gn = gn_ref[...]
    parts = []
    for head in range(B_HEADS):
        ls = slice(head * B_VAL_DIM, (head + 1) * B_VAL_DIM)
        oh = orec[:, ls]
        ms = jnp.mean(oh * oh, axis=-1, keepdims=True)
        parts.append(oh * lax.rsqrt(ms + RMS_EPS) * gn[:, ls])
    on = jnp.concatenate(parts, axis=-1)
    ob = on * (bg * jax.nn.sigmoid(bg))
    pa = jnp.dot(oa_ref[...], wa_ref[...], preferred_element_type=F32)
    pb = jnp.dot(ob.astype(BF16), wb_ref[...], preferred_element_type=F32)
    mixed = jax.nn.sigmoid(ga) * pa + jax.nn.sigmoid(gb) * pb
    mixed = jnp.dot(mixed.astype(BF16), wo_ref[...], preferred_element_type=F32)
    h2 = _layer_norm(ALPHA * h + mixed, g2_ref[...], b2_ref[...])
    y_ref[...] = _layer_norm(_swiglu_step(h2, wg_ref, wu_ref, wd_ref), g3_ref[...], b3_ref[...])


def _mix_out(h, oa, orec, wgate, gn, wa, wb, wo, g2, b2, wg, wu, wd, g3, b3, tm):
    m = h.shape[0]
    assert m % tm == 0
    row = lambda w: pl.BlockSpec((tm, w), lambda i: (i, 0))
    weights = (wgate, gn, wa, wb, wo, g2, b2, wg, wu, wd, g3, b3)
    return pl.pallas_call(
        _mix_out_kernel,
        grid=(m // tm,),
        in_specs=[row(D_MODEL), row(A_Q), row(B_WIDTH)] + [_resident(w.shape) for w in weights],
        out_specs=row(D_MODEL),
        out_shape=jax.ShapeDtypeStruct((m, D_MODEL), F32),
        compiler_params=pltpu.CompilerParams(
            dimension_semantics=("arbitrary",), vmem_limit_bytes=VMEM_LIMIT),
        name="mix_out",
    )(h, oa, orec, *weights)


def _ffn_weights(wg, wu, wd):
    wg = wg.astype(BF16).reshape(D_MODEL, N_FF_CHUNKS, FF_CHUNK).transpose(1, 0, 2)
    wu = wu.astype(BF16).reshape(D_MODEL, N_FF_CHUNKS, FF_CHUNK).transpose(1, 0, 2)
    wd = wd.astype(BF16).reshape(N_FF_CHUNKS, FF_CHUNK, D_MODEL)
    return wg, wu, wd


def _pad_rows(a, n):
    return jnp.pad(a, ((0, 0), (0, n - a.shape[1]), (0, 0)))


FFN_IN_TILE = 512
MIX_OUT_TILE = 512


def _token_tile(m, largest):
    tm = largest
    while tm > 8 and m % tm:
        tm //= 2
    assert m % tm == 0, m
    return tm


def kernel(x_prompt, x_sample, cache_k, cache_v, cache_kidx, state_hgrn, meta, ln1_g, ln1_b, ffn1_wg, ffn1_wu, ffn1_wd, w_in, lb_param, g_norm, w_a_proj, w_b_proj, w_out, ln2_g, ln2_b, ffn2_wg, ffn2_wu, ffn2_wd, ln3_g, ln3_b):
    assert ln1_g.shape[0] == DEPTH
    bp, seq, _ = x_prompt.shape
    bs, dseq, _ = x_sample.shape
    past = cache_k.shape[2]
    l = 0
    vec = lambda a: a.reshape(1, -1).astype(F32)

    lb_all = jnp.cumsum(jax.nn.softmax(lb_param.astype(F32), axis=0), axis=0)
    lb = lb_all[l].reshape(1, B_KW)
    f1 = _ffn_weights(ffn1_wg[l], ffn1_wu[l], ffn1_wd[l])
    f2 = _ffn_weights(ffn2_wg[l], ffn2_wu[l], ffn2_wd[l])
    w_pieces, off = [], 0
    for _, width, _ in _PROJ_OUT:
        w_pieces.append(w_in[l][:, off:off + width].astype(BF16))
        off += width
    w_gate = w_in[l][:, off:].astype(BF16)
    assert w_gate.shape[1] == sum(_GATE_WIDTHS)

    def stage1(x):
        return _ffn_in(x, vec(ln1_g[l]), vec(ln1_b[l]), *f1, w_pieces,
                       _token_tile(x.shape[0], FFN_IN_TILE))

    def stage4(p, oa, orec):
        m = p["h"].shape[0]
        return _mix_out(p["h"], oa.reshape(m, A_Q), orec.reshape(m, B_WIDTH), w_gate,
                        vec(g_norm[l]), w_a_proj[l].astype(BF16), w_b_proj[l].astype(BF16),
                        w_out[l].astype(BF16), vec(ln2_g[l]), vec(ln2_b[l]), *f2,
                        vec(ln3_g[l]), vec(ln3_b[l]), _token_tile(m, MIX_OUT_TILE))

    pm = stage1(meta.astype(F32))
    pp = stage1(x_prompt.reshape(bp * seq, D_MODEL))
    ps = stage1(x_sample.reshape(bs * dseq, D_MODEL))
    per_p = lambda a: a.reshape(bp, seq, a.shape[-1])
    per_s = lambda a: a.reshape(bs, dseq, a.shape[-1])

    def with_meta(name):
        rows = jnp.broadcast_to(pm[name][None], (bp, N_META, pm[name].shape[-1]))
        return jnp.concatenate([rows, per_p(pp[name])], axis=1)

    k_p, v_p, ki_p = with_meta("ak"), with_meta("av"), with_meta("ik")

    kb_p = 512
    s_p = -(-(N_META + seq) // kb_p) * kb_p
    qpos = np.arange(seq, dtype=np.int32)
    lim = N_META + CHUNK * (qpos // CHUNK + 1)
    heads_p = lambda a: a.reshape(a.shape[0], bp, seq, a.shape[-1])
    kv_heads = lambda a: _pad_rows(a.astype(BF16), s_p).reshape(bp, s_p, A_KV_HEADS, A_HEAD_DIM)
    vt = kv_heads(v_p).transpose(2, 0, 3, 1)
    vt = jnp.concatenate([vt, jnp.ones_like(vt[:, :, :1]),
                          jnp.zeros_like(vt[:, :, :VT_ROWS - A_HEAD_DIM - 1])], axis=2)
    oa_p = _attend_t(heads_p(pp["iq"]), per_p(pp["iw"]).transpose(0, 2, 1), heads_p(pp["aq"]),
                     _pad_rows(ki_p.astype(BF16), s_p), kv_heads(k_p).transpose(2, 0, 1, 3),
                     vt, qpos, lim,
                     kb_size=kb_p, topk=min(TOPK_MAX, seq // 4), koff=N_META)

    n_all = past + dseq
    kb_s = 384
    s_s = -(-n_all // kb_s) * kb_s
    k_all = jnp.concatenate([cache_k[l].reshape(bs, past, A_KV).astype(BF16), per_s(ps["ak"]).astype(BF16)], axis=1)
    v_all = jnp.concatenate([cache_v[l].reshape(bs, past, A_KV).astype(BF16), per_s(ps["av"]).astype(BF16)], axis=1)
    ki_all = jnp.concatenate([cache_kidx[l].astype(BF16), per_s(ps["ik"]).astype(BF16)], axis=1)
    tokens_s = lambda a: a.transpose(1, 0, 2).reshape(bs, dseq, a.shape[0] * a.shape[-1])
    oa_s = _attend(tokens_s(ps["iq"]), per_s(ps["iw"]), tokens_s(ps["aq"]),
                   _pad_rows(ki_all, s_s), _pad_rows(k_all, s_s), _pad_rows(v_all, s_s),
                   past + jnp.arange(dseq, dtype=I32), jnp.full((dseq,), n_all, I32),
                   tq=dseq, kb_size=kb_s, topk=min(TOPK_MAX, n_all // 4), koff=0)

    zero_state = jnp.zeros((1, B_HEADS, B_KEY_DIM, B_VAL_DIM), F32)
    _, st_m = _hgrn2(pm["bq"][None], pm["bf"][None], pm["bi"][None], lb, zero_state)
    orec_p, st_p = _hgrn2(per_p(pp["bq"]), per_p(pp["bf"]), per_p(pp["bi"]), lb, st_m)
    orec_s, st_s = _hgrn2(per_s(ps["bq"]), per_s(ps["bf"]), per_s(ps["bi"]), lb,
                          state_hgrn[l].astype(F32))

    y_p = stage4(pp, oa_p, orec_p).reshape(bp, seq, D_MODEL)
    y_s = stage4(ps, oa_s, orec_s).reshape(bs, dseq, D_MODEL)

    kv5 = lambda a: a.reshape(1, a.shape[0], a.shape[1], A_KV_HEADS, A_HEAD_DIM)
    return (y_p, y_s, kv5(k_p), kv5(v_p), ki_p[None], st_p[None].astype(state_hgrn.dtype),
            kv5(per_s(ps["ak"])), kv5(per_s(ps["av"])), per_s(ps["ik"])[None],
            st_s[None].astype(state_hgrn.dtype))
```

```python
import functools

import jax
import jax.numpy as jnp
import numpy as np
from jax import lax
from jax.experimental import pallas as pl
from jax.experimental.pallas import tpu as pltpu

F32 = jnp.float32
BF16 = jnp.bfloat16
I32 = jnp.int32
I16 = jnp.int16

D_MODEL = 1024
D_FF = 2816
FF_CHUNK = 256
N_FF_CHUNKS = D_FF // FF_CHUNK
CHUNK = 64
N_META = 16
A_HEADS = 8
A_KV_HEADS = 2
A_GROUP = A_HEADS // A_KV_HEADS
A_HEAD_DIM = 64
A_Q = A_HEADS * A_HEAD_DIM
A_KV = A_KV_HEADS * A_HEAD_DIM
IDX_HEADS = 8
IDX_DIM = 64
IDX_Q = IDX_HEADS * IDX_DIM
TOPK_MAX = 256
B_HEADS = 4
B_KEY_DIM = 128
B_VAL_DIM = 128
B_KW = B_HEADS * B_KEY_DIM
B_WIDTH = B_HEADS * B_VAL_DIM
DEPTH = 1
ALPHA = (2.0 * DEPTH) ** 0.25
LN_EPS = 1e-5
RMS_EPS = 1e-6
NEG_INF = -1e30
LOG2E = 1.4426950408889634
INT_MIN = -(2 ** 31)
INT16_MIN = -(2 ** 15)
LANES = 128
VMEM_LIMIT = 56 * 1024 * 1024


def _resident(shape):
    n = len(shape)
    return pl.BlockSpec(shape, lambda *_: (0,) * n, pipeline_mode=pl.Buffered(1))


def _layer_norm(x, g, b):
    mu = jnp.mean(x, axis=-1, keepdims=True)
    xc = x - mu
    var = jnp.mean(xc * xc, axis=-1, keepdims=True)
    return xc * lax.rsqrt(var + LN_EPS) * g + b


def _swiglu_step(x, wg_ref, wu_ref, wd_ref):
    xb = x.astype(BF16)

    acc = jnp.zeros(x.shape, F32)
    for c in range(N_FF_CHUNKS):
        cols = slice(c * FF_CHUNK, (c + 1) * FF_CHUNK)
        g = jnp.dot(xb, wg_ref[:, cols], preferred_element_type=F32)
        u = jnp.dot(xb, wu_ref[:, cols], preferred_element_type=F32)
        a = (g * jax.nn.sigmoid(g) * u).astype(BF16)
        acc = acc + jnp.dot(a, wd_ref[c], preferred_element_type=F32)
    return ALPHA * x + 0.5 * acc


_PROJ_OUT = (
    ("aq", A_Q, BF16), ("ak", A_KV, F32), ("av", A_KV, F32), ("iq", IDX_Q, BF16),
    ("ik", IDX_DIM, F32), ("iw", IDX_HEADS, F32),
    ("bq", B_KW, F32), ("bf", B_KW, F32), ("bi", B_WIDTH, F32),
)
_GATE_WIDTHS = (B_WIDTH, D_MODEL, D_MODEL)
_PROJ_SCALE = {"aq": (A_HEAD_DIM ** -0.5) * LOG2E, "iw": (IDX_HEADS ** -0.5) * (IDX_DIM ** -0.5)}
_HEAD_MAJOR = {"aq": A_HEADS, "iq": IDX_HEADS}


def _ffn_in_kernel(x_ref, g_ref, b_ref, wg_ref, wu_ref, wd_ref, *rest):
    n = len(_PROJ_OUT)
    w_refs, h_ref, out_refs = rest[:n], rest[n], rest[n + 1:]
    x = x_ref[...]
    h = _layer_norm(_swiglu_step(x, wg_ref, wu_ref, wd_ref), g_ref[...], b_ref[...])
    h_ref[...] = h
    hb = h.astype(BF16)
    for (name, _, dt), w_ref, o_ref in zip(_PROJ_OUT, w_refs, out_refs):
        y = jnp.dot(hb, w_ref[...], preferred_element_type=F32)
        if name in _PROJ_SCALE:
            y = y * _PROJ_SCALE[name]
        if name in _HEAD_MAJOR:
            hd = y.shape[1] // _HEAD_MAJOR[name]
            for h in range(_HEAD_MAJOR[name]):
                o_ref[h] = y[:, h * hd:(h + 1) * hd].astype(dt)
        else:
            o_ref[...] = y.astype(dt)


def _ffn_in(x, ln_g, ln_b, wg, wu, wd, w_pieces, tm):
    m = x.shape[0]
    assert m % tm == 0
    row = lambda w: pl.BlockSpec((tm, w), lambda i: (i, 0))
    in_specs = [row(D_MODEL), _resident(ln_g.shape), _resident(ln_b.shape),
                _resident(wg.shape), _resident(wu.shape), _resident(wd.shape)]
    in_specs += [_resident(w.shape) for w in w_pieces]
    out_shape = [jax.ShapeDtypeStruct((m, D_MODEL), F32)]
    out_specs = [row(D_MODEL)]
    for name, width, dt in _PROJ_OUT:
        if name in _HEAD_MAJOR:
            nh = _HEAD_MAJOR[name]
            out_shape.append(jax.ShapeDtypeStruct((nh, m, width // nh), dt))
            out_specs.append(pl.BlockSpec((nh, tm, width // nh), lambda i: (0, i, 0)))
        else:
            out_shape.append(jax.ShapeDtypeStruct((m, width), dt))
            out_specs.append(row(width))
    outs = pl.pallas_call(
        _ffn_in_kernel,
        grid=(m // tm,),
        in_specs=in_specs,
        out_specs=out_specs,
        out_shape=out_shape,
        compiler_params=pltpu.CompilerParams(
            dimension_semantics=("arbitrary",), vmem_limit_bytes=VMEM_LIMIT),
        name="ffn_in",
    )(x, ln_g, ln_b, wg, wu, wd, *w_pieces)
    res = {"h": outs[0]}
    for (name, _, _), o in zip(_PROJ_OUT, outs[1:]):
        res[name] = o
    return res


def _attend_kernel(nkb_ref, qpos_ref, lim_ref, iq_ref, wi_ref, q_ref, ki_ref, k_ref, v_ref,
                   o_ref, key_scr, t_scr, *, topk, koff, kb_size, slopes):
    tq = q_ref.shape[0]
    nkb = nkb_ref[pl.program_id(1)]
    lim = lim_ref[...]
    qpos = qpos_ref[...]
    sub = kb_size // LANES

    def lane_ids(s0):
        return s0 + lax.broadcasted_iota(I32, (tq, kb_size), 1)

    def score_block(kb, carry):
        s0 = pl.multiple_of(kb * kb_size, kb_size)
        kib = ki_ref[pl.ds(s0, kb_size), :]
        acc = jnp.zeros((tq, kb_size), F32)
        for h in range(IDX_HEADS):
            lg = lax.dot_general(iq_ref[:, h * IDX_DIM:(h + 1) * IDX_DIM], kib,
                                 (((1,), (1,)), ((), ())), preferred_element_type=F32)
            acc = acc + jnp.maximum(lg, 0.0) * wi_ref[:, h:h + 1]
        bits = lax.bitcast_convert_type(acc, I32)
        key = bits ^ ((bits >> 31) & 0x7FFFFFFF)
        key_scr[:, pl.ds(s0, kb_size)] = jnp.where(lane_ids(s0) < lim, key, INT_MIN)
        return carry

    lax.fori_loop(0, nkb, score_block, 0)

    def bcast(x):
        return jnp.broadcast_to(x, (tq, LANES))

    def count(pred):
        def body(kb, acc):
            for c in range(sub):
                off = pl.multiple_of(kb * kb_size + c * LANES, LANES)
                col = off + lax.broadcasted_iota(I32, (tq, LANES), 1)
                acc = acc + jnp.where(pred(key_scr[:, pl.ds(off, LANES)], col), 1.0, 0.0)
            return acc
        acc = lax.fori_loop(0, nkb, body, jnp.zeros((tq, LANES), F32))
        return jnp.sum(acc, axis=1, keepdims=True)

    def bit_step(state):
        i, t, settled, _ = state
        stop = jnp.min(settled)
        cand = t + lax.shift_left(jnp.int32(1), 31 - i)
        cand_b = bcast(cand)
        cnt = count(lambda key, col: key >= cand_b)
        take = (cnt >= topk) & (settled == 0.0)
        settled = jnp.where(take & (cnt == topk), 1.0, settled)
        return i + 1, jnp.where(take, cand, t), settled, stop

    _, t, _, _ = lax.while_loop(
        lambda state: (state[0] < 32) & (state[3] == 0.0), bit_step,
        (jnp.int32(0), jnp.full((tq, 1), INT_MIN, I32), (lim <= topk).astype(F32), jnp.float32(0.0)))
    thr = jnp.maximum(t, INT_MIN + 1)
    thr_b = bcast(thr)
    n_ge = count(lambda key, col: key >= thr_b)

    @pl.when(jnp.max(n_ge) > topk)
    def _():
        n_gt = count(lambda key, col: key > thr_b)
        need = topk - n_gt
        n_bits = (key_scr.shape[1] - 1).bit_length()

        def idx_step(i, m0):
            cand = m0 + lax.shift_left(jnp.int32(1), n_bits - 1 - i)
            cand_b = bcast(cand)
            g = count(lambda key, col: (key == thr_b) & (col < cand_b))
            return jnp.where(g < need, cand, m0)

        m0 = lax.fori_loop(0, n_bits, idx_step, jnp.zeros((tq, 1), I32))
        drop_row = n_ge > topk

        def demote(kb, carry):
            s0 = pl.multiple_of(kb * kb_size, kb_size)
            key = key_scr[:, pl.ds(s0, kb_size)]
            drop = drop_row & (key == thr) & (lane_ids(s0) > m0)
            key_scr[:, pl.ds(s0, kb_size)] = jnp.where(drop, INT_MIN, key)
            return carry

        lax.fori_loop(0, nkb, demote, 0)

    def attend_block(kb, carry):
        ms, ls, accs = carry
        s0 = pl.multiple_of(kb * kb_size, kb_size)
        col = lane_ids(s0)
        bias = jnp.where(key_scr[:, pl.ds(s0, kb_size)] >= thr, 0.0, NEG_INF)
        dist = jnp.abs(qpos - (col - koff)).astype(F32)
        kblk = k_ref[pl.ds(s0, kb_size), :]
        vblk = v_ref[pl.ds(s0, kb_size), :]
        new_m, new_l, new_acc = [], [], []
        for h in range(A_HEADS):
            g = h // A_GROUP
            s = lax.dot_general(q_ref[:, h * A_HEAD_DIM:(h + 1) * A_HEAD_DIM],
                                kblk[:, g * A_HEAD_DIM:(g + 1) * A_HEAD_DIM],
                                (((1,), (1,)), ((), ())), preferred_element_type=F32)
            s = (s - slopes[h] * dist) + bias
            t_scr[h] = s
            new_m.append(jnp.maximum(ms[h], jnp.max(s, axis=1, keepdims=True)))
        for h in range(A_HEADS):
            g = h // A_GROUP
            p = jnp.exp2(t_scr[h] - new_m[h])
            alpha = jnp.exp2(ms[h] - new_m[h])
            new_l.append(alpha * ls[h] + jnp.sum(p, axis=1, keepdims=True))
            pv = jnp.dot(p.astype(BF16), vblk[:, g * A_HEAD_DIM:(g + 1) * A_HEAD_DIM],
                         preferred_element_type=F32)
            new_acc.append(alpha * accs[h] + pv)
        return tuple(new_m), tuple(new_l), tuple(new_acc)

    init = (tuple(jnp.full((tq, 1), NEG_INF, F32) for _ in range(A_HEADS)),
            tuple(jnp.zeros((tq, 1), F32) for _ in range(A_HEADS)),
            tuple(jnp.zeros((tq, A_HEAD_DIM), F32) for _ in range(A_HEADS)))
    _, ls, accs = lax.fori_loop(0, nkb, attend_block, init)
    for h in range(A_HEADS):
        o_ref[:, h * A_HEAD_DIM:(h + 1) * A_HEAD_DIM] = (accs[h] / ls[h]).astype(o_ref.dtype)


def _attend(iq, wi, q, ki, k, v, qpos, lim, *, tq, kb_size, topk, koff):
    bsz, nq, _ = q.shape
    s_pad = k.shape[1]
    assert nq % tq == 0 and s_pad % kb_size == 0
    nt = nq // tq
    nkb = (jnp.max(lim.reshape(nt, tq), axis=1) + kb_size - 1) // kb_size
    slopes = _alibi_slopes_log2()
    qrow = lambda w: pl.BlockSpec((None, tq, w), lambda b, j, n: (b, j, 0))
    krow = lambda w: pl.BlockSpec((None, s_pad, w), lambda b, j, n: (b, 0, 0))
    pos = pl.BlockSpec((tq, 1), lambda b, j, n: (j, 0))
    return pl.pallas_call(
        functools.partial(_attend_kernel, topk=topk, koff=koff, kb_size=kb_size, slopes=slopes),
        grid_spec=pltpu.PrefetchScalarGridSpec(
            num_scalar_prefetch=1,
            grid=(bsz, nt),
            in_specs=[pos, pos, qrow(IDX_Q), qrow(IDX_HEADS), qrow(A_Q),
                      krow(IDX_DIM), krow(A_KV), krow(A_KV)],
            out_specs=qrow(A_Q),
            scratch_shapes=[pltpu.VMEM((tq, s_pad), I32),
                            pltpu.VMEM((A_HEADS, tq, kb_size), F32)],
        ),
        out_shape=jax.ShapeDtypeStruct((bsz, nq, A_Q), BF16),
        compiler_params=pltpu.CompilerParams(
            dimension_semantics=("arbitrary", "arbitrary"), vmem_limit_bytes=VMEM_LIMIT),
        name="attend",
    )(nkb.astype(I32), qpos.reshape(nq, 1), lim.reshape(nq, 1), iq, wi, q, ki, k, v)


def _fold_rows(x, op, rows=8):
    while x.shape[0] > rows:
        half = x.shape[0] // 2
        x = op(x[:half], x[half:])
    return x


VT_ROWS = A_HEAD_DIM + 16


def _alibi_slopes_log2():
    return tuple(float(2.0 ** (-8.0 * (i + 1) / A_HEADS)) * LOG2E for i in range(A_HEADS))


ALIBI_SPLIT = CHUNK


def _alibi_columns(n_keys):
    q_cols = np.zeros((A_HEADS, A_HEAD_DIM), np.float32)
    for h, c in enumerate(_alibi_slopes_log2()):
        for i in range(3):
            ci = float(np.asarray(c, dtype=BF16).astype(np.float32))
            q_cols[h, 2 * i], q_cols[h, 2 * i + 1] = ALIBI_SPLIT * ci, ci
            c -= ci
    ids = np.arange(n_keys)
    k_cols = np.zeros((n_keys, A_HEAD_DIM), np.float32)
    k_cols[:, 0:6:2] = (ids // ALIBI_SPLIT)[:, None]
    k_cols[:, 1:6:2] = (ids % ALIBI_SPLIT)[:, None]
    return jnp.asarray(q_cols, BF16), jnp.asarray(k_cols, BF16)


def _attend_t_kernel(qpos_ref, lim_ref, iq_ref, wi_ref, q_ref, ki_ref, k_ref, vt_ref,
                     o_ref, key_scr, hi_scr, lo_scr, t_scr, *, nkb, first_qpos, topk, koff, kb_size,
                     slopes):
    tq = o_ref.shape[0]
    lim = lim_ref[...]
    qpos = qpos_ref[...]
    n_pairs = A_HEADS // 2
    blocks = [(kb * kb_size, pl.ds(kb * kb_size, kb_size)) for kb in range(nkb)]

    def row_ids(s0):
        return s0 + lax.broadcasted_iota(I32, (kb_size, tq), 0)

    def head_pair(ref, pr):
        return ref[2 * pr:2 * pr + 2].reshape(2 * tq, ref.shape[-1])

    for s0, rows in blocks:
        kib = ki_ref[rows, :]
        acc = jnp.zeros((kb_size, tq), F32)
        for pr in range(IDX_HEADS // 2):
            lg = lax.dot_general(kib, head_pair(iq_ref, pr), (((1,), (1,)), ((), ())),
                                 preferred_element_type=F32)
            for u in range(2):
                h = 2 * pr + u
                acc = acc + jnp.maximum(lg[:, u * tq:(u + 1) * tq], 0.0) * wi_ref[h:h + 1, :]
        bits = lax.bitcast_convert_type(acc, I32)
        key = bits ^ ((bits >> 31) & 0x7FFFFFFF)
        key = jnp.where(row_ids(s0) < lim, key, INT_MIN)
        key_scr[rows, :] = key
        hi_scr[rows, :] = (key >> 16).astype(I16)

    def count(pred):
        acc = jnp.zeros((8, tq), F32)
        for _, rows in blocks:
            acc = acc + _fold_rows(jnp.where(pred(key_scr[rows, :]), 1.0, 0.0), jnp.add)
        return jnp.sum(acc, axis=0, keepdims=True)

    def count16(scr, pred):
        acc = jnp.zeros((16, tq), I16)
        for _, rows in blocks:
            hit = jnp.where(pred(scr[rows, :]), jnp.int16(1), jnp.int16(0))
            acc = acc + _fold_rows(hit, jnp.add, 16)
        return jnp.sum(acc.astype(F32), axis=0, keepdims=True)

    def kth_largest16(scr, kth):
        def bit_step(i, t):
            cand = t + lax.shift_left(jnp.int32(1), 15 - i)
            cand16 = cand.astype(I16)
            return jnp.where(count16(scr, lambda blk: blk >= cand16) >= kth, cand, t)
        return lax.fori_loop(0, 16, bit_step, jnp.full((1, tq), INT16_MIN, I32))

    t_hi = kth_largest16(hi_scr, topk)
    t_hi16 = t_hi.astype(I16)
    kth_lo = topk - count16(hi_scr, lambda blk: blk > t_hi16)
    for _, rows in blocks:
        key = key_scr[rows, :]
        lo = jnp.where((key >> 16) == t_hi, (key & 0xFFFF) + INT16_MIN, INT16_MIN)
        lo_scr[rows, :] = lo.astype(I16)
    t_lo = kth_largest16(lo_scr, kth_lo)
    t = lax.shift_left(t_hi, 16) | (t_lo - INT16_MIN)
    thr = jnp.maximum(t, INT_MIN + 1)
    n_ge = count(lambda key: key >= thr)

    @pl.when(jnp.max(n_ge) > topk)
    def _():
        need = topk - count(lambda key: key > thr)
        for s0, rows in blocks:
            lo_scr[rows, :] = jnp.where(key_scr[rows, :] == thr, -row_ids(s0), INT16_MIN).astype(I16)
        last = kth_largest16(lo_scr, need)
        drop_q = n_ge > topk
        for s0, rows in blocks:
            key = key_scr[rows, :]
            drop = drop_q & (key == thr) & (-row_ids(s0) < last)
            key_scr[rows, :] = jnp.where(drop, INT_MIN, key)

    ms = [jnp.full((1, tq), NEG_INF, F32) for _ in range(A_HEADS)]
    accs = [jnp.zeros((vt_ref.shape[1], A_GROUP * tq), F32) for _ in range(A_KV_HEADS)]
    for kb, (s0, rows) in enumerate(blocks):
        bias = jnp.where(key_scr[rows, :] >= thr, 0.0, NEG_INF)
        may_follow = (kb + 1) * kb_size - 1 - koff > first_qpos
        if may_follow:
            ahead = jnp.minimum(qpos - (row_ids(s0) - koff), 0).astype(F32)
        new_m = []
        for pr in range(n_pairs):
            g = (2 * pr) // A_GROUP
            s2 = lax.dot_general(k_ref[g, rows, :], head_pair(q_ref, pr),
                                 (((1,), (1,)), ((), ())), preferred_element_type=F32)
            for u in range(2):
                h = 2 * pr + u
                s = s2[:, u * tq:(u + 1) * tq]
                if may_follow:
                    s = s + (2.0 * slopes[h]) * ahead
                s = s + bias
                t_scr[h] = s
                blk_max = jnp.max(_fold_rows(s, jnp.maximum), axis=0, keepdims=True)
                new_m.append(jnp.maximum(ms[h], blk_max))
        for g in range(A_KV_HEADS):
            ps, alphas = [], []
            for h in range(g * A_GROUP, (g + 1) * A_GROUP):
                ps.append(jnp.exp2(t_scr[h] - new_m[h]).astype(BF16))
                alphas.append(jnp.exp2(ms[h] - new_m[h]))
            pv = jnp.dot(vt_ref[g, :, rows], jnp.concatenate(ps, axis=1),
                         preferred_element_type=F32)
            accs[g] = jnp.concatenate(alphas, axis=1) * accs[g] + pv
        ms = new_m
    for pr in range(n_pairs):
        rows = []
        for h in (2 * pr, 2 * pr + 1):
            g, hh = divmod(h, A_GROUP)
            cols = slice(hh * tq, (hh + 1) * tq)
            rows.append(accs[g][:A_HEAD_DIM, cols] / accs[g][A_HEAD_DIM:A_HEAD_DIM + 1, cols])
        o_ref[:, 2 * pr * A_HEAD_DIM:(2 * pr + 2) * A_HEAD_DIM] = (
            jnp.concatenate(rows, axis=0).T.astype(o_ref.dtype))


def _attend_t(iq, wi_t, q, ki, k, vt, qpos, lim, *, kb_size, topk, koff):
    tq = LANES
    _, bsz, nq, _ = q.shape
    s_pad = ki.shape[1]
    assert nq % tq == 0 and s_pad % kb_size == 0
    nkb_of_tile = -(-lim.reshape(nq // tq, tq).max(axis=1) // kb_size)
    runs, j0 = [], 0
    for j in range(1, len(nkb_of_tile) + 1):
        if j == len(nkb_of_tile) or nkb_of_tile[j] != nkb_of_tile[j0]:
            runs.append((j0, j, int(nkb_of_tile[j0])))
            j0 = j
    qpos_row = jnp.asarray(qpos, I32).reshape(1, nq)
    lim_row = jnp.asarray(lim, I32).reshape(1, nq)
    outs = []
    for j0, j1, nkb in runs:
        heads = lambda a: pl.BlockSpec((a.shape[0], None, tq, a.shape[-1]),
                                       lambda b, j: (0, b, j + j0, 0))
        pos = pl.BlockSpec((1, tq), lambda b, j: (0, j + j0))
        outs.append(pl.pallas_call(
            functools.partial(_attend_t_kernel, nkb=nkb, first_qpos=int(qpos[j0 * tq]), topk=topk,
                              koff=koff, kb_size=kb_size, slopes=_alibi_slopes_log2()),
            grid=(bsz, j1 - j0),
            in_specs=[pos, pos, heads(iq),
                      pl.BlockSpec((None, IDX_HEADS, tq), lambda b, j: (b, 0, j + j0)),
                      heads(q),
                      pl.BlockSpec((None, s_pad, IDX_DIM), lambda b, j: (b, 0, 0)),
                      pl.BlockSpec((A_KV_HEADS, None, s_pad, k.shape[-1]), lambda b, j: (0, b, 0, 0)),
                      pl.BlockSpec((A_KV_HEADS, None, VT_ROWS, s_pad), lambda b, j: (0, b, 0, 0))],
            out_specs=pl.BlockSpec((None, tq, A_Q), lambda b, j: (b, j, 0)),
            scratch_shapes=[pltpu.VMEM((s_pad, tq), I32), pltpu.VMEM((s_pad, tq), I16),
                            pltpu.VMEM((s_pad, tq), I16),
                            pltpu.VMEM((A_HEADS, kb_size, tq), F32)],
            out_shape=jax.ShapeDtypeStruct((bsz, (j1 - j0) * tq, A_Q), BF16),
            compiler_params=pltpu.CompilerParams(
                dimension_semantics=("arbitrary", "arbitrary"), vmem_limit_bytes=VMEM_LIMIT),
            name=f"attend_t{nkb}",
        )(qpos_row, lim_row, iq, wi_t, q, ki, k, vt))
    return jnp.concatenate(outs, axis=1)


HGRN_CHUNK = 128


def _hgrn2_tables(ct):
    r = np.arange(ct)
    seg = [r[:, None] >= r[None, :], r[None, :] > r[:, None]]
    pair = []
    g = ct
    while g >= 2:
        mid = (r // g) * g + g // 2
        second = r >= mid
        seg.append(np.where(second[:, None],
                            (r[None, :] >= mid[:, None]) & (r[None, :] <= r[:, None]),
                            (r[None, :] > r[:, None]) & (r[None, :] < mid[:, None])))
        pair.append((r[:, None] // g == r[None, :] // g) & second[:, None] & ~second[None, :])
        g //= 2
    return np.concatenate(seg, 0).astype(np.float32), np.stack(pair).astype(np.float32)


def _hgrn2_kernel(bq_ref, bf_ref, bi_ref, lb_ref, seg_ref, pair_ref, s0_ref, o_ref, sout_ref, st_scr):
    c = pl.program_id(1)
    ct = bq_ref.shape[0]
    n_lvl = pair_ref.shape[0]

    @pl.when(c == 0)
    def _():
        for h in range(B_HEADS):
            st_scr[h] = s0_ref[h].T

    lb = lb_ref[...]
    fx = bf_ref[...]
    logf = jnp.log(lb + (1.0 - lb) * jax.nn.sigmoid(fx))
    kk = (1.0 - lb) * jax.nn.sigmoid(-fx)
    bq = bq_ref[...]
    rq = bq * jax.nn.sigmoid(bq)
    rv = bi_ref[...]
    p1 = logf.astype(BF16)
    p2 = (logf - p1.astype(F32)).astype(BF16)
    seg = seg_ref[...]
    sums = jnp.dot(seg, p1, preferred_element_type=F32) + jnp.dot(seg, p2, preferred_element_type=F32)
    b_in = sums[:ct]
    b_out = sums[ct:2 * ct]
    q_in = (rq * jnp.exp(b_in)).astype(BF16)
    k_out = (kk * jnp.exp(b_out)).astype(BF16)
    rvb = rv.astype(BF16)
    row = lax.broadcasted_iota(I32, (ct, B_KEY_DIM), 0)
    nt = (((1,), (1,)), ((), ()))

    a = [jnp.zeros((ct, ct), F32) for _ in range(B_HEADS)]
    for lvl in range(n_lvl):
        half = ct >> (lvl + 1)
        second = (row & half) != 0
        dec = jnp.exp(sums[(2 + lvl) * ct:(3 + lvl) * ct])
        for h in range(B_HEADS):
            ls = slice(h * B_KEY_DIM, (h + 1) * B_KEY_DIM)
            x = (jnp.where(second, rq[:, ls], kk[:, ls]) * dec[:, ls]).astype(BF16)
            a[h] = a[h] + pair_ref[lvl] * lax.dot_general(x, x, nt, preferred_element_type=F32)

    for h in range(B_HEADS):
        ls = slice(h * B_KEY_DIM, (h + 1) * B_KEY_DIM)
        st = st_scr[h]
        own = jnp.sum(rq[:, ls] * kk[:, ls], axis=1, keepdims=True)
        o_ref[:, ls] = (jnp.dot(a[h].astype(BF16), rvb[:, ls], preferred_element_type=F32)
                        + own * rv[:, ls]
                        + lax.dot_general(q_in[:, ls], st.astype(BF16), nt, preferred_element_type=F32))
        upd = lax.dot_general(rvb[:, ls], k_out[:, ls], (((0,), (0,)), ((), ())),
                              preferred_element_type=F32)
        st_scr[h] = st * jnp.exp(b_in[ct - 1:ct, ls]) + upd

    @pl.when(c == pl.num_programs(1) - 1)
    def _():
        for h in range(B_HEADS):
            sout_ref[h] = st_scr[h].T


def _hgrn2(bq, bf, bi, lb, s0):
    bsz, t, _ = bq.shape
    ct = HGRN_CHUNK
    tp = -(-t // ct) * ct
    if tp != t:
        pad = lambda a, v: jnp.pad(a, ((0, 0), (0, tp - t), (0, 0)), constant_values=v)
        bq, bf, bi = pad(bq, 0.0), pad(bf, 1e4), pad(bi, 0.0)
    seg, pair = _hgrn2_tables(ct)
    seg = jnp.asarray(seg, BF16)
    pair = jnp.asarray(pair, F32)
    tok = pl.BlockSpec((None, ct, B_KW), lambda b, c: (b, c, 0))
    s_map = (lambda b, c: (0, 0, 0, 0)) if s0.shape[0] == 1 else (lambda b, c: (b, 0, 0, 0))
    s_in = pl.BlockSpec((None, B_HEADS, B_KEY_DIM, B_VAL_DIM), s_map)
    s_out = pl.BlockSpec((None, B_HEADS, B_KEY_DIM, B_VAL_DIM), lambda b, c: (b, 0, 0, 0))
    o, s = pl.pallas_call(
        _hgrn2_kernel,
        grid=(bsz, tp // ct),
        in_specs=[tok, tok, tok, _resident(lb.shape), _resident(seg.shape), _resident(pair.shape), s_in],
        out_specs=[tok, s_out],
        out_shape=[jax.ShapeDtypeStruct((bsz, tp, B_WIDTH), F32),
                   jax.ShapeDtypeStruct((bsz, B_HEADS, B_KEY_DIM, B_VAL_DIM), F32)],
        scratch_shapes=[pltpu.VMEM((B_HEADS, B_VAL_DIM, B_KEY_DIM), F32)],
        compiler_params=pltpu.CompilerParams(
            dimension_semantics=("arbitrary", "arbitrary"), vmem_limit_bytes=VMEM_LIMIT),
        name="hgrn2",
    )(bq, bf, bi, lb, seg, pair, s0)
    return o[:, :t], s


def _mix_out_kernel(h_ref, oa_ref, orec_ref, wgate_ref, gn_ref, wa_ref, wb_ref, wo_ref,
                    g2_ref, b2_ref, wg_ref, wu_ref, wd_ref, g3_ref, b3_ref, y_ref):
    h = h_ref[...]
    gates = jnp.dot(h.astype(BF16), wgate_ref[...], preferred_element_type=F32)
    bg = gates[:, :B_WIDTH]
    ga = gates[:, B_WIDTH:B_WIDTH + D_MODEL]
    gb = gates[:, B_WIDTH + D_MODEL:]
    orec = orec_ref[...]
    gn = gn_ref[...]
    parts = []
    for head in range(B_HEADS):
        ls = slice(head * B_VAL_DIM, (head + 1) * B_VAL_DIM)
        oh = orec[:, ls]
        ms = jnp.mean(oh * oh, axis=-1, keepdims=True)
        parts.append(oh * lax.rsqrt(ms + RMS_EPS) * gn[:, ls])
    on = jnp.concatenate(parts, axis=-1)
    ob = on * (bg * jax.nn.sigmoid(bg))
    pa = jnp.dot(oa_ref[...], wa_ref[...], preferred_element_type=F32)
    pb = jnp.dot(ob.astype(BF16), wb_ref[...], preferred_element_type=F32)
    mixed = jax.nn.sigmoid(ga) * pa + jax.nn.sigmoid(gb) * pb
    mixed = jnp.dot(mixed.astype(BF16), wo_ref[...], preferred_element_type=F32)
    h2 = _layer_norm(ALPHA * h + mixed, g2_ref[...], b2_ref[...])
    y_ref[...] = _layer_norm(_swiglu_step(h2, wg_ref, wu_ref, wd_ref), g3_ref[...], b3_ref[...])


def _mix_out(h, oa, orec, wgate, gn, wa, wb, wo, g2, b2, wg, wu, wd, g3, b3, tm):
    m = h.shape[0]
    assert m % tm == 0
    row = lambda w: pl.BlockSpec((tm, w), lambda i: (i, 0))
    weights = (wgate, gn, wa, wb, wo, g2, b2, wg, wu, wd, g3, b3)
    return pl.pallas_call(
        _mix_out_kernel,
        grid=(m // tm,),
        in_specs=[row(D_MODEL), row(A_Q), row(B_WIDTH)] + [_resident(w.shape) for w in weights],
        out_specs=row(D_MODEL),
        out_shape=jax.ShapeDtypeStruct((m, D_MODEL), F32),
        compiler_params=pltpu.CompilerParams(
            dimension_semantics=("arbitrary",), vmem_limit_bytes=VMEM_LIMIT),
        name="mix_out",
    )(h, oa, orec, *weights)


def _ffn_weights(wg, wu, wd):
    return wg.astype(BF16), wu.astype(BF16), wd.astype(BF16).reshape(N_FF_CHUNKS, FF_CHUNK, D_MODEL)


def _pad_rows(a, n):
    return jnp.pad(a, ((0, 0), (0, n - a.shape[1]), (0, 0)))


FFN_IN_TILE = 512
MIX_OUT_TILE = 512


def _token_tile(m, largest):
    tm = largest
    while tm > 8 and m % tm:
        tm //= 2
    assert m % tm == 0, m
    return tm


def kernel(x_prompt, x_sample, cache_k, cache_v, cache_kidx, state_hgrn, meta, ln1_g, ln1_b, ffn1_wg, ffn1_wu, ffn1_wd, w_in, lb_param, g_norm, w_a_proj, w_b_proj, w_out, ln2_g, ln2_b, ffn2_wg, ffn2_wu, ffn2_wd, ln3_g, ln3_b):
    assert ln1_g.shape[0] == DEPTH
    bp, seq, _ = x_prompt.shape
    bs, dseq, _ = x_sample.shape
    past = cache_k.shape[2]
    l = 0
    vec = lambda a: a.reshape(1, -1).astype(F32)

    lb_all = jnp.cumsum(jax.nn.softmax(lb_param.astype(F32), axis=0), axis=0)
    lb = lb_all[l].reshape(1, B_KW)
    f1 = _ffn_weights(ffn1_wg[l], ffn1_wu[l], ffn1_wd[l])
    f2 = _ffn_weights(ffn2_wg[l], ffn2_wu[l], ffn2_wd[l])
    w_pieces, off = [], 0
    for _, width, _ in _PROJ_OUT:
        w_pieces.append(w_in[l][:, off:off + width].astype(BF16))
        off += width
    w_gate = w_in[l][:, off:].astype(BF16)
    assert w_gate.shape[1] == sum(_GATE_WIDTHS)

    def stage1(x):
        return _ffn_in(x, vec(ln1_g[l]), vec(ln1_b[l]), *f1, w_pieces,
                       _token_tile(x.shape[0], FFN_IN_TILE))

    def stage4(p, oa, orec):
        m = p["h"].shape[0]
        return _mix_out(p["h"], oa.reshape(m, A_Q), orec.reshape(m, B_WIDTH), w_gate,
                        vec(g_norm[l]), w_a_proj[l].astype(BF16), w_b_proj[l].astype(BF16),
                        w_out[l].astype(BF16), vec(ln2_g[l]), vec(ln2_b[l]), *f2,
                        vec(ln3_g[l]), vec(ln3_b[l]), _token_tile(m, MIX_OUT_TILE))

    pm = stage1(meta.astype(F32))
    pp = stage1(x_prompt.reshape(bp * seq, D_MODEL))
    ps = stage1(x_sample.reshape(bs * dseq, D_MODEL))
    per_p = lambda a: a.reshape(bp, seq, a.shape[-1])
    per_s = lambda a: a.reshape(bs, dseq, a.shape[-1])

    def with_meta(name):
        rows = jnp.broadcast_to(pm[name][None], (bp, N_META, pm[name].shape[-1]))
        return jnp.concatenate([rows, per_p(pp[name])], axis=1)

    k_p, v_p, ki_p = with_meta("ak"), with_meta("av"), with_meta("ik")

    kb_p = 512
    s_p = -(-(N_META + seq) // kb_p) * kb_p
    qpos = np.arange(seq, dtype=np.int32)
    lim = N_META + CHUNK * (qpos // CHUNK + 1)
    heads_p = lambda a: a.reshape(a.shape[0], bp, seq, a.shape[-1])
    kv_heads = lambda a: _pad_rows(a.astype(BF16), s_p).reshape(bp, s_p, A_KV_HEADS, A_HEAD_DIM)
    vt = kv_heads(v_p).transpose(2, 0, 3, 1)
    vt = jnp.concatenate([vt, jnp.ones_like(vt[:, :, :1]),
                          jnp.zeros_like(vt[:, :, :VT_ROWS - A_HEAD_DIM - 1])], axis=2)
    q_cols, k_cols = _alibi_columns(s_p)
    q_aug = jnp.concatenate([heads_p(pp["aq"]), jnp.broadcast_to(
        q_cols[:, None, None, :], (A_HEADS, bp, seq, A_HEAD_DIM))], axis=-1)
    k_aug = jnp.concatenate([kv_heads(k_p).transpose(2, 0, 1, 3), jnp.broadcast_to(
        k_cols[None, None], (A_KV_HEADS, bp, s_p, A_HEAD_DIM))], axis=-1)
    oa_p = _attend_t(heads_p(pp["iq"]), per_p(pp["iw"]).transpose(0, 2, 1), q_aug,
                     _pad_rows(ki_p.astype(BF16), s_p), k_aug, vt, qpos, lim,
                     kb_size=kb_p, topk=min(TOPK_MAX, seq // 4), koff=N_META)

    n_all = past + dseq
    kb_s = 384
    s_s = -(-n_all // kb_s) * kb_s
    k_all = jnp.concatenate([cache_k[l].reshape(bs, past, A_KV).astype(BF16), per_s(ps["ak"]).astype(BF16)], axis=1)
    v_all = jnp.concatenate([cache_v[l].reshape(bs, past, A_KV).astype(BF16), per_s(ps["av"]).astype(BF16)], axis=1)
    ki_all = jnp.concatenate([cache_kidx[l].astype(BF16), per_s(ps["ik"]).astype(BF16)], axis=1)
    tokens_s = lambda a: a.transpose(1, 0, 2).reshape(bs, dseq, a.shape[0] * a.shape[-1])
    oa_s = _attend(tokens_s(ps["iq"]), per_s(ps["iw"]), tokens_s(ps["aq"]),
                   _pad_rows(ki_all, s_s), _pad_rows(k_all, s_s), _pad_rows(v_all, s_s),
                   past + jnp.arange(dseq, dtype=I32), jnp.full((dseq,), n_all, I32),
                   tq=dseq, kb_size=kb_s, topk=min(TOPK_MAX, n_all // 4), koff=0)

    zero_state = jnp.zeros((1, B_HEADS, B_KEY_DIM, B_VAL_DIM), F32)
    _, st_m = _hgrn2(pm["bq"][None], pm["bf"][None], pm["bi"][None], lb, zero_state)
    orec_p, st_p = _hgrn2(per_p(pp["bq"]), per_p(pp["bf"]), per_p(pp["bi"]), lb, st_m)
    orec_s, st_s = _hgrn2(per_s(ps["bq"]), per_s(ps["bf"]), per_s(ps["bi"]), lb,
                          state_hgrn[l].astype(F32))

    y_p = stage4(pp, oa_p, orec_p).reshape(bp, seq, D_MODEL)
    y_s = stage4(ps, oa_s, orec_s).reshape(bs, dseq, D_MODEL)

    kv5 = lambda a: a.reshape(1, a.shape[0], a.shape[1], A_KV_HEADS, A_HEAD_DIM)
    return (y_p, y_s, kv5(k_p), kv5(v_p), ki_p[None], st_p[None].astype(state_hgrn.dtype),
            kv5(per_s(ps["ak"])), kv5(per_s(ps["av"])), per_s(ps["ik"])[None],
            st_s[None].astype(state_hgrn.dtype))
```

```python
import functools

import jax
import jax.numpy as jnp
import numpy as np
from jax import lax
from jax.experimental import pallas as pl
from jax.experimental.pallas import tpu as pltpu

F32 = jnp.float32
BF16 = jnp.bfloat16
I32 = jnp.int32

D_MODEL = 1024
D_FF = 2816
FF_CHUNK = 256
N_FF_CHUNKS = D_FF // FF_CHUNK
CHUNK = 64
N_META = 16
A_HEADS = 8
A_KV_HEADS = 2
A_GROUP = A_HEADS // A_KV_HEADS
A_HEAD_DIM = 64
A_Q = A_HEADS * A_HEAD_DIM
A_KV = A_KV_HEADS * A_HEAD_DIM
IDX_HEADS = 8
IDX_DIM = 64
IDX_Q = IDX_HEADS * IDX_DIM
TOPK_MAX = 256
B_HEADS = 4
B_KEY_DIM = 128
B_VAL_DIM = 128
B_KW = B_HEADS * B_KEY_DIM
B_WIDTH = B_HEADS * B_VAL_DIM
DEPTH = 1
ALPHA = (2.0 * DEPTH) ** 0.25
LN_EPS = 1e-5
RMS_EPS = 1e-6
NEG_INF = -1e30
LOG2E = 1.4426950408889634
INT_MIN = -(2 ** 31)
LANES = 128
VMEM_LIMIT = 56 * 1024 * 1024


def _resident(shape):
    n = len(shape)
    return pl.BlockSpec(shape, lambda *_: (0,) * n, pipeline_mode=pl.Buffered(1))


def _layer_norm(x, g, b):
    mu = jnp.mean(x, axis=-1, keepdims=True)
    xc = x - mu
    var = jnp.mean(xc * xc, axis=-1, keepdims=True)
    return xc * lax.rsqrt(var + LN_EPS) * g + b


def _swiglu_step(x, wg_ref, wu_ref, wd_ref):
    xb = x.astype(BF16)

    acc = jnp.zeros(x.shape, F32)
    for c in range(N_FF_CHUNKS):
        cols = slice(c * FF_CHUNK, (c + 1) * FF_CHUNK)
        g = jnp.dot(xb, wg_ref[:, cols], preferred_element_type=F32)
        u = jnp.dot(xb, wu_ref[:, cols], preferred_element_type=F32)
        a = (g * jax.nn.sigmoid(g) * u).astype(BF16)
        acc = acc + jnp.dot(a, wd_ref[c], preferred_element_type=F32)
    return ALPHA * x + 0.5 * acc


_PROJ_OUT = (
    ("aq", A_Q, BF16), ("ak", A_KV, F32), ("av", A_KV, F32), ("iq", IDX_Q, BF16),
    ("ik", IDX_DIM, F32), ("iw", IDX_HEADS, F32),
    ("bq", B_KW, F32), ("bf", B_KW, F32), ("bi", B_WIDTH, F32),
)
_GATE_WIDTHS = (B_WIDTH, D_MODEL, D_MODEL)
_PROJ_SCALE = {"aq": (A_HEAD_DIM ** -0.5) * LOG2E, "iw": (IDX_HEADS ** -0.5) * (IDX_DIM ** -0.5)}
_HEAD_MAJOR = {"aq": A_HEADS, "iq": IDX_HEADS}


def _ffn_in_kernel(x_ref, g_ref, b_ref, wg_ref, wu_ref, wd_ref, qcols_ref, *rest):
    n = len(_PROJ_OUT)
    w_refs, h_ref, out_refs = rest[:n], rest[n], rest[n + 1:]
    x = x_ref[...]
    h = _layer_norm(_swiglu_step(x, wg_ref, wu_ref, wd_ref), g_ref[...], b_ref[...])
    h_ref[...] = h
    hb = h.astype(BF16)
    for (name, _, dt), w_ref, o_ref in zip(_PROJ_OUT, w_refs, out_refs):
        y = jnp.dot(hb, w_ref[...], preferred_element_type=F32)
        if name in _PROJ_SCALE:
            y = y * _PROJ_SCALE[name]
        if name in _HEAD_MAJOR:
            hd = y.shape[1] // _HEAD_MAJOR[name]
            for head in range(_HEAD_MAJOR[name]):
                yh = y[:, head * hd:(head + 1) * hd].astype(dt)
                if name == "aq":
                    yh = jnp.concatenate([yh, jnp.broadcast_to(qcols_ref[head], yh.shape)], axis=1)
                o_ref[head] = yh
        else:
            o_ref[...] = y.astype(dt)


def _ffn_in(x, ln_g, ln_b, wg, wu, wd, q_cols, w_pieces, tm):
    m = x.shape[0]
    assert m % tm == 0
    row = lambda w: pl.BlockSpec((tm, w), lambda i: (i, 0))
    in_specs = [row(D_MODEL), _resident(ln_g.shape), _resident(ln_b.shape),
                _resident(wg.shape), _resident(wu.shape), _resident(wd.shape),
                _resident(q_cols.shape)]
    in_specs += [_resident(w.shape) for w in w_pieces]
    out_shape = [jax.ShapeDtypeStruct((m, D_MODEL), F32)]
    out_specs = [row(D_MODEL)]
    for name, width, dt in _PROJ_OUT:
        if name in _HEAD_MAJOR:
            nh = _HEAD_MAJOR[name]
            hd = width // nh + (q_cols.shape[-1] if name == "aq" else 0)
            out_shape.append(jax.ShapeDtypeStruct((nh, m, hd), dt))
            out_specs.append(pl.BlockSpec((nh, tm, hd), lambda i: (0, i, 0)))
        else:
            out_shape.append(jax.ShapeDtypeStruct((m, width), dt))
            out_specs.append(row(width))
    outs = pl.pallas_call(
        _ffn_in_kernel,
        grid=(m // tm,),
        in_specs=in_specs,
        out_specs=out_specs,
        out_shape=out_shape,
        compiler_params=pltpu.CompilerParams(
            dimension_semantics=("arbitrary",), vmem_limit_bytes=VMEM_LIMIT),
        name="ffn_in",
    )(x, ln_g, ln_b, wg, wu, wd, q_cols, *w_pieces)
    res = {"h": outs[0]}
    for (name, _, _), o in zip(_PROJ_OUT, outs[1:]):
        res[name] = o
    return res


def _attend_kernel(nkb_ref, qpos_ref, lim_ref, iq_ref, wi_ref, q_ref, ki_ref, k_ref, v_ref,
                   o_ref, key_scr, t_scr, *, topk, koff, kb_size, slopes):
    tq = q_ref.shape[0]
    nkb = nkb_ref[pl.program_id(1)]
    lim = lim_ref[...]
    qpos = qpos_ref[...]
    sub = kb_size // LANES

    def lane_ids(s0):
        return s0 + lax.broadcasted_iota(I32, (tq, kb_size), 1)

    def score_block(kb, carry):
        s0 = pl.multiple_of(kb * kb_size, kb_size)
        kib = ki_ref[pl.ds(s0, kb_size), :]
        acc = jnp.zeros((tq, kb_size), F32)
        for h in range(IDX_HEADS):
            lg = lax.dot_general(iq_ref[:, h * IDX_DIM:(h + 1) * IDX_DIM], kib,
                                 (((1,), (1,)), ((), ())), preferred_element_type=F32)
            acc = acc + jnp.maximum(lg, 0.0) * wi_ref[:, h:h + 1]
        bits = lax.bitcast_convert_type(acc, I32)
        key = bits ^ ((bits >> 31) & 0x7FFFFFFF)
        key_scr[:, pl.ds(s0, kb_size)] = jnp.where(lane_ids(s0) < lim, key, INT_MIN)
        return carry

    lax.fori_loop(0, nkb, score_block, 0)

    def bcast(x):
        return jnp.broadcast_to(x, (tq, LANES))

    def count(pred):
        def body(kb, acc):
            for c in range(sub):
                off = pl.multiple_of(kb * kb_size + c * LANES, LANES)
                col = off + lax.broadcasted_iota(I32, (tq, LANES), 1)
                acc = acc + jnp.where(pred(key_scr[:, pl.ds(off, LANES)], col), 1.0, 0.0)
            return acc
        acc = lax.fori_loop(0, nkb, body, jnp.zeros((tq, LANES), F32))
        return jnp.sum(acc, axis=1, keepdims=True)

    def bit_step(state):
        i, t, settled, _ = state
        stop = jnp.min(settled)
        cand = t + lax.shift_left(jnp.int32(1), 31 - i)
        cand_b = bcast(cand)
        cnt = count(lambda key, col: key >= cand_b)
        take = (cnt >= topk) & (settled == 0.0)
        settled = jnp.where(take & (cnt == topk), 1.0, settled)
        return i + 1, jnp.where(take, cand, t), settled, stop

    _, t, _, _ = lax.while_loop(
        lambda state: (state[0] < 32) & (state[3] == 0.0), bit_step,
        (jnp.int32(0), jnp.full((tq, 1), INT_MIN, I32), (lim <= topk).astype(F32), jnp.float32(0.0)))
    thr = jnp.maximum(t, INT_MIN + 1)
    thr_b = bcast(thr)
    n_ge = count(lambda key, col: key >= thr_b)

    @pl.when(jnp.max(n_ge) > topk)
    def _():
        n_gt = count(lambda key, col: key > thr_b)
        need = topk - n_gt
        n_bits = (key_scr.shape[1] - 1).bit_length()

        def idx_step(i, m0):
            cand = m0 + lax.shift_left(jnp.int32(1), n_bits - 1 - i)
            cand_b = bcast(cand)
            g = count(lambda key, col: (key == thr_b) & (col < cand_b))
            return jnp.where(g < need, cand, m0)

        m0 = lax.fori_loop(0, n_bits, idx_step, jnp.zeros((tq, 1), I32))
        drop_row = n_ge > topk

        def demote(kb, carry):
            s0 = pl.multiple_of(kb * kb_size, kb_size)
            key = key_scr[:, pl.ds(s0, kb_size)]
            drop = drop_row & (key == thr) & (lane_ids(s0) > m0)
            key_scr[:, pl.ds(s0, kb_size)] = jnp.where(drop, INT_MIN, key)
            return carry

        lax.fori_loop(0, nkb, demote, 0)

    def attend_block(kb, carry):
        ms, ls, accs = carry
        s0 = pl.multiple_of(kb * kb_size, kb_size)
        col = lane_ids(s0)
        bias = jnp.where(key_scr[:, pl.ds(s0, kb_size)] >= thr, 0.0, NEG_INF)
        dist = jnp.abs(qpos - (col - koff)).astype(F32)
        kblk = k_ref[pl.ds(s0, kb_size), :]
        vblk = v_ref[pl.ds(s0, kb_size), :]
        new_m, new_l, new_acc = [], [], []
        for h in range(A_HEADS):
            g = h // A_GROUP
            s = lax.dot_general(q_ref[:, h * A_HEAD_DIM:(h + 1) * A_HEAD_DIM],
                                kblk[:, g * A_HEAD_DIM:(g + 1) * A_HEAD_DIM],
                                (((1,), (1,)), ((), ())), preferred_element_type=F32)
            s = (s - slopes[h] * dist) + bias
            t_scr[h] = s
            new_m.append(jnp.maximum(ms[h], jnp.max(s, axis=1, keepdims=True)))
        for h in range(A_HEADS):
            g = h // A_GROUP
            p = jnp.exp2(t_scr[h] - new_m[h])
            alpha = jnp.exp2(ms[h] - new_m[h])
            new_l.append(alpha * ls[h] + jnp.sum(p, axis=1, keepdims=True))
            pv = jnp.dot(p.astype(BF16), vblk[:, g * A_HEAD_DIM:(g + 1) * A_HEAD_DIM],
                         preferred_element_type=F32)
            new_acc.append(alpha * accs[h] + pv)
        return tuple(new_m), tuple(new_l), tuple(new_acc)

    init = (tuple(jnp.full((tq, 1), NEG_INF, F32) for _ in range(A_HEADS)),
            tuple(jnp.zeros((tq, 1), F32) for _ in range(A_HEADS)),
            tuple(jnp.zeros((tq, A_HEAD_DIM), F32) for _ in range(A_HEADS)))
    _, ls, accs = lax.fori_loop(0, nkb, attend_block, init)
    for h in range(A_HEADS):
        o_ref[:, h * A_HEAD_DIM:(h + 1) * A_HEAD_DIM] = (accs[h] / ls[h]).astype(o_ref.dtype)


def _attend(iq, wi, q, ki, k, v, qpos, lim, *, tq, kb_size, topk, koff):
    bsz, nq, _ = q.shape
    s_pad = k.shape[1]
    assert nq % tq == 0 and s_pad % kb_size == 0
    nt = nq // tq
    nkb = (jnp.max(lim.reshape(nt, tq), axis=1) + kb_size - 1) // kb_size
    slopes = _alibi_slopes_log2()
    qrow = lambda w: pl.BlockSpec((None, tq, w), lambda b, j, n: (b, j, 0))
    krow = lambda w: pl.BlockSpec((None, s_pad, w), lambda b, j, n: (b, 0, 0))
    pos = pl.BlockSpec((tq, 1), lambda b, j, n: (j, 0))
    return pl.pallas_call(
        functools.partial(_attend_kernel, topk=topk, koff=koff, kb_size=kb_size, slopes=slopes),
        grid_spec=pltpu.PrefetchScalarGridSpec(
            num_scalar_prefetch=1,
            grid=(bsz, nt),
            in_specs=[pos, pos, qrow(IDX_Q), qrow(IDX_HEADS), qrow(A_Q),
                      krow(IDX_DIM), krow(A_KV), krow(A_KV)],
            out_specs=qrow(A_Q),
            scratch_shapes=[pltpu.VMEM((tq, s_pad), I32),
                            pltpu.VMEM((A_HEADS, tq, kb_size), F32)],
        ),
        out_shape=jax.ShapeDtypeStruct((bsz, nq, A_Q), BF16),
        compiler_params=pltpu.CompilerParams(
            dimension_semantics=("arbitrary", "arbitrary"), vmem_limit_bytes=VMEM_LIMIT),
        name="attend",
    )(nkb.astype(I32), qpos.reshape(nq, 1), lim.reshape(nq, 1), iq, wi, q, ki, k, v)


def _fold_rows(x, op, rows=8):
    while x.shape[0] > rows:
        half = x.shape[0] // 2
        x = op(x[:half], x[half:])
    return x


VT_ROWS = A_HEAD_DIM + 16


def _alibi_slopes_log2():
    return tuple(float(2.0 ** (-8.0 * (i + 1) / A_HEADS)) * LOG2E for i in range(A_HEADS))


ALIBI_SPLIT = CHUNK


def _alibi_columns(n_keys):
    q_cols = np.zeros((A_HEADS, A_HEAD_DIM), np.float32)
    for h, c in enumerate(_alibi_slopes_log2()):
        for i in range(3):
            ci = float(np.asarray(c, dtype=BF16).astype(np.float32))
            q_cols[h, 2 * i], q_cols[h, 2 * i + 1] = ALIBI_SPLIT * ci, ci
            c -= ci
    ids = np.arange(n_keys)
    k_cols = np.zeros((n_keys, A_HEAD_DIM), np.float32)
    k_cols[:, 0:6:2] = (ids // ALIBI_SPLIT)[:, None]
    k_cols[:, 1:6:2] = (ids % ALIBI_SPLIT)[:, None]
    return jnp.asarray(q_cols, BF16), jnp.asarray(k_cols, BF16)


def _attend_t_kernel(qpos_ref, lim_ref, iq_ref, wi_ref, q_ref, ki_ref, k_ref, vt_ref,
                     o_ref, key_scr, tie_scr, t_scr, *, nkb, first_qpos, topk, koff, kb_size,
                     slopes):
    tq = o_ref.shape[0]
    lim = lim_ref[...]
    qpos = qpos_ref[...]
    n_pairs = A_HEADS // 2
    blocks = [(kb * kb_size, pl.ds(kb * kb_size, kb_size)) for kb in range(nkb)]

    def row_ids(s0):
        return s0 + lax.broadcasted_iota(I32, (kb_size, tq), 0)

    def head_pair(ref, pr):
        return ref[2 * pr:2 * pr + 2].reshape(2 * tq, ref.shape[-1])

    for s0, rows in blocks:
        kib = ki_ref[rows, :]
        acc = jnp.zeros((kb_size, tq), F32)
        for pr in range(IDX_HEADS // 2):
            lg = lax.dot_general(kib, head_pair(iq_ref, pr), (((1,), (1,)), ((), ())),
                                 preferred_element_type=F32)
            for u in range(2):
                h = 2 * pr + u
                acc = acc + jnp.maximum(lg[:, u * tq:(u + 1) * tq], 0.0) * wi_ref[h:h + 1, :]
        bits = lax.bitcast_convert_type(acc, I32)
        key = bits ^ ((bits >> 31) & 0x7FFFFFFF)
        key = jnp.where(row_ids(s0) < lim, key, INT_MIN)
        key_scr[rows, :] = key

    def count(plane, pred):
        acc = jnp.zeros((8, tq), F32)
        for _, rows in blocks:
            acc = acc + _fold_rows(jnp.where(pred(plane[rows, :]), 1.0, 0.0), jnp.add)
        return jnp.sum(acc, axis=0, keepdims=True)

    def kth_largest(plane, kth, bits):
        def bit_step(i, t):
            cand = t + lax.shift_left(jnp.int32(1), bits - 1 - i)
            return jnp.where(count(plane, lambda key: key >= cand) >= kth, cand, t)
        low = INT_MIN if bits == 32 else -(1 << (bits - 1))
        return lax.fori_loop(0, bits, bit_step, jnp.full((1, tq), low, I32))

    t = kth_largest(key_scr, topk, 32)
    thr = jnp.maximum(t, INT_MIN + 1)
    n_ge = count(key_scr, lambda key: key >= thr)

    @pl.when(jnp.max(n_ge) > topk)
    def _():
        need = topk - count(key_scr, lambda key: key > thr)
        for s0, rows in blocks:
            tie_scr[rows, :] = jnp.where(key_scr[rows, :] == thr, -1 - row_ids(s0), INT_MIN)
        last = kth_largest(tie_scr, need, key_scr.shape[0].bit_length() + 1)
        drop_q = n_ge > topk
        for s0, rows in blocks:
            key = key_scr[rows, :]
            drop = drop_q & (key == thr) & (-1 - row_ids(s0) < last)
            key_scr[rows, :] = jnp.where(drop, INT_MIN, key)

    ms = [jnp.full((1, tq), NEG_INF, F32) for _ in range(A_HEADS)]
    accs = [jnp.zeros((vt_ref.shape[1], A_GROUP * tq), F32) for _ in range(A_KV_HEADS)]
    for kb, (s0, rows) in enumerate(blocks):
        bias = jnp.where(key_scr[rows, :] >= thr, 0.0, NEG_INF)
        may_follow = (kb + 1) * kb_size - 1 - koff > first_qpos
        if may_follow:
            ahead = jnp.minimum(qpos - (row_ids(s0) - koff), 0).astype(F32)
        new_m = []
        for pr in range(n_pairs):
            g = (2 * pr) // A_GROUP
            s2 = lax.dot_general(k_ref[g, rows, :], head_pair(q_ref, pr),
                                 (((1,), (1,)), ((), ())), preferred_element_type=F32)
            for u in range(2):
                h = 2 * pr + u
                s = s2[:, u * tq:(u + 1) * tq]
                if may_follow:
                    s = s + (2.0 * slopes[h]) * ahead
                s = s + bias
                t_scr[h] = s
                blk_max = jnp.max(_fold_rows(s, jnp.maximum), axis=0, keepdims=True)
                new_m.append(jnp.maximum(ms[h], blk_max))
        for g in range(A_KV_HEADS):
            ps, alphas = [], []
            for h in range(g * A_GROUP, (g + 1) * A_GROUP):
                ps.append(jnp.exp2(t_scr[h] - new_m[h]).astype(BF16))
                alphas.append(jnp.exp2(ms[h] - new_m[h]))
            pv = jnp.dot(vt_ref[g, :, rows], jnp.concatenate(ps, axis=1),
                         preferred_element_type=F32)
            accs[g] = jnp.concatenate(alphas, axis=1) * accs[g] + pv
        ms = new_m
    for pr in range(n_pairs):
        rows = []
        for h in (2 * pr, 2 * pr + 1):
            g, hh = divmod(h, A_GROUP)
            cols = slice(hh * tq, (hh + 1) * tq)
            rows.append(accs[g][:A_HEAD_DIM, cols] / accs[g][A_HEAD_DIM:A_HEAD_DIM + 1, cols])
        o_ref[:, 2 * pr * A_HEAD_DIM:(2 * pr + 2) * A_HEAD_DIM] = (
            jnp.concatenate(rows, axis=0).T.astype(o_ref.dtype))


def _attend_t(iq, wi_t, q, ki, k, vt, qpos, lim, *, kb_size, topk, koff):
    tq = LANES
    _, bsz, nq, _ = q.shape
    s_pad = ki.shape[1]
    assert nq % tq == 0 and s_pad % kb_size == 0
    nkb_of_tile = -(-lim.reshape(nq // tq, tq).max(axis=1) // kb_size)
    runs, j0 = [], 0
    for j in range(1, len(nkb_of_tile) + 1):
        if j == len(nkb_of_tile) or nkb_of_tile[j] != nkb_of_tile[j0]:
            runs.append((j0, j, int(nkb_of_tile[j0])))
            j0 = j
    qpos_row = jnp.asarray(qpos, I32).reshape(1, nq)
    lim_row = jnp.asarray(lim, I32).reshape(1, nq)
    outs = []
    for j0, j1, nkb in runs:
        heads = lambda a: pl.BlockSpec((a.shape[0], None, tq, a.shape[-1]),
                                       lambda b, j: (0, b, j + j0, 0))
        pos = pl.BlockSpec((1, tq), lambda b, j: (0, j + j0))
        outs.append(pl.pallas_call(
            functools.partial(_attend_t_kernel, nkb=nkb, first_qpos=int(qpos[j0 * tq]), topk=topk,
                              koff=koff, kb_size=kb_size, slopes=_alibi_slopes_log2()),
            grid=(bsz, j1 - j0),
            in_specs=[pos, pos, heads(iq),
                      pl.BlockSpec((None, IDX_HEADS, tq), lambda b, j: (b, 0, j + j0)),
                      heads(q),
                      pl.BlockSpec((None, s_pad, IDX_DIM), lambda b, j: (b, 0, 0)),
                      pl.BlockSpec((A_KV_HEADS, None, s_pad, k.shape[-1]), lambda b, j: (0, b, 0, 0)),
                      pl.BlockSpec((A_KV_HEADS, None, VT_ROWS, s_pad), lambda b, j: (0, b, 0, 0))],
            out_specs=pl.BlockSpec((None, tq, A_Q), lambda b, j: (b, j, 0)),
            scratch_shapes=[pltpu.VMEM((s_pad, tq), I32), pltpu.VMEM((s_pad, tq), I32),
                            pltpu.VMEM((A_HEADS, kb_size, tq), F32)],
            out_shape=jax.ShapeDtypeStruct((bsz, (j1 - j0) * tq, A_Q), BF16),
            compiler_params=pltpu.CompilerParams(
                dimension_semantics=("arbitrary", "arbitrary"), vmem_limit_bytes=VMEM_LIMIT),
            name=f"attend_t{nkb}",
        )(qpos_row, lim_row, iq, wi_t, q, ki, k, vt))
    return jnp.concatenate(outs, axis=1)


HGRN_CHUNK = 128


def _hgrn2_tables(ct):
    r = np.arange(ct)
    seg = [r[:, None] >= r[None, :], r[None, :] > r[:, None]]
    pair = []
    g = ct
    while g >= 2:
        mid = (r // g) * g + g // 2
        second = r >= mid
        seg.append(np.where(second[:, None],
                            (r[None, :] >= mid[:, None]) & (r[None, :] <= r[:, None]),
                            (r[None, :] > r[:, None]) & (r[None, :] < mid[:, None])))
        pair.append((r[:, None] // g == r[None, :] // g) & second[:, None] & ~second[None, :])
        g //= 2
    return np.concatenate(seg, 0).astype(np.float32), np.stack(pair).astype(np.float32)


def _hgrn2_kernel(bq_ref, bf_ref, bi_ref, lb_ref, seg_ref, pair_ref, s0_ref, o_ref, sout_ref, st_scr):
    c = pl.program_id(1)
    ct = bq_ref.shape[0]
    n_lvl = pair_ref.shape[0]

    @pl.when(c == 0)
    def _():
        for h in range(B_HEADS):
            st_scr[h] = s0_ref[h].T

    lb = lb_ref[...]
    fx = bf_ref[...]
    logf = jnp.log(lb + (1.0 - lb) * jax.nn.sigmoid(fx))
    kk = (1.0 - lb) * jax.nn.sigmoid(-fx)
    bq = bq_ref[...]
    rq = bq * jax.nn.sigmoid(bq)
    rv = bi_ref[...]
    p1 = logf.astype(BF16)
    p2 = (logf - p1.astype(F32)).astype(BF16)
    seg = seg_ref[...]
    sums = jnp.dot(seg, p1, preferred_element_type=F32) + jnp.dot(seg, p2, preferred_element_type=F32)
    b_in = sums[:ct]
    b_out = sums[ct:2 * ct]
    q_in = (rq * jnp.exp(b_in)).astype(BF16)
    k_out = (kk * jnp.exp(b_out)).astype(BF16)
    rvb = rv.astype(BF16)
    row = lax.broadcasted_iota(I32, (ct, B_KEY_DIM), 0)
    nt = (((1,), (1,)), ((), ()))

    a = [jnp.zeros((ct, ct), F32) for _ in range(B_HEADS)]
    for lvl in range(n_lvl):
        half = ct >> (lvl + 1)
        second = (row & half) != 0
        dec = jnp.exp(sums[(2 + lvl) * ct:(3 + lvl) * ct])
        for h in range(B_HEADS):
            ls = slice(h * B_KEY_DIM, (h + 1) * B_KEY_DIM)
            x = (jnp.where(second, rq[:, ls], kk[:, ls]) * dec[:, ls]).astype(BF16)
            a[h] = a[h] + pair_ref[lvl] * lax.dot_general(x, x, nt, preferred_element_type=F32)

    for h in range(B_HEADS):
        ls = slice(h * B_KEY_DIM, (h + 1) * B_KEY_DIM)
        st = st_scr[h]
        own = jnp.sum(rq[:, ls] * kk[:, ls], axis=1, keepdims=True)
        o_ref[:, ls] = (jnp.dot(a[h].astype(BF16), rvb[:, ls], preferred_element_type=F32)
                        + own * rv[:, ls]
                        + lax.dot_general(q_in[:, ls], st.astype(BF16), nt, preferred_element_type=F32))
        upd = lax.dot_general(rvb[:, ls], k_out[:, ls], (((0,), (0,)), ((), ())),
                              preferred_element_type=F32)
        st_scr[h] = st * jnp.exp(b_in[ct - 1:ct, ls]) + upd

    @pl.when(c == pl.num_programs(1) - 1)
    def _():
        for h in range(B_HEADS):
            sout_ref[h] = st_scr[h].T


def _hgrn2(bq, bf, bi, lb, s0):
    bsz, t, _ = bq.shape
    ct = HGRN_CHUNK
    tp = -(-t // ct) * ct
    if tp != t:
        pad = lambda a, v: jnp.pad(a, ((0, 0), (0, tp - t), (0, 0)), constant_values=v)
        bq, bf, bi = pad(bq, 0.0), pad(bf, 1e4), pad(bi, 0.0)
    seg, pair = _hgrn2_tables(ct)
    seg = jnp.asarray(seg, BF16)
    pair = jnp.asarray(pair, F32)
    tok = pl.BlockSpec((None, ct, B_KW), lambda b, c: (b, c, 0))
    s_map = (lambda b, c: (0, 0, 0, 0)) if s0.shape[0] == 1 else (lambda b, c: (b, 0, 0, 0))
    s_in = pl.BlockSpec((None, B_HEADS, B_KEY_DIM, B_VAL_DIM), s_map)
    s_out = pl.BlockSpec((None, B_HEADS, B_KEY_DIM, B_VAL_DIM), lambda b, c: (b, 0, 0, 0))
    o, s = pl.pallas_call(
        _hgrn2_kernel,
        grid=(bsz, tp // ct),
        in_specs=[tok, tok, tok, _resident(lb.shape), _resident(seg.shape), _resident(pair.shape), s_in],
        out_specs=[tok, s_out],
        out_shape=[jax.ShapeDtypeStruct((bsz, tp, B_WIDTH), F32),
                   jax.ShapeDtypeStruct((bsz, B_HEADS, B_KEY_DIM, B_VAL_DIM), F32)],
        scratch_shapes=[pltpu.VMEM((B_HEADS, B_VAL_DIM, B_KEY_DIM), F32)],
        compiler_params=pltpu.CompilerParams(
            dimension_semantics=("arbitrary", "arbitrary"), vmem_limit_bytes=VMEM_LIMIT),
        name="hgrn2",
    )(bq, bf, bi, lb, seg, pair, s0)
    return o[:, :t], s


def _mix_out_kernel(h_ref, oa_ref, orec_ref, wgate_ref, gn_ref, wa_ref, wb_ref, wo_ref,
                    g2_ref, b2_ref, wg_ref, wu_ref, wd_ref, g3_ref, b3_ref, y_ref):
    h = h_ref[...]
    gates = jnp.dot(h.astype(BF16), wgate_ref[...], preferred_element_type=F32)
    bg = gates[:, :B_WIDTH]
    ga = gates[:, B_WIDTH:B_WIDTH + D_MODEL]
    gb = gates[:, B_WIDTH + D_MODEL:]
    orec = orec_ref[...]
    gn = gn_ref[...]
    parts = []
    for head in range(B_HEADS):
        ls = slice(head * B_VAL_DIM, (head + 1) * B_VAL_DIM)
        oh = orec[:, ls]
        ms = jnp.mean(oh * oh, axis=-1, keepdims=True)
        parts.append(oh * lax.rsqrt(ms + RMS_EPS) * gn[:, ls])
    on = jnp.concatenate(parts, axis=-1)
    ob = on * (bg * jax.nn.sigmoid(bg))
    pa = jnp.dot(oa_ref[...], wa_ref[...], preferred_element_type=F32)
    pb = jnp.dot(ob.astype(BF16), wb_ref[...], preferred_element_type=F32)
    mixed = jax.nn.sigmoid(ga) * pa + jax.nn.sigmoid(gb) * pb
    mixed = jnp.dot(mixed.astype(BF16), wo_ref[...], preferred_element_type=F32)
    h2 = _layer_norm(ALPHA * h + mixed, g2_ref[...], b2_ref[...])
    y_ref[...] = _layer_norm(_swiglu_step(h2, wg_ref, wu_ref, wd_ref), g3_ref[...], b3_ref[...])


def _mix_out(h, oa, orec, wgate, gn, wa, wb, wo, g2, b2, wg, wu, wd, g3, b3, tm):
    m = h.shape[0]
    assert m % tm == 0
    row = lambda w: pl.BlockSpec((tm, w), lambda i: (i, 0))
    weights = (wgate, gn, wa, wb, wo, g2, b2, wg, wu, wd, g3, b3)
    return pl.pallas_call(
        _mix_out_kernel,
        grid=(m // tm,),
        in_specs=[row(D_MODEL), row(A_Q), row(B_WIDTH)] + [_resident(w.shape) for w in weights],
        out_specs=row(D_MODEL),
        out_shape=jax.ShapeDtypeStruct((m, D_MODEL), F32),
        compiler_params=pltpu.CompilerParams(
            dimension_semantics=("arbitrary",), vmem_limit_bytes=VMEM_LIMIT),
        name="mix_out",
    )(h, oa, orec, *weights)


def _ffn_weights(wg, wu, wd):
    return wg.astype(BF16), wu.astype(BF16), wd.astype(BF16).reshape(N_FF_CHUNKS, FF_CHUNK, D_MODEL)


def _pad_rows(a, n):
    return jnp.pad(a, ((0, 0), (0, n - a.shape[1]), (0, 0)))


FFN_IN_TILE = 512
MIX_OUT_TILE = 512


def _token_tile(m, largest):
    tm = largest
    while tm > 8 and m % tm:
        tm //= 2
    assert m % tm == 0, m
    return tm


def kernel(x_prompt, x_sample, cache_k, cache_v, cache_kidx, state_hgrn, meta, ln1_g, ln1_b, ffn1_wg, ffn1_wu, ffn1_wd, w_in, lb_param, g_norm, w_a_proj, w_b_proj, w_out, ln2_g, ln2_b, ffn2_wg, ffn2_wu, ffn2_wd, ln3_g, ln3_b):
    assert ln1_g.shape[0] == DEPTH
    bp, seq, _ = x_prompt.shape
    bs, dseq, _ = x_sample.shape
    past = cache_k.shape[2]
    l = 0
    vec = lambda a: a.reshape(1, -1).astype(F32)

    lb_all = jnp.cumsum(jax.nn.softmax(lb_param.astype(F32), axis=0), axis=0)
    lb = lb_all[l].reshape(1, B_KW)
    f1 = _ffn_weights(ffn1_wg[l], ffn1_wu[l], ffn1_wd[l])
    f2 = _ffn_weights(ffn2_wg[l], ffn2_wu[l], ffn2_wd[l])
    w_pieces, off = [], 0
    for _, width, _ in _PROJ_OUT:
        w_pieces.append(w_in[l][:, off:off + width].astype(BF16))
        off += width
    w_gate = w_in[l][:, off:].astype(BF16)
    assert w_gate.shape[1] == sum(_GATE_WIDTHS)

    kb_p = 512
    s_p = -(-(N_META + seq) // kb_p) * kb_p
    q_cols, k_cols = _alibi_columns(s_p)

    def stage1(x):
        return _ffn_in(x, vec(ln1_g[l]), vec(ln1_b[l]), *f1, q_cols[:, None, :], w_pieces,
                       _token_tile(x.shape[0], FFN_IN_TILE))

    def stage4(p, oa, orec):
        m = p["h"].shape[0]
        return _mix_out(p["h"], oa.reshape(m, A_Q), orec.reshape(m, B_WIDTH), w_gate,
                        vec(g_norm[l]), w_a_proj[l].astype(BF16), w_b_proj[l].astype(BF16),
                        w_out[l].astype(BF16), vec(ln2_g[l]), vec(ln2_b[l]), *f2,
                        vec(ln3_g[l]), vec(ln3_b[l]), _token_tile(m, MIX_OUT_TILE))

    pm = stage1(meta.astype(F32))
    pp = stage1(x_prompt.reshape(bp * seq, D_MODEL))
    ps = stage1(x_sample.reshape(bs * dseq, D_MODEL))
    per_p = lambda a: a.reshape(bp, seq, a.shape[-1])
    per_s = lambda a: a.reshape(bs, dseq, a.shape[-1])

    def with_meta(name):
        rows = jnp.broadcast_to(pm[name][None], (bp, N_META, pm[name].shape[-1]))
        return jnp.concatenate([rows, per_p(pp[name])], axis=1)

    k_p, v_p, ki_p = with_meta("ak"), with_meta("av"), with_meta("ik")

    qpos = np.arange(seq, dtype=np.int32)
    lim = N_META + CHUNK * (qpos // CHUNK + 1)
    heads_p = lambda a: a.reshape(a.shape[0], bp, seq, a.shape[-1])
    kv_heads = lambda a: _pad_rows(a.astype(BF16), s_p).reshape(bp, s_p, A_KV_HEADS, A_HEAD_DIM)
    vt = kv_heads(v_p).transpose(2, 0, 3, 1)
    vt = jnp.concatenate([vt, jnp.ones_like(vt[:, :, :1]),
                          jnp.zeros_like(vt[:, :, :VT_ROWS - A_HEAD_DIM - 1])], axis=2)
    k_aug = jnp.concatenate([kv_heads(k_p).transpose(2, 0, 1, 3), jnp.broadcast_to(
        k_cols[None, None], (A_KV_HEADS, bp, s_p, A_HEAD_DIM))], axis=-1)
    oa_p = _attend_t(heads_p(pp["iq"]), per_p(pp["iw"]).transpose(0, 2, 1), heads_p(pp["aq"]),
                     _pad_rows(ki_p.astype(BF16), s_p), k_aug, vt, qpos, lim,
                     kb_size=kb_p, topk=min(TOPK_MAX, seq // 4), koff=N_META)

    n_all = past + dseq
    kb_s = 384
    s_s = -(-n_all // kb_s) * kb_s
    k_all = jnp.concatenate([cache_k[l].reshape(bs, past, A_KV).astype(BF16), per_s(ps["ak"]).astype(BF16)], axis=1)
    v_all = jnp.concatenate([cache_v[l].reshape(bs, past, A_KV).astype(BF16), per_s(ps["av"]).astype(BF16)], axis=1)
    ki_all = jnp.concatenate([cache_kidx[l].astype(BF16), per_s(ps["ik"]).astype(BF16)], axis=1)
    tokens_s = lambda a: a.transpose(1, 0, 2).reshape(bs, dseq, a.shape[0] * a.shape[-1])
    oa_s = _attend(tokens_s(ps["iq"]), per_s(ps["iw"]), tokens_s(ps["aq"][..., :A_HEAD_DIM]),
                   _pad_rows(ki_all, s_s), _pad_rows(k_all, s_s), _pad_rows(v_all, s_s),
                   past + jnp.arange(dseq, dtype=I32), jnp.full((dseq,), n_all, I32),
                   tq=dseq, kb_size=kb_s, topk=min(TOPK_MAX, n_all // 4), koff=0)

    zero_state = jnp.zeros((1, B_HEADS, B_KEY_DIM, B_VAL_DIM), F32)
    _, st_m = _hgrn2(pm["bq"][None], pm["bf"][None], pm["bi"][None], lb, zero_state)
    orec_p, st_p = _hgrn2(per_p(pp["bq"]), per_p(pp["bf"]), per_p(pp["bi"]), lb, st_m)
    orec_s, st_s = _hgrn2(per_s(ps["bq"]), per_s(ps["bf"]), per_s(ps["bi"]), lb,
                          state_hgrn[l].astype(F32))

    y_p = stage4(pp, oa_p, orec_p).reshape(bp, seq, D_MODEL)
    y_s = stage4(ps, oa_s, orec_s).reshape(bs, dseq, D_MODEL)

    kv5 = lambda a: a.reshape(1, a.shape[0], a.shape[1], A_KV_HEADS, A_HEAD_DIM)
    return (y_p, y_s, kv5(k_p), kv5(v_p), ki_p[None], st_p[None].astype(state_hgrn.dtype),
            kv5(per_s(ps["ak"])), kv5(per_s(ps["av"])), per_s(ps["ik"])[None],
            st_s[None].astype(state_hgrn.dtype))
```

```python
import functools

import jax
import jax.numpy as jnp
import numpy as np
from jax import lax
from jax.experimental import pallas as pl
from jax.experimental.pallas import tpu as pltpu

F32 = jnp.float32
BF16 = jnp.bfloat16
I32 = jnp.int32

D_MODEL = 1024
D_FF = 2816
FF_CHUNK = 256
N_FF_CHUNKS = D_FF // FF_CHUNK
CHUNK = 64
N_META = 16
A_HEADS = 8
A_KV_HEADS = 2
A_GROUP = A_HEADS // A_KV_HEADS
A_HEAD_DIM = 64
A_Q = A_HEADS * A_HEAD_DIM
A_KV = A_KV_HEADS * A_HEAD_DIM
IDX_HEADS = 8
IDX_DIM = 64
IDX_Q = IDX_HEADS * IDX_DIM
TOPK_MAX = 256
B_HEADS = 4
B_KEY_DIM = 128
B_VAL_DIM = 128
B_KW = B_HEADS * B_KEY_DIM
B_WIDTH = B_HEADS * B_VAL_DIM
DEPTH = 1
ALPHA = (2.0 * DEPTH) ** 0.25
LN_EPS = 1e-5
RMS_EPS = 1e-6
NEG_INF = -1e30
LOG2E = 1.4426950408889634
INT_MIN = -(2 ** 31)
LANES = 128
VMEM_LIMIT = 56 * 1024 * 1024


def _resident(shape):
    n = len(shape)
    return pl.BlockSpec(shape, lambda *_: (0,) * n, pipeline_mode=pl.Buffered(1))


def _layer_norm(x, g, b):
    mu = jnp.mean(x, axis=-1, keepdims=True)
    xc = x - mu
    var = jnp.mean(xc * xc, axis=-1, keepdims=True)
    return xc * lax.rsqrt(var + LN_EPS) * g + b


def _swiglu_step(x, wg_ref, wu_ref, wd_ref):
    xb = x.astype(BF16)

    acc = jnp.zeros(x.shape, F32)
    for c in range(N_FF_CHUNKS):
        cols = slice(c * FF_CHUNK, (c + 1) * FF_CHUNK)
        g = jnp.dot(xb, wg_ref[:, cols], preferred_element_type=F32)
        u = jnp.dot(xb, wu_ref[:, cols], preferred_element_type=F32)
        a = (g * jax.nn.sigmoid(g) * u).astype(BF16)
        acc = acc + jnp.dot(a, wd_ref[c], preferred_element_type=F32)
    return ALPHA * x + 0.5 * acc


_PROJ_OUT = (
    ("aq", A_Q, BF16), ("ak", A_KV, F32), ("av", A_KV, F32), ("iq", IDX_Q, BF16),
    ("ik", IDX_DIM, F32), ("iw", IDX_HEADS, F32),
    ("bq", B_KW, F32), ("bf", B_KW, F32), ("bi", B_WIDTH, F32),
)
_GATE_WIDTHS = (B_WIDTH, D_MODEL, D_MODEL)
_PROJ_SCALE = {"aq": (A_HEAD_DIM ** -0.5) * LOG2E, "iw": (IDX_HEADS ** -0.5) * (IDX_DIM ** -0.5)}
_HEAD_MAJOR = {"aq": A_HEADS, "iq": IDX_HEADS}


def _ffn_in_kernel(x_ref, g_ref, b_ref, wg_ref, wu_ref, wd_ref, qcols_ref, *rest):
    n = len(_PROJ_OUT)
    w_refs, h_ref, out_refs = rest[:n], rest[n], rest[n + 1:]
    x = x_ref[...]
    h = _layer_norm(_swiglu_step(x, wg_ref, wu_ref, wd_ref), g_ref[...], b_ref[...])
    h_ref[...] = h
    hb = h.astype(BF16)
    for (name, _, dt), w_ref, o_ref in zip(_PROJ_OUT, w_refs, out_refs):
        y = jnp.dot(hb, w_ref[...], preferred_element_type=F32)
        if name in _PROJ_SCALE:
            y = y * _PROJ_SCALE[name]
        if name in _HEAD_MAJOR:
            hd = y.shape[1] // _HEAD_MAJOR[name]
            for head in range(_HEAD_MAJOR[name]):
                yh = y[:, head * hd:(head + 1) * hd].astype(dt)
                if name == "aq":
                    yh = jnp.concatenate([yh, jnp.broadcast_to(qcols_ref[head], yh.shape)], axis=1)
                o_ref[head] = yh
        else:
            o_ref[...] = y.astype(dt)


def _ffn_in(x, ln_g, ln_b, wg, wu, wd, q_cols, w_pieces, tm):
    m = x.shape[0]
    assert m % tm == 0
    row = lambda w: pl.BlockSpec((tm, w), lambda i: (i, 0))
    in_specs = [row(D_MODEL), _resident(ln_g.shape), _resident(ln_b.shape),
                _resident(wg.shape), _resident(wu.shape), _resident(wd.shape),
                _resident(q_cols.shape)]
    in_specs += [_resident(w.shape) for w in w_pieces]
    out_shape = [jax.ShapeDtypeStruct((m, D_MODEL), F32)]
    out_specs = [row(D_MODEL)]
    for name, width, dt in _PROJ_OUT:
        if name in _HEAD_MAJOR:
            nh = _HEAD_MAJOR[name]
            hd = width // nh + (q_cols.shape[-1] if name == "aq" else 0)
            out_shape.append(jax.ShapeDtypeStruct((nh, m, hd), dt))
            out_specs.append(pl.BlockSpec((nh, tm, hd), lambda i: (0, i, 0)))
        else:
            out_shape.append(jax.ShapeDtypeStruct((m, width), dt))
            out_specs.append(row(width))
    outs = pl.pallas_call(
        _ffn_in_kernel,
        grid=(m // tm,),
        in_specs=in_specs,
        out_specs=out_specs,
        out_shape=out_shape,
        compiler_params=pltpu.CompilerParams(
            dimension_semantics=("arbitrary",), vmem_limit_bytes=VMEM_LIMIT),
        name="ffn_in",
    )(x, ln_g, ln_b, wg, wu, wd, q_cols, *w_pieces)
    res = {"h": outs[0]}
    for (name, _, _), o in zip(_PROJ_OUT, outs[1:]):
        res[name] = o
    return res


def _attend_kernel(nkb_ref, qpos_ref, lim_ref, iq_ref, wi_ref, q_ref, ki_ref, k_ref, v_ref,
                   o_ref, key_scr, t_scr, *, topk, koff, kb_size, slopes):
    tq = q_ref.shape[0]
    nkb = nkb_ref[pl.program_id(1)]
    lim = lim_ref[...]
    qpos = qpos_ref[...]
    sub = kb_size // LANES

    def lane_ids(s0):
        return s0 + lax.broadcasted_iota(I32, (tq, kb_size), 1)

    def score_block(kb, carry):
        s0 = pl.multiple_of(kb * kb_size, kb_size)
        kib = ki_ref[pl.ds(s0, kb_size), :]
        acc = jnp.zeros((tq, kb_size), F32)
        for h in range(IDX_HEADS):
            lg = lax.dot_general(iq_ref[:, h * IDX_DIM:(h + 1) * IDX_DIM], kib,
                                 (((1,), (1,)), ((), ())), preferred_element_type=F32)
            acc = acc + jnp.maximum(lg, 0.0) * wi_ref[:, h:h + 1]
        bits = lax.bitcast_convert_type(acc, I32)
        key = bits ^ ((bits >> 31) & 0x7FFFFFFF)
        key_scr[:, pl.ds(s0, kb_size)] = jnp.where(lane_ids(s0) < lim, key, INT_MIN)
        return carry

    lax.fori_loop(0, nkb, score_block, 0)

    def bcast(x):
        return jnp.broadcast_to(x, (tq, LANES))

    def count(pred):
        def body(kb, acc):
            for c in range(sub):
                off = pl.multiple_of(kb * kb_size + c * LANES, LANES)
                col = off + lax.broadcasted_iota(I32, (tq, LANES), 1)
                acc = acc + jnp.where(pred(key_scr[:, pl.ds(off, LANES)], col), 1.0, 0.0)
            return acc
        acc = lax.fori_loop(0, nkb, body, jnp.zeros((tq, LANES), F32))
        return jnp.sum(acc, axis=1, keepdims=True)

    def bit_step(state):
        i, t, settled, _ = state
        stop = jnp.min(settled)
        cand = t + lax.shift_left(jnp.int32(1), 31 - i)
        cand_b = bcast(cand)
        cnt = count(lambda key, col: key >= cand_b)
        take = (cnt >= topk) & (settled == 0.0)
        settled = jnp.where(take & (cnt == topk), 1.0, settled)
        return i + 1, jnp.where(take, cand, t), settled, stop

    _, t, _, _ = lax.while_loop(
        lambda state: (state[0] < 32) & (state[3] == 0.0), bit_step,
        (jnp.int32(0), jnp.full((tq, 1), INT_MIN, I32), (lim <= topk).astype(F32), jnp.float32(0.0)))
    thr = jnp.maximum(t, INT_MIN + 1)
    thr_b = bcast(thr)
    n_ge = count(lambda key, col: key >= thr_b)

    @pl.when(jnp.max(n_ge) > topk)
    def _():
        n_gt = count(lambda key, col: key > thr_b)
        need = topk - n_gt
        n_bits = (key_scr.shape[1] - 1).bit_length()

        def idx_step(i, m0):
            cand = m0 + lax.shift_left(jnp.int32(1), n_bits - 1 - i)
            cand_b = bcast(cand)
            g = count(lambda key, col: (key == thr_b) & (col < cand_b))
            return jnp.where(g < need, cand, m0)

        m0 = lax.fori_loop(0, n_bits, idx_step, jnp.zeros((tq, 1), I32))
        drop_row = n_ge > topk

        def demote(kb, carry):
            s0 = pl.multiple_of(kb * kb_size, kb_size)
            key = key_scr[:, pl.ds(s0, kb_size)]
            drop = drop_row & (key == thr) & (lane_ids(s0) > m0)
            key_scr[:, pl.ds(s0, kb_size)] = jnp.where(drop, INT_MIN, key)
            return carry

        lax.fori_loop(0, nkb, demote, 0)

    def attend_block(kb, carry):
        ms, ls, accs = carry
        s0 = pl.multiple_of(kb * kb_size, kb_size)
        col = lane_ids(s0)
        bias = jnp.where(key_scr[:, pl.ds(s0, kb_size)] >= thr, 0.0, NEG_INF)
        dist = jnp.abs(qpos - (col - koff)).astype(F32)
        kblk = k_ref[pl.ds(s0, kb_size), :]
        vblk = v_ref[pl.ds(s0, kb_size), :]
        new_m, new_l, new_acc = [], [], []
        for h in range(A_HEADS):
            g = h // A_GROUP
            s = lax.dot_general(q_ref[:, h * A_HEAD_DIM:(h + 1) * A_HEAD_DIM],
                                kblk[:, g * A_HEAD_DIM:(g + 1) * A_HEAD_DIM],
                                (((1,), (1,)), ((), ())), preferred_element_type=F32)
            s = (s - slopes[h] * dist) + bias
            t_scr[h] = s
            new_m.append(jnp.maximum(ms[h], jnp.max(s, axis=1, keepdims=True)))
        for h in range(A_HEADS):
            g = h // A_GROUP
            p = jnp.exp2(t_scr[h] - new_m[h])
            alpha = jnp.exp2(ms[h] - new_m[h])
            new_l.append(alpha * ls[h] + jnp.sum(p, axis=1, keepdims=True))
            pv = jnp.dot(p.astype(BF16), vblk[:, g * A_HEAD_DIM:(g + 1) * A_HEAD_DIM],
                         preferred_element_type=F32)
            new_acc.append(alpha * accs[h] + pv)
        return tuple(new_m), tuple(new_l), tuple(new_acc)

    init = (tuple(jnp.full((tq, 1), NEG_INF, F32) for _ in range(A_HEADS)),
            tuple(jnp.zeros((tq, 1), F32) for _ in range(A_HEADS)),
            tuple(jnp.zeros((tq, A_HEAD_DIM), F32) for _ in range(A_HEADS)))
    _, ls, accs = lax.fori_loop(0, nkb, attend_block, init)
    for h in range(A_HEADS):
        o_ref[:, h * A_HEAD_DIM:(h + 1) * A_HEAD_DIM] = (accs[h] / ls[h]).astype(o_ref.dtype)


def _attend(iq, wi, q, ki, k, v, qpos, lim, *, tq, kb_size, topk, koff):
    bsz, nq, _ = q.shape
    s_pad = k.shape[1]
    assert nq % tq == 0 and s_pad % kb_size == 0
    nt = nq // tq
    nkb = (jnp.max(lim.reshape(nt, tq), axis=1) + kb_size - 1) // kb_size
    slopes = _alibi_slopes_log2()
    qrow = lambda w: pl.BlockSpec((None, tq, w), lambda b, j, n: (b, j, 0))
    krow = lambda w: pl.BlockSpec((None, s_pad, w), lambda b, j, n: (b, 0, 0))
    pos = pl.BlockSpec((tq, 1), lambda b, j, n: (j, 0))
    return pl.pallas_call(
        functools.partial(_attend_kernel, topk=topk, koff=koff, kb_size=kb_size, slopes=slopes),
        grid_spec=pltpu.PrefetchScalarGridSpec(
            num_scalar_prefetch=1,
            grid=(bsz, nt),
            in_specs=[pos, pos, qrow(IDX_Q), qrow(IDX_HEADS), qrow(A_Q),
                      krow(IDX_DIM), krow(A_KV), krow(A_KV)],
            out_specs=qrow(A_Q),
            scratch_shapes=[pltpu.VMEM((tq, s_pad), I32),
                            pltpu.VMEM((A_HEADS, tq, kb_size), F32)],
        ),
        out_shape=jax.ShapeDtypeStruct((bsz, nq, A_Q), BF16),
        compiler_params=pltpu.CompilerParams(
            dimension_semantics=("arbitrary", "arbitrary"), vmem_limit_bytes=VMEM_LIMIT),
        name="attend",
    )(nkb.astype(I32), qpos.reshape(nq, 1), lim.reshape(nq, 1), iq, wi, q, ki, k, v)


def _fold_rows(x, op, rows=8):
    while x.shape[0] > rows:
        half = x.shape[0] // 2
        x = op(x[:half], x[half:])
    return x


VT_ROWS = A_HEAD_DIM + 16
PROMPT_KEY_BLOCK = 512
SAMPLE_KEY_BLOCK = 384


def _alibi_slopes_log2():
    return tuple(float(2.0 ** (-8.0 * (i + 1) / A_HEADS)) * LOG2E for i in range(A_HEADS))


ALIBI_SPLIT = CHUNK


def _alibi_columns(n_keys):
    q_cols = np.zeros((A_HEADS, A_HEAD_DIM), np.float32)
    for h, c in enumerate(_alibi_slopes_log2()):
        for i in range(3):
            ci = float(np.asarray(c, dtype=BF16).astype(np.float32))
            q_cols[h, 2 * i], q_cols[h, 2 * i + 1] = ALIBI_SPLIT * ci, ci
            c -= ci
    ids = np.arange(n_keys)
    k_cols = np.zeros((n_keys, A_HEAD_DIM), np.float32)
    k_cols[:, 0:6:2] = (ids // ALIBI_SPLIT)[:, None]
    k_cols[:, 1:6:2] = (ids % ALIBI_SPLIT)[:, None]
    return jnp.asarray(q_cols, BF16), jnp.asarray(k_cols, BF16)


def _attend_t_kernel(qpos_ref, lim_ref, iq_ref, wi_ref, q_ref, ki_ref, k_ref, vt_ref,
                     o_ref, key_scr, tie_scr, t_scr, *, nkb, first_qpos, topk, koff, kb_size,
                     slopes):
    tq = o_ref.shape[0]
    lim = lim_ref[...]
    qpos = qpos_ref[...]
    n_pairs = A_HEADS // 2
    blocks = [(kb * kb_size, pl.ds(kb * kb_size, kb_size)) for kb in range(nkb)]

    def row_ids(s0):
        return s0 + lax.broadcasted_iota(I32, (kb_size, tq), 0)

    def head_pair(ref, pr):
        return ref[2 * pr:2 * pr + 2].reshape(2 * tq, ref.shape[-1])

    for s0, rows in blocks:
        kib = ki_ref[rows, :]
        acc = jnp.zeros((kb_size, tq), F32)
        for pr in range(IDX_HEADS // 2):
            lg = lax.dot_general(kib, head_pair(iq_ref, pr), (((1,), (1,)), ((), ())),
                                 preferred_element_type=F32)
            for u in range(2):
                h = 2 * pr + u
                acc = acc + jnp.maximum(lg[:, u * tq:(u + 1) * tq], 0.0) * wi_ref[h:h + 1, :]
        bits = lax.bitcast_convert_type(acc, I32)
        key = bits ^ ((bits >> 31) & 0x7FFFFFFF)
        key = jnp.where(row_ids(s0) < lim, key, INT_MIN)
        key_scr[rows, :] = key

    def count(plane, pred):
        acc = jnp.zeros((8, tq), F32)
        for _, rows in blocks:
            acc = acc + _fold_rows(jnp.where(pred(plane[rows, :]), 1.0, 0.0), jnp.add)
        return jnp.sum(acc, axis=0, keepdims=True)

    def kth_largest(plane, kth, bits):
        def bit_step(i, t):
            cand = t + lax.shift_left(jnp.int32(1), bits - 1 - i)
            return jnp.where(count(plane, lambda key: key >= cand) >= kth, cand, t)
        low = INT_MIN if bits == 32 else -(1 << (bits - 1))
        return lax.fori_loop(0, bits, bit_step, jnp.full((1, tq), low, I32))

    t = kth_largest(key_scr, topk, 32)
    thr = jnp.maximum(t, INT_MIN + 1)
    n_ge = count(key_scr, lambda key: key >= thr)

    @pl.when(jnp.max(n_ge) > topk)
    def _():
        need = topk - count(key_scr, lambda key: key > thr)
        for s0, rows in blocks:
            tie_scr[rows, :] = jnp.where(key_scr[rows, :] == thr, -1 - row_ids(s0), INT_MIN)
        last = kth_largest(tie_scr, need, key_scr.shape[0].bit_length() + 1)
        drop_q = n_ge > topk
        for s0, rows in blocks:
            key = key_scr[rows, :]
            drop = drop_q & (key == thr) & (-1 - row_ids(s0) < last)
            key_scr[rows, :] = jnp.where(drop, INT_MIN, key)

    ms = [jnp.full((1, tq), NEG_INF, F32) for _ in range(A_HEADS)]
    accs = [jnp.zeros((vt_ref.shape[1], A_GROUP * tq), F32) for _ in range(A_KV_HEADS)]
    for kb, (s0, rows) in enumerate(blocks):
        bias = jnp.where(key_scr[rows, :] >= thr, 0.0, NEG_INF)
        may_follow = (kb + 1) * kb_size - 1 - koff > first_qpos
        if may_follow:
            ahead = jnp.minimum(qpos - (row_ids(s0) - koff), 0).astype(F32)
        new_m = []
        for pr in range(n_pairs):
            g = (2 * pr) // A_GROUP
            s2 = lax.dot_general(k_ref[g, rows, :], head_pair(q_ref, pr),
                                 (((1,), (1,)), ((), ())), preferred_element_type=F32)
            for u in range(2):
                h = 2 * pr + u
                s = s2[:, u * tq:(u + 1) * tq]
                if may_follow:
                    s = s + (2.0 * slopes[h]) * ahead
                s = s + bias
                t_scr[h] = s
                blk_max = jnp.max(_fold_rows(s, jnp.maximum), axis=0, keepdims=True)
                new_m.append(jnp.maximum(ms[h], blk_max))
        for g in range(A_KV_HEADS):
            ps, alphas = [], []
            for h in range(g * A_GROUP, (g + 1) * A_GROUP):
                ps.append(jnp.exp2(t_scr[h] - new_m[h]).astype(BF16))
                alphas.append(jnp.exp2(ms[h] - new_m[h]))
            pv = jnp.dot(vt_ref[g, :, rows], jnp.concatenate(ps, axis=1),
                         preferred_element_type=F32)
            accs[g] = jnp.concatenate(alphas, axis=1) * accs[g] + pv
        ms = new_m
    for pr in range(n_pairs):
        rows = []
        for h in (2 * pr, 2 * pr + 1):
            g, hh = divmod(h, A_GROUP)
            cols = slice(hh * tq, (hh + 1) * tq)
            rows.append(accs[g][:A_HEAD_DIM, cols] / accs[g][A_HEAD_DIM:A_HEAD_DIM + 1, cols])
        o_ref[:, 2 * pr * A_HEAD_DIM:(2 * pr + 2) * A_HEAD_DIM] = (
            jnp.concatenate(rows, axis=0).T.astype(o_ref.dtype))


def _attend_t(iq, wi_t, q, ki, k, vt, qpos, lim, *, kb_size, topk, koff):
    tq = LANES
    _, bsz, nq, _ = q.shape
    s_pad = ki.shape[1]
    assert nq % tq == 0 and s_pad % kb_size == 0
    nkb_of_tile = -(-lim.reshape(nq // tq, tq).max(axis=1) // kb_size)
    runs, j0 = [], 0
    for j in range(1, len(nkb_of_tile) + 1):
        if j == len(nkb_of_tile) or nkb_of_tile[j] != nkb_of_tile[j0]:
            runs.append((j0, j, int(nkb_of_tile[j0])))
            j0 = j
    qpos_row = jnp.asarray(qpos, I32).reshape(1, nq)
    lim_row = jnp.asarray(lim, I32).reshape(1, nq)
    outs = []
    for j0, j1, nkb in runs:
        heads = lambda a: pl.BlockSpec((a.shape[0], None, tq, a.shape[-1]),
                                       lambda b, j: (0, b, j + j0, 0))
        pos = pl.BlockSpec((1, tq), lambda b, j: (0, j + j0))
        outs.append(pl.pallas_call(
            functools.partial(_attend_t_kernel, nkb=nkb, first_qpos=int(qpos[j0 * tq]), topk=topk,
                              koff=koff, kb_size=kb_size, slopes=_alibi_slopes_log2()),
            grid=(bsz, j1 - j0),
            in_specs=[pos, pos, heads(iq),
                      pl.BlockSpec((None, IDX_HEADS, tq), lambda b, j: (b, 0, j + j0)),
                      heads(q),
                      pl.BlockSpec((None, s_pad, IDX_DIM), lambda b, j: (b, 0, 0)),
                      pl.BlockSpec((A_KV_HEADS, None, s_pad, k.shape[-1]), lambda b, j: (0, b, 0, 0)),
                      pl.BlockSpec((A_KV_HEADS, None, VT_ROWS, s_pad), lambda b, j: (0, b, 0, 0))],
            out_specs=pl.BlockSpec((None, tq, A_Q), lambda b, j: (b, j, 0)),
            scratch_shapes=[pltpu.VMEM((s_pad, tq), I32), pltpu.VMEM((s_pad, tq), I32),
                            pltpu.VMEM((A_HEADS, kb_size, tq), F32)],
            out_shape=jax.ShapeDtypeStruct((bsz, (j1 - j0) * tq, A_Q), BF16),
            compiler_params=pltpu.CompilerParams(
                dimension_semantics=("arbitrary", "arbitrary"), vmem_limit_bytes=VMEM_LIMIT),
            name=f"attend_t{nkb}",
        )(qpos_row, lim_row, iq, wi_t, q, ki, k, vt))
    return jnp.concatenate(outs, axis=1)


HGRN_CHUNK = 128


def _hgrn2_tables(ct):
    r = np.arange(ct)
    seg = [r[:, None] >= r[None, :], r[None, :] > r[:, None]]
    pair = []
    g = ct
    while g >= 2:
        mid = (r // g) * g + g // 2
        second = r >= mid
        seg.append(np.where(second[:, None],
                            (r[None, :] >= mid[:, None]) & (r[None, :] <= r[:, None]),
                            (r[None, :] > r[:, None]) & (r[None, :] < mid[:, None])))
        pair.append((r[:, None] // g == r[None, :] // g) & second[:, None] & ~second[None, :])
        g //= 2
    return np.concatenate(seg, 0).astype(np.float32), np.stack(pair).astype(np.float32)


def _hgrn2_kernel(bq_ref, bf_ref, bi_ref, lb_ref, seg_ref, pair_ref, s0_ref, o_ref, sout_ref, st_scr):
    c = pl.program_id(1)
    ct = bq_ref.shape[0]
    n_lvl = pair_ref.shape[0]

    @pl.when(c == 0)
    def _():
        for h in range(B_HEADS):
            st_scr[h] = s0_ref[h].T

    lb = lb_ref[...]
    fx = bf_ref[...]
    logf = jnp.log(lb + (1.0 - lb) * jax.nn.sigmoid(fx))
    kk = (1.0 - lb) * jax.nn.sigmoid(-fx)
    bq = bq_ref[...]
    rq = bq * jax.nn.sigmoid(bq)
    rv = bi_ref[...]
    p1 = logf.astype(BF16)
    p2 = (logf - p1.astype(F32)).astype(BF16)
    seg = seg_ref[...]
    sums = jnp.dot(seg, p1, preferred_element_type=F32) + jnp.dot(seg, p2, preferred_element_type=F32)
    b_in = sums[:ct]
    b_out = sums[ct:2 * ct]
    q_in = (rq * jnp.exp(b_in)).astype(BF16)
    k_out = (kk * jnp.exp(b_out)).astype(BF16)
    rvb = rv.astype(BF16)
    row = lax.broadcasted_iota(I32, (ct, B_KEY_DIM), 0)
    nt = (((1,), (1,)), ((), ()))

    a = [jnp.zeros((ct, ct), F32) for _ in range(B_HEADS)]
    for lvl in range(n_lvl):
        half = ct >> (lvl + 1)
        second = (row & half) != 0
        dec = jnp.exp(sums[(2 + lvl) * ct:(3 + lvl) * ct])
        for h in range(B_HEADS):
            ls = slice(h * B_KEY_DIM, (h + 1) * B_KEY_DIM)
            x = (jnp.where(second, rq[:, ls], kk[:, ls]) * dec[:, ls]).astype(BF16)
            a[h] = a[h] + pair_ref[lvl] * lax.dot_general(x, x, nt, preferred_element_type=F32)

    for h in range(B_HEADS):
        ls = slice(h * B_KEY_DIM, (h + 1) * B_KEY_DIM)
        st = st_scr[h]
        own = jnp.sum(rq[:, ls] * kk[:, ls], axis=1, keepdims=True)
        o_ref[:, ls] = (jnp.dot(a[h].astype(BF16), rvb[:, ls], preferred_element_type=F32)
                        + own * rv[:, ls]
                        + lax.dot_general(q_in[:, ls], st.astype(BF16), nt, preferred_element_type=F32))
        upd = lax.dot_general(rvb[:, ls], k_out[:, ls], (((0,), (0,)), ((), ())),
                              preferred_element_type=F32)
        st_scr[h] = st * jnp.exp(b_in[ct - 1:ct, ls]) + upd

    @pl.when(c == pl.num_programs(1) - 1)
    def _():
        for h in range(B_HEADS):
            sout_ref[h] = st_scr[h].T


def _hgrn2(bq, bf, bi, lb, s0):
    bsz, t, _ = bq.shape
    ct = HGRN_CHUNK
    tp = -(-t // ct) * ct
    if tp != t:
        pad = lambda a, v: jnp.pad(a, ((0, 0), (0, tp - t), (0, 0)), constant_values=v)
        bq, bf, bi = pad(bq, 0.0), pad(bf, 1e4), pad(bi, 0.0)
    seg, pair = _hgrn2_tables(ct)
    seg = jnp.asarray(seg, BF16)
    pair = jnp.asarray(pair, F32)
    tok = pl.BlockSpec((None, ct, B_KW), lambda b, c: (b, c, 0))
    s_map = (lambda b, c: (0, 0, 0, 0)) if s0.shape[0] == 1 else (lambda b, c: (b, 0, 0, 0))
    s_in = pl.BlockSpec((None, B_HEADS, B_KEY_DIM, B_VAL_DIM), s_map)
    s_out = pl.BlockSpec((None, B_HEADS, B_KEY_DIM, B_VAL_DIM), lambda b, c: (b, 0, 0, 0))
    o, s = pl.pallas_call(
        _hgrn2_kernel,
        grid=(bsz, tp // ct),
        in_specs=[tok, tok, tok, _resident(lb.shape), _resident(seg.shape), _resident(pair.shape), s_in],
        out_specs=[tok, s_out],
        out_shape=[jax.ShapeDtypeStruct((bsz, tp, B_WIDTH), F32),
                   jax.ShapeDtypeStruct((bsz, B_HEADS, B_KEY_DIM, B_VAL_DIM), F32)],
        scratch_shapes=[pltpu.VMEM((B_HEADS, B_VAL_DIM, B_KEY_DIM), F32)],
        compiler_params=pltpu.CompilerParams(
            dimension_semantics=("arbitrary", "arbitrary"), vmem_limit_bytes=VMEM_LIMIT),
        name="hgrn2",
    )(bq, bf, bi, lb, seg, pair, s0)
    return o[:, :t], s


def _mix_out_kernel(h_ref, oa_ref, orec_ref, wgate_ref, gn_ref, wa_ref, wb_ref, wo_ref,
                    g2_ref, b2_ref, wg_ref, wu_ref, wd_ref, g3_ref, b3_ref, y_ref):
    h = h_ref[...]
    gates = jnp.dot(h.astype(BF16), wgate_ref[...], preferred_element_type=F32)
    bg = gates[:, :B_WIDTH]
    ga = gates[:, B_WIDTH:B_WIDTH + D_MODEL]
    gb = gates[:, B_WIDTH + D_MODEL:]
    orec = orec_ref[...]
    gn = gn_ref[...]
    parts = []
    for head in range(B_HEADS):
        ls = slice(head * B_VAL_DIM, (head + 1) * B_VAL_DIM)
        oh = orec[:, ls]
        ms = jnp.mean(oh * oh, axis=-1, keepdims=True)
        parts.append(oh * lax.rsqrt(ms + RMS_EPS) * gn[:, ls])
    on = jnp.concatenate(parts, axis=-1)
    ob = on * (bg * jax.nn.sigmoid(bg))
    pa = jnp.dot(oa_ref[...], wa_ref[...], preferred_element_type=F32)
    pb = jnp.dot(ob.astype(BF16), wb_ref[...], preferred_element_type=F32)
    mixed = jax.nn.sigmoid(ga) * pa + jax.nn.sigmoid(gb) * pb
    mixed = jnp.dot(mixed.astype(BF16), wo_ref[...], preferred_element_type=F32)
    h2 = _layer_norm(ALPHA * h + mixed, g2_ref[...], b2_ref[...])
    y_ref[...] = _layer_norm(_swiglu_step(h2, wg_ref, wu_ref, wd_ref), g3_ref[...], b3_ref[...])


def _mix_out(h, oa, orec, wgate, gn, wa, wb, wo, g2, b2, wg, wu, wd, g3, b3, tm):
    m = h.shape[0]
    assert m % tm == 0
    row = lambda w: pl.BlockSpec((tm, w), lambda i: (i, 0))
    weights = (wgate, gn, wa, wb, wo, g2, b2, wg, wu, wd, g3, b3)
    return pl.pallas_call(
        _mix_out_kernel,
        grid=(m // tm,),
        in_specs=[row(D_MODEL), row(A_Q), row(B_WIDTH)] + [_resident(w.shape) for w in weights],
        out_specs=row(D_MODEL),
        out_shape=jax.ShapeDtypeStruct((m, D_MODEL), F32),
        compiler_params=pltpu.CompilerParams(
            dimension_semantics=("arbitrary",), vmem_limit_bytes=VMEM_LIMIT),
        name="mix_out",
    )(h, oa, orec, *weights)


def _ffn_weights(wg, wu, wd):
    return wg.astype(BF16), wu.astype(BF16), wd.astype(BF16).reshape(N_FF_CHUNKS, FF_CHUNK, D_MODEL)


def _pad_rows(a, n):
    return jnp.pad(a, ((0, 0), (0, n - a.shape[1]), (0, 0)))


FFN_IN_TILE = 512
MIX_OUT_TILE = 512


def _token_tile(m, largest):
    tm = largest
    while tm > 8 and m % tm:
        tm //= 2
    assert m % tm == 0, m
    return tm


def kernel(x_prompt, x_sample, cache_k, cache_v, cache_kidx, state_hgrn, meta, ln1_g, ln1_b, ffn1_wg, ffn1_wu, ffn1_wd, w_in, lb_param, g_norm, w_a_proj, w_b_proj, w_out, ln2_g, ln2_b, ffn2_wg, ffn2_wu, ffn2_wd, ln3_g, ln3_b):
    assert ln1_g.shape[0] == DEPTH
    bp, seq, _ = x_prompt.shape
    bs, dseq, _ = x_sample.shape
    past = cache_k.shape[2]
    l = 0
    vec = lambda a: a.reshape(1, -1).astype(F32)

    lb_all = jnp.cumsum(jax.nn.softmax(lb_param.astype(F32), axis=0), axis=0)
    lb = lb_all[l].reshape(1, B_KW)
    f1 = _ffn_weights(ffn1_wg[l], ffn1_wu[l], ffn1_wd[l])
    f2 = _ffn_weights(ffn2_wg[l], ffn2_wu[l], ffn2_wd[l])
    w_pieces, off = [], 0
    for _, width, _ in _PROJ_OUT:
        w_pieces.append(w_in[l][:, off:off + width].astype(BF16))
        off += width
    w_gate = w_in[l][:, off:].astype(BF16)
    assert w_gate.shape[1] == sum(_GATE_WIDTHS)

    kb_p = PROMPT_KEY_BLOCK
    s_p = -(-(N_META + seq) // kb_p) * kb_p
    q_cols, k_cols = _alibi_columns(s_p)

    def stage1(x):
        return _ffn_in(x, vec(ln1_g[l]), vec(ln1_b[l]), *f1, q_cols[:, None, :], w_pieces,
                       _token_tile(x.shape[0], FFN_IN_TILE))

    def stage4(p, oa, orec):
        m = p["h"].shape[0]
        return _mix_out(p["h"], oa.reshape(m, A_Q), orec.reshape(m, B_WIDTH), w_gate,
                        vec(g_norm[l]), w_a_proj[l].astype(BF16), w_b_proj[l].astype(BF16),
                        w_out[l].astype(BF16), vec(ln2_g[l]), vec(ln2_b[l]), *f2,
                        vec(ln3_g[l]), vec(ln3_b[l]), _token_tile(m, MIX_OUT_TILE))

    pm = stage1(meta.astype(F32))
    pp = stage1(x_prompt.reshape(bp * seq, D_MODEL))
    ps = stage1(x_sample.reshape(bs * dseq, D_MODEL))
    per_p = lambda a: a.reshape(bp, seq, a.shape[-1])
    per_s = lambda a: a.reshape(bs, dseq, a.shape[-1])

    def with_meta(name):
        rows = jnp.broadcast_to(pm[name][None], (bp, N_META, pm[name].shape[-1]))
        return jnp.concatenate([rows, per_p(pp[name])], axis=1)

    k_p, v_p, ki_p = with_meta("ak"), with_meta("av"), with_meta("ik")

    qpos = np.arange(seq, dtype=np.int32)
    lim = N_META + CHUNK * (qpos // CHUNK + 1)
    heads_p = lambda a: a.reshape(a.shape[0], bp, seq, a.shape[-1])
    kv_heads = lambda a: _pad_rows(a.astype(BF16), s_p).reshape(bp, s_p, A_KV_HEADS, A_HEAD_DIM)
    vt = kv_heads(v_p).transpose(2, 0, 3, 1)
    vt = jnp.concatenate([vt, jnp.ones_like(vt[:, :, :1]),
                          jnp.zeros_like(vt[:, :, :VT_ROWS - A_HEAD_DIM - 1])], axis=2)
    k_aug = jnp.concatenate([kv_heads(k_p).transpose(2, 0, 1, 3), jnp.broadcast_to(
        k_cols[None, None], (A_KV_HEADS, bp, s_p, A_HEAD_DIM))], axis=-1)
    oa_p = _attend_t(heads_p(pp["iq"]), per_p(pp["iw"]).transpose(0, 2, 1), heads_p(pp["aq"]),
                     _pad_rows(ki_p.astype(BF16), s_p), k_aug, vt, qpos, lim,
                     kb_size=kb_p, topk=min(TOPK_MAX, seq // 4), koff=N_META)

    n_all = past + dseq
    kb_s = SAMPLE_KEY_BLOCK
    s_s = -(-n_all // kb_s) * kb_s
    k_all = jnp.concatenate([cache_k[l].reshape(bs, past, A_KV).astype(BF16), per_s(ps["ak"]).astype(BF16)], axis=1)
    v_all = jnp.concatenate([cache_v[l].reshape(bs, past, A_KV).astype(BF16), per_s(ps["av"]).astype(BF16)], axis=1)
    ki_all = jnp.concatenate([cache_kidx[l].astype(BF16), per_s(ps["ik"]).astype(BF16)], axis=1)
    tokens_s = lambda a: a.transpose(1, 0, 2).reshape(bs, dseq, a.shape[0] * a.shape[-1])
    oa_s = _attend(tokens_s(ps["iq"]), per_s(ps["iw"]), tokens_s(ps["aq"][..., :A_HEAD_DIM]),
                   _pad_rows(ki_all, s_s), _pad_rows(k_all, s_s), _pad_rows(v_all, s_s),
                   past + jnp.arange(dseq, dtype=I32), jnp.full((dseq,), n_all, I32),
                   tq=dseq, kb_size=kb_s, topk=min(TOPK_MAX, n_all // 4), koff=0)

    zero_state = jnp.zeros((1, B_HEADS, B_KEY_DIM, B_VAL_DIM), F32)
    _, st_m = _hgrn2(pm["bq"][None], pm["bf"][None], pm["bi"][None], lb, zero_state)
    orec_p, st_p = _hgrn2(per_p(pp["bq"]), per_p(pp["bf"]), per_p(pp["bi"]), lb, st_m)
    orec_s, st_s = _hgrn2(per_s(ps["bq"]), per_s(ps["bf"]), per_s(ps["bi"]), lb,
                          state_hgrn[l].astype(F32))

    y_p = stage4(pp, oa_p, orec_p).reshape(bp, seq, D_MODEL)
    y_s = stage4(ps, oa_s, orec_s).reshape(bs, dseq, D_MODEL)

    kv5 = lambda a: a.reshape(1, a.shape[0], a.shape[1], A_KV_HEADS, A_HEAD_DIM)
    return (y_p, y_s, kv5(k_p), kv5(v_p), ki_p[None], st_p[None].astype(state_hgrn.dtype),
            kv5(per_s(ps["ak"])), kv5(per_s(ps["av"])), per_s(ps["ik"])[None],
            st_s[None].astype(state_hgrn.dtype))
```

```python
import functools

import jax
import jax.numpy as jnp
import numpy as np
from jax import lax
from jax.experimental import pallas as pl
from jax.experimental.pallas import tpu as pltpu

F32 = jnp.float32
BF16 = jnp.bfloat16
I32 = jnp.int32

D_MODEL = 1024
D_FF = 2816
FF_CHUNK = 256
N_FF_CHUNKS = D_FF // FF_CHUNK
CHUNK = 64
N_META = 16
A_HEADS = 8
A_KV_HEADS = 2
A_GROUP = A_HEADS // A_KV_HEADS
A_HEAD_DIM = 64
A_Q = A_HEADS * A_HEAD_DIM
A_KV = A_KV_HEADS * A_HEAD_DIM
IDX_HEADS = 8
IDX_DIM = 64
IDX_Q = IDX_HEADS * IDX_DIM
TOPK_MAX = 256
B_HEADS = 4
B_KEY_DIM = 128
B_VAL_DIM = 128
B_KW = B_HEADS * B_KEY_DIM
B_WIDTH = B_HEADS * B_VAL_DIM
DEPTH = 1
ALPHA = (2.0 * DEPTH) ** 0.25
LN_EPS = 1e-5
RMS_EPS = 1e-6
NEG_INF = -1e30
LOG2E = 1.4426950408889634
INT_MIN = -(2 ** 31)
LANES = 128
VMEM_LIMIT = 56 * 1024 * 1024


def _resident(shape):
    n = len(shape)
    return pl.BlockSpec(shape, lambda *_: (0,) * n, pipeline_mode=pl.Buffered(1))


def _layer_norm(x, g, b):
    mu = jnp.mean(x, axis=-1, keepdims=True)
    xc = x - mu
    var = jnp.mean(xc * xc, axis=-1, keepdims=True)
    return xc * lax.rsqrt(var + LN_EPS) * g + b


def _swiglu_step(x, wg_ref, wu_ref, wd_ref):
    xb = x.astype(BF16)

    acc = jnp.zeros(x.shape, F32)
    for c in range(N_FF_CHUNKS):
        cols = slice(c * FF_CHUNK, (c + 1) * FF_CHUNK)
        g = jnp.dot(xb, wg_ref[:, cols], preferred_element_type=F32)
        u = jnp.dot(xb, wu_ref[:, cols], preferred_element_type=F32)
        a = (g * jax.nn.sigmoid(g) * u).astype(BF16)
        acc = acc + jnp.dot(a, wd_ref[c], preferred_element_type=F32)
    return ALPHA * x + 0.5 * acc


_PROJ_OUT = (
    ("aq", A_Q, BF16), ("ak", A_KV, F32), ("av", A_KV, F32), ("iq", IDX_Q, BF16),
    ("ik", IDX_DIM, F32), ("iw", IDX_HEADS, F32),
    ("bq", B_KW, F32), ("bf", B_KW, F32), ("bi", B_WIDTH, F32),
)
_GATE_WIDTHS = (B_WIDTH, D_MODEL, D_MODEL)
_PROJ_SCALE = {"aq": (A_HEAD_DIM ** -0.5) * LOG2E, "iw": (IDX_HEADS ** -0.5) * (IDX_DIM ** -0.5)}
_HEAD_MAJOR = {"aq": A_HEADS, "iq": IDX_HEADS}


def _ffn_in_kernel(x_ref, g_ref, b_ref, wg_ref, wu_ref, wd_ref, qcols_ref, *rest):
    n = len(_PROJ_OUT)
    w_refs, h_ref, out_refs = rest[:n], rest[n], rest[n + 1:]
    x = x_ref[...]
    h = _layer_norm(_swiglu_step(x, wg_ref, wu_ref, wd_ref), g_ref[...], b_ref[...])
    h_ref[...] = h
    hb = h.astype(BF16)
    for (name, _, dt), w_ref, o_ref in zip(_PROJ_OUT, w_refs, out_refs):
        y = jnp.dot(hb, w_ref[...], preferred_element_type=F32)
        if name in _PROJ_SCALE:
            y = y * _PROJ_SCALE[name]
        if name in _HEAD_MAJOR:
            hd = y.shape[1] // _HEAD_MAJOR[name]
            for head in range(_HEAD_MAJOR[name]):
                yh = y[:, head * hd:(head + 1) * hd].astype(dt)
                if name == "aq":
                    yh = jnp.concatenate([yh, jnp.broadcast_to(qcols_ref[head], yh.shape)], axis=1)
                o_ref[head] = yh
        else:
            o_ref[...] = y.astype(dt)


def _ffn_in(x, ln_g, ln_b, wg, wu, wd, q_cols, w_pieces, tm):
    m = x.shape[0]
    assert m % tm == 0
    row = lambda w: pl.BlockSpec((tm, w), lambda i: (i, 0))
    in_specs = [row(D_MODEL), _resident(ln_g.shape), _resident(ln_b.shape),
                _resident(wg.shape), _resident(wu.shape), _resident(wd.shape),
                _resident(q_cols.shape)]
    in_specs += [_resident(w.shape) for w in w_pieces]
    out_shape = [jax.ShapeDtypeStruct((m, D_MODEL), F32)]
    out_specs = [row(D_MODEL)]
    for name, width, dt in _PROJ_OUT:
        if name in _HEAD_MAJOR:
            nh = _HEAD_MAJOR[name]
            hd = width // nh + (q_cols.shape[-1] if name == "aq" else 0)
            out_shape.append(jax.ShapeDtypeStruct((nh, m, hd), dt))
            out_specs.append(pl.BlockSpec((nh, tm, hd), lambda i: (0, i, 0)))
        else:
            out_shape.append(jax.ShapeDtypeStruct((m, width), dt))
            out_specs.append(row(width))
    outs = pl.pallas_call(
        _ffn_in_kernel,
        grid=(m // tm,),
        in_specs=in_specs,
        out_specs=out_specs,
        out_shape=out_shape,
        compiler_params=pltpu.CompilerParams(
            dimension_semantics=("arbitrary",), vmem_limit_bytes=VMEM_LIMIT),
        name="ffn_in",
    )(x, ln_g, ln_b, wg, wu, wd, q_cols, *w_pieces)
    res = {"h": outs[0]}
    for (name, _, _), o in zip(_PROJ_OUT, outs[1:]):
        res[name] = o
    return res


def _attend_kernel(nkb_ref, qpos_ref, lim_ref, iq_ref, wi_ref, q_ref, ki_ref, k_ref, v_ref,
                   o_ref, key_scr, t_scr, *, topk, koff, kb_size, slopes):
    tq = q_ref.shape[0]
    nkb = nkb_ref[pl.program_id(1)]
    lim = lim_ref[...]
    qpos = qpos_ref[...]
    sub = kb_size // LANES

    def lane_ids(s0):
        return s0 + lax.broadcasted_iota(I32, (tq, kb_size), 1)

    def score_block(kb, carry):
        s0 = pl.multiple_of(kb * kb_size, kb_size)
        kib = ki_ref[pl.ds(s0, kb_size), :]
        acc = jnp.zeros((tq, kb_size), F32)
        for h in range(IDX_HEADS):
            lg = lax.dot_general(iq_ref[:, h * IDX_DIM:(h + 1) * IDX_DIM], kib,
                                 (((1,), (1,)), ((), ())), preferred_element_type=F32)
            acc = acc + jnp.maximum(lg, 0.0) * wi_ref[:, h:h + 1]
        bits = lax.bitcast_convert_type(acc, I32)
        key = bits ^ ((bits >> 31) & 0x7FFFFFFF)
        key_scr[:, pl.ds(s0, kb_size)] = jnp.where(lane_ids(s0) < lim, key, INT_MIN)
        return carry

    lax.fori_loop(0, nkb, score_block, 0)

    def bcast(x):
        return jnp.broadcast_to(x, (tq, LANES))

    def count(pred):
        def body(kb, acc):
            for c in range(sub):
                off = pl.multiple_of(kb * kb_size + c * LANES, LANES)
                col = off + lax.broadcasted_iota(I32, (tq, LANES), 1)
                acc = acc + jnp.where(pred(key_scr[:, pl.ds(off, LANES)], col), 1.0, 0.0)
            return acc
        acc = lax.fori_loop(0, nkb, body, jnp.zeros((tq, LANES), F32))
        return jnp.sum(acc, axis=1, keepdims=True)

    def bit_step(state):
        i, t, settled, _ = state
        stop = jnp.min(settled)
        cand = t + lax.shift_left(jnp.int32(1), 31 - i)
        cand_b = bcast(cand)
        cnt = count(lambda key, col: key >= cand_b)
        take = (cnt >= topk) & (settled == 0.0)
        settled = jnp.where(take & (cnt == topk), 1.0, settled)
        return i + 1, jnp.where(take, cand, t), settled, stop

    _, t, _, _ = lax.while_loop(
        lambda state: (state[0] < 32) & (state[3] == 0.0), bit_step,
        (jnp.int32(0), jnp.full((tq, 1), INT_MIN, I32), (lim <= topk).astype(F32), jnp.float32(0.0)))
    thr = jnp.maximum(t, INT_MIN + 1)
    thr_b = bcast(thr)
    n_ge = count(lambda key, col: key >= thr_b)

    @pl.when(jnp.max(n_ge) > topk)
    def _():
        n_gt = count(lambda key, col: key > thr_b)
        need = topk - n_gt
        n_bits = (key_scr.shape[1] - 1).bit_length()

        def idx_step(i, m0):
            cand = m0 + lax.shift_left(jnp.int32(1), n_bits - 1 - i)
            cand_b = bcast(cand)
            g = count(lambda key, col: (key == thr_b) & (col < cand_b))
            return jnp.where(g < need, cand, m0)

        m0 = lax.fori_loop(0, n_bits, idx_step, jnp.zeros((tq, 1), I32))
        drop_row = n_ge > topk

        def demote(kb, carry):
            s0 = pl.multiple_of(kb * kb_size, kb_size)
            key = key_scr[:, pl.ds(s0, kb_size)]
            drop = drop_row & (key == thr) & (lane_ids(s0) > m0)
            key_scr[:, pl.ds(s0, kb_size)] = jnp.where(drop, INT_MIN, key)
            return carry

        lax.fori_loop(0, nkb, demote, 0)

    def attend_block(kb, carry):
        ms, ls, accs = carry
        s0 = pl.multiple_of(kb * kb_size, kb_size)
        col = lane_ids(s0)
        bias = jnp.where(key_scr[:, pl.ds(s0, kb_size)] >= thr, 0.0, NEG_INF)
        dist = jnp.abs(qpos - (col - koff)).astype(F32)
        kblk = k_ref[pl.ds(s0, kb_size), :]
        vblk = v_ref[pl.ds(s0, kb_size), :]
        new_m, new_l, new_acc = [], [], []
        for h in range(A_HEADS):
            g = h // A_GROUP
            s = lax.dot_general(q_ref[:, h * A_HEAD_DIM:(h + 1) * A_HEAD_DIM],
                                kblk[:, g * A_HEAD_DIM:(g + 1) * A_HEAD_DIM],
                                (((1,), (1,)), ((), ())), preferred_element_type=F32)
            s = (s - slopes[h] * dist) + bias
            t_scr[h] = s
            new_m.append(jnp.maximum(ms[h], jnp.max(s, axis=1, keepdims=True)))
        for h in range(A_HEADS):
            g = h // A_GROUP
            p = jnp.exp2(t_scr[h] - new_m[h])
            alpha = jnp.exp2(ms[h] - new_m[h])
            new_l.append(alpha * ls[h] + jnp.sum(p, axis=1, keepdims=True))
            pv = jnp.dot(p.astype(BF16), vblk[:, g * A_HEAD_DIM:(g + 1) * A_HEAD_DIM],
                         preferred_element_type=F32)
            new_acc.append(alpha * accs[h] + pv)
        return tuple(new_m), tuple(new_l), tuple(new_acc)

    init = (tuple(jnp.full((tq, 1), NEG_INF, F32) for _ in range(A_HEADS)),
            tuple(jnp.zeros((tq, 1), F32) for _ in range(A_HEADS)),
            tuple(jnp.zeros((tq, A_HEAD_DIM), F32) for _ in range(A_HEADS)))
    _, ls, accs = lax.fori_loop(0, nkb, attend_block, init)
    for h in range(A_HEADS):
        o_ref[:, h * A_HEAD_DIM:(h + 1) * A_HEAD_DIM] = (accs[h] / ls[h]).astype(o_ref.dtype)


def _attend(iq, wi, q, ki, k, v, qpos, lim, *, tq, kb_size, topk, koff):
    bsz, nq, _ = q.shape
    s_pad = k.shape[1]
    assert nq % tq == 0 and s_pad % kb_size == 0
    nt = nq // tq
    nkb = (jnp.max(lim.reshape(nt, tq), axis=1) + kb_size - 1) // kb_size
    slopes = _alibi_slopes_log2()
    qrow = lambda w: pl.BlockSpec((None, tq, w), lambda b, j, n: (b, j, 0))
    krow = lambda w: pl.BlockSpec((None, s_pad, w), lambda b, j, n: (b, 0, 0))
    pos = pl.BlockSpec((tq, 1), lambda b, j, n: (j, 0))
    return pl.pallas_call(
        functools.partial(_attend_kernel, topk=topk, koff=koff, kb_size=kb_size, slopes=slopes),
        grid_spec=pltpu.PrefetchScalarGridSpec(
            num_scalar_prefetch=1,
            grid=(bsz, nt),
            in_specs=[pos, pos, qrow(IDX_Q), qrow(IDX_HEADS), qrow(A_Q),
                      krow(IDX_DIM), krow(A_KV), krow(A_KV)],
            out_specs=qrow(A_Q),
            scratch_shapes=[pltpu.VMEM((tq, s_pad), I32),
                            pltpu.VMEM((A_HEADS, tq, kb_size), F32)],
        ),
        out_shape=jax.ShapeDtypeStruct((bsz, nq, A_Q), BF16),
        compiler_params=pltpu.CompilerParams(
            dimension_semantics=("arbitrary", "arbitrary"), vmem_limit_bytes=VMEM_LIMIT),
        name="attend",
    )(nkb.astype(I32), qpos.reshape(nq, 1), lim.reshape(nq, 1), iq, wi, q, ki, k, v)


def _fold_rows(x, op, rows=8):
    while x.shape[0] > rows:
        half = x.shape[0] // 2
        x = op(x[:half], x[half:])
    return x


VT_ROWS = A_HEAD_DIM + 16
PROMPT_KEY_BLOCK = 512
KEY_SCAN_STEP = 256
SAMPLE_KEY_BLOCK = 384


def _alibi_slopes_log2():
    return tuple(float(2.0 ** (-8.0 * (i + 1) / A_HEADS)) * LOG2E for i in range(A_HEADS))


ALIBI_SPLIT = CHUNK


def _alibi_columns(n_keys):
    q_cols = np.zeros((A_HEADS, A_HEAD_DIM), np.float32)
    for h, c in enumerate(_alibi_slopes_log2()):
        for i in range(3):
            ci = float(np.asarray(c, dtype=BF16).astype(np.float32))
            q_cols[h, 2 * i], q_cols[h, 2 * i + 1] = ALIBI_SPLIT * ci, ci
            c -= ci
    ids = np.arange(n_keys)
    k_cols = np.zeros((n_keys, A_HEAD_DIM), np.float32)
    k_cols[:, 0:6:2] = (ids // ALIBI_SPLIT)[:, None]
    k_cols[:, 1:6:2] = (ids % ALIBI_SPLIT)[:, None]
    return jnp.asarray(q_cols, BF16), jnp.asarray(k_cols, BF16)


def _attend_t_kernel(qpos_ref, lim_ref, iq_ref, wi_ref, q_ref, ki_ref, k_ref, vt_ref,
                     o_ref, key_scr, tie_scr, t_scr, *, n_keys, first_qpos, topk, koff, kb_size,
                     slopes):
    tq = o_ref.shape[0]
    lim = lim_ref[...]
    qpos = qpos_ref[...]
    n_pairs = A_HEADS // 2
    blocks = [(s0, min(kb_size, n_keys - s0), pl.ds(s0, min(kb_size, n_keys - s0)))
              for s0 in range(0, n_keys, kb_size)]

    def row_ids(s0, size):
        return s0 + lax.broadcasted_iota(I32, (size, tq), 0)

    def head_pair(ref, pr):
        return ref[2 * pr:2 * pr + 2].reshape(2 * tq, ref.shape[-1])

    for s0, size, rows in blocks:
        kib = ki_ref[rows, :]
        acc = jnp.zeros((size, tq), F32)
        for pr in range(IDX_HEADS // 2):
            lg = lax.dot_general(kib, head_pair(iq_ref, pr), (((1,), (1,)), ((), ())),
                                 preferred_element_type=F32)
            for u in range(2):
                h = 2 * pr + u
                acc = acc + jnp.maximum(lg[:, u * tq:(u + 1) * tq], 0.0) * wi_ref[h:h + 1, :]
        bits = lax.bitcast_convert_type(acc, I32)
        key = bits ^ ((bits >> 31) & 0x7FFFFFFF)
        key = jnp.where(row_ids(s0, size) < lim, key, INT_MIN)
        key_scr[rows, :] = key

    def count(plane, pred):
        acc = jnp.zeros((8, tq), F32)
        for _, _, rows in blocks:
            acc = acc + _fold_rows(jnp.where(pred(plane[rows, :]), 1.0, 0.0), jnp.add)
        return jnp.sum(acc, axis=0, keepdims=True)

    def kth_largest(plane, kth, bits):
        def bit_step(i, t):
            cand = t + lax.shift_left(jnp.int32(1), bits - 1 - i)
            return jnp.where(count(plane, lambda key: key >= cand) >= kth, cand, t)
        low = INT_MIN if bits == 32 else -(1 << (bits - 1))
        return lax.fori_loop(0, bits, bit_step, jnp.full((1, tq), low, I32))

    t = kth_largest(key_scr, topk, 32)
    thr = jnp.maximum(t, INT_MIN + 1)
    n_ge = count(key_scr, lambda key: key >= thr)

    @pl.when(jnp.max(n_ge) > topk)
    def _():
        need = topk - count(key_scr, lambda key: key > thr)
        for s0, size, rows in blocks:
            tie_scr[rows, :] = jnp.where(key_scr[rows, :] == thr, -1 - row_ids(s0, size), INT_MIN)
        last = kth_largest(tie_scr, need, key_scr.shape[0].bit_length() + 1)
        drop_q = n_ge > topk
        for s0, size, rows in blocks:
            key = key_scr[rows, :]
            drop = drop_q & (key == thr) & (-1 - row_ids(s0, size) < last)
            key_scr[rows, :] = jnp.where(drop, INT_MIN, key)

    ms = [jnp.full((1, tq), NEG_INF, F32) for _ in range(A_HEADS)]
    accs = [jnp.zeros((vt_ref.shape[1], A_GROUP * tq), F32) for _ in range(A_KV_HEADS)]
    for s0, size, rows in blocks:
        bias = jnp.where(key_scr[rows, :] >= thr, 0.0, NEG_INF)
        may_follow = s0 + size - 1 - koff > first_qpos
        if may_follow:
            ahead = jnp.minimum(qpos - (row_ids(s0, size) - koff), 0).astype(F32)
        new_m = []
        for pr in range(n_pairs):
            g = (2 * pr) // A_GROUP
            s2 = lax.dot_general(k_ref[g, rows, :], head_pair(q_ref, pr),
                                 (((1,), (1,)), ((), ())), preferred_element_type=F32)
            for u in range(2):
                h = 2 * pr + u
                s = s2[:, u * tq:(u + 1) * tq]
                if may_follow:
                    s = s + (2.0 * slopes[h]) * ahead
                s = s + bias
                t_scr[h, :size] = s
                blk_max = jnp.max(_fold_rows(s, jnp.maximum), axis=0, keepdims=True)
                new_m.append(jnp.maximum(ms[h], blk_max))
        for g in range(A_KV_HEADS):
            ps, alphas = [], []
            for h in range(g * A_GROUP, (g + 1) * A_GROUP):
                ps.append(jnp.exp2(t_scr[h, :size] - new_m[h]).astype(BF16))
                alphas.append(jnp.exp2(ms[h] - new_m[h]))
            pv = jnp.dot(vt_ref[g, :, rows], jnp.concatenate(ps, axis=1),
                         preferred_element_type=F32)
            accs[g] = jnp.concatenate(alphas, axis=1) * accs[g] + pv
        ms = new_m
    for pr in range(n_pairs):
        rows = []
        for h in (2 * pr, 2 * pr + 1):
            g, hh = divmod(h, A_GROUP)
            cols = slice(hh * tq, (hh + 1) * tq)
            rows.append(accs[g][:A_HEAD_DIM, cols] / accs[g][A_HEAD_DIM:A_HEAD_DIM + 1, cols])
        o_ref[:, 2 * pr * A_HEAD_DIM:(2 * pr + 2) * A_HEAD_DIM] = (
            jnp.concatenate(rows, axis=0).T.astype(o_ref.dtype))


def _attend_t(iq, wi_t, q, ki, k, vt, qpos, lim, *, kb_size, topk, koff):
    tq = LANES
    _, bsz, nq, _ = q.shape
    s_pad = ki.shape[1]
    assert nq % tq == 0 and s_pad % kb_size == 0
    keys_of_tile = -(-lim.reshape(nq // tq, tq).max(axis=1) // KEY_SCAN_STEP) * KEY_SCAN_STEP
    assert keys_of_tile.max() <= s_pad
    runs, j0 = [], 0
    for j in range(1, len(keys_of_tile) + 1):
        if j == len(keys_of_tile) or keys_of_tile[j] != keys_of_tile[j0]:
            runs.append((j0, j, int(keys_of_tile[j0])))
            j0 = j
    qpos_row = jnp.asarray(qpos, I32).reshape(1, nq)
    lim_row = jnp.asarray(lim, I32).reshape(1, nq)
    outs = []
    for j0, j1, n_keys in runs:
        heads = lambda a: pl.BlockSpec((a.shape[0], None, tq, a.shape[-1]),
                                       lambda b, j: (0, b, j + j0, 0))
        pos = pl.BlockSpec((1, tq), lambda b, j: (0, j + j0))
        outs.append(pl.pallas_call(
            functools.partial(_attend_t_kernel, n_keys=n_keys, first_qpos=int(qpos[j0 * tq]), topk=topk,
                              koff=koff, kb_size=kb_size, slopes=_alibi_slopes_log2()),
            grid=(bsz, j1 - j0),
            in_specs=[pos, pos, heads(iq),
                      pl.BlockSpec((None, IDX_HEADS, tq), lambda b, j: (b, 0, j + j0)),
                      heads(q),
                      pl.BlockSpec((None, s_pad, IDX_DIM), lambda b, j: (b, 0, 0)),
                      pl.BlockSpec((A_KV_HEADS, None, s_pad, k.shape[-1]), lambda b, j: (0, b, 0, 0)),
                      pl.BlockSpec((A_KV_HEADS, None, VT_ROWS, s_pad), lambda b, j: (0, b, 0, 0))],
            out_specs=pl.BlockSpec((None, tq, A_Q), lambda b, j: (b, j, 0)),
            scratch_shapes=[pltpu.VMEM((s_pad, tq), I32), pltpu.VMEM((s_pad, tq), I32),
                            pltpu.VMEM((A_HEADS, kb_size, tq), F32)],
            out_shape=jax.ShapeDtypeStruct((bsz, (j1 - j0) * tq, A_Q), BF16),
            compiler_params=pltpu.CompilerParams(
                dimension_semantics=("arbitrary", "arbitrary"), vmem_limit_bytes=VMEM_LIMIT),
            name=f"attend_t{n_keys}",
        )(qpos_row, lim_row, iq, wi_t, q, ki, k, vt))
    return jnp.concatenate(outs, axis=1)


HGRN_CHUNK = 128


def _hgrn2_tables(ct):
    r = np.arange(ct)
    seg = [r[:, None] >= r[None, :], r[None, :] > r[:, None]]
    pair = []
    g = ct
    while g >= 2:
        mid = (r // g) * g + g // 2
        second = r >= mid
        seg.append(np.where(second[:, None],
                            (r[None, :] >= mid[:, None]) & (r[None, :] <= r[:, None]),
                            (r[None, :] > r[:, None]) & (r[None, :] < mid[:, None])))
        pair.append((r[:, None] // g == r[None, :] // g) & second[:, None] & ~second[None, :])
        g //= 2
    return np.concatenate(seg, 0).astype(np.float32), np.stack(pair).astype(np.float32)


def _hgrn2_kernel(bq_ref, bf_ref, bi_ref, lb_ref, seg_ref, pair_ref, s0_ref, o_ref, sout_ref, st_scr):
    c = pl.program_id(1)
    ct = bq_ref.shape[0]
    n_lvl = pair_ref.shape[0]

    @pl.when(c == 0)
    def _():
        for h in range(B_HEADS):
            st_scr[h] = s0_ref[h].T

    lb = lb_ref[...]
    fx = bf_ref[...]
    logf = jnp.log(lb + (1.0 - lb) * jax.nn.sigmoid(fx))
    kk = (1.0 - lb) * jax.nn.sigmoid(-fx)
    bq = bq_ref[...]
    rq = bq * jax.nn.sigmoid(bq)
    rv = bi_ref[...]
    p1 = logf.astype(BF16)
    p2 = (logf - p1.astype(F32)).astype(BF16)
    seg = seg_ref[...]
    sums = jnp.dot(seg, p1, preferred_element_type=F32) + jnp.dot(seg, p2, preferred_element_type=F32)
    b_in = sums[:ct]
    b_out = sums[ct:2 * ct]
    q_in = (rq * jnp.exp(b_in)).astype(BF16)
    k_out = (kk * jnp.exp(b_out)).astype(BF16)
    rvb = rv.astype(BF16)
    row = lax.broadcasted_iota(I32, (ct, B_KEY_DIM), 0)
    nt = (((1,), (1,)), ((), ()))

    a = [jnp.zeros((ct, ct), F32) for _ in range(B_HEADS)]
    for lvl in range(n_lvl):
        half = ct >> (lvl + 1)
        second = (row & half) != 0
        dec = jnp.exp(sums[(2 + lvl) * ct:(3 + lvl) * ct])
        for h in range(B_HEADS):
            ls = slice(h * B_KEY_DIM, (h + 1) * B_KEY_DIM)
            x = (jnp.where(second, rq[:, ls], kk[:, ls]) * dec[:, ls]).astype(BF16)
            a[h] = a[h] + pair_ref[lvl] * lax.dot_general(x, x, nt, preferred_element_type=F32)

    for h in range(B_HEADS):
        ls = slice(h * B_KEY_DIM, (h + 1) * B_KEY_DIM)
        st = st_scr[h]
        own = jnp.sum(rq[:, ls] * kk[:, ls], axis=1, keepdims=True)
        o_ref[:, ls] = (jnp.dot(a[h].astype(BF16), rvb[:, ls], preferred_element_type=F32)
                        + own * rv[:, ls]
                        + lax.dot_general(q_in[:, ls], st.astype(BF16), nt, preferred_element_type=F32))
        upd = lax.dot_general(rvb[:, ls], k_out[:, ls], (((0,), (0,)), ((), ())),
                              preferred_element_type=F32)
        st_scr[h] = st * jnp.exp(b_in[ct - 1:ct, ls]) + upd

    @pl.when(c == pl.num_programs(1) - 1)
    def _():
        for h in range(B_HEADS):
            sout_ref[h] = st_scr[h].T


def _hgrn2(bq, bf, bi, lb, s0):
    bsz, t, _ = bq.shape
    ct = HGRN_CHUNK
    tp = -(-t // ct) * ct
    if tp != t:
        pad = lambda a, v: jnp.pad(a, ((0, 0), (0, tp - t), (0, 0)), constant_values=v)
        bq, bf, bi = pad(bq, 0.0), pad(bf, 1e4), pad(bi, 0.0)
    seg, pair = _hgrn2_tables(ct)
    seg = jnp.asarray(seg, BF16)
    pair = jnp.asarray(pair, F32)
    tok = pl.BlockSpec((None, ct, B_KW), lambda b, c: (b, c, 0))
    s_map = (lambda b, c: (0, 0, 0, 0)) if s0.shape[0] == 1 else (lambda b, c: (b, 0, 0, 0))
    s_in = pl.BlockSpec((None, B_HEADS, B_KEY_DIM, B_VAL_DIM), s_map)
    s_out = pl.BlockSpec((None, B_HEADS, B_KEY_DIM, B_VAL_DIM), lambda b, c: (b, 0, 0, 0))
    o, s = pl.pallas_call(
        _hgrn2_kernel,
        grid=(bsz, tp // ct),
        in_specs=[tok, tok, tok, _resident(lb.shape), _resident(seg.shape), _resident(pair.shape), s_in],
        out_specs=[tok, s_out],
        out_shape=[jax.ShapeDtypeStruct((bsz, tp, B_WIDTH), F32),
                   jax.ShapeDtypeStruct((bsz, B_HEADS, B_KEY_DIM, B_VAL_DIM), F32)],
        scratch_shapes=[pltpu.VMEM((B_HEADS, B_VAL_DIM, B_KEY_DIM), F32)],
        compiler_params=pltpu.CompilerParams(
            dimension_semantics=("arbitrary", "arbitrary"), vmem_limit_bytes=VMEM_LIMIT),
        name="hgrn2",
    )(bq, bf, bi, lb, seg, pair, s0)
    return o[:, :t], s


def _mix_out_kernel(h_ref, oa_ref, orec_ref, wgate_ref, gn_ref, wa_ref, wb_ref, wo_ref,
                    g2_ref, b2_ref, wg_ref, wu_ref, wd_ref, g3_ref, b3_ref, y_ref):
    h = h_ref[...]
    gates = jnp.dot(h.astype(BF16), wgate_ref[...], preferred_element_type=F32)
    bg = gates[:, :B_WIDTH]
    ga = gates[:, B_WIDTH:B_WIDTH + D_MODEL]
    gb = gates[:, B_WIDTH + D_MODEL:]
    orec = orec_ref[...]
    gn = gn_ref[...]
    parts = []
    for head in range(B_HEADS):
        ls = slice(head * B_VAL_DIM, (head + 1) * B_VAL_DIM)
        oh = orec[:, ls]
        ms = jnp.mean(oh * oh, axis=-1, keepdims=True)
        parts.append(oh * lax.rsqrt(ms + RMS_EPS) * gn[:, ls])
    on = jnp.concatenate(parts, axis=-1)
    ob = on * (bg * jax.nn.sigmoid(bg))
    pa = jnp.dot(oa_ref[...], wa_ref[...], preferred_element_type=F32)
    pb = jnp.dot(ob.astype(BF16), wb_ref[...], preferred_element_type=F32)
    mixed = jax.nn.sigmoid(ga) * pa + jax.nn.sigmoid(gb) * pb
    mixed = jnp.dot(mixed.astype(BF16), wo_ref[...], preferred_element_type=F32)
    h2 = _layer_norm(ALPHA * h + mixed, g2_ref[...], b2_ref[...])
    y_ref[...] = _layer_norm(_swiglu_step(h2, wg_ref, wu_ref, wd_ref), g3_ref[...], b3_ref[...])


def _mix_out(h, oa, orec, wgate, gn, wa, wb, wo, g2, b2, wg, wu, wd, g3, b3, tm):
    m = h.shape[0]
    assert m % tm == 0
    row = lambda w: pl.BlockSpec((tm, w), lambda i: (i, 0))
    weights = (wgate, gn, wa, wb, wo, g2, b2, wg, wu, wd, g3, b3)
    return pl.pallas_call(
        _mix_out_kernel,
        grid=(m // tm,),
        in_specs=[row(D_MODEL), row(A_Q), row(B_WIDTH)] + [_resident(w.shape) for w in weights],
        out_specs=row(D_MODEL),
        out_shape=jax.ShapeDtypeStruct((m, D_MODEL), F32),
        compiler_params=pltpu.CompilerParams(
            dimension_semantics=("arbitrary",), vmem_limit_bytes=VMEM_LIMIT),
        name="mix_out",
    )(h, oa, orec, *weights)


def _ffn_weights(wg, wu, wd):
    return wg.astype(BF16), wu.astype(BF16), wd.astype(BF16).reshape(N_FF_CHUNKS, FF_CHUNK, D_MODEL)


def _pad_rows(a, n):
    return jnp.pad(a, ((0, 0), (0, n - a.shape[1]), (0, 0)))


FFN_IN_TILE = 512
MIX_OUT_TILE = 512


def _token_tile(m, largest):
    tm = largest
    while tm > 8 and m % tm:
        tm //= 2
    assert m % tm == 0, m
    return tm


def kernel(x_prompt, x_sample, cache_k, cache_v, cache_kidx, state_hgrn, meta, ln1_g, ln1_b, ffn1_wg, ffn1_wu, ffn1_wd, w_in, lb_param, g_norm, w_a_proj, w_b_proj, w_out, ln2_g, ln2_b, ffn2_wg, ffn2_wu, ffn2_wd, ln3_g, ln3_b):
    assert ln1_g.shape[0] == DEPTH
    bp, seq, _ = x_prompt.shape
    bs, dseq, _ = x_sample.shape
    past = cache_k.shape[2]
    l = 0
    vec = lambda a: a.reshape(1, -1).astype(F32)

    lb_all = jnp.cumsum(jax.nn.softmax(lb_param.astype(F32), axis=0), axis=0)
    lb = lb_all[l].reshape(1, B_KW)
    f1 = _ffn_weights(ffn1_wg[l], ffn1_wu[l], ffn1_wd[l])
    f2 = _ffn_weights(ffn2_wg[l], ffn2_wu[l], ffn2_wd[l])
    w_pieces, off = [], 0
    for _, width, _ in _PROJ_OUT:
        w_pieces.append(w_in[l][:, off:off + width].astype(BF16))
        off += width
    w_gate = w_in[l][:, off:].astype(BF16)
    assert w_gate.shape[1] == sum(_GATE_WIDTHS)

    kb_p = PROMPT_KEY_BLOCK
    s_p = -(-(N_META + seq) // kb_p) * kb_p
    q_cols, k_cols = _alibi_columns(s_p)

    def stage1(x):
        return _ffn_in(x, vec(ln1_g[l]), vec(ln1_b[l]), *f1, q_cols[:, None, :], w_pieces,
                       _token_tile(x.shape[0], FFN_IN_TILE))

    def stage4(p, oa, orec):
        m = p["h"].shape[0]
        return _mix_out(p["h"], oa.reshape(m, A_Q), orec.reshape(m, B_WIDTH), w_gate,
                        vec(g_norm[l]), w_a_proj[l].astype(BF16), w_b_proj[l].astype(BF16),
                        w_out[l].astype(BF16), vec(ln2_g[l]), vec(ln2_b[l]), *f2,
                        vec(ln3_g[l]), vec(ln3_b[l]), _token_tile(m, MIX_OUT_TILE))

    pm = stage1(meta.astype(F32))
    pp = stage1(x_prompt.reshape(bp * seq, D_MODEL))
    ps = stage1(x_sample.reshape(bs * dseq, D_MODEL))
    per_p = lambda a: a.reshape(bp, seq, a.shape[-1])
    per_s = lambda a: a.reshape(bs, dseq, a.shape[-1])

    def with_meta(name):
        rows = jnp.broadcast_to(pm[name][None], (bp, N_META, pm[name].shape[-1]))
        return jnp.concatenate([rows, per_p(pp[name])], axis=1)

    k_p, v_p, ki_p = with_meta("ak"), with_meta("av"), with_meta("ik")

    qpos = np.arange(seq, dtype=np.int32)
    lim = N_META + CHUNK * (qpos // CHUNK + 1)
    heads_p = lambda a: a.reshape(a.shape[0], bp, seq, a.shape[-1])
    kv_heads = lambda a: _pad_rows(a.astype(BF16), s_p).reshape(bp, s_p, A_KV_HEADS, A_HEAD_DIM)
    vt = kv_heads(v_p).transpose(2, 0, 3, 1)
    vt = jnp.concatenate([vt, jnp.ones_like(vt[:, :, :1]),
                          jnp.zeros_like(vt[:, :, :VT_ROWS - A_HEAD_DIM - 1])], axis=2)
    k_aug = jnp.concatenate([kv_heads(k_p).transpose(2, 0, 1, 3), jnp.broadcast_to(
        k_cols[None, None], (A_KV_HEADS, bp, s_p, A_HEAD_DIM))], axis=-1)
    oa_p = _attend_t(heads_p(pp["iq"]), per_p(pp["iw"]).transpose(0, 2, 1), heads_p(pp["aq"]),
                     _pad_rows(ki_p.astype(BF16), s_p), k_aug, vt, qpos, lim,
                     kb_size=kb_p, topk=min(TOPK_MAX, seq // 4), koff=N_META)

    n_all = past + dseq
    kb_s = SAMPLE_KEY_BLOCK
    s_s = -(-n_all // kb_s) * kb_s
    k_all = jnp.concatenate([cache_k[l].reshape(bs, past, A_KV).astype(BF16), per_s(ps["ak"]).astype(BF16)], axis=1)
    v_all = jnp.concatenate([cache_v[l].reshape(bs, past, A_KV).astype(BF16), per_s(ps["av"]).astype(BF16)], axis=1)
    ki_all = jnp.concatenate([cache_kidx[l].astype(BF16), per_s(ps["ik"]).astype(BF16)], axis=1)
    tokens_s = lambda a: a.transpose(1, 0, 2).reshape(bs, dseq, a.shape[0] * a.shape[-1])
    oa_s = _attend(tokens_s(ps["iq"]), per_s(ps["iw"]), tokens_s(ps["aq"][..., :A_HEAD_DIM]),
                   _pad_rows(ki_all, s_s), _pad_rows(k_all, s_s), _pad_rows(v_all, s_s),
                   past + jnp.arange(dseq, dtype=I32), jnp.full((dseq,), n_all, I32),
                   tq=dseq, kb_size=kb_s, topk=min(TOPK_MAX, n_all // 4), koff=0)

    zero_state = jnp.zeros((1, B_HEADS, B_KEY_DIM, B_VAL_DIM), F32)
    _, st_m = _hgrn2(pm["bq"][None], pm["bf"][None], pm["bi"][None], lb, zero_state)
    orec_p, st_p = _hgrn2(per_p(pp["bq"]), per_p(pp["bf"]), per_p(pp["bi"]), lb, st_m)
    orec_s, st_s = _hgrn2(per_s(ps["bq"]), per_s(ps["bf"]), per_s(ps["bi"]), lb,
                          state_hgrn[l].astype(F32))

    y_p = stage4(pp, oa_p, orec_p).reshape(bp, seq, D_MODEL)
    y_s = stage4(ps, oa_s, orec_s).reshape(bs, dseq, D_MODEL)

    kv5 = lambda a: a.reshape(1, a.shape[0], a.shape[1], A_KV_HEADS, A_HEAD_DIM)
    return (y_p, y_s, kv5(k_p), kv5(v_p), ki_p[None], st_p[None].astype(state_hgrn.dtype),
            kv5(per_s(ps["ak"])), kv5(per_s(ps["av"])), per_s(ps["ik"])[None],
            st_s[None].astype(state_hgrn.dtype))
```

```python
import functools

import jax
import jax.numpy as jnp
import numpy as np
from jax import lax
from jax.experimental import pallas as pl
from jax.experimental.pallas import tpu as pltpu

F32 = jnp.float32
BF16 = jnp.bfloat16
I32 = jnp.int32

D_MODEL = 1024
D_FF = 2816
FF_CHUNK = 256
N_FF_CHUNKS = D_FF // FF_CHUNK
CHUNK = 64
N_META = 16
A_HEADS = 8
A_KV_HEADS = 2
A_GROUP = A_HEADS // A_KV_HEADS
A_HEAD_DIM = 64
A_Q = A_HEADS * A_HEAD_DIM
A_KV = A_KV_HEADS * A_HEAD_DIM
IDX_HEADS = 8
IDX_DIM = 64
IDX_Q = IDX_HEADS * IDX_DIM
TOPK_MAX = 256
B_HEADS = 4
B_KEY_DIM = 128
B_VAL_DIM = 128
B_KW = B_HEADS * B_KEY_DIM
B_WIDTH = B_HEADS * B_VAL_DIM
DEPTH = 1
ALPHA = (2.0 * DEPTH) ** 0.25
LN_EPS = 1e-5
RMS_EPS = 1e-6
NEG_INF = -1e30
LOG2E = 1.4426950408889634
INT_MIN = -(2 ** 31)
LANES = 128
VMEM_LIMIT = 56 * 1024 * 1024


def _resident(shape):
    n = len(shape)
    return pl.BlockSpec(shape, lambda *_: (0,) * n, pipeline_mode=pl.Buffered(1))


def _layer_norm(x, g, b):
    mu = jnp.mean(x, axis=-1, keepdims=True)
    xc = x - mu
    var = jnp.mean(xc * xc, axis=-1, keepdims=True)
    return xc * lax.rsqrt(var + LN_EPS) * g + b


def _swiglu_step(x, wg_ref, wu_ref, wd_ref):
    xb = x.astype(BF16)

    acc = jnp.zeros(x.shape, F32)
    for c in range(N_FF_CHUNKS):
        cols = slice(c * FF_CHUNK, (c + 1) * FF_CHUNK)
        g = jnp.dot(xb, wg_ref[:, cols], preferred_element_type=F32)
        u = jnp.dot(xb, wu_ref[:, cols], preferred_element_type=F32)
        a = (g * jax.nn.sigmoid(g) * u).astype(BF16)
        acc = acc + jnp.dot(a, wd_ref[c], preferred_element_type=F32)
    return ALPHA * x + 0.5 * acc


_PROJ_OUT = (
    ("aq", A_Q, BF16), ("ak", A_KV, F32), ("av", A_KV, F32), ("iq", IDX_Q, BF16),
    ("ik", IDX_DIM, F32), ("iw", IDX_HEADS, F32),
    ("bq", B_KW, F32), ("bf", B_KW, F32), ("bi", B_WIDTH, F32),
)
_GATE_WIDTHS = (B_WIDTH, D_MODEL, D_MODEL)
_PROJ_SCALE = {"aq": (A_HEAD_DIM ** -0.5) * LOG2E, "iw": (IDX_HEADS ** -0.5) * (IDX_DIM ** -0.5)}
_HEAD_MAJOR = {"aq": A_HEADS, "iq": IDX_HEADS}


def _ffn_in_kernel(x_ref, g_ref, b_ref, wg_ref, wu_ref, wd_ref, qcols_ref, *rest):
    n = len(_PROJ_OUT)
    w_refs, h_ref, out_refs = rest[:n], rest[n], rest[n + 1:]
    x = x_ref[...]
    h = _layer_norm(_swiglu_step(x, wg_ref, wu_ref, wd_ref), g_ref[...], b_ref[...])
    h_ref[...] = h
    hb = h.astype(BF16)
    for (name, _, dt), w_ref, o_ref in zip(_PROJ_OUT, w_refs, out_refs):
        y = jnp.dot(hb, w_ref[...], preferred_element_type=F32)
        if name in _PROJ_SCALE:
            y = y * _PROJ_SCALE[name]
        if name in _HEAD_MAJOR:
            hd = y.shape[1] // _HEAD_MAJOR[name]
            for head in range(_HEAD_MAJOR[name]):
                yh = y[:, head * hd:(head + 1) * hd].astype(dt)
                if name == "aq":
                    yh = jnp.concatenate([yh, jnp.broadcast_to(qcols_ref[head], yh.shape)], axis=1)
                o_ref[head] = yh
        else:
            o_ref[...] = y.astype(dt)


def _ffn_in(x, ln_g, ln_b, wg, wu, wd, q_cols, w_pieces, tm):
    m = x.shape[0]
    assert m % tm == 0
    row = lambda w: pl.BlockSpec((tm, w), lambda i: (i, 0))
    in_specs = [row(D_MODEL), _resident(ln_g.shape), _resident(ln_b.shape),
                _resident(wg.shape), _resident(wu.shape), _resident(wd.shape),
                _resident(q_cols.shape)]
    in_specs += [_resident(w.shape) for w in w_pieces]
    out_shape = [jax.ShapeDtypeStruct((m, D_MODEL), F32)]
    out_specs = [row(D_MODEL)]
    for name, width, dt in _PROJ_OUT:
        if name in _HEAD_MAJOR:
            nh = _HEAD_MAJOR[name]
            hd = width // nh + (q_cols.shape[-1] if name == "aq" else 0)
            out_shape.append(jax.ShapeDtypeStruct((nh, m, hd), dt))
            out_specs.append(pl.BlockSpec((nh, tm, hd), lambda i: (0, i, 0)))
        else:
            out_shape.append(jax.ShapeDtypeStruct((m, width), dt))
            out_specs.append(row(width))
    outs = pl.pallas_call(
        _ffn_in_kernel,
        grid=(m // tm,),
        in_specs=in_specs,
        out_specs=out_specs,
        out_shape=out_shape,
        compiler_params=pltpu.CompilerParams(
            dimension_semantics=("arbitrary",), vmem_limit_bytes=VMEM_LIMIT),
        name="ffn_in",
    )(x, ln_g, ln_b, wg, wu, wd, q_cols, *w_pieces)
    res = {"h": outs[0]}
    for (name, _, _), o in zip(_PROJ_OUT, outs[1:]):
        res[name] = o
    return res


def _attend_kernel(nkb_ref, qpos_ref, lim_ref, iq_ref, wi_ref, q_ref, ki_ref, k_ref, v_ref,
                   o_ref, key_scr, t_scr, *, topk, koff, kb_size, slopes):
    tq = q_ref.shape[0]
    nkb = nkb_ref[pl.program_id(1)]
    lim = lim_ref[...]
    qpos = qpos_ref[...]
    sub = kb_size // LANES

    def lane_ids(s0):
        return s0 + lax.broadcasted_iota(I32, (tq, kb_size), 1)

    def score_block(kb, carry):
        s0 = pl.multiple_of(kb * kb_size, kb_size)
        kib = ki_ref[pl.ds(s0, kb_size), :]
        acc = jnp.zeros((tq, kb_size), F32)
        for h in range(IDX_HEADS):
            lg = lax.dot_general(iq_ref[:, h * IDX_DIM:(h + 1) * IDX_DIM], kib,
                                 (((1,), (1,)), ((), ())), preferred_element_type=F32)
            acc = acc + jnp.maximum(lg, 0.0) * wi_ref[:, h:h + 1]
        bits = lax.bitcast_convert_type(acc, I32)
        key = bits ^ ((bits >> 31) & 0x7FFFFFFF)
        key_scr[:, pl.ds(s0, kb_size)] = jnp.where(lane_ids(s0) < lim, key, INT_MIN)
        return carry

    lax.fori_loop(0, nkb, score_block, 0)

    def bcast(x):
        return jnp.broadcast_to(x, (tq, LANES))

    def count(pred):
        def body(kb, acc):
            for c in range(sub):
                off = pl.multiple_of(kb * kb_size + c * LANES, LANES)
                col = off + lax.broadcasted_iota(I32, (tq, LANES), 1)
                acc = acc + jnp.where(pred(key_scr[:, pl.ds(off, LANES)], col), 1.0, 0.0)
            return acc
        acc = lax.fori_loop(0, nkb, body, jnp.zeros((tq, LANES), F32))
        return jnp.sum(acc, axis=1, keepdims=True)

    def bit_step(state):
        i, t, settled, _ = state
        stop = jnp.min(settled)
        cand = t + lax.shift_left(jnp.int32(1), 31 - i)
        cand_b = bcast(cand)
        cnt = count(lambda key, col: key >= cand_b)
        take = (cnt >= topk) & (settled == 0.0)
        settled = jnp.where(take & (cnt == topk), 1.0, settled)
        return i + 1, jnp.where(take, cand, t), settled, stop

    _, t, _, _ = lax.while_loop(
        lambda state: (state[0] < 32) & (state[3] == 0.0), bit_step,
        (jnp.int32(0), jnp.full((tq, 1), INT_MIN, I32), (lim <= topk).astype(F32), jnp.float32(0.0)))
    thr = jnp.maximum(t, INT_MIN + 1)
    thr_b = bcast(thr)
    n_ge = count(lambda key, col: key >= thr_b)

    @pl.when(jnp.max(n_ge) > topk)
    def _():
        n_gt = count(lambda key, col: key > thr_b)
        need = topk - n_gt
        n_bits = (key_scr.shape[1] - 1).bit_length()

        def idx_step(i, m0):
            cand = m0 + lax.shift_left(jnp.int32(1), n_bits - 1 - i)
            cand_b = bcast(cand)
            g = count(lambda key, col: (key == thr_b) & (col < cand_b))
            return jnp.where(g < need, cand, m0)

        m0 = lax.fori_loop(0, n_bits, idx_step, jnp.zeros((tq, 1), I32))
        drop_row = n_ge > topk

        def demote(kb, carry):
            s0 = pl.multiple_of(kb * kb_size, kb_size)
            key = key_scr[:, pl.ds(s0, kb_size)]
            drop = drop_row & (key == thr) & (lane_ids(s0) > m0)
            key_scr[:, pl.ds(s0, kb_size)] = jnp.where(drop, INT_MIN, key)
            return carry

        lax.fori_loop(0, nkb, demote, 0)

    def attend_block(kb, carry):
        ms, ls, accs = carry
        s0 = pl.multiple_of(kb * kb_size, kb_size)
        col = lane_ids(s0)
        bias = jnp.where(key_scr[:, pl.ds(s0, kb_size)] >= thr, 0.0, NEG_INF)
        dist = jnp.abs(qpos - (col - koff)).astype(F32)
        kblk = k_ref[pl.ds(s0, kb_size), :]
        vblk = v_ref[pl.ds(s0, kb_size), :]
        new_m, new_l, new_acc = [], [], []
        for h in range(A_HEADS):
            g = h // A_GROUP
            s = lax.dot_general(q_ref[:, h * A_HEAD_DIM:(h + 1) * A_HEAD_DIM],
                                kblk[:, g * A_HEAD_DIM:(g + 1) * A_HEAD_DIM],
                                (((1,), (1,)), ((), ())), preferred_element_type=F32)
            s = (s - slopes[h] * dist) + bias
            t_scr[h] = s
            new_m.append(jnp.maximum(ms[h], jnp.max(s, axis=1, keepdims=True)))
        for h in range(A_HEADS):
            g = h // A_GROUP
            p = jnp.exp2(t_scr[h] - new_m[h])
            alpha = jnp.exp2(ms[h] - new_m[h])
            new_l.append(alpha * ls[h] + jnp.sum(p, axis=1, keepdims=True))
            pv = jnp.dot(p.astype(BF16), vblk[:, g * A_HEAD_DIM:(g + 1) * A_HEAD_DIM],
                         preferred_element_type=F32)
            new_acc.append(alpha * accs[h] + pv)
        return tuple(new_m), tuple(new_l), tuple(new_acc)

    init = (tuple(jnp.full((tq, 1), NEG_INF, F32) for _ in range(A_HEADS)),
            tuple(jnp.zeros((tq, 1), F32) for _ in range(A_HEADS)),
            tuple(jnp.zeros((tq, A_HEAD_DIM), F32) for _ in range(A_HEADS)))
    _, ls, accs = lax.fori_loop(0, nkb, attend_block, init)
    for h in range(A_HEADS):
        o_ref[:, h * A_HEAD_DIM:(h + 1) * A_HEAD_DIM] = (accs[h] / ls[h]).astype(o_ref.dtype)


def _attend(iq, wi, q, ki, k, v, qpos, lim, *, tq, kb_size, topk, koff):
    bsz, nq, _ = q.shape
    s_pad = k.shape[1]
    assert nq % tq == 0 and s_pad % kb_size == 0
    nt = nq // tq
    nkb = (jnp.max(lim.reshape(nt, tq), axis=1) + kb_size - 1) // kb_size
    slopes = _alibi_slopes_log2()
    qrow = lambda w: pl.BlockSpec((None, tq, w), lambda b, j, n: (b, j, 0))
    krow = lambda w: pl.BlockSpec((None, s_pad, w), lambda b, j, n: (b, 0, 0))
    pos = pl.BlockSpec((tq, 1), lambda b, j, n: (j, 0))
    return pl.pallas_call(
        functools.partial(_attend_kernel, topk=topk, koff=koff, kb_size=kb_size, slopes=slopes),
        grid_spec=pltpu.PrefetchScalarGridSpec(
            num_scalar_prefetch=1,
            grid=(bsz, nt),
            in_specs=[pos, pos, qrow(IDX_Q), qrow(IDX_HEADS), qrow(A_Q),
                      krow(IDX_DIM), krow(A_KV), krow(A_KV)],
            out_specs=qrow(A_Q),
            scratch_shapes=[pltpu.VMEM((tq, s_pad), I32),
                            pltpu.VMEM((A_HEADS, tq, kb_size), F32)],
        ),
        out_shape=jax.ShapeDtypeStruct((bsz, nq, A_Q), BF16),
        compiler_params=pltpu.CompilerParams(
            dimension_semantics=("arbitrary", "arbitrary"), vmem_limit_bytes=VMEM_LIMIT),
        name="attend",
    )(nkb.astype(I32), qpos.reshape(nq, 1), lim.reshape(nq, 1), iq, wi, q, ki, k, v)


def _fold_rows(x, op, rows=8):
    while x.shape[0] > rows:
        half = x.shape[0] // 2
        x = op(x[:half], x[half:])
    return x


VT_ROWS = A_HEAD_DIM + 16
PROMPT_KEY_BLOCK = 512
KEY_SCAN_STEP = 256
SAMPLE_KEY_BLOCK = 384


def _alibi_slopes_log2():
    return tuple(float(2.0 ** (-8.0 * (i + 1) / A_HEADS)) * LOG2E for i in range(A_HEADS))


ALIBI_SPLIT = CHUNK


def _alibi_columns(n_keys):
    q_cols = np.zeros((A_HEADS, A_HEAD_DIM), np.float32)
    for h, c in enumerate(_alibi_slopes_log2()):
        for i in range(3):
            ci = float(np.asarray(c, dtype=BF16).astype(np.float32))
            q_cols[h, 2 * i], q_cols[h, 2 * i + 1] = ALIBI_SPLIT * ci, ci
            c -= ci
    ids = np.arange(n_keys)
    k_cols = np.zeros((n_keys, A_HEAD_DIM), np.float32)
    k_cols[:, 0:6:2] = (ids // ALIBI_SPLIT)[:, None]
    k_cols[:, 1:6:2] = (ids % ALIBI_SPLIT)[:, None]
    return jnp.asarray(q_cols, BF16), jnp.asarray(k_cols, BF16)


def _attend_t_kernel(qpos_ref, lim_ref, iq_ref, wi_ref, q_ref, ki_ref, k_ref, vt_ref,
                     o_ref, key_scr, tie_scr, t_scr, *, n_keys, first_qpos, topk, koff, kb_size,
                     slopes):
    tq = o_ref.shape[0]
    lim = lim_ref[...]
    qpos = qpos_ref[...]
    n_pairs = A_HEADS // 2
    blocks = [(s0, min(kb_size, n_keys - s0), pl.ds(s0, min(kb_size, n_keys - s0)))
              for s0 in range(0, n_keys, kb_size)]

    def row_ids(s0, size):
        return s0 + lax.broadcasted_iota(I32, (size, tq), 0)

    def head_pair(ref, pr):
        return ref[2 * pr:2 * pr + 2].reshape(2 * tq, ref.shape[-1])

    for s0, size, rows in blocks:
        kib = ki_ref[rows, :]
        acc = jnp.zeros((size, tq), F32)
        for pr in range(IDX_HEADS // 2):
            lg = lax.dot_general(kib, head_pair(iq_ref, pr), (((1,), (1,)), ((), ())),
                                 preferred_element_type=F32)
            for u in range(2):
                h = 2 * pr + u
                acc = acc + jnp.maximum(lg[:, u * tq:(u + 1) * tq], 0.0) * wi_ref[h:h + 1, :]
        bits = lax.bitcast_convert_type(acc, I32)
        key = bits ^ ((bits >> 31) & 0x7FFFFFFF)
        key = jnp.where(row_ids(s0, size) < lim, key, INT_MIN)
        key_scr[rows, :] = key

    def count(plane, pred):
        acc = jnp.zeros((8, tq), F32)
        for _, _, rows in blocks:
            acc = acc + _fold_rows(jnp.where(pred(plane[rows, :]), 1.0, 0.0), jnp.add)
        return jnp.sum(acc, axis=0, keepdims=True)

    def kth_largest(plane, kth, bits):
        def bit_step(i, t):
            cand = t + lax.shift_left(jnp.int32(1), bits - 1 - i)
            return jnp.where(count(plane, lambda key: key >= cand) >= kth, cand, t)
        low = INT_MIN if bits == 32 else -(1 << (bits - 1))
        return lax.fori_loop(0, bits, bit_step, jnp.full((1, tq), low, I32))

    t = kth_largest(key_scr, topk, 32)
    thr = jnp.maximum(t, INT_MIN + 1)
    n_ge = count(key_scr, lambda key: key >= thr)

    @pl.when(jnp.max(n_ge) > topk)
    def _():
        need = topk - count(key_scr, lambda key: key > thr)
        for s0, size, rows in blocks:
            tie_scr[rows, :] = jnp.where(key_scr[rows, :] == thr, -1 - row_ids(s0, size), INT_MIN)
        last = kth_largest(tie_scr, need, key_scr.shape[0].bit_length() + 1)
        drop_q = n_ge > topk
        for s0, size, rows in blocks:
            key = key_scr[rows, :]
            drop = drop_q & (key == thr) & (-1 - row_ids(s0, size) < last)
            key_scr[rows, :] = jnp.where(drop, INT_MIN, key)

    ms = [jnp.full((1, tq), NEG_INF, F32) for _ in range(A_HEADS)]
    accs = [jnp.zeros((vt_ref.shape[1], A_GROUP * tq), F32) for _ in range(A_KV_HEADS)]
    for s0, size, rows in blocks:
        bias = jnp.where(key_scr[rows, :] >= thr, 0.0, NEG_INF)
        may_follow = s0 + size - 1 - koff > first_qpos
        if may_follow:
            ahead = jnp.minimum(qpos - (row_ids(s0, size) - koff), 0).astype(F32)
        new_m = []
        for pr in range(n_pairs):
            g = (2 * pr) // A_GROUP
            s2 = lax.dot_general(k_ref[g, rows, :], head_pair(q_ref, pr),
                                 (((1,), (1,)), ((), ())), preferred_element_type=F32)
            for u in range(2):
                h = 2 * pr + u
                s = s2[:, u * tq:(u + 1) * tq]
                if may_follow:
                    s = s + (2.0 * slopes[h]) * ahead
                s = s + bias
                t_scr[h, :size] = s
                blk_max = jnp.max(_fold_rows(s, jnp.maximum), axis=0, keepdims=True)
                new_m.append(jnp.maximum(ms[h], blk_max))
        for g in range(A_KV_HEADS):
            ps, alphas = [], []
            for h in range(g * A_GROUP, (g + 1) * A_GROUP):
                ps.append(jnp.exp2(t_scr[h, :size] - new_m[h]).astype(BF16))
                alphas.append(jnp.exp2(ms[h] - new_m[h]))
            pv = jnp.dot(vt_ref[g, :, rows], jnp.concatenate(ps, axis=1),
                         preferred_element_type=F32)
            accs[g] = jnp.concatenate(alphas, axis=1) * accs[g] + pv
        ms = new_m
    for pr in range(n_pairs):
        rows = []
        for h in (2 * pr, 2 * pr + 1):
            g, hh = divmod(h, A_GROUP)
            cols = slice(hh * tq, (hh + 1) * tq)
            rows.append(accs[g][:A_HEAD_DIM, cols] / accs[g][A_HEAD_DIM:A_HEAD_DIM + 1, cols])
        o_ref[:, 2 * pr * A_HEAD_DIM:(2 * pr + 2) * A_HEAD_DIM] = (
            jnp.concatenate(rows, axis=0).T.astype(o_ref.dtype))


def _attend_t(iq, wi_t, q, ki, k, vt, qpos, lim, *, kb_size, topk, koff):
    tq = LANES
    _, bsz, nq, _ = q.shape
    s_pad = ki.shape[1]
    assert nq % tq == 0 and s_pad % kb_size == 0
    keys_of_tile = -(-lim.reshape(nq // tq, tq).max(axis=1) // KEY_SCAN_STEP) * KEY_SCAN_STEP
    assert keys_of_tile.max() <= s_pad
    runs, j0 = [], 0
    for j in range(1, len(keys_of_tile) + 1):
        if j == len(keys_of_tile) or keys_of_tile[j] != keys_of_tile[j0]:
            runs.append((j0, j, int(keys_of_tile[j0])))
            j0 = j
    qpos_row = jnp.asarray(qpos, I32).reshape(1, nq)
    lim_row = jnp.asarray(lim, I32).reshape(1, nq)
    outs = []
    for j0, j1, n_keys in runs:
        heads = lambda a: pl.BlockSpec((a.shape[0], None, tq, a.shape[-1]),
                                       lambda b, j: (0, b, j + j0, 0))
        pos = pl.BlockSpec((1, tq), lambda b, j: (0, j + j0))
        outs.append(pl.pallas_call(
            functools.partial(_attend_t_kernel, n_keys=n_keys, first_qpos=int(qpos[j0 * tq]), topk=topk,
                              koff=koff, kb_size=kb_size, slopes=_alibi_slopes_log2()),
            grid=(bsz, j1 - j0),
            in_specs=[pos, pos, heads(iq),
                      pl.BlockSpec((None, IDX_HEADS, tq), lambda b, j: (b, 0, j + j0)),
                      heads(q),
                      pl.BlockSpec((None, s_pad, IDX_DIM), lambda b, j: (b, 0, 0)),
                      pl.BlockSpec((A_KV_HEADS, None, s_pad, k.shape[-1]), lambda b, j: (0, b, 0, 0)),
                      pl.BlockSpec((A_KV_HEADS, None, VT_ROWS, s_pad), lambda b, j: (0, b, 0, 0))],
            out_specs=pl.BlockSpec((None, tq, A_Q), lambda b, j: (b, j, 0)),
            scratch_shapes=[pltpu.VMEM((s_pad, tq), I32), pltpu.VMEM((s_pad, tq), I32),
                            pltpu.VMEM((A_HEADS, kb_size, tq), F32)],
            out_shape=jax.ShapeDtypeStruct((bsz, (j1 - j0) * tq, A_Q), BF16),
            compiler_params=pltpu.CompilerParams(
                dimension_semantics=("arbitrary", "arbitrary"), vmem_limit_bytes=VMEM_LIMIT),
            name=f"attend_t{n_keys}",
        )(qpos_row, lim_row, iq, wi_t, q, ki, k, vt))
    return jnp.concatenate(outs, axis=1)


HGRN_CHUNK = 128


def _hgrn2_tables(ct):
    r = np.arange(ct)
    seg = [r[:, None] >= r[None, :], r[None, :] > r[:, None]]
    pair = []
    g = ct
    while g >= 2:
        mid = (r // g) * g + g // 2
        second = r >= mid
        seg.append(np.where(second[:, None],
                            (r[None, :] >= mid[:, None]) & (r[None, :] <= r[:, None]),
                            (r[None, :] > r[:, None]) & (r[None, :] < mid[:, None])))
        pair.append((r[:, None] // g == r[None, :] // g) & second[:, None] & ~second[None, :])
        g //= 2
    return np.concatenate(seg, 0).astype(np.float32), np.stack(pair).astype(np.float32)


def _hgrn2_kernel(bq_ref, bf_ref, bi_ref, lb_ref, seg_ref, pair_ref, s0_ref, o_ref, sout_ref, st_scr):
    c = pl.program_id(1)
    ct = bq_ref.shape[0]
    n_lvl = pair_ref.shape[0]

    @pl.when(c == 0)
    def _():
        for h in range(B_HEADS):
            st_scr[h] = s0_ref[h].T

    lb = lb_ref[...]
    fx = bf_ref[...]
    logf = jnp.log(lb + (1.0 - lb) * jax.nn.sigmoid(fx))
    kk = (1.0 - lb) * jax.nn.sigmoid(-fx)
    bq = bq_ref[...]
    rq = bq * jax.nn.sigmoid(bq)
    rv = bi_ref[...]
    p1 = logf.astype(BF16)
    p2 = (logf - p1.astype(F32)).astype(BF16)
    seg = seg_ref[...]
    sums = jnp.dot(seg, p1, preferred_element_type=F32) + jnp.dot(seg, p2, preferred_element_type=F32)
    b_in = sums[:ct]
    b_out = sums[ct:2 * ct]
    q_in = (rq * jnp.exp(b_in)).astype(BF16)
    k_out = (kk * jnp.exp(b_out)).astype(BF16)
    rvb = rv.astype(BF16)
    row = lax.broadcasted_iota(I32, (ct, B_KEY_DIM), 0)
    nt = (((1,), (1,)), ((), ()))

    a = [jnp.zeros((ct, ct), F32) for _ in range(B_HEADS)]
    for lvl in range(n_lvl):
        half = ct >> (lvl + 1)
        second = (row & half) != 0
        dec = jnp.exp(sums[(2 + lvl) * ct:(3 + lvl) * ct])
        for h in range(B_HEADS):
            ls = slice(h * B_KEY_DIM, (h + 1) * B_KEY_DIM)
            x = (jnp.where(second, rq[:, ls], kk[:, ls]) * dec[:, ls]).astype(BF16)
            a[h] = a[h] + pair_ref[lvl] * lax.dot_general(x, x, nt, preferred_element_type=F32)

    for h in range(B_HEADS):
        ls = slice(h * B_KEY_DIM, (h + 1) * B_KEY_DIM)
        st = st_scr[h]
        own = jnp.sum(rq[:, ls] * kk[:, ls], axis=1, keepdims=True)
        o_ref[:, ls] = (jnp.dot(a[h].astype(BF16), rvb[:, ls], preferred_element_type=F32)
                        + own * rv[:, ls]
                        + lax.dot_general(q_in[:, ls], st.astype(BF16), nt, preferred_element_type=F32))
        upd = lax.dot_general(rvb[:, ls], k_out[:, ls], (((0,), (0,)), ((), ())),
                              preferred_element_type=F32)
        st_scr[h] = st * jnp.exp(b_in[ct - 1:ct, ls]) + upd

    @pl.when(c == pl.num_programs(1) - 1)
    def _():
        for h in range(B_HEADS):
            sout_ref[h] = st_scr[h].T


def _hgrn2(bq, bf, bi, lb, s0):
    bsz, t, _ = bq.shape
    ct = HGRN_CHUNK
    tp = -(-t // ct) * ct
    if tp != t:
        pad = lambda a, v: jnp.pad(a, ((0, 0), (0, tp - t), (0, 0)), constant_values=v)
        bq, bf, bi = pad(bq, 0.0), pad(bf, 1e4), pad(bi, 0.0)
    seg, pair = _hgrn2_tables(ct)
    seg = jnp.asarray(seg, BF16)
    pair = jnp.asarray(pair, F32)
    tok = pl.BlockSpec((None, ct, B_KW), lambda b, c: (b, c, 0))
    s_map = (lambda b, c: (0, 0, 0, 0)) if s0.shape[0] == 1 else (lambda b, c: (b, 0, 0, 0))
    s_in = pl.BlockSpec((None, B_HEADS, B_KEY_DIM, B_VAL_DIM), s_map)
    s_out = pl.BlockSpec((None, B_HEADS, B_KEY_DIM, B_VAL_DIM), lambda b, c: (b, 0, 0, 0))
    o, s = pl.pallas_call(
        _hgrn2_kernel,
        grid=(bsz, tp // ct),
        in_specs=[tok, tok, tok, _resident(lb.shape), _resident(seg.shape), _resident(pair.shape), s_in],
        out_specs=[tok, s_out],
        out_shape=[jax.ShapeDtypeStruct((bsz, tp, B_WIDTH), F32),
                   jax.ShapeDtypeStruct((bsz, B_HEADS, B_KEY_DIM, B_VAL_DIM), F32)],
        scratch_shapes=[pltpu.VMEM((B_HEADS, B_VAL_DIM, B_KEY_DIM), F32)],
        compiler_params=pltpu.CompilerParams(
            dimension_semantics=("arbitrary", "arbitrary"), vmem_limit_bytes=VMEM_LIMIT),
        name="hgrn2",
    )(bq, bf, bi, lb, seg, pair, s0)
    return o[:, :t], s


def _mix_out_kernel(h_ref, oa_ref, orec_ref, wgate_ref, gn_ref, wa_ref, wb_ref, wo_ref,
                    g2_ref, b2_ref, wg_ref, wu_ref, wd_ref, g3_ref, b3_ref, y_ref):
    h = h_ref[...]
    gates = jnp.dot(h.astype(BF16), wgate_ref[...], preferred_element_type=F32)
    bg = gates[:, :B_WIDTH]
    ga = gates[:, B_WIDTH:B_WIDTH + D_MODEL]
    gb = gates[:, B_WIDTH + D_MODEL:]
    orec = orec_ref[...]
    gn = gn_ref[...]
    parts = []
    for head in range(B_HEADS):
        ls = slice(head * B_VAL_DIM, (head + 1) * B_VAL_DIM)
        oh = orec[:, ls]
        ms = jnp.mean(oh * oh, axis=-1, keepdims=True)
        parts.append(oh * lax.rsqrt(ms + RMS_EPS) * gn[:, ls])
    on = jnp.concatenate(parts, axis=-1)
    ob = on * (bg * jax.nn.sigmoid(bg))
    pa = jnp.dot(oa_ref[...], wa_ref[...], preferred_element_type=F32)
    pb = jnp.dot(ob.astype(BF16), wb_ref[...], preferred_element_type=F32)
    mixed = jax.nn.sigmoid(ga) * pa + jax.nn.sigmoid(gb) * pb
    mixed = jnp.dot(mixed.astype(BF16), wo_ref[...], preferred_element_type=F32)
    h2 = _layer_norm(ALPHA * h + mixed, g2_ref[...], b2_ref[...])
    y_ref[...] = _layer_norm(_swiglu_step(h2, wg_ref, wu_ref, wd_ref), g3_ref[...], b3_ref[...])


def _mix_out(h, oa, orec, wgate, gn, wa, wb, wo, g2, b2, wg, wu, wd, g3, b3, tm):
    m = h.shape[0]
    assert m % tm == 0
    row = lambda w: pl.BlockSpec((tm, w), lambda i: (i, 0))
    weights = (wgate, gn, wa, wb, wo, g2, b2, wg, wu, wd, g3, b3)
    return pl.pallas_call(
        _mix_out_kernel,
        grid=(m // tm,),
        in_specs=[row(D_MODEL), row(A_Q), row(B_WIDTH)] + [_resident(w.shape) for w in weights],
        out_specs=row(D_MODEL),
        out_shape=jax.ShapeDtypeStruct((m, D_MODEL), F32),
        compiler_params=pltpu.CompilerParams(
            dimension_semantics=("arbitrary",), vmem_limit_bytes=VMEM_LIMIT),
        name="mix_out",
    )(h, oa, orec, *weights)


def _ffn_weights(wg, wu, wd):
    return wg.astype(BF16), wu.astype(BF16), wd.astype(BF16).reshape(N_FF_CHUNKS, FF_CHUNK, D_MODEL)


def _pad_rows(a, n):
    return jnp.pad(a, ((0, 0), (0, n - a.shape[1]), (0, 0)))


FFN_IN_TILE = 512
MIX_OUT_TILE = 512


def _token_tile(m, largest):
    tm = largest
    while tm > 8 and m % tm:
        tm //= 2
    assert m % tm == 0, m
    return tm


def kernel(x_prompt, x_sample, cache_k, cache_v, cache_kidx, state_hgrn, meta, ln1_g, ln1_b, ffn1_wg, ffn1_wu, ffn1_wd, w_in, lb_param, g_norm, w_a_proj, w_b_proj, w_out, ln2_g, ln2_b, ffn2_wg, ffn2_wu, ffn2_wd, ln3_g, ln3_b):
    assert ln1_g.shape[0] == DEPTH
    bp, seq, _ = x_prompt.shape
    bs, dseq, _ = x_sample.shape
    past = cache_k.shape[2]
    l = 0
    vec = lambda a: a.reshape(1, -1).astype(F32)

    lb_all = jnp.cumsum(jax.nn.softmax(lb_param.astype(F32), axis=0), axis=0)
    lb = lb_all[l].reshape(1, B_KW)
    f1 = _ffn_weights(ffn1_wg[l], ffn1_wu[l], ffn1_wd[l])
    f2 = _ffn_weights(ffn2_wg[l], ffn2_wu[l], ffn2_wd[l])
    w_pieces, off = [], 0
    for _, width, _ in _PROJ_OUT:
        w_pieces.append(w_in[l][:, off:off + width].astype(BF16))
        off += width
    w_gate = w_in[l][:, off:].astype(BF16)
    assert w_gate.shape[1] == sum(_GATE_WIDTHS)

    kb_p = PROMPT_KEY_BLOCK
    s_p = -(-(N_META + seq) // kb_p) * kb_p
    q_cols, k_cols = _alibi_columns(s_p)

    def stage1(x):
        return _ffn_in(x, vec(ln1_g[l]), vec(ln1_b[l]), *f1, q_cols[:, None, :], w_pieces,
                       _token_tile(x.shape[0], FFN_IN_TILE))

    def stage4(p, oa, orec):
        m = p["h"].shape[0]
        return _mix_out(p["h"], oa.reshape(m, A_Q), orec.reshape(m, B_WIDTH), w_gate,
                        vec(g_norm[l]), w_a_proj[l].astype(BF16), w_b_proj[l].astype(BF16),
                        w_out[l].astype(BF16), vec(ln2_g[l]), vec(ln2_b[l]), *f2,
                        vec(ln3_g[l]), vec(ln3_b[l]), _token_tile(m, MIX_OUT_TILE))

    pm = stage1(meta.astype(F32))
    pp = stage1(x_prompt.reshape(bp * seq, D_MODEL))
    ps = stage1(x_sample.reshape(bs * dseq, D_MODEL))
    per_p = lambda a: a.reshape(bp, seq, a.shape[-1])
    per_s = lambda a: a.reshape(bs, dseq, a.shape[-1])

    def with_meta(name):
        rows = jnp.broadcast_to(pm[name][None], (bp, N_META, pm[name].shape[-1]))
        return jnp.concatenate([rows, per_p(pp[name])], axis=1)

    k_p, v_p, ki_p = with_meta("ak"), with_meta("av"), with_meta("ik")

    qpos = np.arange(seq, dtype=np.int32)
    lim = N_META + CHUNK * (qpos // CHUNK + 1)
    heads_p = lambda a: a.reshape(a.shape[0], bp, seq, a.shape[-1])

    def key_rows(name):
        width = pm[name].shape[-1]
        return jnp.concatenate(
            [jnp.broadcast_to(pm[name].astype(BF16)[None], (bp, N_META, width)),
             per_p(pp[name]).astype(BF16), jnp.zeros((bp, s_p - N_META - seq, width), BF16)], axis=1)

    kv_heads = lambda name: key_rows(name).reshape(bp, s_p, A_KV_HEADS, A_HEAD_DIM)
    vt = kv_heads("av").transpose(2, 0, 3, 1)
    vt = jnp.concatenate([vt, jnp.ones_like(vt[:, :, :1]),
                          jnp.zeros_like(vt[:, :, :VT_ROWS - A_HEAD_DIM - 1])], axis=2)
    k_aug = jnp.concatenate([kv_heads("ak").transpose(2, 0, 1, 3), jnp.broadcast_to(
        k_cols[None, None], (A_KV_HEADS, bp, s_p, A_HEAD_DIM))], axis=-1)
    oa_p = _attend_t(heads_p(pp["iq"]), per_p(pp["iw"]).transpose(0, 2, 1), heads_p(pp["aq"]),
                     key_rows("ik"), k_aug, vt, qpos, lim,
                     kb_size=kb_p, topk=min(TOPK_MAX, seq // 4), koff=N_META)

    n_all = past + dseq
    kb_s = SAMPLE_KEY_BLOCK
    s_s = -(-n_all // kb_s) * kb_s
    k_all = jnp.concatenate([cache_k[l].reshape(bs, past, A_KV).astype(BF16), per_s(ps["ak"]).astype(BF16)], axis=1)
    v_all = jnp.concatenate([cache_v[l].reshape(bs, past, A_KV).astype(BF16), per_s(ps["av"]).astype(BF16)], axis=1)
    ki_all = jnp.concatenate([cache_kidx[l].astype(BF16), per_s(ps["ik"]).astype(BF16)], axis=1)
    tokens_s = lambda a: a.transpose(1, 0, 2).reshape(bs, dseq, a.shape[0] * a.shape[-1])
    oa_s = _attend(tokens_s(ps["iq"]), per_s(ps["iw"]), tokens_s(ps["aq"][..., :A_HEAD_DIM]),
                   _pad_rows(ki_all, s_s), _pad_rows(k_all, s_s), _pad_rows(v_all, s_s),
                   past + jnp.arange(dseq, dtype=I32), jnp.full((dseq,), n_all, I32),
                   tq=dseq, kb_size=kb_s, topk=min(TOPK_MAX, n_all // 4), koff=0)

    zero_state = jnp.zeros((1, B_HEADS, B_KEY_DIM, B_VAL_DIM), F32)
    _, st_m = _hgrn2(pm["bq"][None], pm["bf"][None], pm["bi"][None], lb, zero_state)
    orec_p, st_p = _hgrn2(per_p(pp["bq"]), per_p(pp["bf"]), per_p(pp["bi"]), lb, st_m)
    orec_s, st_s = _hgrn2(per_s(ps["bq"]), per_s(ps["bf"]), per_s(ps["bi"]), lb,
                          state_hgrn[l].astype(F32))

    y_p = stage4(pp, oa_p, orec_p).reshape(bp, seq, D_MODEL)
    y_s = stage4(ps, oa_s, orec_s).reshape(bs, dseq, D_MODEL)

    kv5 = lambda a: a.reshape(1, a.shape[0], a.shape[1], A_KV_HEADS, A_HEAD_DIM)
    return (y_p, y_s, kv5(k_p), kv5(v_p), ki_p[None], st_p[None].astype(state_hgrn.dtype),
            kv5(per_s(ps["ak"])), kv5(per_s(ps["av"])), per_s(ps["ik"])[None],
            st_s[None].astype(state_hgrn.dtype))
```

```python
import functools

import jax
import jax.numpy as jnp
import numpy as np
from jax import lax
from jax.experimental import pallas as pl
from jax.experimental.pallas import tpu as pltpu

F32 = jnp.float32
BF16 = jnp.bfloat16
I32 = jnp.int32

D_MODEL = 1024
D_FF = 2816
FF_CHUNK = 256
N_FF_CHUNKS = D_FF // FF_CHUNK
CHUNK = 64
N_META = 16
A_HEADS = 8
A_KV_HEADS = 2
A_GROUP = A_HEADS // A_KV_HEADS
A_HEAD_DIM = 64
A_Q = A_HEADS * A_HEAD_DIM
A_KV = A_KV_HEADS * A_HEAD_DIM
IDX_HEADS = 8
IDX_DIM = 64
IDX_Q = IDX_HEADS * IDX_DIM
TOPK_MAX = 256
B_HEADS = 4
B_KEY_DIM = 128
B_VAL_DIM = 128
B_KW = B_HEADS * B_KEY_DIM
B_WIDTH = B_HEADS * B_VAL_DIM
DEPTH = 1
ALPHA = (2.0 * DEPTH) ** 0.25
LN_EPS = 1e-5
RMS_EPS = 1e-6
NEG_INF = -1e30
LOG2E = 1.4426950408889634
INT_MIN = -(2 ** 31)
LANES = 128
VMEM_LIMIT = 56 * 1024 * 1024


def _resident(shape):
    n = len(shape)
    return pl.BlockSpec(shape, lambda *_: (0,) * n, pipeline_mode=pl.Buffered(1))


def _layer_norm(x, g, b):
    mu = jnp.mean(x, axis=-1, keepdims=True)
    xc = x - mu
    var = jnp.mean(xc * xc, axis=-1, keepdims=True)
    return xc * lax.rsqrt(var + LN_EPS) * g + b


def _swiglu_step(x, wg_ref, wu_ref, wd_ref):
    xb = x.astype(BF16)

    acc = jnp.zeros(x.shape, F32)
    for c in range(N_FF_CHUNKS):
        cols = slice(c * FF_CHUNK, (c + 1) * FF_CHUNK)
        g = jnp.dot(xb, wg_ref[:, cols], preferred_element_type=F32)
        u = jnp.dot(xb, wu_ref[:, cols], preferred_element_type=F32)
        a = (g * jax.nn.sigmoid(g) * u).astype(BF16)
        acc = acc + jnp.dot(a, wd_ref[c], preferred_element_type=F32)
    return ALPHA * x + 0.5 * acc


_PROJ_OUT = (
    ("aq", A_Q, BF16), ("ak", A_KV, F32), ("av", A_KV, F32), ("iq", IDX_Q, BF16),
    ("ik", IDX_DIM, F32), ("iw", IDX_HEADS, F32),
    ("bq", B_KW, F32), ("bf", B_KW, F32), ("bi", B_WIDTH, F32),
)
_GATE_WIDTHS = (B_WIDTH, D_MODEL, D_MODEL)
_PROJ_SCALE = {"aq": (A_HEAD_DIM ** -0.5) * LOG2E, "iw": (IDX_HEADS ** -0.5) * (IDX_DIM ** -0.5)}
_HEAD_MAJOR = {"aq": A_HEADS, "iq": IDX_HEADS}


def _ffn_in_kernel(x_ref, g_ref, b_ref, wg_ref, wu_ref, wd_ref, qcols_ref, *rest):
    n = len(_PROJ_OUT)
    w_refs, h_ref, out_refs = rest[:n], rest[n], rest[n + 1:]
    x = x_ref[...]
    h = _layer_norm(_swiglu_step(x, wg_ref, wu_ref, wd_ref), g_ref[...], b_ref[...])
    h_ref[...] = h
    hb = h.astype(BF16)
    for (name, _, dt), w_ref, o_ref in zip(_PROJ_OUT, w_refs, out_refs):
        y = jnp.dot(hb, w_ref[...], preferred_element_type=F32)
        if name in _PROJ_SCALE:
            y = y * _PROJ_SCALE[name]
        if name in _HEAD_MAJOR:
            hd = y.shape[1] // _HEAD_MAJOR[name]
            for head in range(_HEAD_MAJOR[name]):
                yh = y[:, head * hd:(head + 1) * hd].astype(dt)
                if name == "aq":
                    yh = jnp.concatenate([yh, jnp.broadcast_to(qcols_ref[head], yh.shape)], axis=1)
                o_ref[head] = yh
        else:
            o_ref[...] = y.astype(dt)


def _ffn_in(x, ln_g, ln_b, wg, wu, wd, q_cols, w_pieces, tm):
    m = x.shape[0]
    assert m % tm == 0
    row = lambda w: pl.BlockSpec((tm, w), lambda i: (i, 0))
    in_specs = [row(D_MODEL), _resident(ln_g.shape), _resident(ln_b.shape),
                _resident(wg.shape), _resident(wu.shape), _resident(wd.shape),
                _resident(q_cols.shape)]
    in_specs += [_resident(w.shape) for w in w_pieces]
    out_shape = [jax.ShapeDtypeStruct((m, D_MODEL), F32)]
    out_specs = [row(D_MODEL)]
    for name, width, dt in _PROJ_OUT:
        if name in _HEAD_MAJOR:
            nh = _HEAD_MAJOR[name]
            hd = width // nh + (q_cols.shape[-1] if name == "aq" else 0)
            out_shape.append(jax.ShapeDtypeStruct((nh, m, hd), dt))
            out_specs.append(pl.BlockSpec((nh, tm, hd), lambda i: (0, i, 0)))
        else:
            out_shape.append(jax.ShapeDtypeStruct((m, width), dt))
            out_specs.append(row(width))
    outs = pl.pallas_call(
        _ffn_in_kernel,
        grid=(m // tm,),
        in_specs=in_specs,
        out_specs=out_specs,
        out_shape=out_shape,
        compiler_params=pltpu.CompilerParams(
            dimension_semantics=("arbitrary",), vmem_limit_bytes=VMEM_LIMIT),
        name="ffn_in",
    )(x, ln_g, ln_b, wg, wu, wd, q_cols, *w_pieces)
    res = {"h": outs[0]}
    for (name, _, _), o in zip(_PROJ_OUT, outs[1:]):
        res[name] = o
    return res


def _attend_kernel(nkb_ref, qpos_ref, lim_ref, iq_ref, wi_ref, q_ref, ki_ref, k_ref, v_ref,
                   o_ref, key_scr, t_scr, *, topk, koff, kb_size, slopes):
    tq = q_ref.shape[0]
    nkb = nkb_ref[pl.program_id(1)]
    lim = lim_ref[...]
    qpos = qpos_ref[...]
    sub = kb_size // LANES

    def lane_ids(s0):
        return s0 + lax.broadcasted_iota(I32, (tq, kb_size), 1)

    def score_block(kb, carry):
        s0 = pl.multiple_of(kb * kb_size, kb_size)
        kib = ki_ref[pl.ds(s0, kb_size), :]
        acc = jnp.zeros((tq, kb_size), F32)
        for h in range(IDX_HEADS):
            lg = lax.dot_general(iq_ref[:, h * IDX_DIM:(h + 1) * IDX_DIM], kib,
                                 (((1,), (1,)), ((), ())), preferred_element_type=F32)
            acc = acc + jnp.maximum(lg, 0.0) * wi_ref[:, h:h + 1]
        bits = lax.bitcast_convert_type(acc, I32)
        key = bits ^ ((bits >> 31) & 0x7FFFFFFF)
        key_scr[:, pl.ds(s0, kb_size)] = jnp.where(lane_ids(s0) < lim, key, INT_MIN)
        return carry

    lax.fori_loop(0, nkb, score_block, 0)

    def bcast(x):
        return jnp.broadcast_to(x, (tq, LANES))

    def count(pred):
        def body(kb, acc):
            for c in range(sub):
                off = pl.multiple_of(kb * kb_size + c * LANES, LANES)
                col = off + lax.broadcasted_iota(I32, (tq, LANES), 1)
                acc = acc + jnp.where(pred(key_scr[:, pl.ds(off, LANES)], col), 1.0, 0.0)
            return acc
        acc = lax.fori_loop(0, nkb, body, jnp.zeros((tq, LANES), F32))
        return jnp.sum(acc, axis=1, keepdims=True)

    def bit_step(state):
        i, t, settled, _ = state
        stop = jnp.min(settled)
        cand = t + lax.shift_left(jnp.int32(1), 31 - i)
        cand_b = bcast(cand)
        cnt = count(lambda key, col: key >= cand_b)
        take = (cnt >= topk) & (settled == 0.0)
        settled = jnp.where(take & (cnt == topk), 1.0, settled)
        return i + 1, jnp.where(take, cand, t), settled, stop

    _, t, _, _ = lax.while_loop(
        lambda state: (state[0] < 32) & (state[3] == 0.0), bit_step,
        (jnp.int32(0), jnp.full((tq, 1), INT_MIN, I32), (lim <= topk).astype(F32), jnp.float32(0.0)))
    thr = jnp.maximum(t, INT_MIN + 1)
    thr_b = bcast(thr)
    n_ge = count(lambda key, col: key >= thr_b)

    @pl.when(jnp.max(n_ge) > topk)
    def _():
        n_gt = count(lambda key, col: key > thr_b)
        need = topk - n_gt
        n_bits = (key_scr.shape[1] - 1).bit_length()

        def idx_step(i, m0):
            cand = m0 + lax.shift_left(jnp.int32(1), n_bits - 1 - i)
            cand_b = bcast(cand)
            g = count(lambda key, col: (key == thr_b) & (col < cand_b))
            return jnp.where(g < need, cand, m0)

        m0 = lax.fori_loop(0, n_bits, idx_step, jnp.zeros((tq, 1), I32))
        drop_row = n_ge > topk

        def demote(kb, carry):
            s0 = pl.multiple_of(kb * kb_size, kb_size)
            key = key_scr[:, pl.ds(s0, kb_size)]
            drop = drop_row & (key == thr) & (lane_ids(s0) > m0)
            key_scr[:, pl.ds(s0, kb_size)] = jnp.where(drop, INT_MIN, key)
            return carry

        lax.fori_loop(0, nkb, demote, 0)

    def attend_block(kb, carry):
        ms, ls, accs = carry
        s0 = pl.multiple_of(kb * kb_size, kb_size)
        col = lane_ids(s0)
        bias = jnp.where(key_scr[:, pl.ds(s0, kb_size)] >= thr, 0.0, NEG_INF)
        dist = jnp.abs(qpos - (col - koff)).astype(F32)
        kblk = k_ref[pl.ds(s0, kb_size), :]
        vblk = v_ref[pl.ds(s0, kb_size), :]
        new_m, new_l, new_acc = [], [], []
        for h in range(A_HEADS):
            g = h // A_GROUP
            s = lax.dot_general(q_ref[:, h * A_HEAD_DIM:(h + 1) * A_HEAD_DIM],
                                kblk[:, g * A_HEAD_DIM:(g + 1) * A_HEAD_DIM],
                                (((1,), (1,)), ((), ())), preferred_element_type=F32)
            s = (s - slopes[h] * dist) + bias
            t_scr[h] = s
            new_m.append(jnp.maximum(ms[h], jnp.max(s, axis=1, keepdims=True)))
        for h in range(A_HEADS):
            g = h // A_GROUP
            p = jnp.exp2(t_scr[h] - new_m[h])
            alpha = jnp.exp2(ms[h] - new_m[h])
            new_l.append(alpha * ls[h] + jnp.sum(p, axis=1, keepdims=True))
            pv = jnp.dot(p.astype(BF16), vblk[:, g * A_HEAD_DIM:(g + 1) * A_HEAD_DIM],
                         preferred_element_type=F32)
            new_acc.append(alpha * accs[h] + pv)
        return tuple(new_m), tuple(new_l), tuple(new_acc)

    init = (tuple(jnp.full((tq, 1), NEG_INF, F32) for _ in range(A_HEADS)),
            tuple(jnp.zeros((tq, 1), F32) for _ in range(A_HEADS)),
            tuple(jnp.zeros((tq, A_HEAD_DIM), F32) for _ in range(A_HEADS)))
    _, ls, accs = lax.fori_loop(0, nkb, attend_block, init)
    for h in range(A_HEADS):
        o_ref[:, h * A_HEAD_DIM:(h + 1) * A_HEAD_DIM] = (accs[h] / ls[h]).astype(o_ref.dtype)


def _attend(iq, wi, q, ki, k, v, qpos, lim, *, tq, kb_size, topk, koff):
    bsz, nq, _ = q.shape
    s_pad = k.shape[1]
    assert nq % tq == 0 and s_pad % kb_size == 0
    nt = nq // tq
    nkb = (jnp.max(lim.reshape(nt, tq), axis=1) + kb_size - 1) // kb_size
    slopes = _alibi_slopes_log2()
    qrow = lambda w: pl.BlockSpec((None, tq, w), lambda b, j, n: (b, j, 0))
    krow = lambda w: pl.BlockSpec((None, s_pad, w), lambda b, j, n: (b, 0, 0))
    pos = pl.BlockSpec((tq, 1), lambda b, j, n: (j, 0))
    return pl.pallas_call(
        functools.partial(_attend_kernel, topk=topk, koff=koff, kb_size=kb_size, slopes=slopes),
        grid_spec=pltpu.PrefetchScalarGridSpec(
            num_scalar_prefetch=1,
            grid=(bsz, nt),
            in_specs=[pos, pos, qrow(IDX_Q), qrow(IDX_HEADS), qrow(A_Q),
                      krow(IDX_DIM), krow(A_KV), krow(A_KV)],
            out_specs=qrow(A_Q),
            scratch_shapes=[pltpu.VMEM((tq, s_pad), I32),
                            pltpu.VMEM((A_HEADS, tq, kb_size), F32)],
        ),
        out_shape=jax.ShapeDtypeStruct((bsz, nq, A_Q), BF16),
        compiler_params=pltpu.CompilerParams(
            dimension_semantics=("arbitrary", "arbitrary"), vmem_limit_bytes=VMEM_LIMIT),
        name="attend",
    )(nkb.astype(I32), qpos.reshape(nq, 1), lim.reshape(nq, 1), iq, wi, q, ki, k, v)


def _fold_rows(x, op, rows=8):
    slabs = [x[i:i + rows] for i in range(0, x.shape[0], rows)]
    while len(slabs) > 1:
        slabs = [op(a, b) for a, b in zip(slabs[0::2], slabs[1::2])] + slabs[len(slabs) & ~1:]
    return slabs[0]


VT_ROWS = A_HEAD_DIM + 16
PROMPT_KEY_BLOCK = 512
KEY_SCAN_STEP = 128
SAMPLE_KEY_BLOCK = 384


def _alibi_slopes_log2():
    return tuple(float(2.0 ** (-8.0 * (i + 1) / A_HEADS)) * LOG2E for i in range(A_HEADS))


ALIBI_SPLIT = CHUNK


def _alibi_columns(n_keys):
    q_cols = np.zeros((A_HEADS, A_HEAD_DIM), np.float32)
    for h, c in enumerate(_alibi_slopes_log2()):
        for i in range(3):
            ci = float(np.asarray(c, dtype=BF16).astype(np.float32))
            q_cols[h, 2 * i], q_cols[h, 2 * i + 1] = ALIBI_SPLIT * ci, ci
            c -= ci
    ids = np.arange(n_keys)
    k_cols = np.zeros((n_keys, A_HEAD_DIM), np.float32)
    k_cols[:, 0:6:2] = (ids // ALIBI_SPLIT)[:, None]
    k_cols[:, 1:6:2] = (ids % ALIBI_SPLIT)[:, None]
    return jnp.asarray(q_cols, BF16), jnp.asarray(k_cols, BF16)


def _attend_t_kernel(qpos_ref, lim_ref, iq_ref, wi_ref, q_ref, ki_ref, k_ref, vt_ref,
                     o_ref, key_scr, tie_scr, t_scr, *, n_keys, first_qpos, topk, koff, kb_size,
                     slopes):
    tq = o_ref.shape[0]
    lim = lim_ref[...]
    qpos = qpos_ref[...]
    n_pairs = A_HEADS // 2
    blocks = [(s0, min(kb_size, n_keys - s0), pl.ds(s0, min(kb_size, n_keys - s0)))
              for s0 in range(0, n_keys, kb_size)]

    def row_ids(s0, size):
        return s0 + lax.broadcasted_iota(I32, (size, tq), 0)

    def head_pair(ref, pr):
        return ref[2 * pr:2 * pr + 2].reshape(2 * tq, ref.shape[-1])

    for s0, size, rows in blocks:
        kib = ki_ref[rows, :]
        acc = jnp.zeros((size, tq), F32)
        for pr in range(IDX_HEADS // 2):
            lg = lax.dot_general(kib, head_pair(iq_ref, pr), (((1,), (1,)), ((), ())),
                                 preferred_element_type=F32)
            for u in range(2):
                h = 2 * pr + u
                acc = acc + jnp.maximum(lg[:, u * tq:(u + 1) * tq], 0.0) * wi_ref[h:h + 1, :]
        bits = lax.bitcast_convert_type(acc, I32)
        key = bits ^ ((bits >> 31) & 0x7FFFFFFF)
        key = jnp.where(row_ids(s0, size) < lim, key, INT_MIN)
        key_scr[rows, :] = key

    def count(plane, pred):
        acc = jnp.zeros((8, tq), F32)
        for _, _, rows in blocks:
            acc = acc + _fold_rows(jnp.where(pred(plane[rows, :]), 1.0, 0.0), jnp.add)
        return jnp.sum(acc, axis=0, keepdims=True)

    def kth_largest(plane, kth, bits):
        def bit_step(i, t):
            cand = t + lax.shift_left(jnp.int32(1), bits - 1 - i)
            return jnp.where(count(plane, lambda key: key >= cand) >= kth, cand, t)
        low = INT_MIN if bits == 32 else -(1 << (bits - 1))
        return lax.fori_loop(0, bits, bit_step, jnp.full((1, tq), low, I32))

    t = kth_largest(key_scr, topk, 32)
    thr = jnp.maximum(t, INT_MIN + 1)
    n_ge = count(key_scr, lambda key: key >= thr)

    @pl.when(jnp.max(n_ge) > topk)
    def _():
        need = topk - count(key_scr, lambda key: key > thr)
        for s0, size, rows in blocks:
            tie_scr[rows, :] = jnp.where(key_scr[rows, :] == thr, -1 - row_ids(s0, size), INT_MIN)
        last = kth_largest(tie_scr, need, key_scr.shape[0].bit_length() + 1)
        drop_q = n_ge > topk
        for s0, size, rows in blocks:
            key = key_scr[rows, :]
            drop = drop_q & (key == thr) & (-1 - row_ids(s0, size) < last)
            key_scr[rows, :] = jnp.where(drop, INT_MIN, key)

    ms = [jnp.full((1, tq), NEG_INF, F32) for _ in range(A_HEADS)]
    accs = [jnp.zeros((vt_ref.shape[1], A_GROUP * tq), F32) for _ in range(A_KV_HEADS)]
    for s0, size, rows in blocks:
        bias = jnp.where(key_scr[rows, :] >= thr, 0.0, NEG_INF)
        may_follow = s0 + size - 1 - koff > first_qpos
        if may_follow:
            ahead = jnp.minimum(qpos - (row_ids(s0, size) - koff), 0).astype(F32)
        new_m = []
        for pr in range(n_pairs):
            g = (2 * pr) // A_GROUP
            s2 = lax.dot_general(k_ref[g, rows, :], head_pair(q_ref, pr),
                                 (((1,), (1,)), ((), ())), preferred_element_type=F32)
            for u in range(2):
                h = 2 * pr + u
                s = s2[:, u * tq:(u + 1) * tq]
                if may_follow:
                    s = s + (2.0 * slopes[h]) * ahead
                s = s + bias
                t_scr[h, :size] = s
                blk_max = jnp.max(_fold_rows(s, jnp.maximum), axis=0, keepdims=True)
                new_m.append(jnp.maximum(ms[h], blk_max))
        for g in range(A_KV_HEADS):
            ps, alphas = [], []
            for h in range(g * A_GROUP, (g + 1) * A_GROUP):
                ps.append(jnp.exp2(t_scr[h, :size] - new_m[h]).astype(BF16))
                alphas.append(jnp.exp2(ms[h] - new_m[h]))
            pv = jnp.dot(vt_ref[g, :, rows], jnp.concatenate(ps, axis=1),
                         preferred_element_type=F32)
            accs[g] = jnp.concatenate(alphas, axis=1) * accs[g] + pv
        ms = new_m
    for pr in range(n_pairs):
        rows = []
        for h in (2 * pr, 2 * pr + 1):
            g, hh = divmod(h, A_GROUP)
            cols = slice(hh * tq, (hh + 1) * tq)
            rows.append(accs[g][:A_HEAD_DIM, cols] / accs[g][A_HEAD_DIM:A_HEAD_DIM + 1, cols])
        o_ref[:, 2 * pr * A_HEAD_DIM:(2 * pr + 2) * A_HEAD_DIM] = (
            jnp.concatenate(rows, axis=0).T.astype(o_ref.dtype))


def _attend_t(iq, wi_t, q, ki, k, vt, qpos, lim, *, kb_size, topk, koff):
    tq = LANES
    _, bsz, nq, _ = q.shape
    s_pad = ki.shape[1]
    assert nq % tq == 0 and s_pad % kb_size == 0
    keys_of_tile = -(-lim.reshape(nq // tq, tq).max(axis=1) // KEY_SCAN_STEP) * KEY_SCAN_STEP
    assert keys_of_tile.max() <= s_pad
    runs, j0 = [], 0
    for j in range(1, len(keys_of_tile) + 1):
        if j == len(keys_of_tile) or keys_of_tile[j] != keys_of_tile[j0]:
            runs.append((j0, j, int(keys_of_tile[j0])))
            j0 = j
    qpos_row = jnp.asarray(qpos, I32).reshape(1, nq)
    lim_row = jnp.asarray(lim, I32).reshape(1, nq)
    outs = []
    for j0, j1, n_keys in runs:
        heads = lambda a: pl.BlockSpec((a.shape[0], None, tq, a.shape[-1]),
                                       lambda b, j: (0, b, j + j0, 0))
        pos = pl.BlockSpec((1, tq), lambda b, j: (0, j + j0))
        outs.append(pl.pallas_call(
            functools.partial(_attend_t_kernel, n_keys=n_keys, first_qpos=int(qpos[j0 * tq]), topk=topk,
                              koff=koff, kb_size=kb_size, slopes=_alibi_slopes_log2()),
            grid=(bsz, j1 - j0),
            in_specs=[pos, pos, heads(iq),
                      pl.BlockSpec((None, IDX_HEADS, tq), lambda b, j: (b, 0, j + j0)),
                      heads(q),
                      pl.BlockSpec((None, s_pad, IDX_DIM), lambda b, j: (b, 0, 0)),
                      pl.BlockSpec((A_KV_HEADS, None, s_pad, k.shape[-1]), lambda b, j: (0, b, 0, 0)),
                      pl.BlockSpec((A_KV_HEADS, None, VT_ROWS, s_pad), lambda b, j: (0, b, 0, 0))],
            out_specs=pl.BlockSpec((None, tq, A_Q), lambda b, j: (b, j, 0)),
            scratch_shapes=[pltpu.VMEM((s_pad, tq), I32), pltpu.VMEM((s_pad, tq), I32),
                            pltpu.VMEM((A_HEADS, kb_size, tq), F32)],
            out_shape=jax.ShapeDtypeStruct((bsz, (j1 - j0) * tq, A_Q), BF16),
            compiler_params=pltpu.CompilerParams(
                dimension_semantics=("arbitrary", "arbitrary"), vmem_limit_bytes=VMEM_LIMIT),
            name=f"attend_t{n_keys}",
        )(qpos_row, lim_row, iq, wi_t, q, ki, k, vt))
    return jnp.concatenate(outs, axis=1)


HGRN_CHUNK = 128


def _hgrn2_tables(ct):
    r = np.arange(ct)
    seg = [r[:, None] >= r[None, :], r[None, :] > r[:, None]]
    pair = []
    g = ct
    while g >= 2:
        mid = (r // g) * g + g // 2
        second = r >= mid
        seg.append(np.where(second[:, None],
                            (r[None, :] >= mid[:, None]) & (r[None, :] <= r[:, None]),
                            (r[None, :] > r[:, None]) & (r[None, :] < mid[:, None])))
        pair.append((r[:, None] // g == r[None, :] // g) & second[:, None] & ~second[None, :])
        g //= 2
    return np.concatenate(seg, 0).astype(np.float32), np.stack(pair).astype(np.float32)


def _hgrn2_kernel(bq_ref, bf_ref, bi_ref, lb_ref, seg_ref, pair_ref, s0_ref, o_ref, sout_ref, st_scr):
    c = pl.program_id(1)
    ct = bq_ref.shape[0]
    n_lvl = pair_ref.shape[0]

    @pl.when(c == 0)
    def _():
        for h in range(B_HEADS):
            st_scr[h] = s0_ref[h].T

    lb = lb_ref[...]
    fx = bf_ref[...]
    logf = jnp.log(lb + (1.0 - lb) * jax.nn.sigmoid(fx))
    kk = (1.0 - lb) * jax.nn.sigmoid(-fx)
    bq = bq_ref[...]
    rq = bq * jax.nn.sigmoid(bq)
    rv = bi_ref[...]
    p1 = logf.astype(BF16)
    p2 = (logf - p1.astype(F32)).astype(BF16)
    seg = seg_ref[...]
    sums = jnp.dot(seg, p1, preferred_element_type=F32) + jnp.dot(seg, p2, preferred_element_type=F32)
    b_in = sums[:ct]
    b_out = sums[ct:2 * ct]
    q_in = (rq * jnp.exp(b_in)).astype(BF16)
    k_out = (kk * jnp.exp(b_out)).astype(BF16)
    rvb = rv.astype(BF16)
    row = lax.broadcasted_iota(I32, (ct, B_KEY_DIM), 0)
    nt = (((1,), (1,)), ((), ()))

    a = [jnp.zeros((ct, ct), F32) for _ in range(B_HEADS)]
    for lvl in range(n_lvl):
        half = ct >> (lvl + 1)
        second = (row & half) != 0
        dec = jnp.exp(sums[(2 + lvl) * ct:(3 + lvl) * ct])
        for h in range(B_HEADS):
            ls = slice(h * B_KEY_DIM, (h + 1) * B_KEY_DIM)
            x = (jnp.where(second, rq[:, ls], kk[:, ls]) * dec[:, ls]).astype(BF16)
            a[h] = a[h] + pair_ref[lvl] * lax.dot_general(x, x, nt, preferred_element_type=F32)

    for h in range(B_HEADS):
        ls = slice(h * B_KEY_DIM, (h + 1) * B_KEY_DIM)
        st = st_scr[h]
        own = jnp.sum(rq[:, ls] * kk[:, ls], axis=1, keepdims=True)
        o_ref[:, ls] = (jnp.dot(a[h].astype(BF16), rvb[:, ls], preferred_element_type=F32)
                        + own * rv[:, ls]
                        + lax.dot_general(q_in[:, ls], st.astype(BF16), nt, preferred_element_type=F32))
        upd = lax.dot_general(rvb[:, ls], k_out[:, ls], (((0,), (0,)), ((), ())),
                              preferred_element_type=F32)
        st_scr[h] = st * jnp.exp(b_in[ct - 1:ct, ls]) + upd

    @pl.when(c == pl.num_programs(1) - 1)
    def _():
        for h in range(B_HEADS):
            sout_ref[h] = st_scr[h].T


def _hgrn2(bq, bf, bi, lb, s0):
    bsz, t, _ = bq.shape
    ct = HGRN_CHUNK
    tp = -(-t // ct) * ct
    if tp != t:
        pad = lambda a, v: jnp.pad(a, ((0, 0), (0, tp - t), (0, 0)), constant_values=v)
        bq, bf, bi = pad(bq, 0.0), pad(bf, 1e4), pad(bi, 0.0)
    seg, pair = _hgrn2_tables(ct)
    seg = jnp.asarray(seg, BF16)
    pair = jnp.asarray(pair, F32)
    tok = pl.BlockSpec((None, ct, B_KW), lambda b, c: (b, c, 0))
    s_map = (lambda b, c: (0, 0, 0, 0)) if s0.shape[0] == 1 else (lambda b, c: (b, 0, 0, 0))
    s_in = pl.BlockSpec((None, B_HEADS, B_KEY_DIM, B_VAL_DIM), s_map)
    s_out = pl.BlockSpec((None, B_HEADS, B_KEY_DIM, B_VAL_DIM), lambda b, c: (b, 0, 0, 0))
    o, s = pl.pallas_call(
        _hgrn2_kernel,
        grid=(bsz, tp // ct),
        in_specs=[tok, tok, tok, _resident(lb.shape), _resident(seg.shape), _resident(pair.shape), s_in],
        out_specs=[tok, s_out],
        out_shape=[jax.ShapeDtypeStruct((bsz, tp, B_WIDTH), F32),
                   jax.ShapeDtypeStruct((bsz, B_HEADS, B_KEY_DIM, B_VAL_DIM), F32)],
        scratch_shapes=[pltpu.VMEM((B_HEADS, B_VAL_DIM, B_KEY_DIM), F32)],
        compiler_params=pltpu.CompilerParams(
            dimension_semantics=("arbitrary", "arbitrary"), vmem_limit_bytes=VMEM_LIMIT),
        name="hgrn2",
    )(bq, bf, bi, lb, seg, pair, s0)
    return o[:, :t], s


def _mix_out_kernel(h_ref, oa_ref, orec_ref, wgate_ref, gn_ref, wa_ref, wb_ref, wo_ref,
                    g2_ref, b2_ref, wg_ref, wu_ref, wd_ref, g3_ref, b3_ref, y_ref):
    h = h_ref[...]
    gates = jnp.dot(h.astype(BF16), wgate_ref[...], preferred_element_type=F32)
    bg = gates[:, :B_WIDTH]
    ga = gates[:, B_WIDTH:B_WIDTH + D_MODEL]
    gb = gates[:, B_WIDTH + D_MODEL:]
    orec = orec_ref[...]
    gn = gn_ref[...]
    parts = []
    for head in range(B_HEADS):
        ls = slice(head * B_VAL_DIM, (head + 1) * B_VAL_DIM)
        oh = orec[:, ls]
        ms = jnp.mean(oh * oh, axis=-1, keepdims=True)
        parts.append(oh * lax.rsqrt(ms + RMS_EPS) * gn[:, ls])
    on = jnp.concatenate(parts, axis=-1)
    ob = on * (bg * jax.nn.sigmoid(bg))
    pa = jnp.dot(oa_ref[...], wa_ref[...], preferred_element_type=F32)
    pb = jnp.dot(ob.astype(BF16), wb_ref[...], preferred_element_type=F32)
    mixed = jax.nn.sigmoid(ga) * pa + jax.nn.sigmoid(gb) * pb
    mixed = jnp.dot(mixed.astype(BF16), wo_ref[...], preferred_element_type=F32)
    h2 = _layer_norm(ALPHA * h + mixed, g2_ref[...], b2_ref[...])
    y_ref[...] = _layer_norm(_swiglu_step(h2, wg_ref, wu_ref, wd_ref), g3_ref[...], b3_ref[...])


def _mix_out(h, oa, orec, wgate, gn, wa, wb, wo, g2, b2, wg, wu, wd, g3, b3, tm):
    m = h.shape[0]
    assert m % tm == 0
    row = lambda w: pl.BlockSpec((tm, w), lambda i: (i, 0))
    weights = (wgate, gn, wa, wb, wo, g2, b2, wg, wu, wd, g3, b3)
    return pl.pallas_call(
        _mix_out_kernel,
        grid=(m // tm,),
        in_specs=[row(D_MODEL), row(A_Q), row(B_WIDTH)] + [_resident(w.shape) for w in weights],
        out_specs=row(D_MODEL),
        out_shape=jax.ShapeDtypeStruct((m, D_MODEL), F32),
        compiler_params=pltpu.CompilerParams(
            dimension_semantics=("arbitrary",), vmem_limit_bytes=VMEM_LIMIT),
        name="mix_out",
    )(h, oa, orec, *weights)


def _ffn_weights(wg, wu, wd):
    return wg.astype(BF16), wu.astype(BF16), wd.astype(BF16).reshape(N_FF_CHUNKS, FF_CHUNK, D_MODEL)


def _pad_rows(a, n):
    return jnp.pad(a, ((0, 0), (0, n - a.shape[1]), (0, 0)))


FFN_IN_TILE = 512
MIX_OUT_TILE = 512


def _token_tile(m, largest):
    tm = largest
    while tm > 8 and m % tm:
        tm //= 2
    assert m % tm == 0, m
    return tm


def kernel(x_prompt, x_sample, cache_k, cache_v, cache_kidx, state_hgrn, meta, ln1_g, ln1_b, ffn1_wg, ffn1_wu, ffn1_wd, w_in, lb_param, g_norm, w_a_proj, w_b_proj, w_out, ln2_g, ln2_b, ffn2_wg, ffn2_wu, ffn2_wd, ln3_g, ln3_b):
    assert ln1_g.shape[0] == DEPTH
    bp, seq, _ = x_prompt.shape
    bs, dseq, _ = x_sample.shape
    past = cache_k.shape[2]
    l = 0
    vec = lambda a: a.reshape(1, -1).astype(F32)

    lb_all = jnp.cumsum(jax.nn.softmax(lb_param.astype(F32), axis=0), axis=0)
    lb = lb_all[l].reshape(1, B_KW)
    f1 = _ffn_weights(ffn1_wg[l], ffn1_wu[l], ffn1_wd[l])
    f2 = _ffn_weights(ffn2_wg[l], ffn2_wu[l], ffn2_wd[l])
    w_pieces, off = [], 0
    for _, width, _ in _PROJ_OUT:
        w_pieces.append(w_in[l][:, off:off + width].astype(BF16))
        off += width
    w_gate = w_in[l][:, off:].astype(BF16)
    assert w_gate.shape[1] == sum(_GATE_WIDTHS)

    kb_p = PROMPT_KEY_BLOCK
    s_p = -(-(N_META + seq) // kb_p) * kb_p
    q_cols, k_cols = _alibi_columns(s_p)

    def stage1(x):
        return _ffn_in(x, vec(ln1_g[l]), vec(ln1_b[l]), *f1, q_cols[:, None, :], w_pieces,
                       _token_tile(x.shape[0], FFN_IN_TILE))

    def stage4(p, oa, orec):
        m = p["h"].shape[0]
        return _mix_out(p["h"], oa.reshape(m, A_Q), orec.reshape(m, B_WIDTH), w_gate,
                        vec(g_norm[l]), w_a_proj[l].astype(BF16), w_b_proj[l].astype(BF16),
                        w_out[l].astype(BF16), vec(ln2_g[l]), vec(ln2_b[l]), *f2,
                        vec(ln3_g[l]), vec(ln3_b[l]), _token_tile(m, MIX_OUT_TILE))

    pm = stage1(meta.astype(F32))
    pp = stage1(x_prompt.reshape(bp * seq, D_MODEL))
    ps = stage1(x_sample.reshape(bs * dseq, D_MODEL))
    per_p = lambda a: a.reshape(bp, seq, a.shape[-1])
    per_s = lambda a: a.reshape(bs, dseq, a.shape[-1])

    def with_meta(name):
        rows = jnp.broadcast_to(pm[name][None], (bp, N_META, pm[name].shape[-1]))
        return jnp.concatenate([rows, per_p(pp[name])], axis=1)

    k_p, v_p, ki_p = with_meta("ak"), with_meta("av"), with_meta("ik")

    qpos = np.arange(seq, dtype=np.int32)
    lim = N_META + CHUNK * (qpos // CHUNK + 1)
    heads_p = lambda a: a.reshape(a.shape[0], bp, seq, a.shape[-1])
    kv_heads = lambda a: _pad_rows(a.astype(BF16), s_p).reshape(bp, s_p, A_KV_HEADS, A_HEAD_DIM)
    vt = kv_heads(v_p).transpose(2, 0, 3, 1)
    vt = jnp.concatenate([vt, jnp.ones_like(vt[:, :, :1]),
                          jnp.zeros_like(vt[:, :, :VT_ROWS - A_HEAD_DIM - 1])], axis=2)
    k_aug = jnp.concatenate([kv_heads(k_p).transpose(2, 0, 1, 3), jnp.broadcast_to(
        k_cols[None, None], (A_KV_HEADS, bp, s_p, A_HEAD_DIM))], axis=-1)
    oa_p = _attend_t(heads_p(pp["iq"]), per_p(pp["iw"]).transpose(0, 2, 1), heads_p(pp["aq"]),
                     _pad_rows(ki_p.astype(BF16), s_p), k_aug, vt, qpos, lim,
                     kb_size=kb_p, topk=min(TOPK_MAX, seq // 4), koff=N_META)

    n_all = past + dseq
    kb_s = SAMPLE_KEY_BLOCK
    s_s = -(-n_all // kb_s) * kb_s
    k_all = jnp.concatenate([cache_k[l].reshape(bs, past, A_KV).astype(BF16), per_s(ps["ak"]).astype(BF16)], axis=1)
    v_all = jnp.concatenate([cache_v[l].reshape(bs, past, A_KV).astype(BF16), per_s(ps["av"]).astype(BF16)], axis=1)
    ki_all = jnp.concatenate([cache_kidx[l].astype(BF16), per_s(ps["ik"]).astype(BF16)], axis=1)
    tokens_s = lambda a: a.transpose(1, 0, 2).reshape(bs, dseq, a.shape[0] * a.shape[-1])
    oa_s = _attend(tokens_s(ps["iq"]), per_s(ps["iw"]), tokens_s(ps["aq"][..., :A_HEAD_DIM]),
                   _pad_rows(ki_all, s_s), _pad_rows(k_all, s_s), _pad_rows(v_all, s_s),
                   past + jnp.arange(dseq, dtype=I32), jnp.full((dseq,), n_all, I32),
                   tq=dseq, kb_size=kb_s, topk=min(TOPK_MAX, n_all // 4), koff=0)

    zero_state = jnp.zeros((1, B_HEADS, B_KEY_DIM, B_VAL_DIM), F32)
    _, st_m = _hgrn2(pm["bq"][None], pm["bf"][None], pm["bi"][None], lb, zero_state)
    orec_p, st_p = _hgrn2(per_p(pp["bq"]), per_p(pp["bf"]), per_p(pp["bi"]), lb, st_m)
    orec_s, st_s = _hgrn2(per_s(ps["bq"]), per_s(ps["bf"]), per_s(ps["bi"]), lb,
                          state_hgrn[l].astype(F32))

    y_p = stage4(pp, oa_p, orec_p).reshape(bp, seq, D_MODEL)
    y_s = stage4(ps, oa_s, orec_s).reshape(bs, dseq, D_MODEL)

    kv5 = lambda a: a.reshape(1, a.shape[0], a.shape[1], A_KV_HEADS, A_HEAD_DIM)
    return (y_p, y_s, kv5(k_p), kv5(v_p), ki_p[None], st_p[None].astype(state_hgrn.dtype),
            kv5(per_s(ps["ak"])), kv5(per_s(ps["av"])), per_s(ps["ik"])[None],
            st_s[None].astype(state_hgrn.dtype))
```

```python
import functools

import jax
import jax.numpy as jnp
import numpy as np
from jax import lax
from jax.experimental import pallas as pl
from jax.experimental.pallas import tpu as pltpu

F32 = jnp.float32
BF16 = jnp.bfloat16
I32 = jnp.int32

D_MODEL = 1024
D_FF = 2816
FF_CHUNK = 256
N_FF_CHUNKS = D_FF // FF_CHUNK
CHUNK = 64
N_META = 16
A_HEADS = 8
A_KV_HEADS = 2
A_GROUP = A_HEADS // A_KV_HEADS
A_HEAD_DIM = 64
A_Q = A_HEADS * A_HEAD_DIM
A_KV = A_KV_HEADS * A_HEAD_DIM
IDX_HEADS = 8
IDX_DIM = 64
IDX_Q = IDX_HEADS * IDX_DIM
TOPK_MAX = 256
B_HEADS = 4
B_KEY_DIM = 128
B_VAL_DIM = 128
B_KW = B_HEADS * B_KEY_DIM
B_WIDTH = B_HEADS * B_VAL_DIM
DEPTH = 1
ALPHA = (2.0 * DEPTH) ** 0.25
LN_EPS = 1e-5
RMS_EPS = 1e-6
NEG_INF = -1e30
LOG2E = 1.4426950408889634
INT_MIN = -(2 ** 31)
LANES = 128
VMEM_LIMIT = 56 * 1024 * 1024


def _resident(shape):
    n = len(shape)
    return pl.BlockSpec(shape, lambda *_: (0,) * n, pipeline_mode=pl.Buffered(1))


def _layer_norm(x, g, b):
    mu = jnp.mean(x, axis=-1, keepdims=True)
    xc = x - mu
    var = jnp.mean(xc * xc, axis=-1, keepdims=True)
    return xc * lax.rsqrt(var + LN_EPS) * g + b


def _swiglu_step(x, wg_ref, wu_ref, wd_ref):
    xb = x.astype(BF16)

    acc = jnp.zeros(x.shape, F32)
    for c in range(N_FF_CHUNKS):
        cols = slice(c * FF_CHUNK, (c + 1) * FF_CHUNK)
        g = jnp.dot(xb, wg_ref[:, cols], preferred_element_type=F32)
        u = jnp.dot(xb, wu_ref[:, cols], preferred_element_type=F32)
        a = (g * jax.nn.sigmoid(g) * u).astype(BF16)
        acc = acc + jnp.dot(a, wd_ref[c], preferred_element_type=F32)
    return ALPHA * x + 0.5 * acc


_PROJ_OUT = (
    ("aq", A_Q, BF16), ("ak", A_KV, F32), ("av", A_KV, F32), ("iq", IDX_Q, BF16),
    ("ik", IDX_DIM, F32), ("iw", IDX_HEADS, F32),
    ("bq", B_KW, F32), ("bf", B_KW, F32), ("bi", B_WIDTH, F32),
)
_GATE_WIDTHS = (B_WIDTH, D_MODEL, D_MODEL)
_PROJ_SCALE = {"aq": (A_HEAD_DIM ** -0.5) * LOG2E, "iw": (IDX_HEADS ** -0.5) * (IDX_DIM ** -0.5)}
_HEAD_MAJOR = {"aq": A_HEADS, "iq": IDX_HEADS}


def _ffn_in_kernel(x_ref, g_ref, b_ref, wg_ref, wu_ref, wd_ref, qcols_ref, *rest):
    n = len(_PROJ_OUT)
    w_refs, h_ref, out_refs = rest[:n], rest[n], rest[n + 1:]
    x = x_ref[...]
    h = _layer_norm(_swiglu_step(x, wg_ref, wu_ref, wd_ref), g_ref[...], b_ref[...])
    h_ref[...] = h
    hb = h.astype(BF16)
    for (name, _, dt), w_ref, o_ref in zip(_PROJ_OUT, w_refs, out_refs):
        y = jnp.dot(hb, w_ref[...], preferred_element_type=F32)
        if name in _PROJ_SCALE:
            y = y * _PROJ_SCALE[name]
        if name in _HEAD_MAJOR:
            hd = y.shape[1] // _HEAD_MAJOR[name]
            for head in range(_HEAD_MAJOR[name]):
                yh = y[:, head * hd:(head + 1) * hd].astype(dt)
                if name == "aq":
                    yh = jnp.concatenate([yh, jnp.broadcast_to(qcols_ref[head], yh.shape)], axis=1)
                o_ref[head] = yh
        else:
            o_ref[...] = y.astype(dt)


def _ffn_in(x, ln_g, ln_b, wg, wu, wd, q_cols, w_pieces, tm):
    m = x.shape[0]
    assert m % tm == 0
    row = lambda w: pl.BlockSpec((tm, w), lambda i: (i, 0))
    in_specs = [row(D_MODEL), _resident(ln_g.shape), _resident(ln_b.shape),
                _resident(wg.shape), _resident(wu.shape), _resident(wd.shape),
                _resident(q_cols.shape)]
    in_specs += [_resident(w.shape) for w in w_pieces]
    out_shape = [jax.ShapeDtypeStruct((m, D_MODEL), F32)]
    out_specs = [row(D_MODEL)]
    for name, width, dt in _PROJ_OUT:
        if name in _HEAD_MAJOR:
            nh = _HEAD_MAJOR[name]
            hd = width // nh + (q_cols.shape[-1] if name == "aq" else 0)
            out_shape.append(jax.ShapeDtypeStruct((nh, m, hd), dt))
            out_specs.append(pl.BlockSpec((nh, tm, hd), lambda i: (0, i, 0)))
        else:
            out_shape.append(jax.ShapeDtypeStruct((m, width), dt))
            out_specs.append(row(width))
    outs = pl.pallas_call(
        _ffn_in_kernel,
        grid=(m // tm,),
        in_specs=in_specs,
        out_specs=out_specs,
        out_shape=out_shape,
        compiler_params=pltpu.CompilerParams(
            dimension_semantics=("arbitrary",), vmem_limit_bytes=VMEM_LIMIT),
        name="ffn_in",
    )(x, ln_g, ln_b, wg, wu, wd, q_cols, *w_pieces)
    res = {"h": outs[0]}
    for (name, _, _), o in zip(_PROJ_OUT, outs[1:]):
        res[name] = o
    return res


def _attend_kernel(nkb_ref, qpos_ref, lim_ref, iq_ref, wi_ref, q_ref, ki_ref, k_ref, v_ref,
                   o_ref, key_scr, t_scr, *, topk, koff, kb_size, slopes):
    tq = q_ref.shape[0]
    nkb = nkb_ref[pl.program_id(1)]
    lim = lim_ref[...]
    qpos = qpos_ref[...]
    sub = kb_size // LANES

    def lane_ids(s0):
        return s0 + lax.broadcasted_iota(I32, (tq, kb_size), 1)

    def score_block(kb, carry):
        s0 = pl.multiple_of(kb * kb_size, kb_size)
        kib = ki_ref[pl.ds(s0, kb_size), :]
        acc = jnp.zeros((tq, kb_size), F32)
        for h in range(IDX_HEADS):
            lg = lax.dot_general(iq_ref[:, h * IDX_DIM:(h + 1) * IDX_DIM], kib,
                                 (((1,), (1,)), ((), ())), preferred_element_type=F32)
            acc = acc + jnp.maximum(lg, 0.0) * wi_ref[:, h:h + 1]
        bits = lax.bitcast_convert_type(acc, I32)
        key = bits ^ ((bits >> 31) & 0x7FFFFFFF)
        key_scr[:, pl.ds(s0, kb_size)] = jnp.where(lane_ids(s0) < lim, key, INT_MIN)
        return carry

    lax.fori_loop(0, nkb, score_block, 0)

    def bcast(x):
        return jnp.broadcast_to(x, (tq, LANES))

    def count(pred):
        def body(kb, acc):
            for c in range(sub):
                off = pl.multiple_of(kb * kb_size + c * LANES, LANES)
                col = off + lax.broadcasted_iota(I32, (tq, LANES), 1)
                acc = acc + jnp.where(pred(key_scr[:, pl.ds(off, LANES)], col), 1.0, 0.0)
            return acc
        acc = lax.fori_loop(0, nkb, body, jnp.zeros((tq, LANES), F32))
        return jnp.sum(acc, axis=1, keepdims=True)

    def bit_step(state):
        i, t, settled, _ = state
        stop = jnp.min(settled)
        cand = t + lax.shift_left(jnp.int32(1), 31 - i)
        cand_b = bcast(cand)
        cnt = count(lambda key, col: key >= cand_b)
        take = (cnt >= topk) & (settled == 0.0)
        settled = jnp.where(take & (cnt == topk), 1.0, settled)
        return i + 1, jnp.where(take, cand, t), settled, stop

    _, t, _, _ = lax.while_loop(
        lambda state: (state[0] < 32) & (state[3] == 0.0), bit_step,
        (jnp.int32(0), jnp.full((tq, 1), INT_MIN, I32), (lim <= topk).astype(F32), jnp.float32(0.0)))
    thr = jnp.maximum(t, INT_MIN + 1)
    thr_b = bcast(thr)
    n_ge = count(lambda key, col: key >= thr_b)

    @pl.when(jnp.max(n_ge) > topk)
    def _():
        n_gt = count(lambda key, col: key > thr_b)
        need = topk - n_gt
        n_bits = (key_scr.shape[1] - 1).bit_length()

        def idx_step(i, m0):
            cand = m0 + lax.shift_left(jnp.int32(1), n_bits - 1 - i)
            cand_b = bcast(cand)
            g = count(lambda key, col: (key == thr_b) & (col < cand_b))
            return jnp.where(g < need, cand, m0)

        m0 = lax.fori_loop(0, n_bits, idx_step, jnp.zeros((tq, 1), I32))
        drop_row = n_ge > topk

        def demote(kb, carry):
            s0 = pl.multiple_of(kb * kb_size, kb_size)
            key = key_scr[:, pl.ds(s0, kb_size)]
            drop = drop_row & (key == thr) & (lane_ids(s0) > m0)
            key_scr[:, pl.ds(s0, kb_size)] = jnp.where(drop, INT_MIN, key)
            return carry

        lax.fori_loop(0, nkb, demote, 0)

    def attend_block(kb, carry):
        ms, ls, accs = carry
        s0 = pl.multiple_of(kb * kb_size, kb_size)
        col = lane_ids(s0)
        bias = jnp.where(key_scr[:, pl.ds(s0, kb_size)] >= thr, 0.0, NEG_INF)
        dist = jnp.abs(qpos - (col - koff)).astype(F32)
        kblk = k_ref[pl.ds(s0, kb_size), :]
        vblk = v_ref[pl.ds(s0, kb_size), :]
        new_m, new_l, new_acc = [], [], []
        for h in range(A_HEADS):
            g = h // A_GROUP
            s = lax.dot_general(q_ref[:, h * A_HEAD_DIM:(h + 1) * A_HEAD_DIM],
                                kblk[:, g * A_HEAD_DIM:(g + 1) * A_HEAD_DIM],
                                (((1,), (1,)), ((), ())), preferred_element_type=F32)
            s = (s - slopes[h] * dist) + bias
            t_scr[h] = s
            new_m.append(jnp.maximum(ms[h], jnp.max(s, axis=1, keepdims=True)))
        for h in range(A_HEADS):
            g = h // A_GROUP
            p = jnp.exp2(t_scr[h] - new_m[h])
            alpha = jnp.exp2(ms[h] - new_m[h])
            new_l.append(alpha * ls[h] + jnp.sum(p, axis=1, keepdims=True))
            pv = jnp.dot(p.astype(BF16), vblk[:, g * A_HEAD_DIM:(g + 1) * A_HEAD_DIM],
                         preferred_element_type=F32)
            new_acc.append(alpha * accs[h] + pv)
        return tuple(new_m), tuple(new_l), tuple(new_acc)

    init = (tuple(jnp.full((tq, 1), NEG_INF, F32) for _ in range(A_HEADS)),
            tuple(jnp.zeros((tq, 1), F32) for _ in range(A_HEADS)),
            tuple(jnp.zeros((tq, A_HEAD_DIM), F32) for _ in range(A_HEADS)))
    _, ls, accs = lax.fori_loop(0, nkb, attend_block, init)
    for h in range(A_HEADS):
        o_ref[:, h * A_HEAD_DIM:(h + 1) * A_HEAD_DIM] = (accs[h] / ls[h]).astype(o_ref.dtype)


def _attend(iq, wi, q, ki, k, v, qpos, lim, *, tq, kb_size, topk, koff):
    bsz, nq, _ = q.shape
    s_pad = k.shape[1]
    assert nq % tq == 0 and s_pad % kb_size == 0
    nt = nq // tq
    nkb = (jnp.max(lim.reshape(nt, tq), axis=1) + kb_size - 1) // kb_size
    slopes = _alibi_slopes_log2()
    qrow = lambda w: pl.BlockSpec((None, tq, w), lambda b, j, n: (b, j, 0))
    krow = lambda w: pl.BlockSpec((None, s_pad, w), lambda b, j, n: (b, 0, 0))
    pos = pl.BlockSpec((tq, 1), lambda b, j, n: (j, 0))
    return pl.pallas_call(
        functools.partial(_attend_kernel, topk=topk, koff=koff, kb_size=kb_size, slopes=slopes),
        grid_spec=pltpu.PrefetchScalarGridSpec(
            num_scalar_prefetch=1,
            grid=(bsz, nt),
            in_specs=[pos, pos, qrow(IDX_Q), qrow(IDX_HEADS), qrow(A_Q),
                      krow(IDX_DIM), krow(A_KV), krow(A_KV)],
            out_specs=qrow(A_Q),
            scratch_shapes=[pltpu.VMEM((tq, s_pad), I32),
                            pltpu.VMEM((A_HEADS, tq, kb_size), F32)],
        ),
        out_shape=jax.ShapeDtypeStruct((bsz, nq, A_Q), BF16),
        compiler_params=pltpu.CompilerParams(
            dimension_semantics=("arbitrary", "arbitrary"), vmem_limit_bytes=VMEM_LIMIT),
        name="attend",
    )(nkb.astype(I32), qpos.reshape(nq, 1), lim.reshape(nq, 1), iq, wi, q, ki, k, v)


def _fold_rows(x, op, rows=8):
    while x.shape[0] > rows:
        half = x.shape[0] // 2
        x = op(x[:half], x[half:])
    return x


VT_ROWS = A_HEAD_DIM + 16
PROMPT_KEY_BLOCK = 512
KEY_SCAN_STEP = 256
SAMPLE_KEY_BLOCK = 384


def _alibi_slopes_log2():
    return tuple(float(2.0 ** (-8.0 * (i + 1) / A_HEADS)) * LOG2E for i in range(A_HEADS))


ALIBI_SPLIT = CHUNK


def _alibi_columns(n_keys):
    q_cols = np.zeros((A_HEADS, A_HEAD_DIM), np.float32)
    for h, c in enumerate(_alibi_slopes_log2()):
        for i in range(3):
            ci = float(np.asarray(c, dtype=BF16).astype(np.float32))
            q_cols[h, 2 * i], q_cols[h, 2 * i + 1] = ALIBI_SPLIT * ci, ci
            c -= ci
    ids = np.arange(n_keys)
    k_cols = np.zeros((n_keys, A_HEAD_DIM), np.float32)
    k_cols[:, 0:6:2] = (ids // ALIBI_SPLIT)[:, None]
    k_cols[:, 1:6:2] = (ids % ALIBI_SPLIT)[:, None]
    return jnp.asarray(q_cols, BF16), jnp.asarray(k_cols, BF16)


def _attend_t_kernel(qpos_ref, lim_ref, iq_ref, wi_ref, q_ref, ki_ref, k_ref, vt_ref,
                     o_ref, key_scr, tie_scr, t_scr, *, n_keys, first_qpos, topk, koff, kb_size,
                     slopes):
    tq = o_ref.shape[0]
    lim = lim_ref[...]
    qpos = qpos_ref[...]
    n_pairs = A_HEADS // 2
    blocks = [(s0, min(kb_size, n_keys - s0), pl.ds(s0, min(kb_size, n_keys - s0)))
              for s0 in range(0, n_keys, kb_size)]

    def row_ids(s0, size):
        return s0 + lax.broadcasted_iota(I32, (size, tq), 0)

    def head_pair(ref, pr):
        return ref[2 * pr:2 * pr + 2].reshape(2 * tq, ref.shape[-1])

    for s0, size, rows in blocks:
        kib = ki_ref[rows, :]
        acc = jnp.zeros((size, tq), F32)
        for pr in range(IDX_HEADS // 2):
            lg = lax.dot_general(kib, head_pair(iq_ref, pr), (((1,), (1,)), ((), ())),
                                 preferred_element_type=F32)
            for u in range(2):
                h = 2 * pr + u
                acc = acc + jnp.maximum(lg[:, u * tq:(u + 1) * tq], 0.0) * wi_ref[h:h + 1, :]
        bits = lax.bitcast_convert_type(acc, I32)
        key = bits ^ ((bits >> 31) & 0x7FFFFFFF)
        key = jnp.where(row_ids(s0, size) < lim, key, INT_MIN)
        key_scr[rows, :] = key

    def count(plane, pred):
        acc = jnp.zeros((8, tq), F32)
        for _, _, rows in blocks:
            acc = acc + _fold_rows(jnp.where(pred(plane[rows, :]), 1.0, 0.0), jnp.add)
        return jnp.sum(acc, axis=0, keepdims=True)

    def kth_largest(plane, kth, bits):
        def bit_step(i, carry):
            t, n_at_t = carry
            cand = t + lax.shift_left(jnp.int32(1), bits - 1 - i)
            cnt = count(plane, lambda key: key >= cand)
            take = cnt >= kth
            return jnp.where(take, cand, t), jnp.where(take, cnt, n_at_t)
        low = INT_MIN if bits == 32 else -(1 << (bits - 1))
        return lax.fori_loop(0, bits, bit_step,
                             (jnp.full((1, tq), low, I32), jnp.zeros((1, tq), F32)))

    t, n_ge = kth_largest(key_scr, topk, 32)
    thr = jnp.maximum(t, INT_MIN + 1)
    n_ge = jnp.where(t == INT_MIN, lim.astype(F32), n_ge)

    @pl.when(jnp.max(n_ge) > topk)
    def _():
        need = topk - count(key_scr, lambda key: key > thr)
        for s0, size, rows in blocks:
            tie_scr[rows, :] = jnp.where(key_scr[rows, :] == thr, -1 - row_ids(s0, size), INT_MIN)
        last, _ = kth_largest(tie_scr, need, key_scr.shape[0].bit_length() + 1)
        drop_q = n_ge > topk
        for s0, size, rows in blocks:
            key = key_scr[rows, :]
            drop = drop_q & (key == thr) & (-1 - row_ids(s0, size) < last)
            key_scr[rows, :] = jnp.where(drop, INT_MIN, key)

    ms = [jnp.full((1, tq), NEG_INF, F32) for _ in range(A_HEADS)]
    accs = [jnp.zeros((vt_ref.shape[1], A_GROUP * tq), F32) for _ in range(A_KV_HEADS)]
    for s0, size, rows in blocks:
        bias = jnp.where(key_scr[rows, :] >= thr, 0.0, NEG_INF)
        may_follow = s0 + size - 1 - koff > first_qpos
        if may_follow:
            ahead = jnp.minimum(qpos - (row_ids(s0, size) - koff), 0).astype(F32)
        new_m = []
        for pr in range(n_pairs):
            g = (2 * pr) // A_GROUP
            s2 = lax.dot_general(k_ref[g, rows, :], head_pair(q_ref, pr),
                                 (((1,), (1,)), ((), ())), preferred_element_type=F32)
            for u in range(2):
                h = 2 * pr + u
                s = s2[:, u * tq:(u + 1) * tq]
                if may_follow:
                    s = s + (2.0 * slopes[h]) * ahead
                s = s + bias
                t_scr[h, :size] = s
                blk_max = jnp.max(_fold_rows(s, jnp.maximum), axis=0, keepdims=True)
                new_m.append(jnp.maximum(ms[h], blk_max))
        for g in range(A_KV_HEADS):
            ps, alphas = [], []
            for h in range(g * A_GROUP, (g + 1) * A_GROUP):
                ps.append(jnp.exp2(t_scr[h, :size] - new_m[h]).astype(BF16))
                alphas.append(jnp.exp2(ms[h] - new_m[h]))
            pv = jnp.dot(vt_ref[g, :, rows], jnp.concatenate(ps, axis=1),
                         preferred_element_type=F32)
            accs[g] = jnp.concatenate(alphas, axis=1) * accs[g] + pv
        ms = new_m
    for pr in range(n_pairs):
        rows = []
        for h in (2 * pr, 2 * pr + 1):
            g, hh = divmod(h, A_GROUP)
            cols = slice(hh * tq, (hh + 1) * tq)
            rows.append(accs[g][:A_HEAD_DIM, cols] / accs[g][A_HEAD_DIM:A_HEAD_DIM + 1, cols])
        o_ref[:, 2 * pr * A_HEAD_DIM:(2 * pr + 2) * A_HEAD_DIM] = (
            jnp.concatenate(rows, axis=0).T.astype(o_ref.dtype))


def _attend_t(iq, wi_t, q, ki, k, vt, qpos, lim, *, kb_size, topk, koff):
    tq = LANES
    _, bsz, nq, _ = q.shape
    s_pad = ki.shape[1]
    assert nq % tq == 0 and s_pad % kb_size == 0
    keys_of_tile = -(-lim.reshape(nq // tq, tq).max(axis=1) // KEY_SCAN_STEP) * KEY_SCAN_STEP
    assert keys_of_tile.max() <= s_pad
    runs, j0 = [], 0
    for j in range(1, len(keys_of_tile) + 1):
        if j == len(keys_of_tile) or keys_of_tile[j] != keys_of_tile[j0]:
            runs.append((j0, j, int(keys_of_tile[j0])))
            j0 = j
    qpos_row = jnp.asarray(qpos, I32).reshape(1, nq)
    lim_row = jnp.asarray(lim, I32).reshape(1, nq)
    outs = []
    for j0, j1, n_keys in runs:
        heads = lambda a: pl.BlockSpec((a.shape[0], None, tq, a.shape[-1]),
                                       lambda b, j: (0, b, j + j0, 0))
        pos = pl.BlockSpec((1, tq), lambda b, j: (0, j + j0))
        outs.append(pl.pallas_call(
            functools.partial(_attend_t_kernel, n_keys=n_keys, first_qpos=int(qpos[j0 * tq]), topk=topk,
                              koff=koff, kb_size=kb_size, slopes=_alibi_slopes_log2()),
            grid=(bsz, j1 - j0),
            in_specs=[pos, pos, heads(iq),
                      pl.BlockSpec((None, IDX_HEADS, tq), lambda b, j: (b, 0, j + j0)),
                      heads(q),
                      pl.BlockSpec((None, s_pad, IDX_DIM), lambda b, j: (b, 0, 0)),
                      pl.BlockSpec((A_KV_HEADS, None, s_pad, k.shape[-1]), lambda b, j: (0, b, 0, 0)),
                      pl.BlockSpec((A_KV_HEADS, None, VT_ROWS, s_pad), lambda b, j: (0, b, 0, 0))],
            out_specs=pl.BlockSpec((None, tq, A_Q), lambda b, j: (b, j, 0)),
            scratch_shapes=[pltpu.VMEM((s_pad, tq), I32), pltpu.VMEM((s_pad, tq), I32),
                            pltpu.VMEM((A_HEADS, kb_size, tq), F32)],
            out_shape=jax.ShapeDtypeStruct((bsz, (j1 - j0) * tq, A_Q), BF16),
            compiler_params=pltpu.CompilerParams(
                dimension_semantics=("arbitrary", "arbitrary"), vmem_limit_bytes=VMEM_LIMIT),
            name=f"attend_t{n_keys}",
        )(qpos_row, lim_row, iq, wi_t, q, ki, k, vt))
    return jnp.concatenate(outs, axis=1)


HGRN_CHUNK = 128
HGRN_ROW_GATHER = 16


def _hgrn2_tables(ct):
    r = np.arange(ct)
    seg = [r[:, None] >= r[None, :]]
    pair = []
    g = ct
    while g >= 2:
        mid = (r // g) * g + g // 2
        second = r >= mid
        if g < HGRN_ROW_GATHER:
            seg.append(np.where(second[:, None],
                                (r[None, :] >= mid[:, None]) & (r[None, :] <= r[:, None]),
                                (r[None, :] > r[:, None]) & (r[None, :] < mid[:, None])))
        pair.append((r[:, None] // g == r[None, :] // g) & second[:, None] & ~second[None, :])
        g //= 2
    return np.concatenate(seg, 0).astype(np.float32), np.stack(pair).astype(np.float32)


def _hgrn2_kernel(bq_ref, bf_ref, bi_ref, lb_ref, seg_ref, pair_ref, s0_ref, o_ref, sout_ref, st_scr):
    c = pl.program_id(1)
    ct = bq_ref.shape[0]
    n_lvl = pair_ref.shape[0]

    @pl.when(c == 0)
    def _():
        for h in range(B_HEADS):
            st_scr[h] = s0_ref[h].T

    lb = lb_ref[...]
    fx = bf_ref[...]
    logf = jnp.log(lb + (1.0 - lb) * jax.nn.sigmoid(fx))
    kk = (1.0 - lb) * jax.nn.sigmoid(-fx)
    bq = bq_ref[...]
    rq = bq * jax.nn.sigmoid(bq)
    rv = bi_ref[...]
    p1 = logf.astype(BF16)
    p2 = (logf - p1.astype(F32)).astype(BF16)
    seg = seg_ref[...]
    sums = jnp.dot(seg, p1, preferred_element_type=F32) + jnp.dot(seg, p2, preferred_element_type=F32)
    b_in = sums[:ct]
    b_out = b_in[ct - 1:ct] - b_in
    q_in = (rq * jnp.exp(b_in)).astype(BF16)
    k_out = (kk * jnp.exp(b_out)).astype(BF16)
    rvb = rv.astype(BF16)
    row = lax.broadcasted_iota(I32, (ct, B_KEY_DIM), 0)
    row_w = lax.broadcasted_iota(I32, (ct, B_KW), 0)
    nt = (((1,), (1,)), ((), ()))

    a = [jnp.zeros((ct, ct), F32) for _ in range(B_HEADS)]
    n_gather = 0
    for lvl in range(n_lvl):
        g = ct >> lvl
        half = g // 2
        second = (row & half) != 0
        if g >= HGRN_ROW_GATHER:
            b_mid = jnp.concatenate(
                [jnp.broadcast_to(b_in[i * g + half - 1:i * g + half], (g, B_KW)) for i in range(ct // g)],
                axis=0)
            dec = jnp.exp(jnp.where((row_w & half) != 0, b_in - b_mid, b_mid - b_in))
            n_gather += 1
        else:
            dec = jnp.exp(sums[(1 + lvl - n_gather) * ct:(2 + lvl - n_gather) * ct])
        for h in range(B_HEADS):
            ls = slice(h * B_KEY_DIM, (h + 1) * B_KEY_DIM)
            x = (jnp.where(second, rq[:, ls], kk[:, ls]) * dec[:, ls]).astype(BF16)
            a[h] = a[h] + pair_ref[lvl] * lax.dot_general(x, x, nt, preferred_element_type=F32)

    for h in range(B_HEADS):
        ls = slice(h * B_KEY_DIM, (h + 1) * B_KEY_DIM)
        st = st_scr[h]
        own = jnp.sum(rq[:, ls] * kk[:, ls], axis=1, keepdims=True)
        o_ref[:, ls] = (jnp.dot(a[h].astype(BF16), rvb[:, ls], preferred_element_type=F32)
                        + own * rv[:, ls]
                        + lax.dot_general(q_in[:, ls], st.astype(BF16), nt, preferred_element_type=F32))
        upd = lax.dot_general(rvb[:, ls], k_out[:, ls], (((0,), (0,)), ((), ())),
                              preferred_element_type=F32)
        st_scr[h] = st * jnp.exp(b_in[ct - 1:ct, ls]) + upd

    @pl.when(c == pl.num_programs(1) - 1)
    def _():
        for h in range(B_HEADS):
            sout_ref[h] = st_scr[h].T


def _hgrn2(bq, bf, bi, lb, s0):
    bsz, t, _ = bq.shape
    ct = HGRN_CHUNK
    tp = -(-t // ct) * ct
    if tp != t:
        pad = lambda a, v: jnp.pad(a, ((0, 0), (0, tp - t), (0, 0)), constant_values=v)
        bq, bf, bi = pad(bq, 0.0), pad(bf, 1e4), pad(bi, 0.0)
    seg, pair = _hgrn2_tables(ct)
    seg = jnp.asarray(seg, BF16)
    pair = jnp.asarray(pair, F32)
    tok = pl.BlockSpec((None, ct, B_KW), lambda b, c: (b, c, 0))
    s_map = (lambda b, c: (0, 0, 0, 0)) if s0.shape[0] == 1 else (lambda b, c: (b, 0, 0, 0))
    s_in = pl.BlockSpec((None, B_HEADS, B_KEY_DIM, B_VAL_DIM), s_map)
    s_out = pl.BlockSpec((None, B_HEADS, B_KEY_DIM, B_VAL_DIM), lambda b, c: (b, 0, 0, 0))
    o, s = pl.pallas_call(
        _hgrn2_kernel,
        grid=(bsz, tp // ct),
        in_specs=[tok, tok, tok, _resident(lb.shape), _resident(seg.shape), _resident(pair.shape), s_in],
        out_specs=[tok, s_out],
        out_shape=[jax.ShapeDtypeStruct((bsz, tp, B_WIDTH), F32),
                   jax.ShapeDtypeStruct((bsz, B_HEADS, B_KEY_DIM, B_VAL_DIM), F32)],
        scratch_shapes=[pltpu.VMEM((B_HEADS, B_VAL_DIM, B_KEY_DIM), F32)],
        compiler_params=pltpu.CompilerParams(
            dimension_semantics=("arbitrary", "arbitrary"), vmem_limit_bytes=VMEM_LIMIT),
        name="hgrn2",
    )(bq, bf, bi, lb, seg, pair, s0)
    return o[:, :t], s


def _mix_out_kernel(h_ref, oa_ref, orec_ref, wgate_ref, gn_ref, wa_ref, wb_ref, wo_ref,
                    g2_ref, b2_ref, wg_ref, wu_ref, wd_ref, g3_ref, b3_ref, y_ref):
    h = h_ref[...]
    gates = jnp.dot(h.astype(BF16), wgate_ref[...], preferred_element_type=F32)
    bg = gates[:, :B_WIDTH]
    ga = gates[:, B_WIDTH:B_WIDTH + D_MODEL]
    gb = gates[:, B_WIDTH + D_MODEL:]
    orec = orec_ref[...]
    gn = gn_ref[...]
    parts = []
    for head in range(B_HEADS):
        ls = slice(head * B_VAL_DIM, (head + 1) * B_VAL_DIM)
        oh = orec[:, ls]
        ms = jnp.mean(oh * oh, axis=-1, keepdims=True)
        parts.append(oh * lax.rsqrt(ms + RMS_EPS) * gn[:, ls])
    on = jnp.concatenate(parts, axis=-1)
    ob = on * (bg * jax.nn.sigmoid(bg))
    pa = jnp.dot(oa_ref[...], wa_ref[...], preferred_element_type=F32)
    pb = jnp.dot(ob.astype(BF16), wb_ref[...], preferred_element_type=F32)
    mixed = jax.nn.sigmoid(ga) * pa + jax.nn.sigmoid(gb) * pb
    mixed = jnp.dot(mixed.astype(BF16), wo_ref[...], preferred_element_type=F32)
    h2 = _layer_norm(ALPHA * h + mixed, g2_ref[...], b2_ref[...])
    y_ref[...] = _layer_norm(_swiglu_step(h2, wg_ref, wu_ref, wd_ref), g3_ref[...], b3_ref[...])


def _mix_out(h, oa, orec, wgate, gn, wa, wb, wo, g2, b2, wg, wu, wd, g3, b3, tm):
    m = h.shape[0]
    assert m % tm == 0
    row = lambda w: pl.BlockSpec((tm, w), lambda i: (i, 0))
    weights = (wgate, gn, wa, wb, wo, g2, b2, wg, wu, wd, g3, b3)
    return pl.pallas_call(
        _mix_out_kernel,
        grid=(m // tm,),
        in_specs=[row(D_MODEL), row(A_Q), row(B_WIDTH)] + [_resident(w.shape) for w in weights],
        out_specs=row(D_MODEL),
        out_shape=jax.ShapeDtypeStruct((m, D_MODEL), F32),
        compiler_params=pltpu.CompilerParams(
            dimension_semantics=("arbitrary",), vmem_limit_bytes=VMEM_LIMIT),
        name="mix_out",
    )(h, oa, orec, *weights)


def _ffn_weights(wg, wu, wd):
    return wg.astype(BF16), wu.astype(BF16), wd.astype(BF16).reshape(N_FF_CHUNKS, FF_CHUNK, D_MODEL)


def _pad_rows(a, n):
    return jnp.pad(a, ((0, 0), (0, n - a.shape[1]), (0, 0)))


FFN_IN_TILE = 512
MIX_OUT_TILE = 512


def _token_tile(m, largest):
    tm = largest
    while tm > 8 and m % tm:
        tm //= 2
    assert m % tm == 0, m
    return tm


def kernel(x_prompt, x_sample, cache_k, cache_v, cache_kidx, state_hgrn, meta, ln1_g, ln1_b, ffn1_wg, ffn1_wu, ffn1_wd, w_in, lb_param, g_norm, w_a_proj, w_b_proj, w_out, ln2_g, ln2_b, ffn2_wg, ffn2_wu, ffn2_wd, ln3_g, ln3_b):
    assert ln1_g.shape[0] == DEPTH
    bp, seq, _ = x_prompt.shape
    bs, dseq, _ = x_sample.shape
    past = cache_k.shape[2]
    l = 0
    vec = lambda a: a.reshape(1, -1).astype(F32)

    lb_all = jnp.cumsum(jax.nn.softmax(lb_param.astype(F32), axis=0), axis=0)
    lb = lb_all[l].reshape(1, B_KW)
    f1 = _ffn_weights(ffn1_wg[l], ffn1_wu[l], ffn1_wd[l])
    f2 = _ffn_weights(ffn2_wg[l], ffn2_wu[l], ffn2_wd[l])
    w_pieces, off = [], 0
    for _, width, _ in _PROJ_OUT:
        w_pieces.append(w_in[l][:, off:off + width].astype(BF16))
        off += width
    w_gate = w_in[l][:, off:].astype(BF16)
    assert w_gate.shape[1] == sum(_GATE_WIDTHS)

    kb_p = PROMPT_KEY_BLOCK
    s_p = -(-(N_META + seq) // kb_p) * kb_p
    q_cols, k_cols = _alibi_columns(s_p)

    def stage1(x):
        return _ffn_in(x, vec(ln1_g[l]), vec(ln1_b[l]), *f1, q_cols[:, None, :], w_pieces,
                       _token_tile(x.shape[0], FFN_IN_TILE))

    def stage4(p, oa, orec):
        m = p["h"].shape[0]
        return _mix_out(p["h"], oa.reshape(m, A_Q), orec.reshape(m, B_WIDTH), w_gate,
                        vec(g_norm[l]), w_a_proj[l].astype(BF16), w_b_proj[l].astype(BF16),
                        w_out[l].astype(BF16), vec(ln2_g[l]), vec(ln2_b[l]), *f2,
                        vec(ln3_g[l]), vec(ln3_b[l]), _token_tile(m, MIX_OUT_TILE))

    pm = stage1(meta.astype(F32))
    pp = stage1(x_prompt.reshape(bp * seq, D_MODEL))
    ps = stage1(x_sample.reshape(bs * dseq, D_MODEL))
    per_p = lambda a: a.reshape(bp, seq, a.shape[-1])
    per_s = lambda a: a.reshape(bs, dseq, a.shape[-1])

    def with_meta(name):
        rows = jnp.broadcast_to(pm[name][None], (bp, N_META, pm[name].shape[-1]))
        return jnp.concatenate([rows, per_p(pp[name])], axis=1)

    k_p, v_p, ki_p = with_meta("ak"), with_meta("av"), with_meta("ik")

    qpos = np.arange(seq, dtype=np.int32)
    lim = N_META + CHUNK * (qpos // CHUNK + 1)
    heads_p = lambda a: a.reshape(a.shape[0], bp, seq, a.shape[-1])
    kv_heads = lambda a: _pad_rows(a.astype(BF16), s_p).reshape(bp, s_p, A_KV_HEADS, A_HEAD_DIM)
    vt = kv_heads(v_p).transpose(2, 0, 3, 1)
    vt = jnp.concatenate([vt, jnp.ones_like(vt[:, :, :1]),
                          jnp.zeros_like(vt[:, :, :VT_ROWS - A_HEAD_DIM - 1])], axis=2)
    k_aug = jnp.concatenate([kv_heads(k_p).transpose(2, 0, 1, 3), jnp.broadcast_to(
        k_cols[None, None], (A_KV_HEADS, bp, s_p, A_HEAD_DIM))], axis=-1)
    oa_p = _attend_t(heads_p(pp["iq"]), per_p(pp["iw"]).transpose(0, 2, 1), heads_p(pp["aq"]),
                     _pad_rows(ki_p.astype(BF16), s_p), k_aug, vt, qpos, lim,
                     kb_size=kb_p, topk=min(TOPK_MAX, seq // 4), koff=N_META)

    n_all = past + dseq
    kb_s = SAMPLE_KEY_BLOCK
    s_s = -(-n_all // kb_s) * kb_s
    k_all = jnp.concatenate([cache_k[l].reshape(bs, past, A_KV).astype(BF16), per_s(ps["ak"]).astype(BF16)], axis=1)
    v_all = jnp.concatenate([cache_v[l].reshape(bs, past, A_KV).astype(BF16), per_s(ps["av"]).astype(BF16)], axis=1)
    ki_all = jnp.concatenate([cache_kidx[l].astype(BF16), per_s(ps["ik"]).astype(BF16)], axis=1)
    tokens_s = lambda a: a.transpose(1, 0, 2).reshape(bs, dseq, a.shape[0] * a.shape[-1])
    oa_s = _attend(tokens_s(ps["iq"]), per_s(ps["iw"]), tokens_s(ps["aq"][..., :A_HEAD_DIM]),
                   _pad_rows(ki_all, s_s), _pad_rows(k_all, s_s), _pad_rows(v_all, s_s),
                   past + jnp.arange(dseq, dtype=I32), jnp.full((dseq,), n_all, I32),
                   tq=dseq, kb_size=kb_s, topk=min(TOPK_MAX, n_all // 4), koff=0)

    zero_state = jnp.zeros((1, B_HEADS, B_KEY_DIM, B_VAL_DIM), F32)
    _, st_m = _hgrn2(pm["bq"][None], pm["bf"][None], pm["bi"][None], lb, zero_state)
    orec_p, st_p = _hgrn2(per_p(pp["bq"]), per_p(pp["bf"]), per_p(pp["bi"]), lb, st_m)
    orec_s, st_s = _hgrn2(per_s(ps["bq"]), per_s(ps["bf"]), per_s(ps["bi"]), lb,
                          state_hgrn[l].astype(F32))

    y_p = stage4(pp, oa_p, orec_p).reshape(bp, seq, D_MODEL)
    y_s = stage4(ps, oa_s, orec_s).reshape(bs, dseq, D_MODEL)

    kv5 = lambda a: a.reshape(1, a.shape[0], a.shape[1], A_KV_HEADS, A_HEAD_DIM)
    return (y_p, y_s, kv5(k_p), kv5(v_p), ki_p[None], st_p[None].astype(state_hgrn.dtype),
            kv5(per_s(ps["ak"])), kv5(per_s(ps["av"])), per_s(ps["ik"])[None],
            st_s[None].astype(state_hgrn.dtype))
```

```python
import functools

import jax
import jax.numpy as jnp
import numpy as np
from jax import lax
from jax.experimental import pallas as pl
from jax.experimental.pallas import tpu as pltpu

F32 = jnp.float32
BF16 = jnp.bfloat16
I32 = jnp.int32

D_MODEL = 1024
D_FF = 2816
FF_CHUNK = 256
N_FF_CHUNKS = D_FF // FF_CHUNK
CHUNK = 64
N_META = 16
A_HEADS = 8
A_KV_HEADS = 2
A_GROUP = A_HEADS // A_KV_HEADS
A_HEAD_DIM = 64
A_Q = A_HEADS * A_HEAD_DIM
A_KV = A_KV_HEADS * A_HEAD_DIM
IDX_HEADS = 8
IDX_DIM = 64
IDX_Q = IDX_HEADS * IDX_DIM
TOPK_MAX = 256
B_HEADS = 4
B_KEY_DIM = 128
B_VAL_DIM = 128
B_KW = B_HEADS * B_KEY_DIM
B_WIDTH = B_HEADS * B_VAL_DIM
DEPTH = 1
ALPHA = (2.0 * DEPTH) ** 0.25
LN_EPS = 1e-5
RMS_EPS = 1e-6
NEG_INF = -1e30
LOG2E = 1.4426950408889634
INT_MIN = -(2 ** 31)
LANES = 128
VMEM_LIMIT = 56 * 1024 * 1024


def _resident(shape):
    n = len(shape)
    return pl.BlockSpec(shape, lambda *_: (0,) * n, pipeline_mode=pl.Buffered(1))


def _layer_norm(x, g, b):
    mu = jnp.mean(x, axis=-1, keepdims=True)
    xc = x - mu
    var = jnp.mean(xc * xc, axis=-1, keepdims=True)
    return xc * lax.rsqrt(var + LN_EPS) * g + b


def _swiglu_step(x, wg_ref, wu_ref, wd_ref):
    xb = x.astype(BF16)

    acc = jnp.zeros(x.shape, F32)
    for c in range(N_FF_CHUNKS):
        cols = slice(c * FF_CHUNK, (c + 1) * FF_CHUNK)
        g = jnp.dot(xb, wg_ref[:, cols], preferred_element_type=F32)
        u = jnp.dot(xb, wu_ref[:, cols], preferred_element_type=F32)
        a = (g * jax.nn.sigmoid(g) * u).astype(BF16)
        acc = acc + jnp.dot(a, wd_ref[c], preferred_element_type=F32)
    return ALPHA * x + 0.5 * acc


_PROJ_OUT = (
    ("aq", A_Q, BF16), ("ak", A_KV, F32), ("av", A_KV, F32), ("iq", IDX_Q, BF16),
    ("ik", IDX_DIM, F32), ("iw", IDX_HEADS, F32),
    ("bq", B_KW, F32), ("bf", B_KW, F32), ("bi", B_WIDTH, F32),
)
_GATE_WIDTHS = (B_WIDTH, D_MODEL, D_MODEL)
_PROJ_SCALE = {"aq": (A_HEAD_DIM ** -0.5) * LOG2E, "iw": (IDX_HEADS ** -0.5) * (IDX_DIM ** -0.5)}
_HEAD_MAJOR = {"aq": A_HEADS, "iq": IDX_HEADS}


def _ffn_in_kernel(x_ref, g_ref, b_ref, wg_ref, wu_ref, wd_ref, qcols_ref, *rest):
    n = len(_PROJ_OUT)
    w_refs, h_ref, out_refs = rest[:n], rest[n], rest[n + 1:]
    x = x_ref[...]
    h = _layer_norm(_swiglu_step(x, wg_ref, wu_ref, wd_ref), g_ref[...], b_ref[...])
    h_ref[...] = h
    hb = h.astype(BF16)
    for (name, _, dt), w_ref, o_ref in zip(_PROJ_OUT, w_refs, out_refs):
        y = jnp.dot(hb, w_ref[...], preferred_element_type=F32)
        if name in _PROJ_SCALE:
            y = y * _PROJ_SCALE[name]
        if name in _HEAD_MAJOR:
            hd = y.shape[1] // _HEAD_MAJOR[name]
            for head in range(_HEAD_MAJOR[name]):
                yh = y[:, head * hd:(head + 1) * hd].astype(dt)
                if name == "aq":
                    yh = jnp.concatenate([yh, jnp.broadcast_to(qcols_ref[head], yh.shape)], axis=1)
                o_ref[head] = yh
        else:
            o_ref[...] = y.astype(dt)


def _ffn_in(x, ln_g, ln_b, wg, wu, wd, q_cols, w_pieces, tm):
    m = x.shape[0]
    assert m % tm == 0
    row = lambda w: pl.BlockSpec((tm, w), lambda i: (i, 0))
    in_specs = [row(D_MODEL), _resident(ln_g.shape), _resident(ln_b.shape),
                _resident(wg.shape), _resident(wu.shape), _resident(wd.shape),
                _resident(q_cols.shape)]
    in_specs += [_resident(w.shape) for w in w_pieces]
    out_shape = [jax.ShapeDtypeStruct((m, D_MODEL), F32)]
    out_specs = [row(D_MODEL)]
    for name, width, dt in _PROJ_OUT:
        if name in _HEAD_MAJOR:
            nh = _HEAD_MAJOR[name]
            hd = width // nh + (q_cols.shape[-1] if name == "aq" else 0)
            out_shape.append(jax.ShapeDtypeStruct((nh, m, hd), dt))
            out_specs.append(pl.BlockSpec((nh, tm, hd), lambda i: (0, i, 0)))
        else:
            out_shape.append(jax.ShapeDtypeStruct((m, width), dt))
            out_specs.append(row(width))
    outs = pl.pallas_call(
        _ffn_in_kernel,
        grid=(m // tm,),
        in_specs=in_specs,
        out_specs=out_specs,
        out_shape=out_shape,
        compiler_params=pltpu.CompilerParams(
            dimension_semantics=("arbitrary",), vmem_limit_bytes=VMEM_LIMIT),
        name="ffn_in",
    )(x, ln_g, ln_b, wg, wu, wd, q_cols, *w_pieces)
    res = {"h": outs[0]}
    for (name, _, _), o in zip(_PROJ_OUT, outs[1:]):
        res[name] = o
    return res


def _fold_rows(x, op, rows=8):
    while x.shape[0] > rows:
        half = x.shape[0] // 2
        x = op(x[:half], x[half:])
    return x


VT_ROWS = A_HEAD_DIM + 16
PROMPT_KEY_BLOCK = 512
KEY_SCAN_STEP = 256


def _alibi_slopes_log2():
    return tuple(float(2.0 ** (-8.0 * (i + 1) / A_HEADS)) * LOG2E for i in range(A_HEADS))


ALIBI_SPLIT = CHUNK


def _alibi_columns(n_keys):
    q_cols = np.zeros((A_HEADS, A_HEAD_DIM), np.float32)
    for h, c in enumerate(_alibi_slopes_log2()):
        for i in range(3):
            ci = float(np.asarray(c, dtype=BF16).astype(np.float32))
            q_cols[h, 2 * i], q_cols[h, 2 * i + 1] = ALIBI_SPLIT * ci, ci
            c -= ci
    ids = np.arange(n_keys)
    k_cols = np.zeros((n_keys, A_HEAD_DIM), np.float32)
    k_cols[:, 0:6:2] = (ids // ALIBI_SPLIT)[:, None]
    k_cols[:, 1:6:2] = (ids % ALIBI_SPLIT)[:, None]
    return jnp.asarray(q_cols, BF16), jnp.asarray(k_cols, BF16)


def _attend_t_kernel(qpos_ref, lim_ref, iq_ref, wi_ref, q_ref, ki_ref, k_ref, vt_ref,
                     o_ref, key_scr, tie_scr, t_scr, *, n_keys, first_qpos, topk, koff, kb_size,
                     slopes):
    tq = o_ref.shape[0]
    lim = lim_ref[...]
    qpos = qpos_ref[...]
    n_pairs = A_HEADS // 2
    blocks = [(s0, min(kb_size, n_keys - s0), pl.ds(s0, min(kb_size, n_keys - s0)))
              for s0 in range(0, n_keys, kb_size)]

    def row_ids(s0, size):
        return s0 + lax.broadcasted_iota(I32, (size, tq), 0)

    def head_pair(ref, pr):
        return ref[2 * pr:2 * pr + 2].reshape(2 * tq, ref.shape[-1])

    for s0, size, rows in blocks:
        kib = ki_ref[rows, :]
        acc = jnp.zeros((size, tq), F32)
        for pr in range(IDX_HEADS // 2):
            lg = lax.dot_general(kib, head_pair(iq_ref, pr), (((1,), (1,)), ((), ())),
                                 preferred_element_type=F32)
            for u in range(2):
                h = 2 * pr + u
                acc = acc + jnp.maximum(lg[:, u * tq:(u + 1) * tq], 0.0) * wi_ref[h:h + 1, :]
        bits = lax.bitcast_convert_type(acc, I32)
        key = bits ^ ((bits >> 31) & 0x7FFFFFFF)
        key = jnp.where(row_ids(s0, size) < lim, key, INT_MIN)
        key_scr[rows, :] = key

    def count(plane, pred):
        acc = jnp.zeros((8, tq), F32)
        for _, _, rows in blocks:
            acc = acc + _fold_rows(jnp.where(pred(plane[rows, :]), 1.0, 0.0), jnp.add)
        return jnp.sum(acc, axis=0, keepdims=True)

    def kth_largest(plane, kth, bits):
        def bit_step(i, carry):
            t, n_at_t = carry
            cand = t + lax.shift_left(jnp.int32(1), bits - 1 - i)
            cnt = count(plane, lambda key: key >= cand)
            take = cnt >= kth
            return jnp.where(take, cand, t), jnp.where(take, cnt, n_at_t)
        low = INT_MIN if bits == 32 else -(1 << (bits - 1))
        return lax.fori_loop(0, bits, bit_step,
                             (jnp.full((1, tq), low, I32), jnp.zeros((1, tq), F32)))

    t, n_ge = kth_largest(key_scr, topk, 32)
    thr = jnp.maximum(t, INT_MIN + 1)
    n_ge = jnp.where(t == INT_MIN, lim.astype(F32), n_ge)

    @pl.when(jnp.max(n_ge) > topk)
    def _():
        need = topk - count(key_scr, lambda key: key > thr)
        for s0, size, rows in blocks:
            tie_scr[rows, :] = jnp.where(key_scr[rows, :] == thr, -1 - row_ids(s0, size), INT_MIN)
        last, _ = kth_largest(tie_scr, need, key_scr.shape[0].bit_length() + 1)
        drop_q = n_ge > topk
        for s0, size, rows in blocks:
            key = key_scr[rows, :]
            drop = drop_q & (key == thr) & (-1 - row_ids(s0, size) < last)
            key_scr[rows, :] = jnp.where(drop, INT_MIN, key)

    ms = [jnp.full((1, tq), NEG_INF, F32) for _ in range(A_HEADS)]
    accs = [jnp.zeros((vt_ref.shape[1], A_GROUP * tq), F32) for _ in range(A_KV_HEADS)]
    for s0, size, rows in blocks:
        bias = jnp.where(key_scr[rows, :] >= thr, 0.0, NEG_INF)
        may_follow = s0 + size - 1 - koff > first_qpos
        if may_follow:
            ahead = jnp.minimum(qpos - (row_ids(s0, size) - koff), 0).astype(F32)
        new_m = []
        for pr in range(n_pairs):
            g = (2 * pr) // A_GROUP
            s2 = lax.dot_general(k_ref[g, rows, :], head_pair(q_ref, pr),
                                 (((1,), (1,)), ((), ())), preferred_element_type=F32)
            for u in range(2):
                h = 2 * pr + u
                s = s2[:, u * tq:(u + 1) * tq]
                if may_follow:
                    s = s + (2.0 * slopes[h]) * ahead
                s = s + bias
                t_scr[h, :size] = s
                blk_max = jnp.max(_fold_rows(s, jnp.maximum), axis=0, keepdims=True)
                new_m.append(jnp.maximum(ms[h], blk_max))
        for g in range(A_KV_HEADS):
            ps, alphas = [], []
            for h in range(g * A_GROUP, (g + 1) * A_GROUP):
                ps.append(jnp.exp2(t_scr[h, :size] - new_m[h]).astype(BF16))
                alphas.append(jnp.exp2(ms[h] - new_m[h]))
            pv = jnp.dot(vt_ref[g, :, rows], jnp.concatenate(ps, axis=1),
                         preferred_element_type=F32)
            accs[g] = jnp.concatenate(alphas, axis=1) * accs[g] + pv
        ms = new_m
    for pr in range(n_pairs):
        rows = []
        for h in (2 * pr, 2 * pr + 1):
            g, hh = divmod(h, A_GROUP)
            cols = slice(hh * tq, (hh + 1) * tq)
            rows.append(accs[g][:A_HEAD_DIM, cols] / accs[g][A_HEAD_DIM:A_HEAD_DIM + 1, cols])
        o_ref[:, 2 * pr * A_HEAD_DIM:(2 * pr + 2) * A_HEAD_DIM] = (
            jnp.concatenate(rows, axis=0).T.astype(o_ref.dtype))


def _key_operands(ki, k, v, s_pad, k_cols):
    bsz = k.shape[0]
    rows = lambda a: _pad_rows(a.astype(BF16), s_pad)
    heads = lambda a: rows(a).reshape(bsz, s_pad, A_KV_HEADS, A_HEAD_DIM)
    vt = heads(v).transpose(2, 0, 3, 1)
    vt = jnp.concatenate([vt, jnp.ones_like(vt[:, :, :1]),
                          jnp.zeros_like(vt[:, :, :VT_ROWS - A_HEAD_DIM - 1])], axis=2)
    k_aug = jnp.concatenate([heads(k).transpose(2, 0, 1, 3), jnp.broadcast_to(
        k_cols[None, None, :s_pad], (A_KV_HEADS, bsz, s_pad, A_HEAD_DIM))], axis=-1)
    return rows(ki), k_aug, vt


def _attend_t(iq, wi_t, q, ki, k, vt, qpos, lim, *, kb_size, topk, koff):
    tq = LANES
    _, bsz, nq, _ = q.shape
    s_pad = ki.shape[1]
    assert nq % tq == 0 and s_pad % kb_size == 0
    keys_of_tile = -(-lim.reshape(nq // tq, tq).max(axis=1) // KEY_SCAN_STEP) * KEY_SCAN_STEP
    assert keys_of_tile.max() <= s_pad
    runs, j0 = [], 0
    for j in range(1, len(keys_of_tile) + 1):
        if j == len(keys_of_tile) or keys_of_tile[j] != keys_of_tile[j0]:
            runs.append((j0, j, int(keys_of_tile[j0])))
            j0 = j
    qpos_row = jnp.asarray(qpos, I32).reshape(1, nq)
    lim_row = jnp.asarray(lim, I32).reshape(1, nq)
    outs = []
    for j0, j1, n_keys in runs:
        heads = lambda a: pl.BlockSpec((a.shape[0], None, tq, a.shape[-1]),
                                       lambda b, j: (0, b, j + j0, 0))
        pos = pl.BlockSpec((1, tq), lambda b, j: (0, j + j0))
        outs.append(pl.pallas_call(
            functools.partial(_attend_t_kernel, n_keys=n_keys, first_qpos=int(qpos[j0 * tq]), topk=topk,
                              koff=koff, kb_size=kb_size, slopes=_alibi_slopes_log2()),
            grid=(bsz, j1 - j0),
            in_specs=[pos, pos, heads(iq),
                      pl.BlockSpec((None, IDX_HEADS, tq), lambda b, j: (b, 0, j + j0)),
                      heads(q),
                      pl.BlockSpec((None, s_pad, IDX_DIM), lambda b, j: (b, 0, 0)),
                      pl.BlockSpec((A_KV_HEADS, None, s_pad, k.shape[-1]), lambda b, j: (0, b, 0, 0)),
                      pl.BlockSpec((A_KV_HEADS, None, VT_ROWS, s_pad), lambda b, j: (0, b, 0, 0))],
            out_specs=pl.BlockSpec((None, tq, A_Q), lambda b, j: (b, j, 0)),
            scratch_shapes=[pltpu.VMEM((s_pad, tq), I32), pltpu.VMEM((s_pad, tq), I32),
                            pltpu.VMEM((A_HEADS, kb_size, tq), F32)],
            out_shape=jax.ShapeDtypeStruct((bsz, (j1 - j0) * tq, A_Q), BF16),
            compiler_params=pltpu.CompilerParams(
                dimension_semantics=("arbitrary", "arbitrary"), vmem_limit_bytes=VMEM_LIMIT),
            name=f"attend_t{n_keys}",
        )(qpos_row, lim_row, iq, wi_t, q, ki, k, vt))
    return jnp.concatenate(outs, axis=1)


HGRN_CHUNK = 128
HGRN_ROW_GATHER = 16


def _hgrn2_tables(ct):
    r = np.arange(ct)
    seg = [r[:, None] >= r[None, :]]
    pair = []
    g = ct
    while g >= 2:
        mid = (r // g) * g + g // 2
        second = r >= mid
        if g < HGRN_ROW_GATHER:
            seg.append(np.where(second[:, None],
                                (r[None, :] >= mid[:, None]) & (r[None, :] <= r[:, None]),
                                (r[None, :] > r[:, None]) & (r[None, :] < mid[:, None])))
        pair.append((r[:, None] // g == r[None, :] // g) & second[:, None] & ~second[None, :])
        g //= 2
    return np.concatenate(seg, 0).astype(np.float32), np.stack(pair).astype(np.float32)


def _hgrn2_kernel(bq_ref, bf_ref, bi_ref, lb_ref, seg_ref, pair_ref, s0_ref, o_ref, sout_ref, st_scr):
    c = pl.program_id(1)
    ct = bq_ref.shape[0]
    n_lvl = pair_ref.shape[0]

    @pl.when(c == 0)
    def _():
        for h in range(B_HEADS):
            st_scr[h] = s0_ref[h].T

    lb = lb_ref[...]
    fx = bf_ref[...]
    logf = jnp.log(lb + (1.0 - lb) * jax.nn.sigmoid(fx))
    kk = (1.0 - lb) * jax.nn.sigmoid(-fx)
    bq = bq_ref[...]
    rq = bq * jax.nn.sigmoid(bq)
    rv = bi_ref[...]
    p1 = logf.astype(BF16)
    p2 = (logf - p1.astype(F32)).astype(BF16)
    seg = seg_ref[...]
    sums = jnp.dot(seg, p1, preferred_element_type=F32) + jnp.dot(seg, p2, preferred_element_type=F32)
    b_in = sums[:ct]
    b_out = b_in[ct - 1:ct] - b_in
    q_in = (rq * jnp.exp(b_in)).astype(BF16)
    k_out = (kk * jnp.exp(b_out)).astype(BF16)
    rvb = rv.astype(BF16)
    row = lax.broadcasted_iota(I32, (ct, B_KEY_DIM), 0)
    row_w = lax.broadcasted_iota(I32, (ct, B_KW), 0)
    nt = (((1,), (1,)), ((), ()))

    a = [jnp.zeros((ct, ct), F32) for _ in range(B_HEADS)]
    n_gather = 0
    for lvl in range(n_lvl):
        g = ct >> lvl
        half = g // 2
        second = (row & half) != 0
        if g >= HGRN_ROW_GATHER:
            b_mid = jnp.concatenate(
                [jnp.broadcast_to(b_in[i * g + half - 1:i * g + half], (g, B_KW)) for i in range(ct // g)],
                axis=0)
            dec = jnp.exp(jnp.where((row_w & half) != 0, b_in - b_mid, b_mid - b_in))
            n_gather += 1
        else:
            dec = jnp.exp(sums[(1 + lvl - n_gather) * ct:(2 + lvl - n_gather) * ct])
        for h in range(B_HEADS):
            ls = slice(h * B_KEY_DIM, (h + 1) * B_KEY_DIM)
            x = (jnp.where(second, rq[:, ls], kk[:, ls]) * dec[:, ls]).astype(BF16)
            a[h] = a[h] + pair_ref[lvl] * lax.dot_general(x, x, nt, preferred_element_type=F32)

    for h in range(B_HEADS):
        ls = slice(h * B_KEY_DIM, (h + 1) * B_KEY_DIM)
        st = st_scr[h]
        own = jnp.sum(rq[:, ls] * kk[:, ls], axis=1, keepdims=True)
        o_ref[:, ls] = (jnp.dot(a[h].astype(BF16), rvb[:, ls], preferred_element_type=F32)
                        + own * rv[:, ls]
                        + lax.dot_general(q_in[:, ls], st.astype(BF16), nt, preferred_element_type=F32))
        upd = lax.dot_general(rvb[:, ls], k_out[:, ls], (((0,), (0,)), ((), ())),
                              preferred_element_type=F32)
        st_scr[h] = st * jnp.exp(b_in[ct - 1:ct, ls]) + upd

    @pl.when(c == pl.num_programs(1) - 1)
    def _():
        for h in range(B_HEADS):
            sout_ref[h] = st_scr[h].T


def _hgrn2(bq, bf, bi, lb, s0):
    bsz, t, _ = bq.shape
    ct = HGRN_CHUNK
    tp = -(-t // ct) * ct
    if tp != t:
        pad = lambda a, v: jnp.pad(a, ((0, 0), (0, tp - t), (0, 0)), constant_values=v)
        bq, bf, bi = pad(bq, 0.0), pad(bf, 1e4), pad(bi, 0.0)
    seg, pair = _hgrn2_tables(ct)
    seg = jnp.asarray(seg, BF16)
    pair = jnp.asarray(pair, F32)
    tok = pl.BlockSpec((None, ct, B_KW), lambda b, c: (b, c, 0))
    s_map = (lambda b, c: (0, 0, 0, 0)) if s0.shape[0] == 1 else (lambda b, c: (b, 0, 0, 0))
    s_in = pl.BlockSpec((None, B_HEADS, B_KEY_DIM, B_VAL_DIM), s_map)
    s_out = pl.BlockSpec((None, B_HEADS, B_KEY_DIM, B_VAL_DIM), lambda b, c: (b, 0, 0, 0))
    o, s = pl.pallas_call(
        _hgrn2_kernel,
        grid=(bsz, tp // ct),
        in_specs=[tok, tok, tok, _resident(lb.shape), _resident(seg.shape), _resident(pair.shape), s_in],
        out_specs=[tok, s_out],
        out_shape=[jax.ShapeDtypeStruct((bsz, tp, B_WIDTH), F32),
                   jax.ShapeDtypeStruct((bsz, B_HEADS, B_KEY_DIM, B_VAL_DIM), F32)],
        scratch_shapes=[pltpu.VMEM((B_HEADS, B_VAL_DIM, B_KEY_DIM), F32)],
        compiler_params=pltpu.CompilerParams(
            dimension_semantics=("arbitrary", "arbitrary"), vmem_limit_bytes=VMEM_LIMIT),
        name="hgrn2",
    )(bq, bf, bi, lb, seg, pair, s0)
    return o[:, :t], s


def _mix_out_kernel(h_ref, oa_ref, orec_ref, wgate_ref, gn_ref, wa_ref, wb_ref, wo_ref,
                    g2_ref, b2_ref, wg_ref, wu_ref, wd_ref, g3_ref, b3_ref, y_ref):
    h = h_ref[...]
    gates = jnp.dot(h.astype(BF16), wgate_ref[...], preferred_element_type=F32)
    bg = gates[:, :B_WIDTH]
    ga = gates[:, B_WIDTH:B_WIDTH + D_MODEL]
    gb = gates[:, B_WIDTH + D_MODEL:]
    orec = orec_ref[...]
    gn = gn_ref[...]
    parts = []
    for head in range(B_HEADS):
        ls = slice(head * B_VAL_DIM, (head + 1) * B_VAL_DIM)
        oh = orec[:, ls]
        ms = jnp.mean(oh * oh, axis=-1, keepdims=True)
        parts.append(oh * lax.rsqrt(ms + RMS_EPS) * gn[:, ls])
    on = jnp.concatenate(parts, axis=-1)
    ob = on * (bg * jax.nn.sigmoid(bg))
    pa = jnp.dot(oa_ref[...], wa_ref[...], preferred_element_type=F32)
    pb = jnp.dot(ob.astype(BF16), wb_ref[...], preferred_element_type=F32)
    mixed = jax.nn.sigmoid(ga) * pa + jax.nn.sigmoid(gb) * pb
    mixed = jnp.dot(mixed.astype(BF16), wo_ref[...], preferred_element_type=F32)
    h2 = _layer_norm(ALPHA * h + mixed, g2_ref[...], b2_ref[...])
    y_ref[...] = _layer_norm(_swiglu_step(h2, wg_ref, wu_ref, wd_ref), g3_ref[...], b3_ref[...])


def _mix_out(h, oa, orec, wgate, gn, wa, wb, wo, g2, b2, wg, wu, wd, g3, b3, tm):
    m = h.shape[0]
    assert m % tm == 0
    row = lambda w: pl.BlockSpec((tm, w), lambda i: (i, 0))
    weights = (wgate, gn, wa, wb, wo, g2, b2, wg, wu, wd, g3, b3)
    return pl.pallas_call(
        _mix_out_kernel,
        grid=(m // tm,),
        in_specs=[row(D_MODEL), row(A_Q), row(B_WIDTH)] + [_resident(w.shape) for w in weights],
        out_specs=row(D_MODEL),
        out_shape=jax.ShapeDtypeStruct((m, D_MODEL), F32),
        compiler_params=pltpu.CompilerParams(
            dimension_semantics=("arbitrary",), vmem_limit_bytes=VMEM_LIMIT),
        name="mix_out",
    )(h, oa, orec, *weights)


def _ffn_weights(wg, wu, wd):
    return wg.astype(BF16), wu.astype(BF16), wd.astype(BF16).reshape(N_FF_CHUNKS, FF_CHUNK, D_MODEL)


def _pad_rows(a, n):
    return jnp.pad(a, ((0, 0), (0, n - a.shape[1]), (0, 0)))


FFN_IN_TILE = 512
MIX_OUT_TILE = 512


def _token_tile(m, largest):
    tm = largest
    while tm > 8 and m % tm:
        tm //= 2
    assert m % tm == 0, m
    return tm


def kernel(x_prompt, x_sample, cache_k, cache_v, cache_kidx, state_hgrn, meta, ln1_g, ln1_b, ffn1_wg, ffn1_wu, ffn1_wd, w_in, lb_param, g_norm, w_a_proj, w_b_proj, w_out, ln2_g, ln2_b, ffn2_wg, ffn2_wu, ffn2_wd, ln3_g, ln3_b):
    assert ln1_g.shape[0] == DEPTH
    bp, seq, _ = x_prompt.shape
    bs, dseq, _ = x_sample.shape
    past = cache_k.shape[2]
    l = 0
    vec = lambda a: a.reshape(1, -1).astype(F32)

    lb_all = jnp.cumsum(jax.nn.softmax(lb_param.astype(F32), axis=0), axis=0)
    lb = lb_all[l].reshape(1, B_KW)
    f1 = _ffn_weights(ffn1_wg[l], ffn1_wu[l], ffn1_wd[l])
    f2 = _ffn_weights(ffn2_wg[l], ffn2_wu[l], ffn2_wd[l])
    w_pieces, off = [], 0
    for _, width, _ in _PROJ_OUT:
        w_pieces.append(w_in[l][:, off:off + width].astype(BF16))
        off += width
    w_gate = w_in[l][:, off:].astype(BF16)
    assert w_gate.shape[1] == sum(_GATE_WIDTHS)

    kb_p = PROMPT_KEY_BLOCK
    s_p = -(-(N_META + seq) // kb_p) * kb_p
    q_cols, k_cols = _alibi_columns(s_p)

    def stage1(x):
        return _ffn_in(x, vec(ln1_g[l]), vec(ln1_b[l]), *f1, q_cols[:, None, :], w_pieces,
                       _token_tile(x.shape[0], FFN_IN_TILE))

    def stage4(p, oa, orec):
        m = p["h"].shape[0]
        return _mix_out(p["h"], oa.reshape(m, A_Q), orec.reshape(m, B_WIDTH), w_gate,
                        vec(g_norm[l]), w_a_proj[l].astype(BF16), w_b_proj[l].astype(BF16),
                        w_out[l].astype(BF16), vec(ln2_g[l]), vec(ln2_b[l]), *f2,
                        vec(ln3_g[l]), vec(ln3_b[l]), _token_tile(m, MIX_OUT_TILE))

    pm = stage1(meta.astype(F32))
    pp = stage1(x_prompt.reshape(bp * seq, D_MODEL))
    ps = stage1(x_sample.reshape(bs * dseq, D_MODEL))
    per_p = lambda a: a.reshape(bp, seq, a.shape[-1])
    per_s = lambda a: a.reshape(bs, dseq, a.shape[-1])

    def with_meta(name):
        rows = jnp.broadcast_to(pm[name][None], (bp, N_META, pm[name].shape[-1]))
        return jnp.concatenate([rows, per_p(pp[name])], axis=1)

    k_p, v_p, ki_p = with_meta("ak"), with_meta("av"), with_meta("ik")

    qpos = np.arange(seq, dtype=np.int32)
    lim = N_META + CHUNK * (qpos // CHUNK + 1)
    heads_p = lambda a: a.reshape(a.shape[0], bp, seq, a.shape[-1])
    oa_p = _attend_t(heads_p(pp["iq"]), per_p(pp["iw"]).transpose(0, 2, 1), heads_p(pp["aq"]),
                     *_key_operands(ki_p, k_p, v_p, s_p, k_cols), qpos, lim,
                     kb_size=kb_p, topk=min(TOPK_MAX, seq // 4), koff=N_META)

    assert dseq <= LANES
    n_all = past + dseq
    s_s = -(-n_all // kb_p) * kb_p
    k_all = jnp.concatenate([cache_k[l].reshape(bs, past, A_KV), per_s(ps["ak"])], axis=1)
    v_all = jnp.concatenate([cache_v[l].reshape(bs, past, A_KV), per_s(ps["av"])], axis=1)
    ki_all = jnp.concatenate([cache_kidx[l], per_s(ps["ik"])], axis=1)
    lane_pad = lambda a: jnp.pad(a, [(0, 0)] * (a.ndim - 2) + [(0, LANES - dseq), (0, 0)])
    heads_s = lambda a: lane_pad(a.reshape(a.shape[0], bs, dseq, a.shape[-1]))
    oa_s = _attend_t(heads_s(ps["iq"]), lane_pad(per_s(ps["iw"])).transpose(0, 2, 1), heads_s(ps["aq"]),
                     *_key_operands(ki_all, k_all, v_all, s_s, k_cols),
                     past + np.arange(LANES, dtype=np.int32),
                     np.where(np.arange(LANES) < dseq, n_all, 0).astype(np.int32),
                     kb_size=kb_p, topk=min(TOPK_MAX, n_all // 4), koff=0)[:, :dseq]

    zero_state = jnp.zeros((1, B_HEADS, B_KEY_DIM, B_VAL_DIM), F32)
    _, st_m = _hgrn2(pm["bq"][None], pm["bf"][None], pm["bi"][None], lb, zero_state)
    orec_p, st_p = _hgrn2(per_p(pp["bq"]), per_p(pp["bf"]), per_p(pp["bi"]), lb, st_m)
    orec_s, st_s = _hgrn2(per_s(ps["bq"]), per_s(ps["bf"]), per_s(ps["bi"]), lb,
                          state_hgrn[l].astype(F32))

    y_p = stage4(pp, oa_p, orec_p).reshape(bp, seq, D_MODEL)
    y_s = stage4(ps, oa_s, orec_s).reshape(bs, dseq, D_MODEL)

    kv5 = lambda a: a.reshape(1, a.shape[0], a.shape[1], A_KV_HEADS, A_HEAD_DIM)
    return (y_p, y_s, kv5(k_p), kv5(v_p), ki_p[None], st_p[None].astype(state_hgrn.dtype),
            kv5(per_s(ps["ak"])), kv5(per_s(ps["av"])), per_s(ps["ik"])[None],
            st_s[None].astype(state_hgrn.dtype))
```

```python
import functools

import jax
import jax.numpy as jnp
import numpy as np
from jax import lax
from jax.experimental import pallas as pl
from jax.experimental.pallas import tpu as pltpu

F32 = jnp.float32
BF16 = jnp.bfloat16
I32 = jnp.int32

D_MODEL = 1024
D_FF = 2816
FF_CHUNK = 256
N_FF_CHUNKS = D_FF // FF_CHUNK
CHUNK = 64
N_META = 16
A_HEADS = 8
A_KV_HEADS = 2
A_GROUP = A_HEADS // A_KV_HEADS
A_HEAD_DIM = 64
A_Q = A_HEADS * A_HEAD_DIM
A_KV = A_KV_HEADS * A_HEAD_DIM
IDX_HEADS = 8
IDX_DIM = 64
IDX_Q = IDX_HEADS * IDX_DIM
TOPK_MAX = 256
B_HEADS = 4
B_KEY_DIM = 128
B_VAL_DIM = 128
B_KW = B_HEADS * B_KEY_DIM
B_WIDTH = B_HEADS * B_VAL_DIM
DEPTH = 1
ALPHA = (2.0 * DEPTH) ** 0.25
LN_EPS = 1e-5
RMS_EPS = 1e-6
NEG_INF = -1e30
LOG2E = 1.4426950408889634
INT_MIN = -(2 ** 31)
LANES = 128
VMEM_LIMIT = 56 * 1024 * 1024


def _resident(shape):
    n = len(shape)
    return pl.BlockSpec(shape, lambda *_: (0,) * n, pipeline_mode=pl.Buffered(1))


def _layer_norm(x, g, b):
    mu = jnp.mean(x, axis=-1, keepdims=True)
    xc = x - mu
    var = jnp.mean(xc * xc, axis=-1, keepdims=True)
    return xc * lax.rsqrt(var + LN_EPS) * g + b


def _swiglu_step(x, wg_ref, wu_ref, wd_ref):
    xb = x.astype(BF16)

    acc = jnp.zeros(x.shape, F32)
    for c in range(N_FF_CHUNKS):
        cols = slice(c * FF_CHUNK, (c + 1) * FF_CHUNK)
        g = jnp.dot(xb, wg_ref[:, cols], preferred_element_type=F32)
        u = jnp.dot(xb, wu_ref[:, cols], preferred_element_type=F32)
        a = (g * jax.nn.sigmoid(g) * u).astype(BF16)
        acc = acc + jnp.dot(a, wd_ref[c], preferred_element_type=F32)
    return ALPHA * x + 0.5 * acc


_PROJ_OUT = (
    ("aq", A_Q, BF16), ("ak", A_KV, F32), ("av", A_KV, F32), ("iq", IDX_Q, BF16),
    ("ik", IDX_DIM, F32), ("iw", IDX_HEADS, F32),
    ("bq", B_KW, F32), ("bf", B_KW, F32), ("bi", B_WIDTH, F32),
)
_GATE_WIDTHS = (B_WIDTH, D_MODEL, D_MODEL)
_PROJ_SCALE = {"aq": (A_HEAD_DIM ** -0.5) * LOG2E, "iw": (IDX_HEADS ** -0.5) * (IDX_DIM ** -0.5)}
_HEAD_MAJOR = {"aq": A_HEADS, "iq": IDX_HEADS}


def _ffn_in_kernel(x_ref, g_ref, b_ref, wg_ref, wu_ref, wd_ref, qcols_ref, *rest):
    n = len(_PROJ_OUT)
    w_refs, h_ref, out_refs = rest[:n], rest[n], rest[n + 1:]
    x = x_ref[...]
    h = _layer_norm(_swiglu_step(x, wg_ref, wu_ref, wd_ref), g_ref[...], b_ref[...])
    h_ref[...] = h
    hb = h.astype(BF16)
    for (name, _, dt), w_ref, o_ref in zip(_PROJ_OUT, w_refs, out_refs):
        y = jnp.dot(hb, w_ref[...], preferred_element_type=F32)
        if name in _PROJ_SCALE:
            y = y * _PROJ_SCALE[name]
        if name in _HEAD_MAJOR:
            hd = y.shape[1] // _HEAD_MAJOR[name]
            for head in range(_HEAD_MAJOR[name]):
                yh = y[:, head * hd:(head + 1) * hd].astype(dt)
                if name == "aq":
                    yh = jnp.concatenate([yh, jnp.broadcast_to(qcols_ref[head], yh.shape)], axis=1)
                o_ref[head] = yh
        else:
            o_ref[...] = y.astype(dt)


def _ffn_in(x, ln_g, ln_b, wg, wu, wd, q_cols, w_pieces, tm):
    m = x.shape[0]
    assert m % tm == 0
    row = lambda w: pl.BlockSpec((tm, w), lambda i: (i, 0))
    in_specs = [row(D_MODEL), _resident(ln_g.shape), _resident(ln_b.shape),
                _resident(wg.shape), _resident(wu.shape), _resident(wd.shape),
                _resident(q_cols.shape)]
    in_specs += [_resident(w.shape) for w in w_pieces]
    out_shape = [jax.ShapeDtypeStruct((m, D_MODEL), F32)]
    out_specs = [row(D_MODEL)]
    for name, width, dt in _PROJ_OUT:
        if name in _HEAD_MAJOR:
            nh = _HEAD_MAJOR[name]
            hd = width // nh + (q_cols.shape[-1] if name == "aq" else 0)
            out_shape.append(jax.ShapeDtypeStruct((nh, m, hd), dt))
            out_specs.append(pl.BlockSpec((nh, tm, hd), lambda i: (0, i, 0)))
        else:
            out_shape.append(jax.ShapeDtypeStruct((m, width), dt))
            out_specs.append(row(width))
    outs = pl.pallas_call(
        _ffn_in_kernel,
        grid=(m // tm,),
        in_specs=in_specs,
        out_specs=out_specs,
        out_shape=out_shape,
        compiler_params=pltpu.CompilerParams(
            dimension_semantics=("arbitrary",), vmem_limit_bytes=VMEM_LIMIT),
        name="ffn_in",
    )(x, ln_g, ln_b, wg, wu, wd, q_cols, *w_pieces)
    res = {"h": outs[0]}
    for (name, _, _), o in zip(_PROJ_OUT, outs[1:]):
        res[name] = o
    return res


def _fold_rows(x, op, rows=8):
    while x.shape[0] > rows:
        half = x.shape[0] // 2
        x = op(x[:half], x[half:])
    return x


VT_ROWS = A_HEAD_DIM + 16
PROMPT_KEY_BLOCK = 512
KEY_SCAN_STEP = 256
TIE_PEEL_MAX = 4


def _alibi_slopes_log2():
    return tuple(float(2.0 ** (-8.0 * (i + 1) / A_HEADS)) * LOG2E for i in range(A_HEADS))


ALIBI_SPLIT = CHUNK


def _alibi_columns(n_keys):
    q_cols = np.zeros((A_HEADS, A_HEAD_DIM), np.float32)
    for h, c in enumerate(_alibi_slopes_log2()):
        for i in range(3):
            ci = float(np.asarray(c, dtype=BF16).astype(np.float32))
            q_cols[h, 2 * i], q_cols[h, 2 * i + 1] = ALIBI_SPLIT * ci, ci
            c -= ci
    ids = np.arange(n_keys)
    k_cols = np.zeros((n_keys, A_HEAD_DIM), np.float32)
    k_cols[:, 0:6:2] = (ids // ALIBI_SPLIT)[:, None]
    k_cols[:, 1:6:2] = (ids % ALIBI_SPLIT)[:, None]
    return jnp.asarray(q_cols, BF16), jnp.asarray(k_cols, BF16)


def _attend_t_kernel(qpos_ref, lim_ref, iq_ref, wi_ref, q_ref, ki_ref, k_ref, vt_ref,
                     o_ref, key_scr, tie_scr, t_scr, *, n_keys, first_qpos, topk, koff, kb_size,
                     slopes):
    tq = o_ref.shape[0]
    lim = lim_ref[...]
    qpos = qpos_ref[...]
    n_pairs = A_HEADS // 2
    blocks = [(s0, min(kb_size, n_keys - s0), pl.ds(s0, min(kb_size, n_keys - s0)))
              for s0 in range(0, n_keys, kb_size)]

    def row_ids(s0, size):
        return s0 + lax.broadcasted_iota(I32, (size, tq), 0)

    def head_pair(ref, pr):
        return ref[2 * pr:2 * pr + 2].reshape(2 * tq, ref.shape[-1])

    for s0, size, rows in blocks:
        kib = ki_ref[rows, :]
        acc = jnp.zeros((size, tq), F32)
        for pr in range(IDX_HEADS // 2):
            lg = lax.dot_general(kib, head_pair(iq_ref, pr), (((1,), (1,)), ((), ())),
                                 preferred_element_type=F32)
            for u in range(2):
                h = 2 * pr + u
                acc = acc + jnp.maximum(lg[:, u * tq:(u + 1) * tq], 0.0) * wi_ref[h:h + 1, :]
        bits = lax.bitcast_convert_type(acc, I32)
        key = bits ^ ((bits >> 31) & 0x7FFFFFFF)
        key = jnp.where(row_ids(s0, size) < lim, key, INT_MIN)
        key_scr[rows, :] = key

    def count(plane, pred):
        acc = jnp.zeros((8, tq), F32)
        for _, _, rows in blocks:
            acc = acc + _fold_rows(jnp.where(pred(plane[rows, :]), 1.0, 0.0), jnp.add)
        return jnp.sum(acc, axis=0, keepdims=True)

    def kth_largest(plane, kth, bits):
        def bit_step(i, carry):
            t, n_at_t = carry
            cand = t + lax.shift_left(jnp.int32(1), bits - 1 - i)
            cnt = count(plane, lambda key: key >= cand)
            take = cnt >= kth
            return jnp.where(take, cand, t), jnp.where(take, cnt, n_at_t)
        low = INT_MIN if bits == 32 else -(1 << (bits - 1))
        return lax.fori_loop(0, bits, bit_step,
                             (jnp.full((1, tq), low, I32), jnp.zeros((1, tq), F32)))

    t, n_ge = kth_largest(key_scr, topk, 32)
    thr = jnp.maximum(t, INT_MIN + 1)
    n_ge = jnp.where(t == INT_MIN, lim.astype(F32), n_ge)

    surplus = jnp.maximum(n_ge - topk, 0.0)
    max_surplus = jnp.max(surplus)

    @pl.when((max_surplus > 0.0) & (max_surplus <= TIE_PEEL_MAX))
    def _():
        def peel(_, left):
            worst = jnp.full((8, tq), -1, I32)
            for s0, size, rows in blocks:
                ids = jnp.where(key_scr[rows, :] == thr, row_ids(s0, size), -1)
                worst = jnp.maximum(worst, _fold_rows(ids, jnp.maximum))
            worst = jnp.where(left > 0.0, jnp.max(worst, axis=0, keepdims=True), -1)
            for s0, size, rows in blocks:
                key_scr[rows, :] = jnp.where(row_ids(s0, size) == worst, INT_MIN, key_scr[rows, :])
            return left - 1.0
        lax.fori_loop(0, max_surplus.astype(I32), peel, surplus)

    @pl.when(max_surplus > TIE_PEEL_MAX)
    def _():
        need = topk - count(key_scr, lambda key: key > thr)
        for s0, size, rows in blocks:
            tie_scr[rows, :] = jnp.where(key_scr[rows, :] == thr, -1 - row_ids(s0, size), INT_MIN)
        last, _ = kth_largest(tie_scr, need, key_scr.shape[0].bit_length() + 1)
        drop_q = n_ge > topk
        for s0, size, rows in blocks:
            key = key_scr[rows, :]
            drop = drop_q & (key == thr) & (-1 - row_ids(s0, size) < last)
            key_scr[rows, :] = jnp.where(drop, INT_MIN, key)

    ms = [jnp.full((1, tq), NEG_INF, F32) for _ in range(A_HEADS)]
    accs = [jnp.zeros((vt_ref.shape[1], A_GROUP * tq), F32) for _ in range(A_KV_HEADS)]
    for s0, size, rows in blocks:
        bias = jnp.where(key_scr[rows, :] >= thr, 0.0, NEG_INF)
        may_follow = s0 + size - 1 - koff > first_qpos
        if may_follow:
            ahead = jnp.minimum(qpos - (row_ids(s0, size) - koff), 0).astype(F32)
        new_m = []
        for pr in range(n_pairs):
            g = (2 * pr) // A_GROUP
            s2 = lax.dot_general(k_ref[g, rows, :], head_pair(q_ref, pr),
                                 (((1,), (1,)), ((), ())), preferred_element_type=F32)
            for u in range(2):
                h = 2 * pr + u
                s = s2[:, u * tq:(u + 1) * tq]
                if may_follow:
                    s = s + (2.0 * slopes[h]) * ahead
                s = s + bias
                t_scr[h, :size] = s
                blk_max = jnp.max(_fold_rows(s, jnp.maximum), axis=0, keepdims=True)
                new_m.append(jnp.maximum(ms[h], blk_max))
        for g in range(A_KV_HEADS):
            ps, alphas = [], []
            for h in range(g * A_GROUP, (g + 1) * A_GROUP):
                ps.append(jnp.exp2(t_scr[h, :size] - new_m[h]).astype(BF16))
                alphas.append(jnp.exp2(ms[h] - new_m[h]))
            pv = jnp.dot(vt_ref[g, :, rows], jnp.concatenate(ps, axis=1),
                         preferred_element_type=F32)
            accs[g] = jnp.concatenate(alphas, axis=1) * accs[g] + pv
        ms = new_m
    for pr in range(n_pairs):
        rows = []
        for h in (2 * pr, 2 * pr + 1):
            g, hh = divmod(h, A_GROUP)
            cols = slice(hh * tq, (hh + 1) * tq)
            rows.append(accs[g][:A_HEAD_DIM, cols] / accs[g][A_HEAD_DIM:A_HEAD_DIM + 1, cols])
        o_ref[:, 2 * pr * A_HEAD_DIM:(2 * pr + 2) * A_HEAD_DIM] = (
            jnp.concatenate(rows, axis=0).T.astype(o_ref.dtype))


def _key_operands(ki, k, v, s_pad, k_cols):
    bsz = k.shape[0]
    rows = lambda a: _pad_rows(a.astype(BF16), s_pad)
    heads = lambda a: rows(a).reshape(bsz, s_pad, A_KV_HEADS, A_HEAD_DIM)
    vt = heads(v).transpose(2, 0, 3, 1)
    vt = jnp.concatenate([vt, jnp.ones_like(vt[:, :, :1]),
                          jnp.zeros_like(vt[:, :, :VT_ROWS - A_HEAD_DIM - 1])], axis=2)
    k_aug = jnp.concatenate([heads(k).transpose(2, 0, 1, 3), jnp.broadcast_to(
        k_cols[None, None, :s_pad], (A_KV_HEADS, bsz, s_pad, A_HEAD_DIM))], axis=-1)
    return rows(ki), k_aug, vt


def _attend_t(iq, wi_t, q, ki, k, vt, qpos, lim, *, kb_size, topk, koff):
    tq = LANES
    _, bsz, nq, _ = q.shape
    s_pad = ki.shape[1]
    assert nq % tq == 0 and s_pad % kb_size == 0
    keys_of_tile = -(-lim.reshape(nq // tq, tq).max(axis=1) // KEY_SCAN_STEP) * KEY_SCAN_STEP
    assert keys_of_tile.max() <= s_pad
    runs, j0 = [], 0
    for j in range(1, len(keys_of_tile) + 1):
        if j == len(keys_of_tile) or keys_of_tile[j] != keys_of_tile[j0]:
            runs.append((j0, j, int(keys_of_tile[j0])))
            j0 = j
    qpos_row = jnp.asarray(qpos, I32).reshape(1, nq)
    lim_row = jnp.asarray(lim, I32).reshape(1, nq)
    outs = []
    for j0, j1, n_keys in runs:
        heads = lambda a: pl.BlockSpec((a.shape[0], None, tq, a.shape[-1]),
                                       lambda b, j: (0, b, j + j0, 0))
        pos = pl.BlockSpec((1, tq), lambda b, j: (0, j + j0))
        outs.append(pl.pallas_call(
            functools.partial(_attend_t_kernel, n_keys=n_keys, first_qpos=int(qpos[j0 * tq]), topk=topk,
                              koff=koff, kb_size=kb_size, slopes=_alibi_slopes_log2()),
            grid=(bsz, j1 - j0),
            in_specs=[pos, pos, heads(iq),
                      pl.BlockSpec((None, IDX_HEADS, tq), lambda b, j: (b, 0, j + j0)),
                      heads(q),
                      pl.BlockSpec((None, s_pad, IDX_DIM), lambda b, j: (b, 0, 0)),
                      pl.BlockSpec((A_KV_HEADS, None, s_pad, k.shape[-1]), lambda b, j: (0, b, 0, 0)),
                      pl.BlockSpec((A_KV_HEADS, None, VT_ROWS, s_pad), lambda b, j: (0, b, 0, 0))],
            out_specs=pl.BlockSpec((None, tq, A_Q), lambda b, j: (b, j, 0)),
            scratch_shapes=[pltpu.VMEM((s_pad, tq), I32), pltpu.VMEM((s_pad, tq), I32),
                            pltpu.VMEM((A_HEADS, kb_size, tq), F32)],
            out_shape=jax.ShapeDtypeStruct((bsz, (j1 - j0) * tq, A_Q), BF16),
            compiler_params=pltpu.CompilerParams(
                dimension_semantics=("arbitrary", "arbitrary"), vmem_limit_bytes=VMEM_LIMIT),
            name=f"attend_t{n_keys}",
        )(qpos_row, lim_row, iq, wi_t, q, ki, k, vt))
    return jnp.concatenate(outs, axis=1)


HGRN_CHUNK = 128
HGRN_ROW_GATHER = 16


def _hgrn2_tables(ct):
    r = np.arange(ct)
    seg = [r[:, None] >= r[None, :]]
    pair = []
    g = ct
    while g >= 2:
        mid = (r // g) * g + g // 2
        second = r >= mid
        if g < HGRN_ROW_GATHER:
            seg.append(np.where(second[:, None],
                                (r[None, :] >= mid[:, None]) & (r[None, :] <= r[:, None]),
                                (r[None, :] > r[:, None]) & (r[None, :] < mid[:, None])))
        pair.append((r[:, None] // g == r[None, :] // g) & second[:, None] & ~second[None, :])
        g //= 2
    return np.concatenate(seg, 0).astype(np.float32), np.stack(pair).astype(np.float32)


def _hgrn2_kernel(bq_ref, bf_ref, bi_ref, lb_ref, seg_ref, pair_ref, s0_ref, o_ref, sout_ref, st_scr):
    c = pl.program_id(1)
    ct = bq_ref.shape[0]
    n_lvl = pair_ref.shape[0]

    @pl.when(c == 0)
    def _():
        for h in range(B_HEADS):
            st_scr[h] = s0_ref[h].T

    lb = lb_ref[...]
    fx = bf_ref[...]
    logf = jnp.log(lb + (1.0 - lb) * jax.nn.sigmoid(fx))
    kk = (1.0 - lb) * jax.nn.sigmoid(-fx)
    bq = bq_ref[...]
    rq = bq * jax.nn.sigmoid(bq)
    rv = bi_ref[...]
    p1 = logf.astype(BF16)
    p2 = (logf - p1.astype(F32)).astype(BF16)
    seg = seg_ref[...]
    sums = jnp.dot(seg, p1, preferred_element_type=F32) + jnp.dot(seg, p2, preferred_element_type=F32)
    b_in = sums[:ct]
    b_out = b_in[ct - 1:ct] - b_in
    q_in = (rq * jnp.exp(b_in)).astype(BF16)
    k_out = (kk * jnp.exp(b_out)).astype(BF16)
    rvb = rv.astype(BF16)
    row = lax.broadcasted_iota(I32, (ct, B_KEY_DIM), 0)
    row_w = lax.broadcasted_iota(I32, (ct, B_KW), 0)
    nt = (((1,), (1,)), ((), ()))

    a = [jnp.zeros((ct, ct), F32) for _ in range(B_HEADS)]
    n_gather = 0
    for lvl in range(n_lvl):
        g = ct >> lvl
        half = g // 2
        second = (row & half) != 0
        if g >= HGRN_ROW_GATHER:
            b_mid = jnp.concatenate(
                [jnp.broadcast_to(b_in[i * g + half - 1:i * g + half], (g, B_KW)) for i in range(ct // g)],
                axis=0)
            dec = jnp.exp(jnp.where((row_w & half) != 0, b_in - b_mid, b_mid - b_in))
            n_gather += 1
        else:
            dec = jnp.exp(sums[(1 + lvl - n_gather) * ct:(2 + lvl - n_gather) * ct])
        for h in range(B_HEADS):
            ls = slice(h * B_KEY_DIM, (h + 1) * B_KEY_DIM)
            x = (jnp.where(second, rq[:, ls], kk[:, ls]) * dec[:, ls]).astype(BF16)
            a[h] = a[h] + pair_ref[lvl] * lax.dot_general(x, x, nt, preferred_element_type=F32)

    for h in range(B_HEADS):
        ls = slice(h * B_KEY_DIM, (h + 1) * B_KEY_DIM)
        st = st_scr[h]
        own = jnp.sum(rq[:, ls] * kk[:, ls], axis=1, keepdims=True)
        o_ref[:, ls] = (jnp.dot(a[h].astype(BF16), rvb[:, ls], preferred_element_type=F32)
                        + own * rv[:, ls]
                        + lax.dot_general(q_in[:, ls], st.astype(BF16), nt, preferred_element_type=F32))
        upd = lax.dot_general(rvb[:, ls], k_out[:, ls], (((0,), (0,)), ((), ())),
                              preferred_element_type=F32)
        st_scr[h] = st * jnp.exp(b_in[ct - 1:ct, ls]) + upd

    @pl.when(c == pl.num_programs(1) - 1)
    def _():
        for h in range(B_HEADS):
            sout_ref[h] = st_scr[h].T


def _hgrn2(bq, bf, bi, lb, s0):
    bsz, t, _ = bq.shape
    ct = HGRN_CHUNK
    tp = -(-t // ct) * ct
    if tp != t:
        pad = lambda a, v: jnp.pad(a, ((0, 0), (0, tp - t), (0, 0)), constant_values=v)
        bq, bf, bi = pad(bq, 0.0), pad(bf, 1e4), pad(bi, 0.0)
    seg, pair = _hgrn2_tables(ct)
    seg = jnp.asarray(seg, BF16)
    pair = jnp.asarray(pair, F32)
    tok = pl.BlockSpec((None, ct, B_KW), lambda b, c: (b, c, 0))
    s_map = (lambda b, c: (0, 0, 0, 0)) if s0.shape[0] == 1 else (lambda b, c: (b, 0, 0, 0))
    s_in = pl.BlockSpec((None, B_HEADS, B_KEY_DIM, B_VAL_DIM), s_map)
    s_out = pl.BlockSpec((None, B_HEADS, B_KEY_DIM, B_VAL_DIM), lambda b, c: (b, 0, 0, 0))
    o, s = pl.pallas_call(
        _hgrn2_kernel,
        grid=(bsz, tp // ct),
        in_specs=[tok, tok, tok, _resident(lb.shape), _resident(seg.shape), _resident(pair.shape), s_in],
        out_specs=[tok, s_out],
        out_shape=[jax.ShapeDtypeStruct((bsz, tp, B_WIDTH), F32),
                   jax.ShapeDtypeStruct((bsz, B_HEADS, B_KEY_DIM, B_VAL_DIM), F32)],
        scratch_shapes=[pltpu.VMEM((B_HEADS, B_VAL_DIM, B_KEY_DIM), F32)],
        compiler_params=pltpu.CompilerParams(
            dimension_semantics=("arbitrary", "arbitrary"), vmem_limit_bytes=VMEM_LIMIT),
        name="hgrn2",
    )(bq, bf, bi, lb, seg, pair, s0)
    return o[:, :t], s


def _mix_out_kernel(h_ref, oa_ref, orec_ref, wgate_ref, gn_ref, wa_ref, wb_ref, wo_ref,
                    g2_ref, b2_ref, wg_ref, wu_ref, wd_ref, g3_ref, b3_ref, y_ref):
    h = h_ref[...]
    gates = jnp.dot(h.astype(BF16), wgate_ref[...], preferred_element_type=F32)
    bg = gates[:, :B_WIDTH]
    ga = gates[:, B_WIDTH:B_WIDTH + D_MODEL]
    gb = gates[:, B_WIDTH + D_MODEL:]
    orec = orec_ref[...]
    gn = gn_ref[...]
    parts = []
    for head in range(B_HEADS):
        ls = slice(head * B_VAL_DIM, (head + 1) * B_VAL_DIM)
        oh = orec[:, ls]
        ms = jnp.mean(oh * oh, axis=-1, keepdims=True)
        parts.append(oh * lax.rsqrt(ms + RMS_EPS) * gn[:, ls])
    on = jnp.concatenate(parts, axis=-1)
    ob = on * (bg * jax.nn.sigmoid(bg))
    pa = jnp.dot(oa_ref[...], wa_ref[...], preferred_element_type=F32)
    pb = jnp.dot(ob.astype(BF16), wb_ref[...], preferred_element_type=F32)
    mixed = jax.nn.sigmoid(ga) * pa + jax.nn.sigmoid(gb) * pb
    mixed = jnp.dot(mixed.astype(BF16), wo_ref[...], preferred_element_type=F32)
    h2 = _layer_norm(ALPHA * h + mixed, g2_ref[...], b2_ref[...])
    y_ref[...] = _layer_norm(_swiglu_step(h2, wg_ref, wu_ref, wd_ref), g3_ref[...], b3_ref[...])


def _mix_out(h, oa, orec, wgate, gn, wa, wb, wo, g2, b2, wg, wu, wd, g3, b3, tm):
    m = h.shape[0]
    assert m % tm == 0
    row = lambda w: pl.BlockSpec((tm, w), lambda i: (i, 0))
    weights = (wgate, gn, wa, wb, wo, g2, b2, wg, wu, wd, g3, b3)
    return pl.pallas_call(
        _mix_out_kernel,
        grid=(m // tm,),
        in_specs=[row(D_MODEL), row(A_Q), row(B_WIDTH)] + [_resident(w.shape) for w in weights],
        out_specs=row(D_MODEL),
        out_shape=jax.ShapeDtypeStruct((m, D_MODEL), F32),
        compiler_params=pltpu.CompilerParams(
            dimension_semantics=("arbitrary",), vmem_limit_bytes=VMEM_LIMIT),
        name="mix_out",
    )(h, oa, orec, *weights)


def _ffn_weights(wg, wu, wd):
    return wg.astype(BF16), wu.astype(BF16), wd.astype(BF16).reshape(N_FF_CHUNKS, FF_CHUNK, D_MODEL)


def _pad_rows(a, n):
    return jnp.pad(a, ((0, 0), (0, n - a.shape[1]), (0, 0)))


FFN_IN_TILE = 512
MIX_OUT_TILE = 512


def _token_tile(m, largest):
    tm = largest
    while tm > 8 and m % tm:
        tm //= 2
    assert m % tm == 0, m
    return tm


def kernel(x_prompt, x_sample, cache_k, cache_v, cache_kidx, state_hgrn, meta, ln1_g, ln1_b, ffn1_wg, ffn1_wu, ffn1_wd, w_in, lb_param, g_norm, w_a_proj, w_b_proj, w_out, ln2_g, ln2_b, ffn2_wg, ffn2_wu, ffn2_wd, ln3_g, ln3_b):
    assert ln1_g.shape[0] == DEPTH
    bp, seq, _ = x_prompt.shape
    bs, dseq, _ = x_sample.shape
    past = cache_k.shape[2]
    l = 0
    vec = lambda a: a.reshape(1, -1).astype(F32)

    lb_all = jnp.cumsum(jax.nn.softmax(lb_param.astype(F32), axis=0), axis=0)
    lb = lb_all[l].reshape(1, B_KW)
    f1 = _ffn_weights(ffn1_wg[l], ffn1_wu[l], ffn1_wd[l])
    f2 = _ffn_weights(ffn2_wg[l], ffn2_wu[l], ffn2_wd[l])
    w_pieces, off = [], 0
    for _, width, _ in _PROJ_OUT:
        w_pieces.append(w_in[l][:, off:off + width].astype(BF16))
        off += width
    w_gate = w_in[l][:, off:].astype(BF16)
    assert w_gate.shape[1] == sum(_GATE_WIDTHS)

    kb_p = PROMPT_KEY_BLOCK
    s_p = -(-(N_META + seq) // kb_p) * kb_p
    q_cols, k_cols = _alibi_columns(s_p)

    def stage1(x):
        return _ffn_in(x, vec(ln1_g[l]), vec(ln1_b[l]), *f1, q_cols[:, None, :], w_pieces,
                       _token_tile(x.shape[0], FFN_IN_TILE))

    def stage4(p, oa, orec):
        m = p["h"].shape[0]
        return _mix_out(p["h"], oa.reshape(m, A_Q), orec.reshape(m, B_WIDTH), w_gate,
                        vec(g_norm[l]), w_a_proj[l].astype(BF16), w_b_proj[l].astype(BF16),
                        w_out[l].astype(BF16), vec(ln2_g[l]), vec(ln2_b[l]), *f2,
                        vec(ln3_g[l]), vec(ln3_b[l]), _token_tile(m, MIX_OUT_TILE))

    pm = stage1(meta.astype(F32))
    pp = stage1(x_prompt.reshape(bp * seq, D_MODEL))
    ps = stage1(x_sample.reshape(bs * dseq, D_MODEL))
    per_p = lambda a: a.reshape(bp, seq, a.shape[-1])
    per_s = lambda a: a.reshape(bs, dseq, a.shape[-1])

    def with_meta(name):
        rows = jnp.broadcast_to(pm[name][None], (bp, N_META, pm[name].shape[-1]))
        return jnp.concatenate([rows, per_p(pp[name])], axis=1)

    k_p, v_p, ki_p = with_meta("ak"), with_meta("av"), with_meta("ik")

    qpos = np.arange(seq, dtype=np.int32)
    lim = N_META + CHUNK * (qpos // CHUNK + 1)
    heads_p = lambda a: a.reshape(a.shape[0], bp, seq, a.shape[-1])
    oa_p = _attend_t(heads_p(pp["iq"]), per_p(pp["iw"]).transpose(0, 2, 1), heads_p(pp["aq"]),
                     *_key_operands(ki_p, k_p, v_p, s_p, k_cols), qpos, lim,
                     kb_size=kb_p, topk=min(TOPK_MAX, seq // 4), koff=N_META)

    assert dseq <= LANES
    n_all = past + dseq
    s_s = -(-n_all // kb_p) * kb_p
    k_all = jnp.concatenate([cache_k[l].reshape(bs, past, A_KV), per_s(ps["ak"])], axis=1)
    v_all = jnp.concatenate([cache_v[l].reshape(bs, past, A_KV), per_s(ps["av"])], axis=1)
    ki_all = jnp.concatenate([cache_kidx[l], per_s(ps["ik"])], axis=1)
    lane_pad = lambda a: jnp.pad(a, [(0, 0)] * (a.ndim - 2) + [(0, LANES - dseq), (0, 0)])
    heads_s = lambda a: lane_pad(a.reshape(a.shape[0], bs, dseq, a.shape[-1]))
    oa_s = _attend_t(heads_s(ps["iq"]), lane_pad(per_s(ps["iw"])).transpose(0, 2, 1), heads_s(ps["aq"]),
                     *_key_operands(ki_all, k_all, v_all, s_s, k_cols),
                     past + np.arange(LANES, dtype=np.int32),
                     np.where(np.arange(LANES) < dseq, n_all, 0).astype(np.int32),
                     kb_size=kb_p, topk=min(TOPK_MAX, n_all // 4), koff=0)[:, :dseq]

    zero_state = jnp.zeros((1, B_HEADS, B_KEY_DIM, B_VAL_DIM), F32)
    _, st_m = _hgrn2(pm["bq"][None], pm["bf"][None], pm["bi"][None], lb, zero_state)
    orec_p, st_p = _hgrn2(per_p(pp["bq"]), per_p(pp["bf"]), per_p(pp["bi"]), lb, st_m)
    orec_s, st_s = _hgrn2(per_s(ps["bq"]), per_s(ps["bf"]), per_s(ps["bi"]), lb,
                          state_hgrn[l].astype(F32))

    y_p = stage4(pp, oa_p, orec_p).reshape(bp, seq, D_MODEL)
    y_s = stage4(ps, oa_s, orec_s).reshape(bs, dseq, D_MODEL)

    kv5 = lambda a: a.reshape(1, a.shape[0], a.shape[1], A_KV_HEADS, A_HEAD_DIM)
    return (y_p, y_s, kv5(k_p), kv5(v_p), ki_p[None], st_p[None].astype(state_hgrn.dtype),
            kv5(per_s(ps["ak"])), kv5(per_s(ps["av"])), per_s(ps["ik"])[None],
            st_s[None].astype(state_hgrn.dtype))
```

```python
import functools

import jax
import jax.numpy as jnp
import numpy as np
from jax import lax
from jax.experimental import pallas as pl
from jax.experimental.pallas import tpu as pltpu

F32 = jnp.float32
BF16 = jnp.bfloat16
I32 = jnp.int32

D_MODEL = 1024
D_FF = 2816
FF_CHUNK = 256
N_FF_CHUNKS = D_FF // FF_CHUNK
CHUNK = 64
N_META = 16
A_HEADS = 8
A_KV_HEADS = 2
A_GROUP = A_HEADS // A_KV_HEADS
A_HEAD_DIM = 64
A_Q = A_HEADS * A_HEAD_DIM
A_KV = A_KV_HEADS * A_HEAD_DIM
IDX_HEADS = 8
IDX_DIM = 64
IDX_Q = IDX_HEADS * IDX_DIM
TOPK_MAX = 256
B_HEADS = 4
B_KEY_DIM = 128
B_VAL_DIM = 128
B_KW = B_HEADS * B_KEY_DIM
B_WIDTH = B_HEADS * B_VAL_DIM
DEPTH = 1
ALPHA = (2.0 * DEPTH) ** 0.25
LN_EPS = 1e-5
RMS_EPS = 1e-6
NEG_INF = -1e30
LOG2E = 1.4426950408889634
INT_MIN = -(2 ** 31)
LANES = 128
VMEM_LIMIT = 56 * 1024 * 1024


def _resident(shape):
    n = len(shape)
    return pl.BlockSpec(shape, lambda *_: (0,) * n, pipeline_mode=pl.Buffered(1))


def _layer_norm(x, g, b):
    mu = jnp.mean(x, axis=-1, keepdims=True)
    xc = x - mu
    var = jnp.mean(xc * xc, axis=-1, keepdims=True)
    return xc * lax.rsqrt(var + LN_EPS) * g + b


def _swiglu_step(x, wg_ref, wu_ref, wd_ref):
    xb = x.astype(BF16)

    acc = jnp.zeros(x.shape, F32)
    for c in range(N_FF_CHUNKS):
        cols = slice(c * FF_CHUNK, (c + 1) * FF_CHUNK)
        g = jnp.dot(xb, wg_ref[:, cols], preferred_element_type=F32)
        u = jnp.dot(xb, wu_ref[:, cols], preferred_element_type=F32)
        a = (g * jax.nn.sigmoid(g) * u).astype(BF16)
        acc = acc + jnp.dot(a, wd_ref[c], preferred_element_type=F32)
    return ALPHA * x + 0.5 * acc


_PROJ_OUT = (
    ("aq", A_Q, BF16), ("ak", A_KV, F32), ("av", A_KV, F32), ("iq", IDX_Q, BF16),
    ("ik", IDX_DIM, F32), ("iw", IDX_HEADS, F32),
    ("bq", B_KW, F32), ("bf", B_KW, F32), ("bi", B_WIDTH, F32),
)
_GATE_WIDTHS = (B_WIDTH, D_MODEL, D_MODEL)
_PROJ_SCALE = {"aq": (A_HEAD_DIM ** -0.5) * LOG2E, "iw": (IDX_HEADS ** -0.5) * (IDX_DIM ** -0.5)}
_HEAD_MAJOR = {"aq": A_HEADS, "iq": IDX_HEADS}


def _ffn_in_kernel(x_ref, g_ref, b_ref, wg_ref, wu_ref, wd_ref, qcols_ref, *rest):
    n = len(_PROJ_OUT)
    w_refs, h_ref, out_refs = rest[:n], rest[n], rest[n + 1:]
    x = x_ref[...]
    h = _layer_norm(_swiglu_step(x, wg_ref, wu_ref, wd_ref), g_ref[...], b_ref[...])
    h_ref[...] = h
    hb = h.astype(BF16)
    for (name, _, dt), w_ref, o_ref in zip(_PROJ_OUT, w_refs, out_refs):
        y = jnp.dot(hb, w_ref[...], preferred_element_type=F32)
        if name in _PROJ_SCALE:
            y = y * _PROJ_SCALE[name]
        if name in _HEAD_MAJOR:
            hd = y.shape[1] // _HEAD_MAJOR[name]
            for head in range(_HEAD_MAJOR[name]):
                yh = y[:, head * hd:(head + 1) * hd].astype(dt)
                if name == "aq":
                    yh = jnp.concatenate([yh, jnp.broadcast_to(qcols_ref[head], yh.shape)], axis=1)
                o_ref[head] = yh
        else:
            o_ref[...] = y.astype(dt)


def _ffn_in(x, ln_g, ln_b, wg, wu, wd, q_cols, w_pieces, tm):
    m = x.shape[0]
    assert m % tm == 0
    row = lambda w: pl.BlockSpec((tm, w), lambda i: (i, 0))
    in_specs = [row(D_MODEL), _resident(ln_g.shape), _resident(ln_b.shape),
                _resident(wg.shape), _resident(wu.shape), _resident(wd.shape),
                _resident(q_cols.shape)]
    in_specs += [_resident(w.shape) for w in w_pieces]
    out_shape = [jax.ShapeDtypeStruct((m, D_MODEL), F32)]
    out_specs = [row(D_MODEL)]
    for name, width, dt in _PROJ_OUT:
        if name in _HEAD_MAJOR:
            nh = _HEAD_MAJOR[name]
            hd = width // nh + (q_cols.shape[-1] if name == "aq" else 0)
            out_shape.append(jax.ShapeDtypeStruct((nh, m, hd), dt))
            out_specs.append(pl.BlockSpec((nh, tm, hd), lambda i: (0, i, 0)))
        else:
            out_shape.append(jax.ShapeDtypeStruct((m, width), dt))
            out_specs.append(row(width))
    outs = pl.pallas_call(
        _ffn_in_kernel,
        grid=(m // tm,),
        in_specs=in_specs,
        out_specs=out_specs,
        out_shape=out_shape,
        compiler_params=pltpu.CompilerParams(
            dimension_semantics=("arbitrary",), vmem_limit_bytes=VMEM_LIMIT),
        name="ffn_in",
    )(x, ln_g, ln_b, wg, wu, wd, q_cols, *w_pieces)
    res = {"h": outs[0]}
    for (name, _, _), o in zip(_PROJ_OUT, outs[1:]):
        res[name] = o
    return res


def _fold_rows(x, op, rows=8):
    while x.shape[0] > rows:
        half = x.shape[0] // 2
        x = op(x[:half], x[half:])
    return x


VT_ROWS = A_HEAD_DIM + 16
PROMPT_KEY_BLOCK = 512
KEY_SCAN_STEP = 256
TIE_PEEL_MAX = 4


def _bit_planes(u):
    slab = u.shape[0] // 32
    a = [u[i * slab:(i + 1) * slab] for i in range(32)]
    j, m = 16, 0x0000FFFF
    while j:
        k = 0
        while k < 32:
            t = (a[k] ^ lax.shift_right_logical(a[k + j], jnp.int32(j))) & m
            a[k] = a[k] ^ t
            a[k + j] = a[k + j] ^ lax.shift_left(t, jnp.int32(j))
            k = (k + j + 1) & ~j
        j >>= 1
        m ^= (m << j) & 0x7FFFFFFF
    return a


def _alibi_slopes_log2():
    return tuple(float(2.0 ** (-8.0 * (i + 1) / A_HEADS)) * LOG2E for i in range(A_HEADS))


ALIBI_SPLIT = CHUNK


def _alibi_columns(n_keys):
    q_cols = np.zeros((A_HEADS, A_HEAD_DIM), np.float32)
    for h, c in enumerate(_alibi_slopes_log2()):
        for i in range(3):
            ci = float(np.asarray(c, dtype=BF16).astype(np.float32))
            q_cols[h, 2 * i], q_cols[h, 2 * i + 1] = ALIBI_SPLIT * ci, ci
            c -= ci
    ids = np.arange(n_keys)
    k_cols = np.zeros((n_keys, A_HEAD_DIM), np.float32)
    k_cols[:, 0:6:2] = (ids // ALIBI_SPLIT)[:, None]
    k_cols[:, 1:6:2] = (ids % ALIBI_SPLIT)[:, None]
    return jnp.asarray(q_cols, BF16), jnp.asarray(k_cols, BF16)


def _attend_t_kernel(qpos_ref, lim_ref, iq_ref, wi_ref, q_ref, ki_ref, k_ref, vt_ref,
                     o_ref, key_scr, tie_scr, t_scr, plane_scr, *, n_keys, first_qpos, topk, koff, kb_size,
                     slopes):
    tq = o_ref.shape[0]
    lim = lim_ref[...]
    qpos = qpos_ref[...]
    n_pairs = A_HEADS // 2
    blocks = [(s0, min(kb_size, n_keys - s0), pl.ds(s0, min(kb_size, n_keys - s0)))
              for s0 in range(0, n_keys, kb_size)]

    def row_ids(s0, size):
        return s0 + lax.broadcasted_iota(I32, (size, tq), 0)

    def head_pair(ref, pr):
        return ref[2 * pr:2 * pr + 2].reshape(2 * tq, ref.shape[-1])

    for s0, size, rows in blocks:
        kib = ki_ref[rows, :]
        acc = jnp.zeros((size, tq), F32)
        for pr in range(IDX_HEADS // 2):
            lg = lax.dot_general(kib, head_pair(iq_ref, pr), (((1,), (1,)), ((), ())),
                                 preferred_element_type=F32)
            for u in range(2):
                h = 2 * pr + u
                acc = acc + jnp.maximum(lg[:, u * tq:(u + 1) * tq], 0.0) * wi_ref[h:h + 1, :]
        bits = lax.bitcast_convert_type(acc, I32)
        key = bits ^ ((bits >> 31) & 0x7FFFFFFF)
        key = jnp.where(row_ids(s0, size) < lim, key, INT_MIN)
        key_scr[rows, :] = key
        for p, plane in enumerate(_bit_planes(key ^ INT_MIN)):
            plane_scr[pl.ds(s0 + p * (size // 32), size // 32), :] = plane

    def count(plane, pred):
        acc = jnp.zeros((8, tq), F32)
        for _, _, rows in blocks:
            acc = acc + _fold_rows(jnp.where(pred(plane[rows, :]), 1.0, 0.0), jnp.add)
        return jnp.sum(acc, axis=0, keepdims=True)

    def kth_largest(plane, kth, bits):
        def bit_step(i, carry):
            t, n_at_t = carry
            cand = t + lax.shift_left(jnp.int32(1), bits - 1 - i)
            cnt = count(plane, lambda key: key >= cand)
            take = cnt >= kth
            return jnp.where(take, cand, t), jnp.where(take, cnt, n_at_t)
        low = INT_MIN if bits == 32 else -(1 << (bits - 1))
        return lax.fori_loop(0, bits, bit_step,
                             (jnp.full((1, tq), low, I32), jnp.zeros((1, tq), F32)))

    def ones_per_query(words):
        acc = None
        for w in words:
            pc = _fold_rows(lax.population_count(w), jnp.add)
            acc = pc if acc is None else acc + pc
        return jnp.sum(acc.astype(F32), axis=0, keepdims=True)

    def radix_select(kth):
        def bit_step(p, carry):
            t, n_above, live = carry
            ones = []
            for (s0, size, _), words in zip(blocks, live):
                slab = size // 32
                ones.append(words & plane_scr[pl.ds(pl.multiple_of(s0 + p * slab, 8), slab), :])
            n_ones = ones_per_query(ones)
            take = n_above + n_ones >= kth
            t = jnp.where(take, t | lax.shift_left(jnp.int32(1), 31 - p), t)
            n_above = jnp.where(take, n_above, n_above + n_ones)
            live = tuple(jnp.where(take, o, w ^ o) for o, w in zip(ones, live))
            return t, n_above, live
        live = tuple(jnp.full((size // 32, tq), -1, I32) for _, size, _ in blocks)
        t, n_above, live = lax.fori_loop(
            0, 32, bit_step, (jnp.zeros((1, tq), I32), jnp.zeros((1, tq), F32), live))
        return t ^ INT_MIN, n_above + ones_per_query(live)

    t, n_ge = radix_select(topk)
    thr = jnp.maximum(t, INT_MIN + 1)
    n_ge = jnp.where(t == INT_MIN, lim.astype(F32), n_ge)

    surplus = jnp.maximum(n_ge - topk, 0.0)
    max_surplus = jnp.max(surplus)

    @pl.when((max_surplus > 0.0) & (max_surplus <= TIE_PEEL_MAX))
    def _():
        def peel(_, left):
            worst = jnp.full((8, tq), -1, I32)
            for s0, size, rows in blocks:
                ids = jnp.where(key_scr[rows, :] == thr, row_ids(s0, size), -1)
                worst = jnp.maximum(worst, _fold_rows(ids, jnp.maximum))
            worst = jnp.where(left > 0.0, jnp.max(worst, axis=0, keepdims=True), -1)
            for s0, size, rows in blocks:
                key_scr[rows, :] = jnp.where(row_ids(s0, size) == worst, INT_MIN, key_scr[rows, :])
            return left - 1.0
        lax.fori_loop(0, max_surplus.astype(I32), peel, surplus)

    @pl.when(max_surplus > TIE_PEEL_MAX)
    def _():
        need = topk - count(key_scr, lambda key: key > thr)
        for s0, size, rows in blocks:
            tie_scr[rows, :] = jnp.where(key_scr[rows, :] == thr, -1 - row_ids(s0, size), INT_MIN)
        last, _ = kth_largest(tie_scr, need, key_scr.shape[0].bit_length() + 1)
        drop_q = n_ge > topk
        for s0, size, rows in blocks:
            key = key_scr[rows, :]
            drop = drop_q & (key == thr) & (-1 - row_ids(s0, size) < last)
            key_scr[rows, :] = jnp.where(drop, INT_MIN, key)

    ms = [jnp.full((1, tq), NEG_INF, F32) for _ in range(A_HEADS)]
    accs = [jnp.zeros((vt_ref.shape[1], A_GROUP * tq), F32) for _ in range(A_KV_HEADS)]
    for s0, size, rows in blocks:
        bias = jnp.where(key_scr[rows, :] >= thr, 0.0, NEG_INF)
        may_follow = s0 + size - 1 - koff > first_qpos
        if may_follow:
            ahead = jnp.minimum(qpos - (row_ids(s0, size) - koff), 0).astype(F32)
        new_m = []
        for pr in range(n_pairs):
            g = (2 * pr) // A_GROUP
            s2 = lax.dot_general(k_ref[g, rows, :], head_pair(q_ref, pr),
                                 (((1,), (1,)), ((), ())), preferred_element_type=F32)
            for u in range(2):
                h = 2 * pr + u
                s = s2[:, u * tq:(u + 1) * tq]
                if may_follow:
                    s = s + (2.0 * slopes[h]) * ahead
                s = s + bias
                t_scr[h, :size] = s
                blk_max = jnp.max(_fold_rows(s, jnp.maximum), axis=0, keepdims=True)
                new_m.append(jnp.maximum(ms[h], blk_max))
        for g in range(A_KV_HEADS):
            ps, alphas = [], []
            for h in range(g * A_GROUP, (g + 1) * A_GROUP):
                ps.append(jnp.exp2(t_scr[h, :size] - new_m[h]).astype(BF16))
                alphas.append(jnp.exp2(ms[h] - new_m[h]))
            pv = jnp.dot(vt_ref[g, :, rows], jnp.concatenate(ps, axis=1),
                         preferred_element_type=F32)
            accs[g] = jnp.concatenate(alphas, axis=1) * accs[g] + pv
        ms = new_m
    for pr in range(n_pairs):
        rows = []
        for h in (2 * pr, 2 * pr + 1):
            g, hh = divmod(h, A_GROUP)
            cols = slice(hh * tq, (hh + 1) * tq)
            rows.append(accs[g][:A_HEAD_DIM, cols] / accs[g][A_HEAD_DIM:A_HEAD_DIM + 1, cols])
        o_ref[:, 2 * pr * A_HEAD_DIM:(2 * pr + 2) * A_HEAD_DIM] = (
            jnp.concatenate(rows, axis=0).T.astype(o_ref.dtype))


def _key_operands(ki, k, v, s_pad, k_cols):
    bsz = k.shape[0]
    rows = lambda a: _pad_rows(a.astype(BF16), s_pad)
    heads = lambda a: rows(a).reshape(bsz, s_pad, A_KV_HEADS, A_HEAD_DIM)
    vt = heads(v).transpose(2, 0, 3, 1)
    vt = jnp.concatenate([vt, jnp.ones_like(vt[:, :, :1]),
                          jnp.zeros_like(vt[:, :, :VT_ROWS - A_HEAD_DIM - 1])], axis=2)
    k_aug = jnp.concatenate([heads(k).transpose(2, 0, 1, 3), jnp.broadcast_to(
        k_cols[None, None, :s_pad], (A_KV_HEADS, bsz, s_pad, A_HEAD_DIM))], axis=-1)
    return rows(ki), k_aug, vt


def _attend_t(iq, wi_t, q, ki, k, vt, qpos, lim, *, kb_size, topk, koff):
    tq = LANES
    _, bsz, nq, _ = q.shape
    s_pad = ki.shape[1]
    assert nq % tq == 0 and s_pad % kb_size == 0
    keys_of_tile = -(-lim.reshape(nq // tq, tq).max(axis=1) // KEY_SCAN_STEP) * KEY_SCAN_STEP
    assert keys_of_tile.max() <= s_pad
    runs, j0 = [], 0
    for j in range(1, len(keys_of_tile) + 1):
        if j == len(keys_of_tile) or keys_of_tile[j] != keys_of_tile[j0]:
            runs.append((j0, j, int(keys_of_tile[j0])))
            j0 = j
    qpos_row = jnp.asarray(qpos, I32).reshape(1, nq)
    lim_row = jnp.asarray(lim, I32).reshape(1, nq)
    outs = []
    for j0, j1, n_keys in runs:
        heads = lambda a: pl.BlockSpec((a.shape[0], None, tq, a.shape[-1]),
                                       lambda b, j: (0, b, j + j0, 0))
        pos = pl.BlockSpec((1, tq), lambda b, j: (0, j + j0))
        outs.append(pl.pallas_call(
            functools.partial(_attend_t_kernel, n_keys=n_keys, first_qpos=int(qpos[j0 * tq]), topk=topk,
                              koff=koff, kb_size=kb_size, slopes=_alibi_slopes_log2()),
            grid=(bsz, j1 - j0),
            in_specs=[pos, pos, heads(iq),
                      pl.BlockSpec((None, IDX_HEADS, tq), lambda b, j: (b, 0, j + j0)),
                      heads(q),
                      pl.BlockSpec((None, s_pad, IDX_DIM), lambda b, j: (b, 0, 0)),
                      pl.BlockSpec((A_KV_HEADS, None, s_pad, k.shape[-1]), lambda b, j: (0, b, 0, 0)),
                      pl.BlockSpec((A_KV_HEADS, None, VT_ROWS, s_pad), lambda b, j: (0, b, 0, 0))],
            out_specs=pl.BlockSpec((None, tq, A_Q), lambda b, j: (b, j, 0)),
            scratch_shapes=[pltpu.VMEM((s_pad, tq), I32), pltpu.VMEM((s_pad, tq), I32),
                            pltpu.VMEM((A_HEADS, kb_size, tq), F32), pltpu.VMEM((s_pad, tq), I32)],
            out_shape=jax.ShapeDtypeStruct((bsz, (j1 - j0) * tq, A_Q), BF16),
            compiler_params=pltpu.CompilerParams(
                dimension_semantics=("arbitrary", "arbitrary"), vmem_limit_bytes=VMEM_LIMIT),
            name=f"attend_t{n_keys}",
        )(qpos_row, lim_row, iq, wi_t, q, ki, k, vt))
    return jnp.concatenate(outs, axis=1)


HGRN_CHUNK = 128
HGRN_ROW_GATHER = 16


def _hgrn2_tables(ct):
    r = np.arange(ct)
    seg = [r[:, None] >= r[None, :]]
    pair = []
    g = ct
    while g >= 2:
        mid = (r // g) * g + g // 2
        second = r >= mid
        if g < HGRN_ROW_GATHER:
            seg.append(np.where(second[:, None],
                                (r[None, :] >= mid[:, None]) & (r[None, :] <= r[:, None]),
                                (r[None, :] > r[:, None]) & (r[None, :] < mid[:, None])))
        pair.append((r[:, None] // g == r[None, :] // g) & second[:, None] & ~second[None, :])
        g //= 2
    return np.concatenate(seg, 0).astype(np.float32), np.stack(pair).astype(np.float32)


def _hgrn2_kernel(bq_ref, bf_ref, bi_ref, lb_ref, seg_ref, pair_ref, s0_ref, o_ref, sout_ref, st_scr):
    c = pl.program_id(1)
    ct = bq_ref.shape[0]
    n_lvl = pair_ref.shape[0]

    @pl.when(c == 0)
    def _():
        for h in range(B_HEADS):
            st_scr[h] = s0_ref[h].T

    lb = lb_ref[...]
    fx = bf_ref[...]
    logf = jnp.log(lb + (1.0 - lb) * jax.nn.sigmoid(fx))
    kk = (1.0 - lb) * jax.nn.sigmoid(-fx)
    bq = bq_ref[...]
    rq = bq * jax.nn.sigmoid(bq)
    rv = bi_ref[...]
    p1 = logf.astype(BF16)
    p2 = (logf - p1.astype(F32)).astype(BF16)
    seg = seg_ref[...]
    sums = jnp.dot(seg, p1, preferred_element_type=F32) + jnp.dot(seg, p2, preferred_element_type=F32)
    b_in = sums[:ct]
    b_out = b_in[ct - 1:ct] - b_in
    q_in = (rq * jnp.exp(b_in)).astype(BF16)
    k_out = (kk * jnp.exp(b_out)).astype(BF16)
    rvb = rv.astype(BF16)
    row = lax.broadcasted_iota(I32, (ct, B_KEY_DIM), 0)
    row_w = lax.broadcasted_iota(I32, (ct, B_KW), 0)
    nt = (((1,), (1,)), ((), ()))

    a = [jnp.zeros((ct, ct), F32) for _ in range(B_HEADS)]
    n_gather = 0
    for lvl in range(n_lvl):
        g = ct >> lvl
        half = g // 2
        second = (row & half) != 0
        if g >= HGRN_ROW_GATHER:
            b_mid = jnp.concatenate(
                [jnp.broadcast_to(b_in[i * g + half - 1:i * g + half], (g, B_KW)) for i in range(ct // g)],
                axis=0)
            dec = jnp.exp(jnp.where((row_w & half) != 0, b_in - b_mid, b_mid - b_in))
            n_gather += 1
        else:
            dec = jnp.exp(sums[(1 + lvl - n_gather) * ct:(2 + lvl - n_gather) * ct])
        for h in range(B_HEADS):
            ls = slice(h * B_KEY_DIM, (h + 1) * B_KEY_DIM)
            x = (jnp.where(second, rq[:, ls], kk[:, ls]) * dec[:, ls]).astype(BF16)
            a[h] = a[h] + pair_ref[lvl] * lax.dot_general(x, x, nt, preferred_element_type=F32)

    for h in range(B_HEADS):
        ls = slice(h * B_KEY_DIM, (h + 1) * B_KEY_DIM)
        st = st_scr[h]
        own = jnp.sum(rq[:, ls] * kk[:, ls], axis=1, keepdims=True)
        o_ref[:, ls] = (jnp.dot(a[h].astype(BF16), rvb[:, ls], preferred_element_type=F32)
                        + own * rv[:, ls]
                        + lax.dot_general(q_in[:, ls], st.astype(BF16), nt, preferred_element_type=F32))
        upd = lax.dot_general(rvb[:, ls], k_out[:, ls], (((0,), (0,)), ((), ())),
                              preferred_element_type=F32)
        st_scr[h] = st * jnp.exp(b_in[ct - 1:ct, ls]) + upd

    @pl.when(c == pl.num_programs(1) - 1)
    def _():
        for h in range(B_HEADS):
            sout_ref[h] = st_scr[h].T


def _hgrn2(bq, bf, bi, lb, s0):
    bsz, t, _ = bq.shape
    ct = HGRN_CHUNK
    tp = -(-t // ct) * ct
    if tp != t:
        pad = lambda a, v: jnp.pad(a, ((0, 0), (0, tp - t), (0, 0)), constant_values=v)
        bq, bf, bi = pad(bq, 0.0), pad(bf, 1e4), pad(bi, 0.0)
    seg, pair = _hgrn2_tables(ct)
    seg = jnp.asarray(seg, BF16)
    pair = jnp.asarray(pair, F32)
    tok = pl.BlockSpec((None, ct, B_KW), lambda b, c: (b, c, 0))
    s_map = (lambda b, c: (0, 0, 0, 0)) if s0.shape[0] == 1 else (lambda b, c: (b, 0, 0, 0))
    s_in = pl.BlockSpec((None, B_HEADS, B_KEY_DIM, B_VAL_DIM), s_map)
    s_out = pl.BlockSpec((None, B_HEADS, B_KEY_DIM, B_VAL_DIM), lambda b, c: (b, 0, 0, 0))
    o, s = pl.pallas_call(
        _hgrn2_kernel,
        grid=(bsz, tp // ct),
        in_specs=[tok, tok, tok, _resident(lb.shape), _resident(seg.shape), _resident(pair.shape), s_in],
        out_specs=[tok, s_out],
        out_shape=[jax.ShapeDtypeStruct((bsz, tp, B_WIDTH), F32),
                   jax.ShapeDtypeStruct((bsz, B_HEADS, B_KEY_DIM, B_VAL_DIM), F32)],
        scratch_shapes=[pltpu.VMEM((B_HEADS, B_VAL_DIM, B_KEY_DIM), F32)],
        compiler_params=pltpu.CompilerParams(
            dimension_semantics=("arbitrary", "arbitrary"), vmem_limit_bytes=VMEM_LIMIT),
        name="hgrn2",
    )(bq, bf, bi, lb, seg, pair, s0)
    return o[:, :t], s


def _mix_out_kernel(h_ref, oa_ref, orec_ref, wgate_ref, gn_ref, wa_ref, wb_ref, wo_ref,
                    g2_ref, b2_ref, wg_ref, wu_ref, wd_ref, g3_ref, b3_ref, y_ref):
    h = h_ref[...]
    gates = jnp.dot(h.astype(BF16), wgate_ref[...], preferred_element_type=F32)
    bg = gates[:, :B_WIDTH]
    ga = gates[:, B_WIDTH:B_WIDTH + D_MODEL]
    gb = gates[:, B_WIDTH + D_MODEL:]
    orec = orec_ref[...]
    gn = gn_ref[...]
    parts = []
    for head in range(B_HEADS):
        ls = slice(head * B_VAL_DIM, (head + 1) * B_VAL_DIM)
        oh = orec[:, ls]
        ms = jnp.mean(oh * oh, axis=-1, keepdims=True)
        parts.append(oh * lax.rsqrt(ms + RMS_EPS) * gn[:, ls])
    on = jnp.concatenate(parts, axis=-1)
    ob = on * (bg * jax.nn.sigmoid(bg))
    pa = jnp.dot(oa_ref[...], wa_ref[...], preferred_element_type=F32)
    pb = jnp.dot(ob.astype(BF16), wb_ref[...], preferred_element_type=F32)
    mixed = jax.nn.sigmoid(ga) * pa + jax.nn.sigmoid(gb) * pb
    mixed = jnp.dot(mixed.astype(BF16), wo_ref[...], preferred_element_type=F32)
    h2 = _layer_norm(ALPHA * h + mixed, g2_ref[...], b2_ref[...])
    y_ref[...] = _layer_norm(_swiglu_step(h2, wg_ref, wu_ref, wd_ref), g3_ref[...], b3_ref[...])


def _mix_out(h, oa, orec, wgate, gn, wa, wb, wo, g2, b2, wg, wu, wd, g3, b3, tm):
    m = h.shape[0]
    assert m % tm == 0
    row = lambda w: pl.BlockSpec((tm, w), lambda i: (i, 0))
    weights = (wgate, gn, wa, wb, wo, g2, b2, wg, wu, wd, g3, b3)
    return pl.pallas_call(
        _mix_out_kernel,
        grid=(m // tm,),
        in_specs=[row(D_MODEL), row(A_Q), row(B_WIDTH)] + [_resident(w.shape) for w in weights],
        out_specs=row(D_MODEL),
        out_shape=jax.ShapeDtypeStruct((m, D_MODEL), F32),
        compiler_params=pltpu.CompilerParams(
            dimension_semantics=("arbitrary",), vmem_limit_bytes=VMEM_LIMIT),
        name="mix_out",
    )(h, oa, orec, *weights)


def _ffn_weights(wg, wu, wd):
    return wg.astype(BF16), wu.astype(BF16), wd.astype(BF16).reshape(N_FF_CHUNKS, FF_CHUNK, D_MODEL)


def _pad_rows(a, n):
    return jnp.pad(a, ((0, 0), (0, n - a.shape[1]), (0, 0)))


FFN_IN_TILE = 512
MIX_OUT_TILE = 512


def _token_tile(m, largest):
    tm = largest
    while tm > 8 and m % tm:
        tm //= 2
    assert m % tm == 0, m
    return tm


def kernel(x_prompt, x_sample, cache_k, cache_v, cache_kidx, state_hgrn, meta, ln1_g, ln1_b, ffn1_wg, ffn1_wu, ffn1_wd, w_in, lb_param, g_norm, w_a_proj, w_b_proj, w_out, ln2_g, ln2_b, ffn2_wg, ffn2_wu, ffn2_wd, ln3_g, ln3_b):
    assert ln1_g.shape[0] == DEPTH
    bp, seq, _ = x_prompt.shape
    bs, dseq, _ = x_sample.shape
    past = cache_k.shape[2]
    l = 0
    vec = lambda a: a.reshape(1, -1).astype(F32)

    lb_all = jnp.cumsum(jax.nn.softmax(lb_param.astype(F32), axis=0), axis=0)
    lb = lb_all[l].reshape(1, B_KW)
    f1 = _ffn_weights(ffn1_wg[l], ffn1_wu[l], ffn1_wd[l])
    f2 = _ffn_weights(ffn2_wg[l], ffn2_wu[l], ffn2_wd[l])
    w_pieces, off = [], 0
    for _, width, _ in _PROJ_OUT:
        w_pieces.append(w_in[l][:, off:off + width].astype(BF16))
        off += width
    w_gate = w_in[l][:, off:].astype(BF16)
    assert w_gate.shape[1] == sum(_GATE_WIDTHS)

    kb_p = PROMPT_KEY_BLOCK
    s_p = -(-(N_META + seq) // kb_p) * kb_p
    q_cols, k_cols = _alibi_columns(s_p)

    def stage1(x):
        return _ffn_in(x, vec(ln1_g[l]), vec(ln1_b[l]), *f1, q_cols[:, None, :], w_pieces,
                       _token_tile(x.shape[0], FFN_IN_TILE))

    def stage4(p, oa, orec):
        m = p["h"].shape[0]
        return _mix_out(p["h"], oa.reshape(m, A_Q), orec.reshape(m, B_WIDTH), w_gate,
                        vec(g_norm[l]), w_a_proj[l].astype(BF16), w_b_proj[l].astype(BF16),
                        w_out[l].astype(BF16), vec(ln2_g[l]), vec(ln2_b[l]), *f2,
                        vec(ln3_g[l]), vec(ln3_b[l]), _token_tile(m, MIX_OUT_TILE))

    pm = stage1(meta.astype(F32))
    pp = stage1(x_prompt.reshape(bp * seq, D_MODEL))
    ps = stage1(x_sample.reshape(bs * dseq, D_MODEL))
    per_p = lambda a: a.reshape(bp, seq, a.shape[-1])
    per_s = lambda a: a.reshape(bs, dseq, a.shape[-1])

    def with_meta(name):
        rows = jnp.broadcast_to(pm[name][None], (bp, N_META, pm[name].shape[-1]))
        return jnp.concatenate([rows, per_p(pp[name])], axis=1)

    k_p, v_p, ki_p = with_meta("ak"), with_meta("av"), with_meta("ik")

    qpos = np.arange(seq, dtype=np.int32)
    lim = N_META + CHUNK * (qpos // CHUNK + 1)
    heads_p = lambda a: a.reshape(a.shape[0], bp, seq, a.shape[-1])
    oa_p = _attend_t(heads_p(pp["iq"]), per_p(pp["iw"]).transpose(0, 2, 1), heads_p(pp["aq"]),
                     *_key_operands(ki_p, k_p, v_p, s_p, k_cols), qpos, lim,
                     kb_size=kb_p, topk=min(TOPK_MAX, seq // 4), koff=N_META)

    assert dseq <= LANES
    n_all = past + dseq
    s_s = -(-n_all // kb_p) * kb_p
    k_all = jnp.concatenate([cache_k[l].reshape(bs, past, A_KV), per_s(ps["ak"])], axis=1)
    v_all = jnp.concatenate([cache_v[l].reshape(bs, past, A_KV), per_s(ps["av"])], axis=1)
    ki_all = jnp.concatenate([cache_kidx[l], per_s(ps["ik"])], axis=1)
    lane_pad = lambda a: jnp.pad(a, [(0, 0)] * (a.ndim - 2) + [(0, LANES - dseq), (0, 0)])
    heads_s = lambda a: lane_pad(a.reshape(a.shape[0], bs, dseq, a.shape[-1]))
    oa_s = _attend_t(heads_s(ps["iq"]), lane_pad(per_s(ps["iw"])).transpose(0, 2, 1), heads_s(ps["aq"]),
                     *_key_operands(ki_all, k_all, v_all, s_s, k_cols),
                     past + np.arange(LANES, dtype=np.int32),
                     np.where(np.arange(LANES) < dseq, n_all, 0).astype(np.int32),
                     kb_size=kb_p, topk=min(TOPK_MAX, n_all // 4), koff=0)[:, :dseq]

    zero_state = jnp.zeros((1, B_HEADS, B_KEY_DIM, B_VAL_DIM), F32)
    _, st_m = _hgrn2(pm["bq"][None], pm["bf"][None], pm["bi"][None], lb, zero_state)
    orec_p, st_p = _hgrn2(per_p(pp["bq"]), per_p(pp["bf"]), per_p(pp["bi"]), lb, st_m)
    orec_s, st_s = _hgrn2(per_s(ps["bq"]), per_s(ps["bf"]), per_s(ps["bi"]), lb,
                          state_hgrn[l].astype(F32))

    y_p = stage4(pp, oa_p, orec_p).reshape(bp, seq, D_MODEL)
    y_s = stage4(ps, oa_s, orec_s).reshape(bs, dseq, D_MODEL)

    kv5 = lambda a: a.reshape(1, a.shape[0], a.shape[1], A_KV_HEADS, A_HEAD_DIM)
    return (y_p, y_s, kv5(k_p), kv5(v_p), ki_p[None], st_p[None].astype(state_hgrn.dtype),
            kv5(per_s(ps["ak"])), kv5(per_s(ps["av"])), per_s(ps["ik"])[None],
            st_s[None].astype(state_hgrn.dtype))
```

```python
import functools

import jax
import jax.numpy as jnp
import numpy as np
from jax import lax
from jax.experimental import pallas as pl
from jax.experimental.pallas import tpu as pltpu

F32 = jnp.float32
BF16 = jnp.bfloat16
I32 = jnp.int32

D_MODEL = 1024
D_FF = 2816
FF_CHUNK = 256
N_FF_CHUNKS = D_FF // FF_CHUNK
CHUNK = 64
N_META = 16
A_HEADS = 8
A_KV_HEADS = 2
A_GROUP = A_HEADS // A_KV_HEADS
A_HEAD_DIM = 64
A_Q = A_HEADS * A_HEAD_DIM
A_KV = A_KV_HEADS * A_HEAD_DIM
IDX_HEADS = 8
IDX_DIM = 64
IDX_Q = IDX_HEADS * IDX_DIM
TOPK_MAX = 256
B_HEADS = 4
B_KEY_DIM = 128
B_VAL_DIM = 128
B_KW = B_HEADS * B_KEY_DIM
B_WIDTH = B_HEADS * B_VAL_DIM
DEPTH = 1
ALPHA = (2.0 * DEPTH) ** 0.25
LN_EPS = 1e-5
RMS_EPS = 1e-6
NEG_INF = -1e30
LOG2E = 1.4426950408889634
INT_MIN = -(2 ** 31)
LANES = 128
VMEM_LIMIT = 56 * 1024 * 1024


def _resident(shape):
    n = len(shape)
    return pl.BlockSpec(shape, lambda *_: (0,) * n, pipeline_mode=pl.Buffered(1))


def _layer_norm(x, g, b):
    mu = jnp.mean(x, axis=-1, keepdims=True)
    xc = x - mu
    var = jnp.mean(xc * xc, axis=-1, keepdims=True)
    return xc * lax.rsqrt(var + LN_EPS) * g + b


def _swiglu_step(x, wg_ref, wu_ref, wd_ref):
    xb = x.astype(BF16)

    acc = jnp.zeros(x.shape, F32)
    for c in range(N_FF_CHUNKS):
        cols = slice(c * FF_CHUNK, (c + 1) * FF_CHUNK)
        g = jnp.dot(xb, wg_ref[:, cols], preferred_element_type=F32)
        u = jnp.dot(xb, wu_ref[:, cols], preferred_element_type=F32)
        a = (g * jax.nn.sigmoid(g) * u).astype(BF16)
        acc = acc + jnp.dot(a, wd_ref[c], preferred_element_type=F32)
    return ALPHA * x + 0.5 * acc


_PROJ_OUT = (
    ("aq", A_Q, BF16), ("ak", A_KV, F32), ("av", A_KV, F32), ("iq", IDX_Q, BF16),
    ("ik", IDX_DIM, F32), ("iw", IDX_HEADS, F32),
    ("bq", B_KW, F32), ("bf", B_KW, F32), ("bi", B_WIDTH, F32),
)
_GATE_WIDTHS = (B_WIDTH, D_MODEL, D_MODEL)
_PROJ_SCALE = {"aq": (A_HEAD_DIM ** -0.5) * LOG2E, "iw": (IDX_HEADS ** -0.5) * (IDX_DIM ** -0.5)}
_HEAD_MAJOR = {"aq": A_HEADS, "iq": IDX_HEADS}


def _ffn_in_kernel(x_ref, g_ref, b_ref, wg_ref, wu_ref, wd_ref, qcols_ref, *rest):
    n = len(_PROJ_OUT)
    w_refs, h_ref, out_refs = rest[:n], rest[n], rest[n + 1:]
    x = x_ref[...]
    h = _layer_norm(_swiglu_step(x, wg_ref, wu_ref, wd_ref), g_ref[...], b_ref[...])
    h_ref[...] = h
    hb = h.astype(BF16)
    for (name, _, dt), w_ref, o_ref in zip(_PROJ_OUT, w_refs, out_refs):
        y = jnp.dot(hb, w_ref[...], preferred_element_type=F32)
        if name in _PROJ_SCALE:
            y = y * _PROJ_SCALE[name]
        if name in _HEAD_MAJOR:
            hd = y.shape[1] // _HEAD_MAJOR[name]
            for head in range(_HEAD_MAJOR[name]):
                yh = y[:, head * hd:(head + 1) * hd].astype(dt)
                if name == "aq":
                    yh = jnp.concatenate([yh, jnp.broadcast_to(qcols_ref[head], yh.shape)], axis=1)
                o_ref[head] = yh
        else:
            o_ref[...] = y.astype(dt)


def _ffn_in(x, ln_g, ln_b, wg, wu, wd, q_cols, w_pieces, tm):
    m = x.shape[0]
    assert m % tm == 0
    row = lambda w: pl.BlockSpec((tm, w), lambda i: (i, 0))
    in_specs = [row(D_MODEL), _resident(ln_g.shape), _resident(ln_b.shape),
                _resident(wg.shape), _resident(wu.shape), _resident(wd.shape),
                _resident(q_cols.shape)]
    in_specs += [_resident(w.shape) for w in w_pieces]
    out_shape = [jax.ShapeDtypeStruct((m, D_MODEL), F32)]
    out_specs = [row(D_MODEL)]
    for name, width, dt in _PROJ_OUT:
        if name in _HEAD_MAJOR:
            nh = _HEAD_MAJOR[name]
            hd = width // nh + (q_cols.shape[-1] if name == "aq" else 0)
            out_shape.append(jax.ShapeDtypeStruct((nh, m, hd), dt))
            out_specs.append(pl.BlockSpec((nh, tm, hd), lambda i: (0, i, 0)))
        else:
            out_shape.append(jax.ShapeDtypeStruct((m, width), dt))
            out_specs.append(row(width))
    outs = pl.pallas_call(
        _ffn_in_kernel,
        grid=(m // tm,),
        in_specs=in_specs,
        out_specs=out_specs,
        out_shape=out_shape,
        compiler_params=pltpu.CompilerParams(
            dimension_semantics=("arbitrary",), vmem_limit_bytes=VMEM_LIMIT),
        name="ffn_in",
    )(x, ln_g, ln_b, wg, wu, wd, q_cols, *w_pieces)
    res = {"h": outs[0]}
    for (name, _, _), o in zip(_PROJ_OUT, outs[1:]):
        res[name] = o
    return res


def _fold_rows(x, op, rows=8):
    while x.shape[0] > rows:
        half = x.shape[0] // 2
        x = op(x[:half], x[half:])
    return x


VT_ROWS = A_HEAD_DIM + 16
PROMPT_KEY_BLOCK = 512
KEY_SCAN_STEP = 256
QK_ROW_CHUNK = 128
TIE_PEEL_MAX = 4


def _bit_planes(u):
    slab = u.shape[0] // 32
    a = [u[i * slab:(i + 1) * slab] for i in range(32)]
    j, m = 16, 0x0000FFFF
    while j:
        k = 0
        while k < 32:
            t = (a[k] ^ lax.shift_right_logical(a[k + j], jnp.int32(j))) & m
            a[k] = a[k] ^ t
            a[k + j] = a[k + j] ^ lax.shift_left(t, jnp.int32(j))
            k = (k + j + 1) & ~j
        j >>= 1
        m ^= (m << j) & 0x7FFFFFFF
    return a


def _alibi_slopes_log2():
    return tuple(float(2.0 ** (-8.0 * (i + 1) / A_HEADS)) * LOG2E for i in range(A_HEADS))


ALIBI_SPLIT = CHUNK


def _alibi_columns(n_keys):
    q_cols = np.zeros((A_HEADS, A_HEAD_DIM), np.float32)
    for h, c in enumerate(_alibi_slopes_log2()):
        for i in range(3):
            ci = float(np.asarray(c, dtype=BF16).astype(np.float32))
            q_cols[h, 2 * i], q_cols[h, 2 * i + 1] = ALIBI_SPLIT * ci, ci
            c -= ci
    ids = np.arange(n_keys)
    k_cols = np.zeros((n_keys, A_HEAD_DIM), np.float32)
    k_cols[:, 0:6:2] = (ids // ALIBI_SPLIT)[:, None]
    k_cols[:, 1:6:2] = (ids % ALIBI_SPLIT)[:, None]
    return jnp.asarray(q_cols, BF16), jnp.asarray(k_cols, BF16)


def _attend_t_kernel(qpos_ref, lim_ref, iq_ref, wi_ref, q_ref, ki_ref, k_ref, vt_ref,
                     o_ref, key_scr, tie_scr, t_scr, plane_scr, *, n_keys, first_qpos, topk, koff, kb_size,
                     slopes):
    tq = o_ref.shape[0]
    lim = lim_ref[...]
    qpos = qpos_ref[...]
    n_pairs = A_HEADS // 2
    blocks = [(s0, min(kb_size, n_keys - s0), pl.ds(s0, min(kb_size, n_keys - s0)))
              for s0 in range(0, n_keys, kb_size)]

    def row_ids(s0, size):
        return s0 + lax.broadcasted_iota(I32, (size, tq), 0)

    def head_pair(ref, pr):
        return ref[2 * pr:2 * pr + 2].reshape(2 * tq, ref.shape[-1])

    for s0, size, rows in blocks:
        kib = ki_ref[rows, :]
        acc = jnp.zeros((size, tq), F32)
        for pr in range(IDX_HEADS // 2):
            lg = lax.dot_general(kib, head_pair(iq_ref, pr), (((1,), (1,)), ((), ())),
                                 preferred_element_type=F32)
            for u in range(2):
                h = 2 * pr + u
                acc = acc + jnp.maximum(lg[:, u * tq:(u + 1) * tq], 0.0) * wi_ref[h:h + 1, :]
        bits = lax.bitcast_convert_type(acc, I32)
        key = bits ^ ((bits >> 31) & 0x7FFFFFFF)
        key = jnp.where(row_ids(s0, size) < lim, key, INT_MIN)
        key_scr[rows, :] = key
        for p, plane in enumerate(_bit_planes(key ^ INT_MIN)):
            plane_scr[pl.ds(s0 + p * (size // 32), size // 32), :] = plane

    def count(plane, pred):
        acc = jnp.zeros((8, tq), F32)
        for _, _, rows in blocks:
            acc = acc + _fold_rows(jnp.where(pred(plane[rows, :]), 1.0, 0.0), jnp.add)
        return jnp.sum(acc, axis=0, keepdims=True)

    def kth_largest(plane, kth, bits):
        def bit_step(i, carry):
            t, n_at_t = carry
            cand = t + lax.shift_left(jnp.int32(1), bits - 1 - i)
            cnt = count(plane, lambda key: key >= cand)
            take = cnt >= kth
            return jnp.where(take, cand, t), jnp.where(take, cnt, n_at_t)
        low = INT_MIN if bits == 32 else -(1 << (bits - 1))
        return lax.fori_loop(0, bits, bit_step,
                             (jnp.full((1, tq), low, I32), jnp.zeros((1, tq), F32)))

    def ones_per_query(words):
        acc = None
        for w in words:
            pc = _fold_rows(lax.population_count(w), jnp.add)
            acc = pc if acc is None else acc + pc
        return jnp.sum(acc.astype(F32), axis=0, keepdims=True)

    def radix_select(kth):
        def bit_step(p, carry):
            t, n_above, live = carry
            ones = []
            for (s0, size, _), words in zip(blocks, live):
                slab = size // 32
                ones.append(words & plane_scr[pl.ds(pl.multiple_of(s0 + p * slab, 8), slab), :])
            n_ones = ones_per_query(ones)
            take = n_above + n_ones >= kth
            t = jnp.where(take, t | lax.shift_left(jnp.int32(1), 31 - p), t)
            n_above = jnp.where(take, n_above, n_above + n_ones)
            live = tuple(jnp.where(take, o, w ^ o) for o, w in zip(ones, live))
            return t, n_above, live
        live = tuple(jnp.full((size // 32, tq), -1, I32) for _, size, _ in blocks)
        t, n_above, live = lax.fori_loop(
            0, 32, bit_step, (jnp.zeros((1, tq), I32), jnp.zeros((1, tq), F32), live))
        return t ^ INT_MIN, n_above + ones_per_query(live)

    t, n_ge = radix_select(topk)
    thr = jnp.maximum(t, INT_MIN + 1)
    n_ge = jnp.where(t == INT_MIN, lim.astype(F32), n_ge)

    surplus = jnp.maximum(n_ge - topk, 0.0)
    max_surplus = jnp.max(surplus)

    @pl.when((max_surplus > 0.0) & (max_surplus <= TIE_PEEL_MAX))
    def _():
        def peel(_, left):
            worst = jnp.full((8, tq), -1, I32)
            for s0, size, rows in blocks:
                ids = jnp.where(key_scr[rows, :] == thr, row_ids(s0, size), -1)
                worst = jnp.maximum(worst, _fold_rows(ids, jnp.maximum))
            worst = jnp.where(left > 0.0, jnp.max(worst, axis=0, keepdims=True), -1)
            for s0, size, rows in blocks:
                key_scr[rows, :] = jnp.where(row_ids(s0, size) == worst, INT_MIN, key_scr[rows, :])
            return left - 1.0
        lax.fori_loop(0, max_surplus.astype(I32), peel, surplus)

    @pl.when(max_surplus > TIE_PEEL_MAX)
    def _():
        need = topk - count(key_scr, lambda key: key > thr)
        for s0, size, rows in blocks:
            tie_scr[rows, :] = jnp.where(key_scr[rows, :] == thr, -1 - row_ids(s0, size), INT_MIN)
        last, _ = kth_largest(tie_scr, need, key_scr.shape[0].bit_length() + 1)
        drop_q = n_ge > topk
        for s0, size, rows in blocks:
            key = key_scr[rows, :]
            drop = drop_q & (key == thr) & (-1 - row_ids(s0, size) < last)
            key_scr[rows, :] = jnp.where(drop, INT_MIN, key)

    ms = [jnp.full((1, tq), NEG_INF, F32) for _ in range(A_HEADS)]
    accs = [jnp.zeros((vt_ref.shape[1], A_GROUP * tq), F32) for _ in range(A_KV_HEADS)]
    for s0, size, rows in blocks:
        bias = jnp.where(key_scr[rows, :] >= thr, 0.0, NEG_INF)
        may_follow = s0 + size - 1 - koff > first_qpos
        if may_follow:
            ahead = jnp.minimum(qpos - (row_ids(s0, size) - koff), 0).astype(F32)
        new_m = []
        for pr in range(n_pairs):
            g = (2 * pr) // A_GROUP
            q_pair = head_pair(q_ref, pr)
            maxes = [[], []]
            for c0 in range(0, size, QK_ROW_CHUNK):
                cr = slice(c0, c0 + QK_ROW_CHUNK)
                s2 = lax.dot_general(k_ref[g, pl.ds(s0 + c0, QK_ROW_CHUNK), :], q_pair,
                                     (((1,), (1,)), ((), ())), preferred_element_type=F32)
                for u in range(2):
                    h = 2 * pr + u
                    s = s2[:, u * tq:(u + 1) * tq]
                    if may_follow:
                        s = s + (2.0 * slopes[h]) * ahead[cr]
                    s = s + bias[cr]
                    t_scr[h, cr] = s
                    maxes[u].append(_fold_rows(s, jnp.maximum))
            for u in range(2):
                blk_max = jnp.max(functools.reduce(jnp.maximum, maxes[u]), axis=0, keepdims=True)
                new_m.append(jnp.maximum(ms[2 * pr + u], blk_max))
        for g in range(A_KV_HEADS):
            ps, alphas = [], []
            for h in range(g * A_GROUP, (g + 1) * A_GROUP):
                ps.append(jnp.exp2(t_scr[h, :size] - new_m[h]).astype(BF16))
                alphas.append(jnp.exp2(ms[h] - new_m[h]))
            pv = jnp.dot(vt_ref[g, :, rows], jnp.concatenate(ps, axis=1),
                         preferred_element_type=F32)
            accs[g] = jnp.concatenate(alphas, axis=1) * accs[g] + pv
        ms = new_m
    for pr in range(n_pairs):
        rows = []
        for h in (2 * pr, 2 * pr + 1):
            g, hh = divmod(h, A_GROUP)
            cols = slice(hh * tq, (hh + 1) * tq)
            rows.append(accs[g][:A_HEAD_DIM, cols] / accs[g][A_HEAD_DIM:A_HEAD_DIM + 1, cols])
        o_ref[:, 2 * pr * A_HEAD_DIM:(2 * pr + 2) * A_HEAD_DIM] = (
            jnp.concatenate(rows, axis=0).T.astype(o_ref.dtype))


def _key_operands(ki, k, v, s_pad, k_cols):
    bsz = k.shape[0]
    rows = lambda a: _pad_rows(a.astype(BF16), s_pad)
    heads = lambda a: rows(a).reshape(bsz, s_pad, A_KV_HEADS, A_HEAD_DIM)
    vt = heads(v).transpose(2, 0, 3, 1)
    vt = jnp.concatenate([vt, jnp.ones_like(vt[:, :, :1]),
                          jnp.zeros_like(vt[:, :, :VT_ROWS - A_HEAD_DIM - 1])], axis=2)
    k_aug = jnp.concatenate([heads(k).transpose(2, 0, 1, 3), jnp.broadcast_to(
        k_cols[None, None, :s_pad], (A_KV_HEADS, bsz, s_pad, A_HEAD_DIM))], axis=-1)
    return rows(ki), k_aug, vt


def _attend_t(iq, wi_t, q, ki, k, vt, qpos, lim, *, kb_size, topk, koff):
    tq = LANES
    _, bsz, nq, _ = q.shape
    s_pad = ki.shape[1]
    assert nq % tq == 0 and s_pad % kb_size == 0
    keys_of_tile = -(-lim.reshape(nq // tq, tq).max(axis=1) // KEY_SCAN_STEP) * KEY_SCAN_STEP
    assert keys_of_tile.max() <= s_pad
    runs, j0 = [], 0
    for j in range(1, len(keys_of_tile) + 1):
        if j == len(keys_of_tile) or keys_of_tile[j] != keys_of_tile[j0]:
            runs.append((j0, j, int(keys_of_tile[j0])))
            j0 = j
    qpos_row = jnp.asarray(qpos, I32).reshape(1, nq)
    lim_row = jnp.asarray(lim, I32).reshape(1, nq)
    outs = []
    for j0, j1, n_keys in runs:
        heads = lambda a: pl.BlockSpec((a.shape[0], None, tq, a.shape[-1]),
                                       lambda b, j: (0, b, j + j0, 0))
        pos = pl.BlockSpec((1, tq), lambda b, j: (0, j + j0))
        outs.append(pl.pallas_call(
            functools.partial(_attend_t_kernel, n_keys=n_keys, first_qpos=int(qpos[j0 * tq]), topk=topk,
                              koff=koff, kb_size=kb_size, slopes=_alibi_slopes_log2()),
            grid=(bsz, j1 - j0),
            in_specs=[pos, pos, heads(iq),
                      pl.BlockSpec((None, IDX_HEADS, tq), lambda b, j: (b, 0, j + j0)),
                      heads(q),
                      pl.BlockSpec((None, s_pad, IDX_DIM), lambda b, j: (b, 0, 0)),
                      pl.BlockSpec((A_KV_HEADS, None, s_pad, k.shape[-1]), lambda b, j: (0, b, 0, 0)),
                      pl.BlockSpec((A_KV_HEADS, None, VT_ROWS, s_pad), lambda b, j: (0, b, 0, 0))],
            out_specs=pl.BlockSpec((None, tq, A_Q), lambda b, j: (b, j, 0)),
            scratch_shapes=[pltpu.VMEM((s_pad, tq), I32), pltpu.VMEM((s_pad, tq), I32),
                            pltpu.VMEM((A_HEADS, kb_size, tq), F32), pltpu.VMEM((s_pad, tq), I32)],
            out_shape=jax.ShapeDtypeStruct((bsz, (j1 - j0) * tq, A_Q), BF16),
            compiler_params=pltpu.CompilerParams(
                dimension_semantics=("arbitrary", "arbitrary"), vmem_limit_bytes=VMEM_LIMIT),
            name=f"attend_t{n_keys}",
        )(qpos_row, lim_row, iq, wi_t, q, ki, k, vt))
    return jnp.concatenate(outs, axis=1)


HGRN_CHUNK = 128
HGRN_ROW_GATHER = 16


def _hgrn2_tables(ct):
    r = np.arange(ct)
    seg = [r[:, None] >= r[None, :]]
    pair = []
    g = ct
    while g >= 2:
        mid = (r // g) * g + g // 2
        second = r >= mid
        if g < HGRN_ROW_GATHER:
            seg.append(np.where(second[:, None],
                                (r[None, :] >= mid[:, None]) & (r[None, :] <= r[:, None]),
                                (r[None, :] > r[:, None]) & (r[None, :] < mid[:, None])))
        pair.append((r[:, None] // g == r[None, :] // g) & second[:, None] & ~second[None, :])
        g //= 2
    return np.concatenate(seg, 0).astype(np.float32), np.stack(pair).astype(np.float32)


def _hgrn2_kernel(bq_ref, bf_ref, bi_ref, lb_ref, seg_ref, pair_ref, s0_ref, o_ref, sout_ref, st_scr):
    c = pl.program_id(1)
    ct = bq_ref.shape[0]
    n_lvl = pair_ref.shape[0]

    @pl.when(c == 0)
    def _():
        for h in range(B_HEADS):
            st_scr[h] = s0_ref[h].T

    lb = lb_ref[...]
    fx = bf_ref[...]
    logf = jnp.log(lb + (1.0 - lb) * jax.nn.sigmoid(fx))
    kk = (1.0 - lb) * jax.nn.sigmoid(-fx)
    bq = bq_ref[...]
    rq = bq * jax.nn.sigmoid(bq)
    rv = bi_ref[...]
    p1 = logf.astype(BF16)
    p2 = (logf - p1.astype(F32)).astype(BF16)
    seg = seg_ref[...]
    sums = jnp.dot(seg, p1, preferred_element_type=F32) + jnp.dot(seg, p2, preferred_element_type=F32)
    b_in = sums[:ct]
    b_out = b_in[ct - 1:ct] - b_in
    q_in = (rq * jnp.exp(b_in)).astype(BF16)
    k_out = (kk * jnp.exp(b_out)).astype(BF16)
    rvb = rv.astype(BF16)
    row = lax.broadcasted_iota(I32, (ct, B_KEY_DIM), 0)
    row_w = lax.broadcasted_iota(I32, (ct, B_KW), 0)
    nt = (((1,), (1,)), ((), ()))

    a = [jnp.zeros((ct, ct), F32) for _ in range(B_HEADS)]
    n_gather = 0
    for lvl in range(n_lvl):
        g = ct >> lvl
        half = g // 2
        second = (row & half) != 0
        if g >= HGRN_ROW_GATHER:
            b_mid = jnp.concatenate(
                [jnp.broadcast_to(b_in[i * g + half - 1:i * g + half], (g, B_KW)) for i in range(ct // g)],
                axis=0)
            dec = jnp.exp(jnp.where((row_w & half) != 0, b_in - b_mid, b_mid - b_in))
            n_gather += 1
        else:
            dec = jnp.exp(sums[(1 + lvl - n_gather) * ct:(2 + lvl - n_gather) * ct])
        for h in range(B_HEADS):
            ls = slice(h * B_KEY_DIM, (h + 1) * B_KEY_DIM)
            x = (jnp.where(second, rq[:, ls], kk[:, ls]) * dec[:, ls]).astype(BF16)
            a[h] = a[h] + pair_ref[lvl] * lax.dot_general(x, x, nt, preferred_element_type=F32)

    for h in range(B_HEADS):
        ls = slice(h * B_KEY_DIM, (h + 1) * B_KEY_DIM)
        st = st_scr[h]
        own = jnp.sum(rq[:, ls] * kk[:, ls], axis=1, keepdims=True)
        o_ref[:, ls] = (jnp.dot(a[h].astype(BF16), rvb[:, ls], preferred_element_type=F32)
                        + own * rv[:, ls]
                        + lax.dot_general(q_in[:, ls], st.astype(BF16), nt, preferred_element_type=F32))
        upd = lax.dot_general(rvb[:, ls], k_out[:, ls], (((0,), (0,)), ((), ())),
                              preferred_element_type=F32)
        st_scr[h] = st * jnp.exp(b_in[ct - 1:ct, ls]) + upd

    @pl.when(c == pl.num_programs(1) - 1)
    def _():
        for h in range(B_HEADS):
            sout_ref[h] = st_scr[h].T


def _hgrn2(bq, bf, bi, lb, s0):
    bsz, t, _ = bq.shape
    ct = HGRN_CHUNK
    tp = -(-t // ct) * ct
    if tp != t:
        pad = lambda a, v: jnp.pad(a, ((0, 0), (0, tp - t), (0, 0)), constant_values=v)
        bq, bf, bi = pad(bq, 0.0), pad(bf, 1e4), pad(bi, 0.0)
    seg, pair = _hgrn2_tables(ct)
    seg = jnp.asarray(seg, BF16)
    pair = jnp.asarray(pair, F32)
    tok = pl.BlockSpec((None, ct, B_KW), lambda b, c: (b, c, 0))
    s_map = (lambda b, c: (0, 0, 0, 0)) if s0.shape[0] == 1 else (lambda b, c: (b, 0, 0, 0))
    s_in = pl.BlockSpec((None, B_HEADS, B_KEY_DIM, B_VAL_DIM), s_map)
    s_out = pl.BlockSpec((None, B_HEADS, B_KEY_DIM, B_VAL_DIM), lambda b, c: (b, 0, 0, 0))
    o, s = pl.pallas_call(
        _hgrn2_kernel,
        grid=(bsz, tp // ct),
        in_specs=[tok, tok, tok, _resident(lb.shape), _resident(seg.shape), _resident(pair.shape), s_in],
        out_specs=[tok, s_out],
        out_shape=[jax.ShapeDtypeStruct((bsz, tp, B_WIDTH), F32),
                   jax.ShapeDtypeStruct((bsz, B_HEADS, B_KEY_DIM, B_VAL_DIM), F32)],
        scratch_shapes=[pltpu.VMEM((B_HEADS, B_VAL_DIM, B_KEY_DIM), F32)],
        compiler_params=pltpu.CompilerParams(
            dimension_semantics=("arbitrary", "arbitrary"), vmem_limit_bytes=VMEM_LIMIT),
        name="hgrn2",
    )(bq, bf, bi, lb, seg, pair, s0)
    return o[:, :t], s


def _mix_out_kernel(h_ref, oa_ref, orec_ref, wgate_ref, gn_ref, wa_ref, wb_ref, wo_ref,
                    g2_ref, b2_ref, wg_ref, wu_ref, wd_ref, g3_ref, b3_ref, y_ref):
    h = h_ref[...]
    gates = jnp.dot(h.astype(BF16), wgate_ref[...], preferred_element_type=F32)
    bg = gates[:, :B_WIDTH]
    ga = gates[:, B_WIDTH:B_WIDTH + D_MODEL]
    gb = gates[:, B_WIDTH + D_MODEL:]
    orec = orec_ref[...]
    gn = gn_ref[...]
    parts = []
    for head in range(B_HEADS):
        ls = slice(head * B_VAL_DIM, (head + 1) * B_VAL_DIM)
        oh = orec[:, ls]
        ms = jnp.mean(oh * oh, axis=-1, keepdims=True)
        parts.append(oh * lax.rsqrt(ms + RMS_EPS) * gn[:, ls])
    on = jnp.concatenate(parts, axis=-1)
    ob = on * (bg * jax.nn.sigmoid(bg))
    pa = jnp.dot(oa_ref[...], wa_ref[...], preferred_element_type=F32)
    pb = jnp.dot(ob.astype(BF16), wb_ref[...], preferred_element_type=F32)
    mixed = jax.nn.sigmoid(ga) * pa + jax.nn.sigmoid(gb) * pb
    mixed = jnp.dot(mixed.astype(BF16), wo_ref[...], preferred_element_type=F32)
    h2 = _layer_norm(ALPHA * h + mixed, g2_ref[...], b2_ref[...])
    y_ref[...] = _layer_norm(_swiglu_step(h2, wg_ref, wu_ref, wd_ref), g3_ref[...], b3_ref[...])


def _mix_out(h, oa, orec, wgate, gn, wa, wb, wo, g2, b2, wg, wu, wd, g3, b3, tm):
    m = h.shape[0]
    assert m % tm == 0
    row = lambda w: pl.BlockSpec((tm, w), lambda i: (i, 0))
    weights = (wgate, gn, wa, wb, wo, g2, b2, wg, wu, wd, g3, b3)
    return pl.pallas_call(
        _mix_out_kernel,
        grid=(m // tm,),
        in_specs=[row(D_MODEL), row(A_Q), row(B_WIDTH)] + [_resident(w.shape) for w in weights],
        out_specs=row(D_MODEL),
        out_shape=jax.ShapeDtypeStruct((m, D_MODEL), F32),
        compiler_params=pltpu.CompilerParams(
            dimension_semantics=("arbitrary",), vmem_limit_bytes=VMEM_LIMIT),
        name="mix_out",
    )(h, oa, orec, *weights)


def _ffn_weights(wg, wu, wd):
    return wg.astype(BF16), wu.astype(BF16), wd.astype(BF16).reshape(N_FF_CHUNKS, FF_CHUNK, D_MODEL)


def _pad_rows(a, n):
    return jnp.pad(a, ((0, 0), (0, n - a.shape[1]), (0, 0)))


FFN_IN_TILE = 512
MIX_OUT_TILE = 512


def _token_tile(m, largest):
    tm = largest
    while tm > 8 and m % tm:
        tm //= 2
    assert m % tm == 0, m
    return tm


def kernel(x_prompt, x_sample, cache_k, cache_v, cache_kidx, state_hgrn, meta, ln1_g, ln1_b, ffn1_wg, ffn1_wu, ffn1_wd, w_in, lb_param, g_norm, w_a_proj, w_b_proj, w_out, ln2_g, ln2_b, ffn2_wg, ffn2_wu, ffn2_wd, ln3_g, ln3_b):
    assert ln1_g.shape[0] == DEPTH
    bp, seq, _ = x_prompt.shape
    bs, dseq, _ = x_sample.shape
    past = cache_k.shape[2]
    l = 0
    vec = lambda a: a.reshape(1, -1).astype(F32)

    lb_all = jnp.cumsum(jax.nn.softmax(lb_param.astype(F32), axis=0), axis=0)
    lb = lb_all[l].reshape(1, B_KW)
    f1 = _ffn_weights(ffn1_wg[l], ffn1_wu[l], ffn1_wd[l])
    f2 = _ffn_weights(ffn2_wg[l], ffn2_wu[l], ffn2_wd[l])
    w_pieces, off = [], 0
    for _, width, _ in _PROJ_OUT:
        w_pieces.append(w_in[l][:, off:off + width].astype(BF16))
        off += width
    w_gate = w_in[l][:, off:].astype(BF16)
    assert w_gate.shape[1] == sum(_GATE_WIDTHS)

    kb_p = PROMPT_KEY_BLOCK
    s_p = -(-(N_META + seq) // kb_p) * kb_p
    q_cols, k_cols = _alibi_columns(s_p)

    def stage1(x):
        return _ffn_in(x, vec(ln1_g[l]), vec(ln1_b[l]), *f1, q_cols[:, None, :], w_pieces,
                       _token_tile(x.shape[0], FFN_IN_TILE))

    def stage4(p, oa, orec):
        m = p["h"].shape[0]
        return _mix_out(p["h"], oa.reshape(m, A_Q), orec.reshape(m, B_WIDTH), w_gate,
                        vec(g_norm[l]), w_a_proj[l].astype(BF16), w_b_proj[l].astype(BF16),
                        w_out[l].astype(BF16), vec(ln2_g[l]), vec(ln2_b[l]), *f2,
                        vec(ln3_g[l]), vec(ln3_b[l]), _token_tile(m, MIX_OUT_TILE))

    pm = stage1(meta.astype(F32))
    pp = stage1(x_prompt.reshape(bp * seq, D_MODEL))
    ps = stage1(x_sample.reshape(bs * dseq, D_MODEL))
    per_p = lambda a: a.reshape(bp, seq, a.shape[-1])
    per_s = lambda a: a.reshape(bs, dseq, a.shape[-1])

    def with_meta(name):
        rows = jnp.broadcast_to(pm[name][None], (bp, N_META, pm[name].shape[-1]))
        return jnp.concatenate([rows, per_p(pp[name])], axis=1)

    k_p, v_p, ki_p = with_meta("ak"), with_meta("av"), with_meta("ik")

    qpos = np.arange(seq, dtype=np.int32)
    lim = N_META + CHUNK * (qpos // CHUNK + 1)
    heads_p = lambda a: a.reshape(a.shape[0], bp, seq, a.shape[-1])
    oa_p = _attend_t(heads_p(pp["iq"]), per_p(pp["iw"]).transpose(0, 2, 1), heads_p(pp["aq"]),
                     *_key_operands(ki_p, k_p, v_p, s_p, k_cols), qpos, lim,
                     kb_size=kb_p, topk=min(TOPK_MAX, seq // 4), koff=N_META)

    assert dseq <= LANES
    n_all = past + dseq
    s_s = -(-n_all // kb_p) * kb_p
    k_all = jnp.concatenate([cache_k[l].reshape(bs, past, A_KV), per_s(ps["ak"])], axis=1)
    v_all = jnp.concatenate([cache_v[l].reshape(bs, past, A_KV), per_s(ps["av"])], axis=1)
    ki_all = jnp.concatenate([cache_kidx[l], per_s(ps["ik"])], axis=1)
    lane_pad = lambda a: jnp.pad(a, [(0, 0)] * (a.ndim - 2) + [(0, LANES - dseq), (0, 0)])
    heads_s = lambda a: lane_pad(a.reshape(a.shape[0], bs, dseq, a.shape[-1]))
    oa_s = _attend_t(heads_s(ps["iq"]), lane_pad(per_s(ps["iw"])).transpose(0, 2, 1), heads_s(ps["aq"]),
                     *_key_operands(ki_all, k_all, v_all, s_s, k_cols),
                     past + np.arange(LANES, dtype=np.int32),
                     np.where(np.arange(LANES) < dseq, n_all, 0).astype(np.int32),
                     kb_size=kb_p, topk=min(TOPK_MAX, n_all // 4), koff=0)[:, :dseq]

    zero_state = jnp.zeros((1, B_HEADS, B_KEY_DIM, B_VAL_DIM), F32)
    _, st_m = _hgrn2(pm["bq"][None], pm["bf"][None], pm["bi"][None], lb, zero_state)
    orec_p, st_p = _hgrn2(per_p(pp["bq"]), per_p(pp["bf"]), per_p(pp["bi"]), lb, st_m)
    orec_s, st_s = _hgrn2(per_s(ps["bq"]), per_s(ps["bf"]), per_s(ps["bi"]), lb,
                          state_hgrn[l].astype(F32))

    y_p = stage4(pp, oa_p, orec_p).reshape(bp, seq, D_MODEL)
    y_s = stage4(ps, oa_s, orec_s).reshape(bs, dseq, D_MODEL)

    kv5 = lambda a: a.reshape(1, a.shape[0], a.shape[1], A_KV_HEADS, A_HEAD_DIM)
    return (y_p, y_s, kv5(k_p), kv5(v_p), ki_p[None], st_p[None].astype(state_hgrn.dtype),
            kv5(per_s(ps["ak"])), kv5(per_s(ps["av"])), per_s(ps["ik"])[None],
            st_s[None].astype(state_hgrn.dtype))
```
